```python
import jax, jax.numpy as jnp
from jax import lax
import numpy as np

D_MODEL = 1024
BATCH = 2
SEQ = 8192
DEPTH = 1
DEC_BATCH = 32
DEC_SEQ = 32
PAST_LEN = 1024

CHUNK = 64
N_MEM = 256
D_LRU = 512
LRU_BLOCKS = 8
LRU_BLOCK = D_LRU // LRU_BLOCKS
LRU_CONV = 4
LRU_C = 8.0
D_CONV = 256
CONV_W = 31
N_XHEADS = 4
XHEAD_DIM = 64
D_XATTN = N_XHEADS * XHEAD_DIM
D_MIX = D_LRU + D_CONV + D_XATTN
D_IN = 2 * D_LRU + 3 * D_CONV + 2 * D_XATTN
EPS = 1e-6

kernel_name = 'hybrid_lru_conformer_memory_step'


def rmsnorm(x, g):
    x32 = x.astype(jnp.float32)
    y = x32 * lax.rsqrt(jnp.mean(x32 * x32, axis=-1, keepdims=True) + EPS)
    return (y * g.astype(jnp.float32)).astype(x.dtype)


def layernorm(x, g, b):
    x32 = x.astype(jnp.float32)
    mu = jnp.mean(x32, axis=-1, keepdims=True)
    xc = x32 - mu
    y = xc * lax.rsqrt(jnp.mean(xc * xc, axis=-1, keepdims=True) + EPS)
    return (y * g.astype(jnp.float32) + b.astype(jnp.float32)).astype(x.dtype)


def causal_depthwise_conv(x, buf, w, b):
    xp = jnp.concatenate([buf.astype(x.dtype), x], axis=1)
    width, ch = w.shape
    y = lax.conv_general_dilated(
        xp, w.astype(x.dtype).reshape(width, 1, ch), window_strides=(1,), padding='VALID',
        dimension_numbers=('NWC', 'WIO', 'NWC'), feature_group_count=ch)
    return y + b.astype(x.dtype), xp[:, xp.shape[1] - (width - 1):]


def _linear_combine(left, right):
    a_l, u_l = left
    a_r, u_r = right
    return a_l * a_r, a_r * u_l + u_r


def rg_lru(x, h0, w_a, b_a, w_x, b_x, lam):
    bsz, t, _ = x.shape
    xb = x.reshape(bsz, t, LRU_BLOCKS, LRU_BLOCK)
    r = jax.nn.sigmoid((jnp.einsum('btnd,nde->btne', xb, w_a).reshape(bsz, t, D_LRU) + b_a).astype(jnp.float32))
    i = jax.nn.sigmoid((jnp.einsum('btnd,nde->btne', xb, w_x).reshape(bsz, t, D_LRU) + b_x).astype(jnp.float32))
    log_a = LRU_C * r * jax.nn.log_sigmoid(lam.astype(jnp.float32))
    a = jnp.exp(log_a)
    u = jnp.sqrt(-jnp.expm1(2.0 * log_a)) * (i * x.astype(jnp.float32))
    a_cum, u_cum = lax.associative_scan(_linear_combine, (a, u), axis=1)
    h = u_cum + a_cum * h0.astype(jnp.float32)[:, None, :]
    return h.astype(x.dtype), h[:, -1].astype(h0.dtype)


def memory_kv(mem, g_mem, w_k, w_v):
    bsz = mem.shape[0]
    m = rmsnorm(mem, g_mem)
    k = (m @ w_k).reshape(bsz, N_MEM, N_XHEADS, XHEAD_DIM)
    v = (m @ w_v).reshape(bsz, N_MEM, N_XHEADS, XHEAD_DIM)
    return k, v


def memory_attention(q, k, v):
    bsz, t, _ = q.shape
    qh = q.reshape(bsz, t, N_XHEADS, XHEAD_DIM).astype(jnp.float32)
    s = jnp.einsum('bthd,bmhd->bhtm', qh, k.astype(jnp.float32)) * (XHEAD_DIM ** -0.5)
    p = jax.nn.softmax(s, axis=-1)
    o = jnp.einsum('bhtm,bmhd->bthd', p, v.astype(jnp.float32))
    return o.reshape(bsz, t, D_XATTN).astype(q.dtype)


def hybrid_layer(x, lru_h0, lru_buf, conv_buf, mem_k, mem_v, g_norm, w_in, w_lru_conv, b_lru_conv,
                 w_gate_a, b_gate_a, w_gate_x, b_gate_x, lru_lambda, w_dw, b_dw, ln_g, ln_b,
                 w_pw, b_pw, w_out):
    h = rmsnorm(x, g_norm)
    z = h @ w_in
    splits = [D_LRU, 2 * D_LRU, 2 * D_LRU + D_CONV, 2 * D_LRU + 2 * D_CONV,
              2 * D_LRU + 3 * D_CONV, 2 * D_LRU + 3 * D_CONV + D_XATTN]
    lru_in, lru_gate, glu_a, glu_b, conv_gate, q, attn_gate = jnp.split(z, splits, axis=-1)
    xl, lru_buf_new = causal_depthwise_conv(lru_in, lru_buf, w_lru_conv, b_lru_conv)
    yl, h_last = rg_lru(xl, lru_h0, w_gate_a, b_gate_a, w_gate_x, b_gate_x, lru_lambda)
    yl = yl * jax.nn.silu(lru_gate)
    uc = glu_a * jax.nn.sigmoid(glu_b)
    uc, conv_buf_new = causal_depthwise_conv(uc, conv_buf, w_dw, b_dw)
    uc = jax.nn.silu(layernorm(uc, ln_g, ln_b))
    yc = (uc @ w_pw + b_pw) * jax.nn.silu(conv_gate)
    ya = memory_attention(q, mem_k, mem_v) * jax.nn.silu(attn_gate)
    y = jnp.concatenate([yl, yc, ya], axis=-1) @ w_out
    return x + y, h_last, lru_buf_new, conv_buf_new


def setup_inputs(seed: int = 0) -> dict:
    key = jax.random.key(seed)
    ks = jax.random.split(key, 32)
    f32 = jnp.float32
    nrm = lambda k, shape, scale: scale * jax.random.normal(k, shape, f32)
    u = jax.random.uniform(ks[15], (DEPTH, D_LRU), f32, minval=0.9, maxval=0.999)
    s = u ** (1.0 / LRU_C)
    return {
        'x_prompt': nrm(ks[0], (BATCH, SEQ, D_MODEL), 1.0),
        'x_sample': nrm(ks[1], (DEC_BATCH, DEC_SEQ, D_MODEL), 1.0),
        'state_lru_h': nrm(ks[2], (DEPTH, DEC_BATCH, D_LRU), 0.5),
        'state_lru_conv': nrm(ks[3], (DEPTH, DEC_BATCH, LRU_CONV - 1, D_LRU), 1.0),
        'state_conv': nrm(ks[4], (DEPTH, DEC_BATCH, CONV_W - 1, D_CONV), 1.0),
        'cache_mem_k': nrm(ks[5], (DEPTH, DEC_BATCH, N_MEM, N_XHEADS, XHEAD_DIM), 1.0),
        'cache_mem_v': nrm(ks[6], (DEPTH, DEC_BATCH, N_MEM, N_XHEADS, XHEAD_DIM), 1.0),
        'mem_prompt': nrm(ks[7], (BATCH, N_MEM, D_MODEL), 1.0),
        'g_norm': 1.0 + nrm(ks[8], (DEPTH, D_MODEL), 0.02),
        'w_in': nrm(ks[9], (DEPTH, D_MODEL, D_IN), D_MODEL ** -0.5),
        'w_lru_conv': nrm(ks[10], (DEPTH, LRU_CONV, D_LRU), LRU_CONV ** -0.5),
        'b_lru_conv': nrm(ks[11], (DEPTH, D_LRU), 0.02),
        'w_gate_a': nrm(ks[12], (DEPTH, LRU_BLOCKS, LRU_BLOCK, LRU_BLOCK), LRU_BLOCK ** -0.5),
        'b_gate_a': nrm(ks[13], (DEPTH, D_LRU), 0.02),
        'w_gate_x': nrm(ks[14], (DEPTH, LRU_BLOCKS, LRU_BLOCK, LRU_BLOCK), LRU_BLOCK ** -0.5),
        'b_gate_x': nrm(ks[16], (DEPTH, D_LRU), 0.02),
        'lru_lambda': jnp.log(s) - jnp.log1p(-s),
        'w_dw': nrm(ks[17], (DEPTH, CONV_W, D_CONV), CONV_W ** -0.5),
        'b_dw': nrm(ks[18], (DEPTH, D_CONV), 0.02),
        'ln_g': 1.0 + nrm(ks[19], (DEPTH, D_CONV), 0.02),
        'ln_b': nrm(ks[20], (DEPTH, D_CONV), 0.02),
        'w_pw': nrm(ks[21], (DEPTH, D_CONV, D_CONV), D_CONV ** -0.5),
        'b_pw': nrm(ks[22], (DEPTH, D_CONV), 0.02),
        'g_mem': 1.0 + nrm(ks[23], (DEPTH, D_MODEL), 0.02),
        'w_mem_k': nrm(ks[24], (DEPTH, D_MODEL, D_XATTN), D_MODEL ** -0.5),
        'w_mem_v': nrm(ks[25], (DEPTH, D_MODEL, D_XATTN), D_MODEL ** -0.5),
        'w_out': nrm(ks[26], (DEPTH, D_MIX, D_MODEL), D_MIX ** -0.5),
        'g_final': 1.0 + nrm(ks[27], (D_MODEL,), 0.02),
    }


def reference(x_prompt, x_sample, state_lru_h, state_lru_conv, state_conv, cache_mem_k, cache_mem_v,
              mem_prompt, g_norm, w_in, w_lru_conv, b_lru_conv, w_gate_a, b_gate_a, w_gate_x, b_gate_x,
              lru_lambda, w_dw, b_dw, ln_g, ln_b, w_pw, b_pw, g_mem, w_mem_k, w_mem_v, w_out, g_final):
    bp = x_prompt.shape[0]
    dt = x_prompt.dtype
    hp, hs = x_prompt, x_sample
    p_h, p_lb, p_cb, p_mk, p_mv = [], [], [], [], []
    s_h, s_lb, s_cb = [], [], []
    for l in range(DEPTH):
        mk, mv = memory_kv(mem_prompt, g_mem[l], w_mem_k[l], w_mem_v[l])
        hp, h_new, lb_new, cb_new = hybrid_layer(
            hp, jnp.zeros((bp, D_LRU), dt), jnp.zeros((bp, LRU_CONV - 1, D_LRU), dt),
            jnp.zeros((bp, CONV_W - 1, D_CONV), dt), mk, mv,
            g_norm[l], w_in[l], w_lru_conv[l], b_lru_conv[l], w_gate_a[l], b_gate_a[l], w_gate_x[l],
            b_gate_x[l], lru_lambda[l], w_dw[l], b_dw[l], ln_g[l], ln_b[l], w_pw[l], b_pw[l], w_out[l])
        p_h.append(h_new); p_lb.append(lb_new); p_cb.append(cb_new); p_mk.append(mk); p_mv.append(mv)
        hs, h_new, lb_new, cb_new = hybrid_layer(
            hs, state_lru_h[l], state_lru_conv[l], state_conv[l], cache_mem_k[l], cache_mem_v[l],
            g_norm[l], w_in[l], w_lru_conv[l], b_lru_conv[l], w_gate_a[l], b_gate_a[l], w_gate_x[l],
            b_gate_x[l], lru_lambda[l], w_dw[l], b_dw[l], ln_g[l], ln_b[l], w_pw[l], b_pw[l], w_out[l])
        s_h.append(h_new); s_lb.append(lb_new); s_cb.append(cb_new)
    y_prompt = rmsnorm(hp, g_final)
    y_sample = rmsnorm(hs, g_final)
    return (y_prompt, y_sample, jnp.stack(p_h), jnp.stack(p_lb), jnp.stack(p_cb), jnp.stack(p_mk),
            jnp.stack(p_mv), jnp.stack(s_h), jnp.stack(s_lb), jnp.stack(s_cb))
```

```python
import functools

import jax
import jax.numpy as jnp
from jax import lax
from jax.experimental import pallas as pl
from jax.experimental.pallas import tpu as pltpu

D_MODEL = 1024
N_MEM = 256
D_LRU = 512
LRU_BLOCKS = 8
LRU_BLOCK = D_LRU // LRU_BLOCKS
LRU_CONV = 4
LRU_C = 8.0
D_CONV = 256
CONV_W = 31
N_XHEADS = 4
XHEAD_DIM = 64
D_XATTN = N_XHEADS * XHEAD_DIM
D_MIX = D_LRU + D_CONV + D_XATTN
D_IN = 2 * D_LRU + 3 * D_CONV + 2 * D_XATTN
EPS = 1e-6

LANES = 128
SUBLANES = 8
N_LRU_SLABS = D_LRU // LANES
N_CONV_SLABS = D_CONV // LANES

OFF_LRU_IN = 0
OFF_LRU_GATE = D_LRU
OFF_GLU_A = 2 * D_LRU
OFF_GLU_B = OFF_GLU_A + D_CONV
OFF_CONV_GATE = OFF_GLU_B + D_CONV
OFF_Q = OFF_CONV_GATE + D_CONV
OFF_ATTN_GATE = OFF_Q + D_XATTN

LRU_HALO = 8
CONV_HALO = 32

PROMPT_TILE = 512
PROMPT_SEG = 64
PROMPT_PITCH = 72
SAMPLE_PITCH = 40
ATTN_ROWS = 128

VMEM_LIMIT_BYTES = 56 * 1024 * 1024


def _sigmoid(x):
    return jax.nn.sigmoid(x)


def _silu(x):
    return x * jax.nn.sigmoid(x)


def _rmsnorm(x, g):
    ms = jnp.mean(x * x, axis=-1, keepdims=True)
    return x * lax.rsqrt(ms + EPS) * g


def _log_sigmoid(x):
    y = -x
    return -(jnp.maximum(y, 0.0) + jnp.log1p(jnp.exp(-jnp.abs(y))))


def _bf16(x):
    return x.astype(jnp.bfloat16)


def _dot(a, b):
    return jnp.dot(a, b, preferred_element_type=jnp.float32)


def _head_masks():
    lane = lax.broadcasted_iota(jnp.int32, (1, D_XATTN), 1)
    head = lax.shift_right_logical(lane, XHEAD_DIM.bit_length() - 1)
    return [(head == h).astype(jnp.float32) for h in range(N_XHEADS)]


def _attention(q, k_bf, v_bf, masks):
    rows = q.shape[0]
    qm = jnp.concatenate([_bf16(q * m) for m in masks], axis=0)
    s = lax.dot_general(qm, k_bf, (((1,), (1,)), ((), ())),
                        preferred_element_type=jnp.float32)
    s = s * (XHEAD_DIM ** -0.5)
    s = s - jnp.max(s, axis=-1, keepdims=True)
    e = jnp.exp(s)
    p = e / jnp.sum(e, axis=-1, keepdims=True)
    o_all = _dot(_bf16(p), v_bf)
    o = o_all[0:rows] * masks[0]
    for h in range(1, N_XHEADS):
        o = o + o_all[h * rows:(h + 1) * rows] * masks[h]
    return o


def _lru_gates(xl_bf, wg_ref, pre_s):
    half = D_LRU // 2
    pre_s[:, 0:D_LRU] = _dot(xl_bf[:, 0:half], wg_ref[0])
    pre_s[:, D_LRU:2 * D_LRU] = _dot(xl_bf[:, half:D_LRU], wg_ref[1])


def _pre_cols(l):
    half_slabs = N_LRU_SLABS // 2
    base = (l // half_slabs) * D_LRU + (l % half_slabs) * LANES
    return base, base + D_LRU // 2


def _lru_coeffs(pre_a, pre_x, xl, b_a, b_x, ls):
    r = _sigmoid(pre_a + b_a)
    i = _sigmoid(pre_x + b_x)
    log_a = (LRU_C * r) * ls
    a = jnp.exp(log_a)
    one_minus_a2 = -jnp.tanh(log_a) * (a * a + 1.0)
    u = jnp.sqrt(one_minus_a2) * (i * xl)
    return a, u


def _conv_taps(src_s, l, w_ref, row0, nrows, width, halo):
    lo = l * LANES
    acc = None
    for k in range(width):
        w_k = w_ref[k:k + 1, lo:lo + LANES]
        x_k = src_s[l, pl.ds(row0 + (halo - (width - 1) + k), nrows), :]
        acc = w_k * x_k if acc is None else acc + w_k * x_k
    return acc


def _layernorm_silu_slabs(ys, g_ref, b_ref):
    tot = ys[0]
    for y in ys[1:]:
        tot = tot + y
    mu = jnp.sum(tot, axis=-1, keepdims=True) * (1.0 / D_CONV)
    cs = [y - mu for y in ys]
    sq = cs[0] * cs[0]
    for c in cs[1:]:
        sq = sq + c * c
    var = jnp.sum(sq, axis=-1, keepdims=True) * (1.0 / D_CONV)
    inv = lax.rsqrt(var + EPS)
    outs = []
    for l, c in enumerate(cs):
        lo = l * LANES
        outs.append(_silu(c * inv * g_ref[:, lo:lo + LANES] + b_ref[:, lo:lo + LANES]))
    return outs


def _memkv_kernel(mem_ref, g_ref, w_ref, k_ref, v_ref):
    m = _rmsnorm(mem_ref[0], g_ref[...])
    kv = _dot(_bf16(m), w_ref[...])
    k_ref[0] = kv[:, 0:D_XATTN]
    v_ref[0] = kv[:, D_XATTN:2 * D_XATTN]


def _memkv(mem, g_mem, w_kv_bf):
    bsz = mem.shape[0]
    return pl.pallas_call(
        _memkv_kernel,
        grid=(bsz,),
        in_specs=[
            pl.BlockSpec((1, N_MEM, D_MODEL), lambda b: (b, 0, 0)),
            pl.BlockSpec((1, D_MODEL), lambda b: (0, 0)),
            pl.BlockSpec((D_MODEL, 2 * D_XATTN), lambda b: (0, 0)),
        ],
        out_specs=[
            pl.BlockSpec((1, N_MEM, D_XATTN), lambda b: (b, 0, 0)),
            pl.BlockSpec((1, N_MEM, D_XATTN), lambda b: (b, 0, 0)),
        ],
        out_shape=[jax.ShapeDtypeStruct((bsz, N_MEM, D_XATTN), jnp.float32)] * 2,
        compiler_params=pltpu.CompilerParams(dimension_semantics=("arbitrary",)),
        name="memkv",
    )(mem, g_mem, w_kv_bf)


def _prompt_kernel(x_ref, mk_ref, mv_ref, g_norm_ref, w_in_ref, w_lc_ref, b_lc_ref, wg_ref,
                   b_ga_ref, b_gx_ref, lam_ref, w_dw_ref, b_dw_ref, ln_g_ref, ln_b_ref,
                   w_pw_ref, b_pw_ref, w_out_ref, g_final_ref,
                   y_ref, h_out_ref, lb_out_ref, cb_out_ref,
                   z_s, xc_s, xl_s, pre_s, a_s, u_s, uc_s, ln_s, mix_s, hin_s):
    t = pl.program_id(1)
    n_t = pl.num_programs(1)
    tile = PROMPT_TILE
    seg = PROMPT_SEG
    n_seg = tile // seg
    pitch = PROMPT_PITCH

    @pl.when(t == 0)
    def _():
        xc_s[:, 0:LRU_HALO, :] = jnp.zeros((N_LRU_SLABS, LRU_HALO, LANES), jnp.float32)
        uc_s[:, 0:CONV_HALO, :] = jnp.zeros((N_CONV_SLABS, CONV_HALO, LANES), jnp.float32)
        hin_s[:, 0:1, :] = jnp.zeros((N_LRU_SLABS, 1, LANES), jnp.float32)

    hn = _rmsnorm(x_ref[0], g_norm_ref[...])
    z_s[...] = _dot(_bf16(hn), w_in_ref[...])

    for l in range(N_LRU_SLABS):
        lo = l * LANES
        xc_s[l, LRU_HALO:LRU_HALO + tile, :] = z_s[:, OFF_LRU_IN + lo:OFF_LRU_IN + lo + LANES]
    for l in range(N_LRU_SLABS):
        lo = l * LANES
        xl = _conv_taps(xc_s, l, w_lc_ref, 0, tile, LRU_CONV, LRU_HALO) + b_lc_ref[:, lo:lo + LANES]
        xl_s[:, lo:lo + LANES] = xl
    _lru_gates(_bf16(xl_s[...]), wg_ref, pre_s)

    ls_all = _log_sigmoid(lam_ref[...])

    def coeff_body(c, carry):
        r0 = pl.multiple_of(c * seg, seg)
        p0 = pl.multiple_of(c * pitch, SUBLANES)
        for l in range(N_LRU_SLABS):
            lo = l * LANES
            ca, cx = _pre_cols(l)
            a, u = _lru_coeffs(pre_s[pl.ds(r0, seg), ca:ca + LANES],
                               pre_s[pl.ds(r0, seg), cx:cx + LANES],
                               xl_s[pl.ds(r0, seg), lo:lo + LANES],
                               b_ga_ref[:, lo:lo + LANES], b_gx_ref[:, lo:lo + LANES],
                               ls_all[:, lo:lo + LANES])
            a_s[l, pl.ds(p0, seg), :] = a
            u_s[l, pl.ds(p0, seg), :] = u
        return carry

    lax.fori_loop(0, n_seg, coeff_body, 0)

    def tot_body(j, carry):
        new = []
        for l in range(N_LRU_SLABS):
            a_tot, u_tot = carry[l]
            a_j = a_s[l, pl.ds(j, n_seg, stride=pitch), :]
            u_j = u_s[l, pl.ds(j, n_seg, stride=pitch), :]
            new.append((a_j * a_tot, a_j * u_tot + u_j))
        return tuple(new)

    init = tuple((jnp.ones((n_seg, LANES), jnp.float32), jnp.zeros((n_seg, LANES), jnp.float32))
                 for _ in range(N_LRU_SLABS))
    totals = lax.fori_loop(0, seg, tot_body, init, unroll=4)

    h_in = []
    for l in range(N_LRU_SLABS):
        a_tot, u_tot = totals[l]
        h = hin_s[l, 0:1, :]
        for c in range(n_seg):
            h = a_tot[c:c + 1, :] * h + u_tot[c:c + 1, :]
            hin_s[l, c + 1:c + 2, :] = h
        h_in.append(hin_s[l, 0:n_seg, :])
        hin_s[l, 0:1, :] = h

    def scan_body(j, carry):
        new = []
        for l in range(N_LRU_SLABS):
            a_j = a_s[l, pl.ds(j, n_seg, stride=pitch), :]
            u_j = u_s[l, pl.ds(j, n_seg, stride=pitch), :]
            h = a_j * carry[l] + u_j
            u_s[l, pl.ds(j, n_seg, stride=pitch), :] = h
            new.append(h)
        return tuple(new)

    lax.fori_loop(0, seg, scan_body, tuple(h_in), unroll=4)

    for l in range(N_CONV_SLABS):
        lo = l * LANES
        uc_s[l, CONV_HALO:CONV_HALO + tile, :] = (
            z_s[:, OFF_GLU_A + lo:OFF_GLU_A + lo + LANES]
            * _sigmoid(z_s[:, OFF_GLU_B + lo:OFF_GLU_B + lo + LANES]))

    masks = _head_masks()
    k_bf = _bf16(mk_ref[0])
    v_bf = _bf16(mv_ref[0])

    def seg_body(c, carry):
        r0 = pl.multiple_of(c * seg, seg)
        p0 = pl.multiple_of(c * pitch, SUBLANES)
        for l in range(N_LRU_SLABS):
            lo = l * LANES
            gate = z_s[pl.ds(r0, seg), OFF_LRU_GATE + lo:OFF_LRU_GATE + lo + LANES]
            mix_s[pl.ds(r0, seg), lo:lo + LANES] = _bf16(u_s[l, pl.ds(p0, seg), :] * _silu(gate))
        ys = []
        for l in range(N_CONV_SLABS):
            lo = l * LANES
            ys.append(_conv_taps(uc_s, l, w_dw_ref, r0, seg, CONV_W, CONV_HALO)
                      + b_dw_ref[:, lo:lo + LANES])
        outs = _layernorm_silu_slabs(ys, ln_g_ref, ln_b_ref)
        for l in range(N_CONV_SLABS):
            lo = l * LANES
            ln_s[pl.ds(r0, seg), lo:lo + LANES] = _bf16(outs[l])
        return carry

    lax.fori_loop(0, n_seg, seg_body, 0)

    yc = (_dot(ln_s[...], w_pw_ref[...]) + b_pw_ref[...]) * _silu(
        z_s[:, OFF_CONV_GATE:OFF_CONV_GATE + D_CONV])
    mix_s[:, D_LRU:D_LRU + D_CONV] = _bf16(yc)

    def attn_body(c, carry):
        r0 = pl.multiple_of(c * ATTN_ROWS, ATTN_ROWS)
        q = z_s[pl.ds(r0, ATTN_ROWS), OFF_Q:OFF_Q + D_XATTN]
        o = _attention(q, k_bf, v_bf, masks)
        gate = z_s[pl.ds(r0, ATTN_ROWS), OFF_ATTN_GATE:OFF_ATTN_GATE + D_XATTN]
        mix_s[pl.ds(r0, ATTN_ROWS), D_LRU + D_CONV:D_MIX] = _bf16(o * _silu(gate))
        return carry

    lax.fori_loop(0, tile // ATTN_ROWS, attn_body, 0)

    y = x_ref[0] + _dot(mix_s[...], w_out_ref[...])
    y_ref[0] = _rmsnorm(y, g_final_ref[...])

    for l in range(N_LRU_SLABS):
        xc_s[l, LRU_HALO - (LRU_CONV - 1):LRU_HALO, :] = (
            xc_s[l, LRU_HALO + tile - (LRU_CONV - 1):LRU_HALO + tile, :])
    for l in range(N_CONV_SLABS):
        uc_s[l, CONV_HALO - (CONV_W - 1):CONV_HALO, :] = (
            uc_s[l, CONV_HALO + tile - (CONV_W - 1):CONV_HALO + tile, :])

    @pl.when(t == n_t - 1)
    def _():
        for l in range(N_LRU_SLABS):
            lo = l * LANES
            h_out_ref[0, :, lo:lo + LANES] = hin_s[l, 0:1, :]
            lb_out_ref[0, :, lo:lo + LANES] = xc_s[l, LRU_HALO - (LRU_CONV - 1):LRU_HALO, :]
        for l in range(N_CONV_SLABS):
            lo = l * LANES
            cb_out_ref[0, :, lo:lo + LANES] = uc_s[l, CONV_HALO - (CONV_W - 1):CONV_HALO, :]


def _const_spec(shape):
    nd = len(shape)
    return pl.BlockSpec(shape, lambda *_: (0,) * nd)


def _prompt(x, mk, mv, p):
    bsz, seq, _ = x.shape
    tile = PROMPT_TILE
    n_t = seq // tile
    n_seg = tile // PROMPT_SEG
    weights = [p["g_norm"], p["w_in"], p["w_lc"], p["b_lc"], p["wg"], p["b_ga"], p["b_gx"],
               p["lam"], p["w_dw"], p["b_dw"], p["ln_g"], p["ln_b"], p["w_pw"], p["b_pw"],
               p["w_out"], p["g_final"]]
    in_specs = [
        pl.BlockSpec((1, tile, D_MODEL), lambda b, t: (b, t, 0)),
        pl.BlockSpec((1, N_MEM, D_XATTN), lambda b, t: (b, 0, 0)),
        pl.BlockSpec((1, N_MEM, D_XATTN), lambda b, t: (b, 0, 0)),
    ] + [_const_spec(w.shape) for w in weights]
    out_specs = [
        pl.BlockSpec((1, tile, D_MODEL), lambda b, t: (b, t, 0)),
        pl.BlockSpec((1, 1, D_LRU), lambda b, t: (b, 0, 0)),
        pl.BlockSpec((1, LRU_CONV - 1, D_LRU), lambda b, t: (b, 0, 0)),
        pl.BlockSpec((1, CONV_W - 1, D_CONV), lambda b, t: (b, 0, 0)),
    ]
    out_shape = [
        jax.ShapeDtypeStruct((bsz, seq, D_MODEL), jnp.float32),
        jax.ShapeDtypeStruct((bsz, 1, D_LRU), jnp.float32),
        jax.ShapeDtypeStruct((bsz, LRU_CONV - 1, D_LRU), jnp.float32),
        jax.ShapeDtypeStruct((bsz, CONV_W - 1, D_CONV), jnp.float32),
    ]
    scratch = [
        pltpu.VMEM((tile, D_IN), jnp.float32),
        pltpu.VMEM((N_LRU_SLABS, LRU_HALO + tile, LANES), jnp.float32),
        pltpu.VMEM((tile, D_LRU), jnp.float32),
        pltpu.VMEM((tile, 2 * D_LRU), jnp.float32),
        pltpu.VMEM((N_LRU_SLABS, n_seg * PROMPT_PITCH, LANES), jnp.float32),
        pltpu.VMEM((N_LRU_SLABS, n_seg * PROMPT_PITCH, LANES), jnp.float32),
        pltpu.VMEM((N_CONV_SLABS, CONV_HALO + tile, LANES), jnp.float32),
        pltpu.VMEM((tile, D_CONV), jnp.bfloat16),
        pltpu.VMEM((tile, D_MIX), jnp.bfloat16),
        pltpu.VMEM((N_LRU_SLABS, 2 * SUBLANES, LANES), jnp.float32),
    ]
    return pl.pallas_call(
        _prompt_kernel,
        grid=(bsz, n_t),
        in_specs=in_specs,
        out_specs=out_specs,
        out_shape=out_shape,
        scratch_shapes=scratch,
        compiler_params=pltpu.CompilerParams(
            dimension_semantics=("arbitrary", "arbitrary"),
            vmem_limit_bytes=VMEM_LIMIT_BYTES),
        name="prompt_layer",
    )(x, mk, mv, *weights)


def _sample_kernel(nb, steps,
                   x_ref, h0_ref, lb_ref, cb_ref, k_ref, v_ref,
                   g_norm_ref, w_in_ref, w_lc_ref, b_lc_ref, wg_ref,
                   b_ga_ref, b_gx_ref, lam_ref, w_dw_ref, b_dw_ref, ln_g_ref, ln_b_ref,
                   w_pw_ref, b_pw_ref, w_out_ref, g_final_ref,
                   y_ref, h_out_ref, lb_out_ref, cb_out_ref,
                   z_s, xc_s, xl_s, pre_s, a_s, u_s, uc_s, ln_s, mix_s):
    rows = nb * steps
    pitch = SAMPLE_PITCH
    cpitch = CONV_HALO + steps
    prow = nb * pitch

    x = x_ref[...].reshape(rows, D_MODEL)
    hn = _rmsnorm(x, g_norm_ref[...])
    z_s[...] = _dot(_bf16(hn), w_in_ref[...])

    for b in range(nb):
        for l in range(N_LRU_SLABS):
            lo = l * LANES
            xc_s[l, b * pitch:b * pitch + LRU_HALO, :] = jnp.zeros((LRU_HALO, LANES), jnp.float32)
            xc_s[l, b * pitch + LRU_HALO - (LRU_CONV - 1):b * pitch + LRU_HALO, :] = (
                lb_ref[b, :, lo:lo + LANES])
            xc_s[l, b * pitch + LRU_HALO:(b + 1) * pitch, :] = (
                z_s[b * steps:(b + 1) * steps, OFF_LRU_IN + lo:OFF_LRU_IN + lo + LANES])
            lb_out_ref[b, :, lo:lo + LANES] = (
                z_s[(b + 1) * steps - (LRU_CONV - 1):(b + 1) * steps,
                    OFF_LRU_IN + lo:OFF_LRU_IN + lo + LANES])
    n_conv = prow - LRU_HALO
    for l in range(N_LRU_SLABS):
        lo = l * LANES
        xl = _conv_taps(xc_s, l, w_lc_ref, 0, n_conv, LRU_CONV, LRU_HALO) + b_lc_ref[:, lo:lo + LANES]
        xl_s[0:n_conv, lo:lo + LANES] = xl
    xl_s[n_conv:prow, :] = jnp.zeros((LRU_HALO, D_LRU), jnp.float32)
    _lru_gates(_bf16(xl_s[...]), wg_ref, pre_s)

    ls_all = _log_sigmoid(lam_ref[...])
    for l in range(N_LRU_SLABS):
        lo = l * LANES
        ca, cx = _pre_cols(l)
        a, u = _lru_coeffs(pre_s[:, ca:ca + LANES], pre_s[:, cx:cx + LANES], xl_s[:, lo:lo + LANES],
                           b_ga_ref[:, lo:lo + LANES], b_gx_ref[:, lo:lo + LANES],
                           ls_all[:, lo:lo + LANES])
        a_s[l, :, :] = a
        u_s[l, :, :] = u

    n_grp = nb // SUBLANES

    def scan_body(j, carry):
        new = []
        for g in range(n_grp):
            for l in range(N_LRU_SLABS):
                base = g * SUBLANES * pitch
                a_j = a_s[l, pl.ds(base + j, SUBLANES, stride=pitch), :]
                u_j = u_s[l, pl.ds(base + j, SUBLANES, stride=pitch), :]
                h = a_j * carry[g * N_LRU_SLABS + l] + u_j
                u_s[l, pl.ds(base + j, SUBLANES, stride=pitch), :] = h
                new.append(h)
        return tuple(new)

    h0 = tuple(h0_ref[g * SUBLANES:(g + 1) * SUBLANES, l * LANES:(l + 1) * LANES]
               for g in range(n_grp) for l in range(N_LRU_SLABS))
    h_fin = lax.fori_loop(0, steps, scan_body, h0, unroll=4)
    for g in range(n_grp):
        for l in range(N_LRU_SLABS):
            h_out_ref[g * SUBLANES:(g + 1) * SUBLANES, l * LANES:(l + 1) * LANES] = (
                h_fin[g * N_LRU_SLABS + l])

    for b in range(nb):
        for l in range(N_LRU_SLABS):
            lo = l * LANES
            gate = z_s[b * steps:(b + 1) * steps, OFF_LRU_GATE + lo:OFF_LRU_GATE + lo + LANES]
            mix_s[b * steps:(b + 1) * steps, lo:lo + LANES] = _bf16(
                u_s[l, b * pitch:b * pitch + steps, :] * _silu(gate))

    for b in range(nb):
        for l in range(N_CONV_SLABS):
            lo = l * LANES
            uc = (z_s[b * steps:(b + 1) * steps, OFF_GLU_A + lo:OFF_GLU_A + lo + LANES]
                  * _sigmoid(z_s[b * steps:(b + 1) * steps, OFF_GLU_B + lo:OFF_GLU_B + lo + LANES]))
            uc_s[l, b * cpitch + CONV_HALO - (CONV_W - 1):b * cpitch + CONV_HALO, :] = (
                cb_ref[b, :, lo:lo + LANES])
            uc_s[l, b * cpitch + CONV_HALO:(b + 1) * cpitch, :] = uc
            cb_out_ref[b, :, lo:lo + LANES] = (
                uc_s[l, (b + 1) * cpitch - (CONV_W - 1):(b + 1) * cpitch, :])

    def conv_body(b, carry):
        r0 = pl.multiple_of(b * cpitch, SUBLANES)
        o0 = pl.multiple_of(b * steps, steps)
        ys = []
        for l in range(N_CONV_SLABS):
            lo = l * LANES
            ys.append(_conv_taps(uc_s, l, w_dw_ref, r0, steps, CONV_W, CONV_HALO)
                      + b_dw_ref[:, lo:lo + LANES])
        outs = _layernorm_silu_slabs(ys, ln_g_ref, ln_b_ref)
        for l in range(N_CONV_SLABS):
            lo = l * LANES
            ln_s[pl.ds(o0, steps), lo:lo + LANES] = _bf16(outs[l])
        return carry

    lax.fori_loop(0, nb, conv_body, 0)

    yc = (_dot(ln_s[...], w_pw_ref[...]) + b_pw_ref[...]) * _silu(
        z_s[:, OFF_CONV_GATE:OFF_CONV_GATE + D_CONV])
    mix_s[:, D_LRU:D_LRU + D_CONV] = _bf16(yc)

    masks = _head_masks()

    def attn_body(b, carry):
        o0 = pl.multiple_of(b * steps, steps)
        q = z_s[pl.ds(o0, steps), OFF_Q:OFF_Q + D_XATTN]
        o = _attention(q, _bf16(k_ref[b]), _bf16(v_ref[b]), masks)
        gate = z_s[pl.ds(o0, steps), OFF_ATTN_GATE:OFF_ATTN_GATE + D_XATTN]
        mix_s[pl.ds(o0, steps), D_LRU + D_CONV:D_MIX] = _bf16(o * _silu(gate))
        return carry

    lax.fori_loop(0, nb, attn_body, 0)

    y = x + _dot(mix_s[...], w_out_ref[...])
    y_ref[...] = _rmsnorm(y, g_final_ref[...]).reshape(nb, steps, D_MODEL)


def _sample(x, h0, lb, cb, mk, mv, p, nb):
    bsz, steps, _ = x.shape
    assert bsz % nb == 0 and nb % SUBLANES == 0 and steps + LRU_HALO == SAMPLE_PITCH
    rows = nb * steps
    prow = nb * SAMPLE_PITCH
    weights = [p["g_norm"], p["w_in"], p["w_lc"], p["b_lc"], p["wg"], p["b_ga"], p["b_gx"],
               p["lam"], p["w_dw"], p["b_dw"], p["ln_g"], p["ln_b"], p["w_pw"], p["b_pw"],
               p["w_out"], p["g_final"]]
    in_specs = [
        pl.BlockSpec((nb, steps, D_MODEL), lambda g: (g, 0, 0)),
        pl.BlockSpec((nb, D_LRU), lambda g: (g, 0)),
        pl.BlockSpec((nb, LRU_CONV - 1, D_LRU), lambda g: (g, 0, 0)),
        pl.BlockSpec((nb, CONV_W - 1, D_CONV), lambda g: (g, 0, 0)),
        pl.BlockSpec((nb, N_MEM, D_XATTN), lambda g: (g, 0, 0)),
        pl.BlockSpec((nb, N_MEM, D_XATTN), lambda g: (g, 0, 0)),
    ] + [_const_spec(w.shape) for w in weights]
    out_specs = [
        pl.BlockSpec((nb, steps, D_MODEL), lambda g: (g, 0, 0)),
        pl.BlockSpec((nb, D_LRU), lambda g: (g, 0)),
        pl.BlockSpec((nb, LRU_CONV - 1, D_LRU), lambda g: (g, 0, 0)),
        pl.BlockSpec((nb, CONV_W - 1, D_CONV), lambda g: (g, 0, 0)),
    ]
    out_shape = [
        jax.ShapeDtypeStruct((bsz, steps, D_MODEL), jnp.float32),
        jax.ShapeDtypeStruct((bsz, D_LRU), jnp.float32),
        jax.ShapeDtypeStruct((bsz, LRU_CONV - 1, D_LRU), jnp.float32),
        jax.ShapeDtypeStruct((bsz, CONV_W - 1, D_CONV), jnp.float32),
    ]
    scratch = [
        pltpu.VMEM((rows, D_IN), jnp.float32),
        pltpu.VMEM((N_LRU_SLABS, prow, LANES), jnp.float32),
        pltpu.VMEM((prow, D_LRU), jnp.float32),
        pltpu.VMEM((prow, 2 * D_LRU), jnp.float32),
        pltpu.VMEM((N_LRU_SLABS, prow, LANES), jnp.float32),
        pltpu.VMEM((N_LRU_SLABS, prow, LANES), jnp.float32),
        pltpu.VMEM((N_CONV_SLABS, nb * (CONV_HALO + steps), LANES), jnp.float32),
        pltpu.VMEM((rows, D_CONV), jnp.bfloat16),
        pltpu.VMEM((rows, D_MIX), jnp.bfloat16),
    ]
    return pl.pallas_call(
        functools.partial(_sample_kernel, nb, steps),
        grid=(bsz // nb,),
        in_specs=in_specs,
        out_specs=out_specs,
        out_shape=out_shape,
        scratch_shapes=scratch,
        compiler_params=pltpu.CompilerParams(
            dimension_semantics=("arbitrary",),
            vmem_limit_bytes=VMEM_LIMIT_BYTES),
        name="sample_layer",
    )(x, h0, lb, cb, mk, mv, *weights)


def _block_diag4(w4):
    eye = jnp.eye(4, dtype=w4.dtype)
    n = 4 * LRU_BLOCK
    return (eye[:, None, :, None] * w4[:, :, None, :]).reshape(n, n)


def _layer_params(l, g_norm, w_in, w_lru_conv, b_lru_conv, w_gate_a, b_gate_a, w_gate_x, b_gate_x,
                  lru_lambda, w_dw, b_dw, ln_g, ln_b, w_pw, b_pw, w_out, g_final):
    row = lambda v: v.reshape(1, -1)
    half = LRU_BLOCKS // 2
    wg = jnp.stack([
        jnp.concatenate([_block_diag4(w_gate_a[l, h * half:(h + 1) * half]),
                         _block_diag4(w_gate_x[l, h * half:(h + 1) * half])], axis=1)
        for h in range(2)])
    return {
        "g_norm": row(g_norm[l]), "w_in": _bf16(w_in[l]), "w_lc": w_lru_conv[l],
        "b_lc": row(b_lru_conv[l]), "wg": _bf16(wg), "b_ga": row(b_gate_a[l]),
        "b_gx": row(b_gate_x[l]), "lam": row(lru_lambda[l]), "w_dw": w_dw[l], "b_dw": row(b_dw[l]),
        "ln_g": row(ln_g[l]), "ln_b": row(ln_b[l]), "w_pw": _bf16(w_pw[l]), "b_pw": row(b_pw[l]),
        "w_out": _bf16(w_out[l]), "g_final": row(g_final),
    }


def kernel(x_prompt, x_sample, state_lru_h, state_lru_conv, state_conv, cache_mem_k, cache_mem_v,
           mem_prompt, g_norm, w_in, w_lru_conv, b_lru_conv, w_gate_a, b_gate_a, w_gate_x, b_gate_x,
           lru_lambda, w_dw, b_dw, ln_g, ln_b, w_pw, b_pw, g_mem, w_mem_k, w_mem_v, w_out, g_final):
    depth = g_norm.shape[0]
    assert depth == 1
    bp = x_prompt.shape[0]
    bs = x_sample.shape[0]
    l = 0
    p = _layer_params(l, g_norm, w_in, w_lru_conv, b_lru_conv, w_gate_a, b_gate_a, w_gate_x,
                      b_gate_x, lru_lambda, w_dw, b_dw, ln_g, ln_b, w_pw, b_pw, w_out, g_final)
    w_kv = _bf16(jnp.concatenate([w_mem_k[l], w_mem_v[l]], axis=1))
    mk, mv = _memkv(mem_prompt, g_mem[l].reshape(1, -1), w_kv)

    y_prompt, ph, plb, pcb = _prompt(x_prompt, mk, mv, p)

    ck = cache_mem_k[l].reshape(bs, N_MEM, D_XATTN)
    cv = cache_mem_v[l].reshape(bs, N_MEM, D_XATTN)
    y_sample, sh, slb, scb = _sample(x_sample, state_lru_h[l], state_lru_conv[l], state_conv[l],
                                     ck, cv, p, nb=16)

    return (y_prompt, y_sample,
            ph.reshape(depth, bp, D_LRU), plb[None], pcb[None],
            mk.reshape(depth, bp, N_MEM, N_XHEADS, XHEAD_DIM),
            mv.reshape(depth, bp, N_MEM, N_XHEADS, XHEAD_DIM),
            sh[None], slb[None], scb[None])
```

```python
import functools

import jax
import jax.numpy as jnp
from jax import lax
from jax.experimental import pallas as pl
from jax.experimental.pallas import tpu as pltpu

D_MODEL = 1024
N_MEM = 256
D_LRU = 512
LRU_BLOCKS = 8
LRU_BLOCK = D_LRU // LRU_BLOCKS
LRU_CONV = 4
LRU_C = 8.0
D_CONV = 256
CONV_W = 31
N_XHEADS = 4
XHEAD_DIM = 64
D_XATTN = N_XHEADS * XHEAD_DIM
D_MIX = D_LRU + D_CONV + D_XATTN
D_IN = 2 * D_LRU + 3 * D_CONV + 2 * D_XATTN
EPS = 1e-6

LANES = 128
SUBLANES = 8
N_LRU_SLABS = D_LRU // LANES
N_CONV_SLABS = D_CONV // LANES

OFF_LRU_IN = 0
OFF_LRU_GATE = D_LRU
OFF_GLU_A = 2 * D_LRU
OFF_GLU_B = OFF_GLU_A + D_CONV
OFF_CONV_GATE = OFF_GLU_B + D_CONV
OFF_Q = OFF_CONV_GATE + D_CONV
OFF_ATTN_GATE = OFF_Q + D_XATTN

LRU_HALO = 8
CONV_HALO = 32

PROMPT_TILE = 512
PROMPT_SEG = 64
PROMPT_PITCH = 72
SAMPLE_PITCH = 40
ATTN_ROWS = 128

VMEM_LIMIT_BYTES = 56 * 1024 * 1024


def _sigmoid(x):
    return jax.nn.sigmoid(x)


def _silu(x):
    return x * jax.nn.sigmoid(x)


def _rmsnorm(x, g):
    ms = jnp.mean(x * x, axis=-1, keepdims=True)
    return x * lax.rsqrt(ms + EPS) * g


def _log_sigmoid(x):
    y = -x
    return -(jnp.maximum(y, 0.0) + jnp.log1p(jnp.exp(-jnp.abs(y))))


def _bf16(x):
    return x.astype(jnp.bfloat16)


def _dot(a, b):
    return jnp.dot(a, b, preferred_element_type=jnp.float32)


def _unrolled(n, body, init):
    carry = init
    for i in range(n):
        carry = body(i, carry)
    return carry


def _aligned(x, m):
    return x if isinstance(x, int) else pl.multiple_of(x, m)


def _head_masks():
    lane = lax.broadcasted_iota(jnp.int32, (1, D_XATTN), 1)
    head = lax.shift_right_logical(lane, XHEAD_DIM.bit_length() - 1)
    return [(head == h).astype(jnp.float32) for h in range(N_XHEADS)]


def _attention(q, k_bf, v_bf, masks):
    rows = q.shape[0]
    qm = jnp.concatenate([_bf16(q * m) for m in masks], axis=0)
    s = lax.dot_general(qm, k_bf, (((1,), (1,)), ((), ())),
                        preferred_element_type=jnp.float32)
    s = s * (XHEAD_DIM ** -0.5)
    s = s - jnp.max(s, axis=-1, keepdims=True)
    e = jnp.exp(s)
    p = e / jnp.sum(e, axis=-1, keepdims=True)
    o_all = _dot(_bf16(p), v_bf)
    o = o_all[0:rows] * masks[0]
    for h in range(1, N_XHEADS):
        o = o + o_all[h * rows:(h + 1) * rows] * masks[h]
    return o


def _lru_gates(xl_bf, wg_ref, pre_s):
    half = D_LRU // 2
    pre_s[:, 0:D_LRU] = _dot(xl_bf[:, 0:half], wg_ref[0])
    pre_s[:, D_LRU:2 * D_LRU] = _dot(xl_bf[:, half:D_LRU], wg_ref[1])


def _pre_cols(l):
    half_slabs = N_LRU_SLABS // 2
    base = (l // half_slabs) * D_LRU + (l % half_slabs) * LANES
    return base, base + D_LRU // 2


def _lru_coeffs(pre_a, pre_x, xl, b_a, b_x, ls):
    r = _sigmoid(pre_a + b_a)
    i = _sigmoid(pre_x + b_x)
    log_a = (LRU_C * r) * ls
    a = jnp.exp(log_a)
    one_minus_a2 = -jnp.tanh(log_a) * (a * a + 1.0)
    u = jnp.sqrt(one_minus_a2) * (i * xl)
    return a, u


def _conv_taps(src_s, l, w_ref, row0, nrows, width, halo):
    lo = l * LANES
    acc = None
    for k in range(width):
        w_k = w_ref[k:k + 1, lo:lo + LANES]
        x_k = src_s[l, pl.ds(row0 + (halo - (width - 1) + k), nrows), :]
        acc = w_k * x_k if acc is None else acc + w_k * x_k
    return acc


def _layernorm_silu_slabs(ys, g_ref, b_ref):
    tot = ys[0]
    for y in ys[1:]:
        tot = tot + y
    mu = jnp.sum(tot, axis=-1, keepdims=True) * (1.0 / D_CONV)
    cs = [y - mu for y in ys]
    sq = cs[0] * cs[0]
    for c in cs[1:]:
        sq = sq + c * c
    var = jnp.sum(sq, axis=-1, keepdims=True) * (1.0 / D_CONV)
    inv = lax.rsqrt(var + EPS)
    outs = []
    for l, c in enumerate(cs):
        lo = l * LANES
        outs.append(_silu(c * inv * g_ref[:, lo:lo + LANES] + b_ref[:, lo:lo + LANES]))
    return outs


def _memkv_kernel(mem_ref, g_ref, w_ref, k_ref, v_ref):
    m = _rmsnorm(mem_ref[0], g_ref[...])
    kv = _dot(_bf16(m), w_ref[...])
    k_ref[0] = kv[:, 0:D_XATTN]
    v_ref[0] = kv[:, D_XATTN:2 * D_XATTN]


def _memkv(mem, g_mem, w_kv_bf):
    bsz = mem.shape[0]
    return pl.pallas_call(
        _memkv_kernel,
        grid=(bsz,),
        in_specs=[
            pl.BlockSpec((1, N_MEM, D_MODEL), lambda b: (b, 0, 0)),
            pl.BlockSpec((1, D_MODEL), lambda b: (0, 0)),
            pl.BlockSpec((D_MODEL, 2 * D_XATTN), lambda b: (0, 0)),
        ],
        out_specs=[
            pl.BlockSpec((1, N_MEM, D_XATTN), lambda b: (b, 0, 0)),
            pl.BlockSpec((1, N_MEM, D_XATTN), lambda b: (b, 0, 0)),
        ],
        out_shape=[jax.ShapeDtypeStruct((bsz, N_MEM, D_XATTN), jnp.float32)] * 2,
        compiler_params=pltpu.CompilerParams(dimension_semantics=("arbitrary",)),
        name="memkv",
    )(mem, g_mem, w_kv_bf)


def _prompt_kernel(x_ref, mk_ref, mv_ref, g_norm_ref, w_in_ref, w_lc_ref, b_lc_ref, wg_ref,
                   b_ga_ref, b_gx_ref, lam_ref, w_dw_ref, b_dw_ref, ln_g_ref, ln_b_ref,
                   w_pw_ref, b_pw_ref, w_out_ref, g_final_ref,
                   y_ref, h_out_ref, lb_out_ref, cb_out_ref,
                   z_s, xc_s, xl_s, pre_s, a_s, u_s, uc_s, ln_s, mix_s, hin_s):
    t = pl.program_id(1)
    n_t = pl.num_programs(1)
    tile = PROMPT_TILE
    seg = PROMPT_SEG
    n_seg = tile // seg
    pitch = PROMPT_PITCH

    @pl.when(t == 0)
    def _():
        xc_s[:, 0:LRU_HALO, :] = jnp.zeros((N_LRU_SLABS, LRU_HALO, LANES), jnp.float32)
        uc_s[:, 0:CONV_HALO, :] = jnp.zeros((N_CONV_SLABS, CONV_HALO, LANES), jnp.float32)
        hin_s[:, 0:1, :] = jnp.zeros((N_LRU_SLABS, 1, LANES), jnp.float32)

    hn = _rmsnorm(x_ref[0], g_norm_ref[...])
    z_s[...] = _dot(_bf16(hn), w_in_ref[...])

    for l in range(N_LRU_SLABS):
        lo = l * LANES
        xc_s[l, LRU_HALO:LRU_HALO + tile, :] = z_s[:, OFF_LRU_IN + lo:OFF_LRU_IN + lo + LANES]
    for l in range(N_LRU_SLABS):
        lo = l * LANES
        xl = _conv_taps(xc_s, l, w_lc_ref, 0, tile, LRU_CONV, LRU_HALO) + b_lc_ref[:, lo:lo + LANES]
        xl_s[:, lo:lo + LANES] = xl
    _lru_gates(_bf16(xl_s[...]), wg_ref, pre_s)

    ls_all = _log_sigmoid(lam_ref[...])

    def coeff_body(c, carry):
        r0 = _aligned(c * seg, seg)
        p0 = _aligned(c * pitch, SUBLANES)
        for l in range(N_LRU_SLABS):
            lo = l * LANES
            ca, cx = _pre_cols(l)
            a, u = _lru_coeffs(pre_s[pl.ds(r0, seg), ca:ca + LANES],
                               pre_s[pl.ds(r0, seg), cx:cx + LANES],
                               xl_s[pl.ds(r0, seg), lo:lo + LANES],
                               b_ga_ref[:, lo:lo + LANES], b_gx_ref[:, lo:lo + LANES],
                               ls_all[:, lo:lo + LANES])
            a_s[l, pl.ds(p0, seg), :] = a
            u_s[l, pl.ds(p0, seg), :] = u
        return carry

    _unrolled(n_seg, coeff_body, 0)

    def tot_body(j, carry):
        new = []
        for l in range(N_LRU_SLABS):
            a_tot, u_tot = carry[l]
            a_j = a_s[l, pl.ds(j, n_seg, stride=pitch), :]
            u_j = u_s[l, pl.ds(j, n_seg, stride=pitch), :]
            new.append((a_j * a_tot, a_j * u_tot + u_j))
        return tuple(new)

    init = tuple((jnp.ones((n_seg, LANES), jnp.float32), jnp.zeros((n_seg, LANES), jnp.float32))
                 for _ in range(N_LRU_SLABS))
    totals = _unrolled(seg, tot_body, init)

    h_in = []
    for l in range(N_LRU_SLABS):
        a_tot, u_tot = totals[l]
        h = hin_s[l, 0:1, :]
        for c in range(n_seg):
            h = a_tot[c:c + 1, :] * h + u_tot[c:c + 1, :]
            hin_s[l, c + 1:c + 2, :] = h
        h_in.append(hin_s[l, 0:n_seg, :])
        hin_s[l, 0:1, :] = h

    def scan_body(j, carry):
        new = []
        for l in range(N_LRU_SLABS):
            a_j = a_s[l, pl.ds(j, n_seg, stride=pitch), :]
            u_j = u_s[l, pl.ds(j, n_seg, stride=pitch), :]
            h = a_j * carry[l] + u_j
            u_s[l, pl.ds(j, n_seg, stride=pitch), :] = h
            new.append(h)
        return tuple(new)

    _unrolled(seg, scan_body, tuple(h_in))

    for l in range(N_CONV_SLABS):
        lo = l * LANES
        uc_s[l, CONV_HALO:CONV_HALO + tile, :] = (
            z_s[:, OFF_GLU_A + lo:OFF_GLU_A + lo + LANES]
            * _sigmoid(z_s[:, OFF_GLU_B + lo:OFF_GLU_B + lo + LANES]))

    masks = _head_masks()
    k_bf = _bf16(mk_ref[0])
    v_bf = _bf16(mv_ref[0])

    def seg_body(c, carry):
        r0 = _aligned(c * seg, seg)
        p0 = _aligned(c * pitch, SUBLANES)
        for l in range(N_LRU_SLABS):
            lo = l * LANES
            gate = z_s[pl.ds(r0, seg), OFF_LRU_GATE + lo:OFF_LRU_GATE + lo + LANES]
            mix_s[pl.ds(r0, seg), lo:lo + LANES] = _bf16(u_s[l, pl.ds(p0, seg), :] * _silu(gate))
        ys = []
        for l in range(N_CONV_SLABS):
            lo = l * LANES
            ys.append(_conv_taps(uc_s, l, w_dw_ref, r0, seg, CONV_W, CONV_HALO)
                      + b_dw_ref[:, lo:lo + LANES])
        outs = _layernorm_silu_slabs(ys, ln_g_ref, ln_b_ref)
        for l in range(N_CONV_SLABS):
            lo = l * LANES
            ln_s[pl.ds(r0, seg), lo:lo + LANES] = _bf16(outs[l])
        return carry

    _unrolled(n_seg, seg_body, 0)

    yc = (_dot(ln_s[...], w_pw_ref[...]) + b_pw_ref[...]) * _silu(
        z_s[:, OFF_CONV_GATE:OFF_CONV_GATE + D_CONV])
    mix_s[:, D_LRU:D_LRU + D_CONV] = _bf16(yc)

    def attn_body(c, carry):
        r0 = _aligned(c * ATTN_ROWS, ATTN_ROWS)
        q = z_s[pl.ds(r0, ATTN_ROWS), OFF_Q:OFF_Q + D_XATTN]
        o = _attention(q, k_bf, v_bf, masks)
        gate = z_s[pl.ds(r0, ATTN_ROWS), OFF_ATTN_GATE:OFF_ATTN_GATE + D_XATTN]
        mix_s[pl.ds(r0, ATTN_ROWS), D_LRU + D_CONV:D_MIX] = _bf16(o * _silu(gate))
        return carry

    _unrolled(tile // ATTN_ROWS, attn_body, 0)

    y = x_ref[0] + _dot(mix_s[...], w_out_ref[...])
    y_ref[0] = _rmsnorm(y, g_final_ref[...])

    for l in range(N_LRU_SLABS):
        xc_s[l, LRU_HALO - (LRU_CONV - 1):LRU_HALO, :] = (
            xc_s[l, LRU_HALO + tile - (LRU_CONV - 1):LRU_HALO + tile, :])
    for l in range(N_CONV_SLABS):
        uc_s[l, CONV_HALO - (CONV_W - 1):CONV_HALO, :] = (
            uc_s[l, CONV_HALO + tile - (CONV_W - 1):CONV_HALO + tile, :])

    @pl.when(t == n_t - 1)
    def _():
        for l in range(N_LRU_SLABS):
            lo = l * LANES
            h_out_ref[0, :, lo:lo + LANES] = hin_s[l, 0:1, :]
            lb_out_ref[0, :, lo:lo + LANES] = xc_s[l, LRU_HALO - (LRU_CONV - 1):LRU_HALO, :]
        for l in range(N_CONV_SLABS):
            lo = l * LANES
            cb_out_ref[0, :, lo:lo + LANES] = uc_s[l, CONV_HALO - (CONV_W - 1):CONV_HALO, :]


def _const_spec(shape):
    nd = len(shape)
    return pl.BlockSpec(shape, lambda *_: (0,) * nd)


def _prompt(x, mk, mv, p):
    bsz, seq, _ = x.shape
    tile = PROMPT_TILE
    n_t = seq // tile
    n_seg = tile // PROMPT_SEG
    weights = [p["g_norm"], p["w_in"], p["w_lc"], p["b_lc"], p["wg"], p["b_ga"], p["b_gx"],
               p["lam"], p["w_dw"], p["b_dw"], p["ln_g"], p["ln_b"], p["w_pw"], p["b_pw"],
               p["w_out"], p["g_final"]]
    in_specs = [
        pl.BlockSpec((1, tile, D_MODEL), lambda b, t: (b, t, 0)),
        pl.BlockSpec((1, N_MEM, D_XATTN), lambda b, t: (b, 0, 0)),
        pl.BlockSpec((1, N_MEM, D_XATTN), lambda b, t: (b, 0, 0)),
    ] + [_const_spec(w.shape) for w in weights]
    out_specs = [
        pl.BlockSpec((1, tile, D_MODEL), lambda b, t: (b, t, 0)),
        pl.BlockSpec((1, 1, D_LRU), lambda b, t: (b, 0, 0)),
        pl.BlockSpec((1, LRU_CONV - 1, D_LRU), lambda b, t: (b, 0, 0)),
        pl.BlockSpec((1, CONV_W - 1, D_CONV), lambda b, t: (b, 0, 0)),
    ]
    out_shape = [
        jax.ShapeDtypeStruct((bsz, seq, D_MODEL), jnp.float32),
        jax.ShapeDtypeStruct((bsz, 1, D_LRU), jnp.float32),
        jax.ShapeDtypeStruct((bsz, LRU_CONV - 1, D_LRU), jnp.float32),
        jax.ShapeDtypeStruct((bsz, CONV_W - 1, D_CONV), jnp.float32),
    ]
    scratch = [
        pltpu.VMEM((tile, D_IN), jnp.float32),
        pltpu.VMEM((N_LRU_SLABS, LRU_HALO + tile, LANES), jnp.float32),
        pltpu.VMEM((tile, D_LRU), jnp.float32),
        pltpu.VMEM((tile, 2 * D_LRU), jnp.float32),
        pltpu.VMEM((N_LRU_SLABS, n_seg * PROMPT_PITCH, LANES), jnp.float32),
        pltpu.VMEM((N_LRU_SLABS, n_seg * PROMPT_PITCH, LANES), jnp.float32),
        pltpu.VMEM((N_CONV_SLABS, CONV_HALO + tile, LANES), jnp.float32),
        pltpu.VMEM((tile, D_CONV), jnp.bfloat16),
        pltpu.VMEM((tile, D_MIX), jnp.bfloat16),
        pltpu.VMEM((N_LRU_SLABS, 2 * SUBLANES, LANES), jnp.float32),
    ]
    return pl.pallas_call(
        _prompt_kernel,
        grid=(bsz, n_t),
        in_specs=in_specs,
        out_specs=out_specs,
        out_shape=out_shape,
        scratch_shapes=scratch,
        compiler_params=pltpu.CompilerParams(
            dimension_semantics=("arbitrary", "arbitrary"),
            vmem_limit_bytes=VMEM_LIMIT_BYTES),
        name="prompt_layer",
    )(x, mk, mv, *weights)


def _sample_kernel(nb, steps,
                   x_ref, h0_ref, lb_ref, cb_ref, k_ref, v_ref,
                   g_norm_ref, w_in_ref, w_lc_ref, b_lc_ref, wg_ref,
                   b_ga_ref, b_gx_ref, lam_ref, w_dw_ref, b_dw_ref, ln_g_ref, ln_b_ref,
                   w_pw_ref, b_pw_ref, w_out_ref, g_final_ref,
                   y_ref, h_out_ref, lb_out_ref, cb_out_ref,
                   z_s, xc_s, xl_s, pre_s, a_s, u_s, uc_s, ln_s, mix_s):
    rows = nb * steps
    pitch = SAMPLE_PITCH
    cpitch = CONV_HALO + steps
    prow = nb * pitch

    x = x_ref[...].reshape(rows, D_MODEL)
    hn = _rmsnorm(x, g_norm_ref[...])
    z_s[...] = _dot(_bf16(hn), w_in_ref[...])

    for b in range(nb):
        for l in range(N_LRU_SLABS):
            lo = l * LANES
            xc_s[l, b * pitch:b * pitch + LRU_HALO, :] = jnp.zeros((LRU_HALO, LANES), jnp.float32)
            xc_s[l, b * pitch + LRU_HALO - (LRU_CONV - 1):b * pitch + LRU_HALO, :] = (
                lb_ref[b, :, lo:lo + LANES])
            xc_s[l, b * pitch + LRU_HALO:(b + 1) * pitch, :] = (
                z_s[b * steps:(b + 1) * steps, OFF_LRU_IN + lo:OFF_LRU_IN + lo + LANES])
            lb_out_ref[b, :, lo:lo + LANES] = (
                z_s[(b + 1) * steps - (LRU_CONV - 1):(b + 1) * steps,
                    OFF_LRU_IN + lo:OFF_LRU_IN + lo + LANES])
    n_conv = prow - LRU_HALO
    for l in range(N_LRU_SLABS):
        lo = l * LANES
        xl = _conv_taps(xc_s, l, w_lc_ref, 0, n_conv, LRU_CONV, LRU_HALO) + b_lc_ref[:, lo:lo + LANES]
        xl_s[0:n_conv, lo:lo + LANES] = xl
    xl_s[n_conv:prow, :] = jnp.zeros((LRU_HALO, D_LRU), jnp.float32)
    _lru_gates(_bf16(xl_s[...]), wg_ref, pre_s)

    ls_all = _log_sigmoid(lam_ref[...])
    for l in range(N_LRU_SLABS):
        lo = l * LANES
        ca, cx = _pre_cols(l)
        a, u = _lru_coeffs(pre_s[:, ca:ca + LANES], pre_s[:, cx:cx + LANES], xl_s[:, lo:lo + LANES],
                           b_ga_ref[:, lo:lo + LANES], b_gx_ref[:, lo:lo + LANES],
                           ls_all[:, lo:lo + LANES])
        a_s[l, :, :] = a
        u_s[l, :, :] = u

    n_grp = nb // SUBLANES

    def scan_body(j, carry):
        new = []
        for g in range(n_grp):
            for l in range(N_LRU_SLABS):
                base = g * SUBLANES * pitch
                a_j = a_s[l, pl.ds(base + j, SUBLANES, stride=pitch), :]
                u_j = u_s[l, pl.ds(base + j, SUBLANES, stride=pitch), :]
                h = a_j * carry[g * N_LRU_SLABS + l] + u_j
                u_s[l, pl.ds(base + j, SUBLANES, stride=pitch), :] = h
                new.append(h)
        return tuple(new)

    h0 = tuple(h0_ref[g * SUBLANES:(g + 1) * SUBLANES, l * LANES:(l + 1) * LANES]
               for g in range(n_grp) for l in range(N_LRU_SLABS))
    h_fin = lax.fori_loop(0, steps, scan_body, h0, unroll=4)
    for g in range(n_grp):
        for l in range(N_LRU_SLABS):
            h_out_ref[g * SUBLANES:(g + 1) * SUBLANES, l * LANES:(l + 1) * LANES] = (
                h_fin[g * N_LRU_SLABS + l])

    for b in range(nb):
        for l in range(N_LRU_SLABS):
            lo = l * LANES
            gate = z_s[b * steps:(b + 1) * steps, OFF_LRU_GATE + lo:OFF_LRU_GATE + lo + LANES]
            mix_s[b * steps:(b + 1) * steps, lo:lo + LANES] = _bf16(
                u_s[l, b * pitch:b * pitch + steps, :] * _silu(gate))

    for b in range(nb):
        for l in range(N_CONV_SLABS):
            lo = l * LANES
            uc = (z_s[b * steps:(b + 1) * steps, OFF_GLU_A + lo:OFF_GLU_A + lo + LANES]
                  * _sigmoid(z_s[b * steps:(b + 1) * steps, OFF_GLU_B + lo:OFF_GLU_B + lo + LANES]))
            uc_s[l, b * cpitch + CONV_HALO - (CONV_W - 1):b * cpitch + CONV_HALO, :] = (
                cb_ref[b, :, lo:lo + LANES])
            uc_s[l, b * cpitch + CONV_HALO:(b + 1) * cpitch, :] = uc
            cb_out_ref[b, :, lo:lo + LANES] = (
                uc_s[l, (b + 1) * cpitch - (CONV_W - 1):(b + 1) * cpitch, :])

    def conv_body(b, carry):
        r0 = pl.multiple_of(b * cpitch, SUBLANES)
        o0 = pl.multiple_of(b * steps, steps)
        ys = []
        for l in range(N_CONV_SLABS):
            lo = l * LANES
            ys.append(_conv_taps(uc_s, l, w_dw_ref, r0, steps, CONV_W, CONV_HALO)
                      + b_dw_ref[:, lo:lo + LANES])
        outs = _layernorm_silu_slabs(ys, ln_g_ref, ln_b_ref)
        for l in range(N_CONV_SLABS):
            lo = l * LANES
            ln_s[pl.ds(o0, steps), lo:lo + LANES] = _bf16(outs[l])
        return carry

    lax.fori_loop(0, nb, conv_body, 0)

    yc = (_dot(ln_s[...], w_pw_ref[...]) + b_pw_ref[...]) * _silu(
        z_s[:, OFF_CONV_GATE:OFF_CONV_GATE + D_CONV])
    mix_s[:, D_LRU:D_LRU + D_CONV] = _bf16(yc)

    masks = _head_masks()

    def attn_body(b, carry):
        o0 = pl.multiple_of(b * steps, steps)
        q = z_s[pl.ds(o0, steps), OFF_Q:OFF_Q + D_XATTN]
        o = _attention(q, _bf16(k_ref[b]), _bf16(v_ref[b]), masks)
        gate = z_s[pl.ds(o0, steps), OFF_ATTN_GATE:OFF_ATTN_GATE + D_XATTN]
        mix_s[pl.ds(o0, steps), D_LRU + D_CONV:D_MIX] = _bf16(o * _silu(gate))
        return carry

    lax.fori_loop(0, nb, attn_body, 0)

    y = x + _dot(mix_s[...], w_out_ref[...])
    y_ref[...] = _rmsnorm(y, g_final_ref[...]).reshape(nb, steps, D_MODEL)


def _sample(x, h0, lb, cb, mk, mv, p, nb):
    bsz, steps, _ = x.shape
    assert bsz % nb == 0 and nb % SUBLANES == 0 and steps + LRU_HALO == SAMPLE_PITCH
    rows = nb * steps
    prow = nb * SAMPLE_PITCH
    weights = [p["g_norm"], p["w_in"], p["w_lc"], p["b_lc"], p["wg"], p["b_ga"], p["b_gx"],
               p["lam"], p["w_dw"], p["b_dw"], p["ln_g"], p["ln_b"], p["w_pw"], p["b_pw"],
               p["w_out"], p["g_final"]]
    in_specs = [
        pl.BlockSpec((nb, steps, D_MODEL), lambda g: (g, 0, 0)),
        pl.BlockSpec((nb, D_LRU), lambda g: (g, 0)),
        pl.BlockSpec((nb, LRU_CONV - 1, D_LRU), lambda g: (g, 0, 0)),
        pl.BlockSpec((nb, CONV_W - 1, D_CONV), lambda g: (g, 0, 0)),
        pl.BlockSpec((nb, N_MEM, D_XATTN), lambda g: (g, 0, 0)),
        pl.BlockSpec((nb, N_MEM, D_XATTN), lambda g: (g, 0, 0)),
    ] + [_const_spec(w.shape) for w in weights]
    out_specs = [
        pl.BlockSpec((nb, steps, D_MODEL), lambda g: (g, 0, 0)),
        pl.BlockSpec((nb, D_LRU), lambda g: (g, 0)),
        pl.BlockSpec((nb, LRU_CONV - 1, D_LRU), lambda g: (g, 0, 0)),
        pl.BlockSpec((nb, CONV_W - 1, D_CONV), lambda g: (g, 0, 0)),
    ]
    out_shape = [
        jax.ShapeDtypeStruct((bsz, steps, D_MODEL), jnp.float32),
        jax.ShapeDtypeStruct((bsz, D_LRU), jnp.float32),
        jax.ShapeDtypeStruct((bsz, LRU_CONV - 1, D_LRU), jnp.float32),
        jax.ShapeDtypeStruct((bsz, CONV_W - 1, D_CONV), jnp.float32),
    ]
    scratch = [
        pltpu.VMEM((rows, D_IN), jnp.float32),
        pltpu.VMEM((N_LRU_SLABS, prow, LANES), jnp.float32),
        pltpu.VMEM((prow, D_LRU), jnp.float32),
        pltpu.VMEM((prow, 2 * D_LRU), jnp.float32),
        pltpu.VMEM((N_LRU_SLABS, prow, LANES), jnp.float32),
        pltpu.VMEM((N_LRU_SLABS, prow, LANES), jnp.float32),
        pltpu.VMEM((N_CONV_SLABS, nb * (CONV_HALO + steps), LANES), jnp.float32),
        pltpu.VMEM((rows, D_CONV), jnp.bfloat16),
        pltpu.VMEM((rows, D_MIX), jnp.bfloat16),
    ]
    return pl.pallas_call(
        functools.partial(_sample_kernel, nb, steps),
        grid=(bsz // nb,),
        in_specs=in_specs,
        out_specs=out_specs,
        out_shape=out_shape,
        scratch_shapes=scratch,
        compiler_params=pltpu.CompilerParams(
            dimension_semantics=("arbitrary",),
            vmem_limit_bytes=VMEM_LIMIT_BYTES),
        name="sample_layer",
    )(x, h0, lb, cb, mk, mv, *weights)


def _block_diag4(w4):
    eye = jnp.eye(4, dtype=w4.dtype)
    n = 4 * LRU_BLOCK
    return (eye[:, None, :, None] * w4[:, :, None, :]).reshape(n, n)


def _layer_params(l, g_norm, w_in, w_lru_conv, b_lru_conv, w_gate_a, b_gate_a, w_gate_x, b_gate_x,
                  lru_lambda, w_dw, b_dw, ln_g, ln_b, w_pw, b_pw, w_out, g_final):
    row = lambda v: v.reshape(1, -1)
    half = LRU_BLOCKS // 2
    wg = jnp.stack([
        jnp.concatenate([_block_diag4(w_gate_a[l, h * half:(h + 1) * half]),
                         _block_diag4(w_gate_x[l, h * half:(h + 1) * half])], axis=1)
        for h in range(2)])
    return {
        "g_norm": row(g_norm[l]), "w_in": _bf16(w_in[l]), "w_lc": w_lru_conv[l],
        "b_lc": row(b_lru_conv[l]), "wg": _bf16(wg), "b_ga": row(b_gate_a[l]),
        "b_gx": row(b_gate_x[l]), "lam": row(lru_lambda[l]), "w_dw": w_dw[l], "b_dw": row(b_dw[l]),
        "ln_g": row(ln_g[l]), "ln_b": row(ln_b[l]), "w_pw": _bf16(w_pw[l]), "b_pw": row(b_pw[l]),
        "w_out": _bf16(w_out[l]), "g_final": row(g_final),
    }


def kernel(x_prompt, x_sample, state_lru_h, state_lru_conv, state_conv, cache_mem_k, cache_mem_v,
           mem_prompt, g_norm, w_in, w_lru_conv, b_lru_conv, w_gate_a, b_gate_a, w_gate_x, b_gate_x,
           lru_lambda, w_dw, b_dw, ln_g, ln_b, w_pw, b_pw, g_mem, w_mem_k, w_mem_v, w_out, g_final):
    depth = g_norm.shape[0]
    assert depth == 1
    bp = x_prompt.shape[0]
    bs = x_sample.shape[0]
    l = 0
    p = _layer_params(l, g_norm, w_in, w_lru_conv, b_lru_conv, w_gate_a, b_gate_a, w_gate_x,
                      b_gate_x, lru_lambda, w_dw, b_dw, ln_g, ln_b, w_pw, b_pw, w_out, g_final)
    w_kv = _bf16(jnp.concatenate([w_mem_k[l], w_mem_v[l]], axis=1))
    mk, mv = _memkv(mem_prompt, g_mem[l].reshape(1, -1), w_kv)

    y_prompt, ph, plb, pcb = _prompt(x_prompt, mk, mv, p)

    ck = cache_mem_k[l].reshape(bs, N_MEM, D_XATTN)
    cv = cache_mem_v[l].reshape(bs, N_MEM, D_XATTN)
    y_sample, sh, slb, scb = _sample(x_sample, state_lru_h[l], state_lru_conv[l], state_conv[l],
                                     ck, cv, p, nb=16)

    return (y_prompt, y_sample,
            ph.reshape(depth, bp, D_LRU), plb[None], pcb[None],
            mk.reshape(depth, bp, N_MEM, N_XHEADS, XHEAD_DIM),
            mv.reshape(depth, bp, N_MEM, N_XHEADS, XHEAD_DIM),
            sh[None], slb[None], scb[None])
```

```python
import functools

import jax
import jax.numpy as jnp
from jax import lax
from jax.experimental import pallas as pl
from jax.experimental.pallas import tpu as pltpu

D_MODEL = 1024
N_MEM = 256
D_LRU = 512
LRU_BLOCKS = 8
LRU_BLOCK = D_LRU // LRU_BLOCKS
LRU_CONV = 4
LRU_C = 8.0
D_CONV = 256
CONV_W = 31
N_XHEADS = 4
XHEAD_DIM = 64
D_XATTN = N_XHEADS * XHEAD_DIM
D_MIX = D_LRU + D_CONV + D_XATTN
D_IN = 2 * D_LRU + 3 * D_CONV + 2 * D_XATTN
EPS = 1e-6

LANES = 128
SUBLANES = 8
N_LRU_SLABS = D_LRU // LANES
N_CONV_SLABS = D_CONV // LANES

OFF_LRU_IN = 0
OFF_LRU_GATE = D_LRU
OFF_GLU_A = 2 * D_LRU
OFF_GLU_B = OFF_GLU_A + D_CONV
OFF_CONV_GATE = OFF_GLU_B + D_CONV
OFF_Q = OFF_CONV_GATE + D_CONV
OFF_ATTN_GATE = OFF_Q + D_XATTN

LRU_HALO = 8
CONV_HALO = 32

PROMPT_TILE = 512
PROMPT_SEG = 64
PROMPT_PITCH = 72
SAMPLE_PITCH = 40
ATTN_ROWS = 128

VMEM_LIMIT_BYTES = 56 * 1024 * 1024


def _silu(x):
    hx = 0.5 * x
    return hx + hx * jnp.tanh(hx)


def _gated(a, b):
    ha = 0.5 * a
    return ha + ha * jnp.tanh(0.5 * b)


def _sqrt_nonneg(x):
    return jnp.where(x > 0.0, x * lax.rsqrt(x), 0.0)


def _rmsnorm(x, g):
    ms = jnp.mean(x * x, axis=-1, keepdims=True)
    return x * lax.rsqrt(ms + EPS) * g


def _log_sigmoid(x):
    y = -x
    return -(jnp.maximum(y, 0.0) + jnp.log1p(jnp.exp(-jnp.abs(y))))


def _bf16(x):
    return x.astype(jnp.bfloat16)


def _dot(a, b):
    return jnp.dot(a, b, preferred_element_type=jnp.float32)


def _unrolled(n, body, init):
    carry = init
    for i in range(n):
        carry = body(i, carry)
    return carry


def _aligned(x, m):
    return x if isinstance(x, int) else pl.multiple_of(x, m)


def _head_masks():
    lane = lax.broadcasted_iota(jnp.int32, (1, D_XATTN), 1)
    head = lax.shift_right_logical(lane, XHEAD_DIM.bit_length() - 1)
    return [(head == h).astype(jnp.float32) for h in range(N_XHEADS)]


def _head_stack(k, v, masks):
    scale = XHEAD_DIM ** -0.5
    kst = jnp.concatenate([_bf16(k * (m * scale)) for m in masks], axis=0)
    vst = jnp.concatenate([_bf16(v * m) for m in masks], axis=0)
    return kst, vst


def _attention_qstack(q, k_bf, v_bf, masks):
    rows = q.shape[0]
    scale = XHEAD_DIM ** -0.5
    qm = jnp.concatenate([_bf16(q * (m * scale)) for m in masks], axis=0)
    s = lax.dot_general(qm, k_bf, (((1,), (1,)), ((), ())),
                        preferred_element_type=jnp.float32)
    e = jnp.exp(s - jnp.max(s, axis=-1, keepdims=True))
    inv = 1.0 / jnp.sum(e, axis=-1, keepdims=True)
    o_all = _dot(_bf16(e * inv), v_bf)
    o = o_all[0:rows] * masks[0]
    for h in range(1, N_XHEADS):
        o = o + o_all[h * rows:(h + 1) * rows] * masks[h]
    return o


def _attention(q_bf, kst_ref, vst_ref):
    s = lax.dot_general(q_bf, kst_ref[...], (((1,), (1,)), ((), ())),
                        preferred_element_type=jnp.float32)
    ps = []
    for h in range(N_XHEADS):
        sh = s[:, h * N_MEM:(h + 1) * N_MEM]
        e = jnp.exp(sh - jnp.max(sh, axis=-1, keepdims=True))
        inv = 1.0 / jnp.sum(e, axis=-1, keepdims=True)
        ps.append(_bf16(e * inv))
    return _dot(jnp.concatenate(ps, axis=1), vst_ref[...])


def _lru_gates(xl_bf, wg_ref, pre_s):
    half = D_LRU // 2
    pre_s[:, 0:D_LRU] = _dot(xl_bf[:, 0:half], wg_ref[0])
    pre_s[:, D_LRU:2 * D_LRU] = _dot(xl_bf[:, half:D_LRU], wg_ref[1])


def _pre_cols(l):
    half_slabs = N_LRU_SLABS // 2
    base = (l // half_slabs) * D_LRU + (l % half_slabs) * LANES
    return base, base + D_LRU // 2


def _lru_coeffs(hpre_a, hpre_x, xl, hb_a, hb_x, hc_ls):
    t_r = jnp.tanh(hpre_a + hb_a)
    t_i = jnp.tanh(hpre_x + hb_x)
    log_a = hc_ls + hc_ls * t_r
    a = jnp.exp(log_a)
    one_minus_a2 = -jnp.tanh(log_a) * (a * a + 1.0)
    hx = 0.5 * xl
    u = _sqrt_nonneg(one_minus_a2) * (hx + hx * t_i)
    return a, u


def _conv_taps(src_s, l, w_ref, row0, nrows, width, halo):
    lo = l * LANES
    acc = None
    for k in range(width):
        w_k = w_ref[k:k + 1, lo:lo + LANES]
        x_k = src_s[l, pl.ds(row0 + (halo - (width - 1) + k), nrows), :]
        acc = w_k * x_k if acc is None else acc + w_k * x_k
    return acc


def _layernorm_silu_slabs(ys, g_ref, b_ref):
    tot = ys[0]
    for y in ys[1:]:
        tot = tot + y
    mu = jnp.sum(tot, axis=-1, keepdims=True) * (1.0 / D_CONV)
    cs = [y - mu for y in ys]
    sq = cs[0] * cs[0]
    for c in cs[1:]:
        sq = sq + c * c
    var = jnp.sum(sq, axis=-1, keepdims=True) * (1.0 / D_CONV)
    inv = lax.rsqrt(var + EPS)
    outs = []
    for l, c in enumerate(cs):
        lo = l * LANES
        outs.append(_silu(c * inv * g_ref[:, lo:lo + LANES] + b_ref[:, lo:lo + LANES]))
    return outs


def _memkv_kernel(mem_ref, g_ref, w_ref, k_ref, v_ref):
    m = _rmsnorm(mem_ref[0], g_ref[...])
    kv = _dot(_bf16(m), w_ref[...])
    k_ref[0] = kv[:, 0:D_XATTN]
    v_ref[0] = kv[:, D_XATTN:2 * D_XATTN]


def _memkv(mem, g_mem, w_kv_bf):
    bsz = mem.shape[0]
    return pl.pallas_call(
        _memkv_kernel,
        grid=(bsz,),
        in_specs=[
            pl.BlockSpec((1, N_MEM, D_MODEL), lambda b: (b, 0, 0)),
            pl.BlockSpec((1, D_MODEL), lambda b: (0, 0)),
            pl.BlockSpec((D_MODEL, 2 * D_XATTN), lambda b: (0, 0)),
        ],
        out_specs=[
            pl.BlockSpec((1, N_MEM, D_XATTN), lambda b: (b, 0, 0)),
            pl.BlockSpec((1, N_MEM, D_XATTN), lambda b: (b, 0, 0)),
        ],
        out_shape=[jax.ShapeDtypeStruct((bsz, N_MEM, D_XATTN), jnp.float32)] * 2,
        compiler_params=pltpu.CompilerParams(dimension_semantics=("arbitrary",)),
        name="memkv",
    )(mem, g_mem, w_kv_bf)


def _prompt_kernel(x_ref, mk_ref, mv_ref, g_norm_ref, w_in_ref, w_lc_ref, b_lc_ref, wg_ref,
                   b_ga_ref, b_gx_ref, lam_ref, w_dw_ref, b_dw_ref, ln_g_ref, ln_b_ref,
                   w_pw_ref, b_pw_ref, w_out_ref, g_final_ref,
                   y_ref, h_out_ref, lb_out_ref, cb_out_ref,
                   z_s, xc_s, xl_s, pre_s, a_s, u_s, uc_s, ln_s, mix_s, hin_s, kst_s, vst_s):
    t = pl.program_id(1)
    n_t = pl.num_programs(1)
    tile = PROMPT_TILE
    seg = PROMPT_SEG
    n_seg = tile // seg
    pitch = PROMPT_PITCH

    @pl.when(t == 0)
    def _():
        xc_s[:, 0:LRU_HALO, :] = jnp.zeros((N_LRU_SLABS, LRU_HALO, LANES), jnp.float32)
        uc_s[:, 0:CONV_HALO, :] = jnp.zeros((N_CONV_SLABS, CONV_HALO, LANES), jnp.float32)
        hin_s[:, 0:1, :] = jnp.zeros((N_LRU_SLABS, 1, LANES), jnp.float32)
        kst, vst = _head_stack(mk_ref[0], mv_ref[0], _head_masks())
        kst_s[...] = kst
        vst_s[...] = vst

    hn = _rmsnorm(x_ref[0], g_norm_ref[...])
    z_s[...] = _dot(_bf16(hn), w_in_ref[...])

    for l in range(N_LRU_SLABS):
        lo = l * LANES
        xc_s[l, LRU_HALO:LRU_HALO + tile, :] = z_s[:, OFF_LRU_IN + lo:OFF_LRU_IN + lo + LANES]
    for l in range(N_LRU_SLABS):
        lo = l * LANES
        xl = _conv_taps(xc_s, l, w_lc_ref, 0, tile, LRU_CONV, LRU_HALO) + b_lc_ref[:, lo:lo + LANES]
        xl_s[:, lo:lo + LANES] = xl
    _lru_gates(_bf16(xl_s[...]), wg_ref, pre_s)

    hc_ls = (0.5 * LRU_C) * _log_sigmoid(lam_ref[...])
    hb_ga = 0.5 * b_ga_ref[...]
    hb_gx = 0.5 * b_gx_ref[...]

    def coeff_body(c, carry):
        r0 = _aligned(c * seg, seg)
        p0 = _aligned(c * pitch, SUBLANES)
        for l in range(N_LRU_SLABS):
            lo = l * LANES
            ca, cx = _pre_cols(l)
            a, u = _lru_coeffs(pre_s[pl.ds(r0, seg), ca:ca + LANES],
                               pre_s[pl.ds(r0, seg), cx:cx + LANES],
                               xl_s[pl.ds(r0, seg), lo:lo + LANES],
                               hb_ga[:, lo:lo + LANES], hb_gx[:, lo:lo + LANES],
                               hc_ls[:, lo:lo + LANES])
            a_s[l, pl.ds(p0, seg), :] = a
            u_s[l, pl.ds(p0, seg), :] = u
        return carry

    _unrolled(n_seg, coeff_body, 0)

    def tot_body(j, carry):
        new = []
        for l in range(N_LRU_SLABS):
            a_tot, u_tot = carry[l]
            a_j = a_s[l, pl.ds(j, n_seg, stride=pitch), :]
            u_j = u_s[l, pl.ds(j, n_seg, stride=pitch), :]
            new.append((a_j * a_tot, a_j * u_tot + u_j))
        return tuple(new)

    init = tuple((jnp.ones((n_seg, LANES), jnp.float32), jnp.zeros((n_seg, LANES), jnp.float32))
                 for _ in range(N_LRU_SLABS))
    totals = _unrolled(seg, tot_body, init)

    h_in = []
    for l in range(N_LRU_SLABS):
        a_tot, u_tot = totals[l]
        h = hin_s[l, 0:1, :]
        for c in range(n_seg):
            h = a_tot[c:c + 1, :] * h + u_tot[c:c + 1, :]
            hin_s[l, c + 1:c + 2, :] = h
        h_in.append(hin_s[l, 0:n_seg, :])
        hin_s[l, 0:1, :] = h

    def scan_body(j, carry):
        new = []
        for l in range(N_LRU_SLABS):
            a_j = a_s[l, pl.ds(j, n_seg, stride=pitch), :]
            u_j = u_s[l, pl.ds(j, n_seg, stride=pitch), :]
            h = a_j * carry[l] + u_j
            u_s[l, pl.ds(j, n_seg, stride=pitch), :] = h
            new.append(h)
        return tuple(new)

    _unrolled(seg, scan_body, tuple(h_in))

    for l in range(N_CONV_SLABS):
        lo = l * LANES
        uc_s[l, CONV_HALO:CONV_HALO + tile, :] = _gated(
            z_s[:, OFF_GLU_A + lo:OFF_GLU_A + lo + LANES],
            z_s[:, OFF_GLU_B + lo:OFF_GLU_B + lo + LANES])

    def seg_body(c, carry):
        r0 = _aligned(c * seg, seg)
        p0 = _aligned(c * pitch, SUBLANES)
        for l in range(N_LRU_SLABS):
            lo = l * LANES
            gate = z_s[pl.ds(r0, seg), OFF_LRU_GATE + lo:OFF_LRU_GATE + lo + LANES]
            mix_s[pl.ds(r0, seg), lo:lo + LANES] = _bf16(u_s[l, pl.ds(p0, seg), :] * _silu(gate))
        ys = []
        for l in range(N_CONV_SLABS):
            lo = l * LANES
            ys.append(_conv_taps(uc_s, l, w_dw_ref, r0, seg, CONV_W, CONV_HALO)
                      + b_dw_ref[:, lo:lo + LANES])
        outs = _layernorm_silu_slabs(ys, ln_g_ref, ln_b_ref)
        for l in range(N_CONV_SLABS):
            lo = l * LANES
            ln_s[pl.ds(r0, seg), lo:lo + LANES] = _bf16(outs[l])
        return carry

    _unrolled(n_seg, seg_body, 0)

    yc = (_dot(ln_s[...], w_pw_ref[...]) + b_pw_ref[...]) * _silu(
        z_s[:, OFF_CONV_GATE:OFF_CONV_GATE + D_CONV])
    mix_s[:, D_LRU:D_LRU + D_CONV] = _bf16(yc)

    def attn_body(c, carry):
        r0 = _aligned(c * ATTN_ROWS, ATTN_ROWS)
        q = z_s[pl.ds(r0, ATTN_ROWS), OFF_Q:OFF_Q + D_XATTN]
        o = _attention(_bf16(q), kst_s, vst_s)
        gate = z_s[pl.ds(r0, ATTN_ROWS), OFF_ATTN_GATE:OFF_ATTN_GATE + D_XATTN]
        mix_s[pl.ds(r0, ATTN_ROWS), D_LRU + D_CONV:D_MIX] = _bf16(o * _silu(gate))
        return carry

    _unrolled(tile // ATTN_ROWS, attn_body, 0)

    y = x_ref[0] + _dot(mix_s[...], w_out_ref[...])
    y_ref[0] = _rmsnorm(y, g_final_ref[...])

    for l in range(N_LRU_SLABS):
        xc_s[l, LRU_HALO - (LRU_CONV - 1):LRU_HALO, :] = (
            xc_s[l, LRU_HALO + tile - (LRU_CONV - 1):LRU_HALO + tile, :])
    for l in range(N_CONV_SLABS):
        uc_s[l, CONV_HALO - (CONV_W - 1):CONV_HALO, :] = (
            uc_s[l, CONV_HALO + tile - (CONV_W - 1):CONV_HALO + tile, :])

    @pl.when(t == n_t - 1)
    def _():
        for l in range(N_LRU_SLABS):
            lo = l * LANES
            h_out_ref[0, :, lo:lo + LANES] = hin_s[l, 0:1, :]
            lb_out_ref[0, :, lo:lo + LANES] = xc_s[l, LRU_HALO - (LRU_CONV - 1):LRU_HALO, :]
        for l in range(N_CONV_SLABS):
            lo = l * LANES
            cb_out_ref[0, :, lo:lo + LANES] = uc_s[l, CONV_HALO - (CONV_W - 1):CONV_HALO, :]


def _const_spec(shape):
    nd = len(shape)
    return pl.BlockSpec(shape, lambda *_: (0,) * nd)


def _prompt(x, mk, mv, p):
    bsz, seq, _ = x.shape
    tile = PROMPT_TILE
    n_t = seq // tile
    n_seg = tile // PROMPT_SEG
    weights = [p["g_norm"], p["w_in"], p["w_lc"], p["b_lc"], p["wg"], p["b_ga"], p["b_gx"],
               p["lam"], p["w_dw"], p["b_dw"], p["ln_g"], p["ln_b"], p["w_pw"], p["b_pw"],
               p["w_out"], p["g_final"]]
    in_specs = [
        pl.BlockSpec((1, tile, D_MODEL), lambda b, t: (b, t, 0)),
        pl.BlockSpec((1, N_MEM, D_XATTN), lambda b, t: (b, 0, 0)),
        pl.BlockSpec((1, N_MEM, D_XATTN), lambda b, t: (b, 0, 0)),
    ] + [_const_spec(w.shape) for w in weights]
    out_specs = [
        pl.BlockSpec((1, tile, D_MODEL), lambda b, t: (b, t, 0)),
        pl.BlockSpec((1, 1, D_LRU), lambda b, t: (b, 0, 0)),
        pl.BlockSpec((1, LRU_CONV - 1, D_LRU), lambda b, t: (b, 0, 0)),
        pl.BlockSpec((1, CONV_W - 1, D_CONV), lambda b, t: (b, 0, 0)),
    ]
    out_shape = [
        jax.ShapeDtypeStruct((bsz, seq, D_MODEL), jnp.float32),
        jax.ShapeDtypeStruct((bsz, 1, D_LRU), jnp.float32),
        jax.ShapeDtypeStruct((bsz, LRU_CONV - 1, D_LRU), jnp.float32),
        jax.ShapeDtypeStruct((bsz, CONV_W - 1, D_CONV), jnp.float32),
    ]
    scratch = [
        pltpu.VMEM((tile, D_IN), jnp.float32),
        pltpu.VMEM((N_LRU_SLABS, LRU_HALO + tile, LANES), jnp.float32),
        pltpu.VMEM((tile, D_LRU), jnp.float32),
        pltpu.VMEM((tile, 2 * D_LRU), jnp.float32),
        pltpu.VMEM((N_LRU_SLABS, n_seg * PROMPT_PITCH, LANES), jnp.float32),
        pltpu.VMEM((N_LRU_SLABS, n_seg * PROMPT_PITCH, LANES), jnp.float32),
        pltpu.VMEM((N_CONV_SLABS, CONV_HALO + tile, LANES), jnp.float32),
        pltpu.VMEM((tile, D_CONV), jnp.bfloat16),
        pltpu.VMEM((tile, D_MIX), jnp.bfloat16),
        pltpu.VMEM((N_LRU_SLABS, 2 * SUBLANES, LANES), jnp.float32),
        pltpu.VMEM((N_XHEADS * N_MEM, D_XATTN), jnp.bfloat16),
        pltpu.VMEM((N_XHEADS * N_MEM, D_XATTN), jnp.bfloat16),
    ]
    return pl.pallas_call(
        _prompt_kernel,
        grid=(bsz, n_t),
        in_specs=in_specs,
        out_specs=out_specs,
        out_shape=out_shape,
        scratch_shapes=scratch,
        compiler_params=pltpu.CompilerParams(
            dimension_semantics=("arbitrary", "arbitrary"),
            vmem_limit_bytes=VMEM_LIMIT_BYTES),
        name="prompt_layer",
    )(x, mk, mv, *weights)


def _sample_kernel(nb, steps,
                   x_ref, h0_ref, lb_ref, cb_ref, k_ref, v_ref,
                   g_norm_ref, w_in_ref, w_lc_ref, b_lc_ref, wg_ref,
                   b_ga_ref, b_gx_ref, lam_ref, w_dw_ref, b_dw_ref, ln_g_ref, ln_b_ref,
                   w_pw_ref, b_pw_ref, w_out_ref, g_final_ref,
                   y_ref, h_out_ref, lb_out_ref, cb_out_ref,
                   z_s, xc_s, xl_s, pre_s, a_s, u_s, uc_s, ln_s, mix_s):
    rows = nb * steps
    pitch = SAMPLE_PITCH
    cpitch = CONV_HALO + steps
    prow = nb * pitch

    x = x_ref[...].reshape(rows, D_MODEL)
    hn = _rmsnorm(x, g_norm_ref[...])
    z_s[...] = _dot(_bf16(hn), w_in_ref[...])

    for b in range(nb):
        for l in range(N_LRU_SLABS):
            lo = l * LANES
            xc_s[l, b * pitch:b * pitch + LRU_HALO, :] = jnp.zeros((LRU_HALO, LANES), jnp.float32)
            xc_s[l, b * pitch + LRU_HALO - (LRU_CONV - 1):b * pitch + LRU_HALO, :] = (
                lb_ref[b, :, lo:lo + LANES])
            xc_s[l, b * pitch + LRU_HALO:(b + 1) * pitch, :] = (
                z_s[b * steps:(b + 1) * steps, OFF_LRU_IN + lo:OFF_LRU_IN + lo + LANES])
            lb_out_ref[b, :, lo:lo + LANES] = (
                z_s[(b + 1) * steps - (LRU_CONV - 1):(b + 1) * steps,
                    OFF_LRU_IN + lo:OFF_LRU_IN + lo + LANES])
    n_conv = prow - LRU_HALO
    for l in range(N_LRU_SLABS):
        lo = l * LANES
        xl = _conv_taps(xc_s, l, w_lc_ref, 0, n_conv, LRU_CONV, LRU_HALO) + b_lc_ref[:, lo:lo + LANES]
        xl_s[0:n_conv, lo:lo + LANES] = xl
    xl_s[n_conv:prow, :] = jnp.zeros((LRU_HALO, D_LRU), jnp.float32)
    _lru_gates(_bf16(xl_s[...]), wg_ref, pre_s)

    hc_ls = (0.5 * LRU_C) * _log_sigmoid(lam_ref[...])
    hb_ga = 0.5 * b_ga_ref[...]
    hb_gx = 0.5 * b_gx_ref[...]
    for l in range(N_LRU_SLABS):
        lo = l * LANES
        ca, cx = _pre_cols(l)
        a, u = _lru_coeffs(pre_s[:, ca:ca + LANES], pre_s[:, cx:cx + LANES], xl_s[:, lo:lo + LANES],
                           hb_ga[:, lo:lo + LANES], hb_gx[:, lo:lo + LANES],
                           hc_ls[:, lo:lo + LANES])
        a_s[l, :, :] = a
        u_s[l, :, :] = u

    n_grp = nb // SUBLANES

    def scan_body(j, carry):
        new = []
        for g in range(n_grp):
            for l in range(N_LRU_SLABS):
                base = g * SUBLANES * pitch
                a_j = a_s[l, pl.ds(base + j, SUBLANES, stride=pitch), :]
                u_j = u_s[l, pl.ds(base + j, SUBLANES, stride=pitch), :]
                h = a_j * carry[g * N_LRU_SLABS + l] + u_j
                u_s[l, pl.ds(base + j, SUBLANES, stride=pitch), :] = h
                new.append(h)
        return tuple(new)

    h0 = tuple(h0_ref[g * SUBLANES:(g + 1) * SUBLANES, l * LANES:(l + 1) * LANES]
               for g in range(n_grp) for l in range(N_LRU_SLABS))
    h_fin = lax.fori_loop(0, steps, scan_body, h0, unroll=4)
    for g in range(n_grp):
        for l in range(N_LRU_SLABS):
            h_out_ref[g * SUBLANES:(g + 1) * SUBLANES, l * LANES:(l + 1) * LANES] = (
                h_fin[g * N_LRU_SLABS + l])

    for b in range(nb):
        for l in range(N_LRU_SLABS):
            lo = l * LANES
            gate = z_s[b * steps:(b + 1) * steps, OFF_LRU_GATE + lo:OFF_LRU_GATE + lo + LANES]
            mix_s[b * steps:(b + 1) * steps, lo:lo + LANES] = _bf16(
                u_s[l, b * pitch:b * pitch + steps, :] * _silu(gate))

    for b in range(nb):
        for l in range(N_CONV_SLABS):
            lo = l * LANES
            uc = _gated(z_s[b * steps:(b + 1) * steps, OFF_GLU_A + lo:OFF_GLU_A + lo + LANES],
                        z_s[b * steps:(b + 1) * steps, OFF_GLU_B + lo:OFF_GLU_B + lo + LANES])
            uc_s[l, b * cpitch + CONV_HALO - (CONV_W - 1):b * cpitch + CONV_HALO, :] = (
                cb_ref[b, :, lo:lo + LANES])
            uc_s[l, b * cpitch + CONV_HALO:(b + 1) * cpitch, :] = uc
            cb_out_ref[b, :, lo:lo + LANES] = (
                uc_s[l, (b + 1) * cpitch - (CONV_W - 1):(b + 1) * cpitch, :])

    def conv_body(b, carry):
        r0 = pl.multiple_of(b * cpitch, SUBLANES)
        o0 = pl.multiple_of(b * steps, steps)
        ys = []
        for l in range(N_CONV_SLABS):
            lo = l * LANES
            ys.append(_conv_taps(uc_s, l, w_dw_ref, r0, steps, CONV_W, CONV_HALO)
                      + b_dw_ref[:, lo:lo + LANES])
        outs = _layernorm_silu_slabs(ys, ln_g_ref, ln_b_ref)
        for l in range(N_CONV_SLABS):
            lo = l * LANES
            ln_s[pl.ds(o0, steps), lo:lo + LANES] = _bf16(outs[l])
        return carry

    lax.fori_loop(0, nb, conv_body, 0)

    yc = (_dot(ln_s[...], w_pw_ref[...]) + b_pw_ref[...]) * _silu(
        z_s[:, OFF_CONV_GATE:OFF_CONV_GATE + D_CONV])
    mix_s[:, D_LRU:D_LRU + D_CONV] = _bf16(yc)

    masks = _head_masks()

    def attn_body(b, carry):
        o0 = pl.multiple_of(b * steps, steps)
        q = z_s[pl.ds(o0, steps), OFF_Q:OFF_Q + D_XATTN]
        o = _attention_qstack(q, _bf16(k_ref[b]), _bf16(v_ref[b]), masks)
        gate = z_s[pl.ds(o0, steps), OFF_ATTN_GATE:OFF_ATTN_GATE + D_XATTN]
        mix_s[pl.ds(o0, steps), D_LRU + D_CONV:D_MIX] = _bf16(o * _silu(gate))
        return carry

    lax.fori_loop(0, nb, attn_body, 0)

    y = x + _dot(mix_s[...], w_out_ref[...])
    y_ref[...] = _rmsnorm(y, g_final_ref[...]).reshape(nb, steps, D_MODEL)


def _sample(x, h0, lb, cb, mk, mv, p, nb):
    bsz, steps, _ = x.shape
    assert bsz % nb == 0 and nb % SUBLANES == 0 and steps + LRU_HALO == SAMPLE_PITCH
    rows = nb * steps
    prow = nb * SAMPLE_PITCH
    weights = [p["g_norm"], p["w_in"], p["w_lc"], p["b_lc"], p["wg"], p["b_ga"], p["b_gx"],
               p["lam"], p["w_dw"], p["b_dw"], p["ln_g"], p["ln_b"], p["w_pw"], p["b_pw"],
               p["w_out"], p["g_final"]]
    in_specs = [
        pl.BlockSpec((nb, steps, D_MODEL), lambda g: (g, 0, 0)),
        pl.BlockSpec((nb, D_LRU), lambda g: (g, 0)),
        pl.BlockSpec((nb, LRU_CONV - 1, D_LRU), lambda g: (g, 0, 0)),
        pl.BlockSpec((nb, CONV_W - 1, D_CONV), lambda g: (g, 0, 0)),
        pl.BlockSpec((nb, N_MEM, D_XATTN), lambda g: (g, 0, 0)),
        pl.BlockSpec((nb, N_MEM, D_XATTN), lambda g: (g, 0, 0)),
    ] + [_const_spec(w.shape) for w in weights]
    out_specs = [
        pl.BlockSpec((nb, steps, D_MODEL), lambda g: (g, 0, 0)),
        pl.BlockSpec((nb, D_LRU), lambda g: (g, 0)),
        pl.BlockSpec((nb, LRU_CONV - 1, D_LRU), lambda g: (g, 0, 0)),
        pl.BlockSpec((nb, CONV_W - 1, D_CONV), lambda g: (g, 0, 0)),
    ]
    out_shape = [
        jax.ShapeDtypeStruct((bsz, steps, D_MODEL), jnp.float32),
        jax.ShapeDtypeStruct((bsz, D_LRU), jnp.float32),
        jax.ShapeDtypeStruct((bsz, LRU_CONV - 1, D_LRU), jnp.float32),
        jax.ShapeDtypeStruct((bsz, CONV_W - 1, D_CONV), jnp.float32),
    ]
    scratch = [
        pltpu.VMEM((rows, D_IN), jnp.float32),
        pltpu.VMEM((N_LRU_SLABS, prow, LANES), jnp.float32),
        pltpu.VMEM((prow, D_LRU), jnp.float32),
        pltpu.VMEM((prow, 2 * D_LRU), jnp.float32),
        pltpu.VMEM((N_LRU_SLABS, prow, LANES), jnp.float32),
        pltpu.VMEM((N_LRU_SLABS, prow, LANES), jnp.float32),
        pltpu.VMEM((N_CONV_SLABS, nb * (CONV_HALO + steps), LANES), jnp.float32),
        pltpu.VMEM((rows, D_CONV), jnp.bfloat16),
        pltpu.VMEM((rows, D_MIX), jnp.bfloat16),
    ]
    return pl.pallas_call(
        functools.partial(_sample_kernel, nb, steps),
        grid=(bsz // nb,),
        in_specs=in_specs,
        out_specs=out_specs,
        out_shape=out_shape,
        scratch_shapes=scratch,
        compiler_params=pltpu.CompilerParams(
            dimension_semantics=("arbitrary",),
            vmem_limit_bytes=VMEM_LIMIT_BYTES),
        name="sample_layer",
    )(x, h0, lb, cb, mk, mv, *weights)


def _block_diag4(w4):
    eye = jnp.eye(4, dtype=w4.dtype)
    n = 4 * LRU_BLOCK
    return (eye[:, None, :, None] * w4[:, :, None, :]).reshape(n, n)


def _layer_params(l, g_norm, w_in, w_lru_conv, b_lru_conv, w_gate_a, b_gate_a, w_gate_x, b_gate_x,
                  lru_lambda, w_dw, b_dw, ln_g, ln_b, w_pw, b_pw, w_out, g_final):
    row = lambda v: v.reshape(1, -1)
    half = LRU_BLOCKS // 2
    wg = 0.5 * jnp.stack([
        jnp.concatenate([_block_diag4(w_gate_a[l, h * half:(h + 1) * half]),
                         _block_diag4(w_gate_x[l, h * half:(h + 1) * half])], axis=1)
        for h in range(2)])
    return {
        "g_norm": row(g_norm[l]), "w_in": _bf16(w_in[l]), "w_lc": w_lru_conv[l],
        "b_lc": row(b_lru_conv[l]), "wg": _bf16(wg), "b_ga": row(b_gate_a[l]),
        "b_gx": row(b_gate_x[l]), "lam": row(lru_lambda[l]), "w_dw": w_dw[l], "b_dw": row(b_dw[l]),
        "ln_g": row(ln_g[l]), "ln_b": row(ln_b[l]), "w_pw": _bf16(w_pw[l]), "b_pw": row(b_pw[l]),
        "w_out": _bf16(w_out[l]), "g_final": row(g_final),
    }


def kernel(x_prompt, x_sample, state_lru_h, state_lru_conv, state_conv, cache_mem_k, cache_mem_v,
           mem_prompt, g_norm, w_in, w_lru_conv, b_lru_conv, w_gate_a, b_gate_a, w_gate_x, b_gate_x,
           lru_lambda, w_dw, b_dw, ln_g, ln_b, w_pw, b_pw, g_mem, w_mem_k, w_mem_v, w_out, g_final):
    depth = g_norm.shape[0]
    assert depth == 1
    bp = x_prompt.shape[0]
    bs = x_sample.shape[0]
    l = 0
    p = _layer_params(l, g_norm, w_in, w_lru_conv, b_lru_conv, w_gate_a, b_gate_a, w_gate_x,
                      b_gate_x, lru_lambda, w_dw, b_dw, ln_g, ln_b, w_pw, b_pw, w_out, g_final)
    w_kv = _bf16(jnp.concatenate([w_mem_k[l], w_mem_v[l]], axis=1))
    mk, mv = _memkv(mem_prompt, g_mem[l].reshape(1, -1), w_kv)

    y_prompt, ph, plb, pcb = _prompt(x_prompt, mk, mv, p)

    ck = cache_mem_k[l].reshape(bs, N_MEM, D_XATTN)
    cv = cache_mem_v[l].reshape(bs, N_MEM, D_XATTN)
    y_sample, sh, slb, scb = _sample(x_sample, state_lru_h[l], state_lru_conv[l], state_conv[l],
                                     ck, cv, p, nb=16)

    return (y_prompt, y_sample,
            ph.reshape(depth, bp, D_LRU), plb[None], pcb[None],
            mk.reshape(depth, bp, N_MEM, N_XHEADS, XHEAD_DIM),
            mv.reshape(depth, bp, N_MEM, N_XHEADS, XHEAD_DIM),
            sh[None], slb[None], scb[None])
```

```python
import functools

import jax
import jax.numpy as jnp
from jax import lax
from jax.experimental import pallas as pl
from jax.experimental.pallas import tpu as pltpu

D_MODEL = 1024
N_MEM = 256
D_LRU = 512
LRU_BLOCKS = 8
LRU_BLOCK = D_LRU // LRU_BLOCKS
LRU_CONV = 4
LRU_C = 8.0
D_CONV = 256
CONV_W = 31
N_XHEADS = 4
XHEAD_DIM = 64
D_XATTN = N_XHEADS * XHEAD_DIM
D_MIX = D_LRU + D_CONV + D_XATTN
D_IN = 2 * D_LRU + 3 * D_CONV + 2 * D_XATTN
EPS = 1e-6

LANES = 128
SUBLANES = 8
N_LRU_SLABS = D_LRU // LANES
N_CONV_SLABS = D_CONV // LANES
GATE_HALF = D_LRU // 2

OFF_LRU_IN = 0
OFF_LRU_GATE = D_LRU
OFF_GLU_A = 2 * D_LRU
OFF_GLU_B = OFF_GLU_A + D_CONV
OFF_CONV_GATE = OFF_GLU_B + D_CONV
OFF_Q = OFF_CONV_GATE + D_CONV
OFF_ATTN_GATE = OFF_Q + D_XATTN

LRU_HALO = 8
CONV_HALO = 32

PROMPT_TILE = 512
PROMPT_SEG = 64
PROMPT_PITCH = 72
SAMPLE_PITCH = 40
SAMPLE_GROUP = 16
SAMPLE_UNROLL = 4
ATTN_ROWS = 128

VMEM_LIMIT_BYTES = 56 * 1024 * 1024


def _silu(x):
    hx = 0.5 * x
    return hx + hx * jnp.tanh(hx)


def _gated(a, b):
    ha = 0.5 * a
    return ha + ha * jnp.tanh(0.5 * b)


def _sqrt_nonneg(x):
    return jnp.where(x > 0.0, x * lax.rsqrt(x), 0.0)


def _rmsnorm(x, g):
    ms = jnp.mean(x * x, axis=-1, keepdims=True)
    return x * lax.rsqrt(ms + EPS) * g


def _log_sigmoid(x):
    y = -x
    return -(jnp.maximum(y, 0.0) + jnp.log1p(jnp.exp(-jnp.abs(y))))


def _bf16(x):
    return x.astype(jnp.bfloat16)


def _dot(a, b):
    return jnp.dot(a, b, preferred_element_type=jnp.float32)


def _dot_nt(a, b):
    return lax.dot_general(a, b, (((1,), (1,)), ((), ())), preferred_element_type=jnp.float32)


def _unrolled(n, body, init):
    carry = init
    for i in range(n):
        carry = body(i, carry)
    return carry


def _aligned(x, m):
    return x if isinstance(x, int) else pl.multiple_of(x, m)


def _head_of(index):
    return lax.shift_right_logical(index, XHEAD_DIM.bit_length() - 1)


def _lane_head_masks():
    head = _head_of(lax.broadcasted_iota(jnp.int32, (1, D_XATTN), 1))
    return [(head == h).astype(jnp.float32) for h in range(N_XHEADS)]


def _row_head_masks():
    head = _head_of(lax.broadcasted_iota(jnp.int32, (D_XATTN, 1), 0))
    return [(head == h).astype(jnp.float32) for h in range(N_XHEADS)]


def _head_stack_t(kt, vt, kst_ref, vst_ref):
    scale = XHEAD_DIM ** -0.5
    for h, m in enumerate(_row_head_masks()):
        kst_ref[:, h * N_MEM:(h + 1) * N_MEM] = _bf16(kt * (m * scale))
        vst_ref[:, h * N_MEM:(h + 1) * N_MEM] = _bf16(vt * m)


def _attention(q_bf, kst_ref, vst_ref):
    s = _dot(q_bf, kst_ref[...])
    ps = []
    for h in range(N_XHEADS):
        sh = s[:, h * N_MEM:(h + 1) * N_MEM]
        e = jnp.exp(sh - jnp.max(sh, axis=-1, keepdims=True))
        inv = 1.0 / jnp.sum(e, axis=-1, keepdims=True)
        ps.append(_bf16(e * inv))
    return _dot_nt(jnp.concatenate(ps, axis=1), vst_ref[...])


def _attention_qstack(q, kt_bf, vt_bf, masks):
    rows = q.shape[0]
    scale = XHEAD_DIM ** -0.5
    qm = jnp.concatenate([_bf16(q * (m * scale)) for m in masks], axis=0)
    s = _dot(qm, kt_bf)
    e = jnp.exp(s - jnp.max(s, axis=-1, keepdims=True))
    inv = 1.0 / jnp.sum(e, axis=-1, keepdims=True)
    o_all = _dot_nt(_bf16(e * inv), vt_bf)
    o = o_all[0:rows] * masks[0]
    for h in range(1, N_XHEADS):
        o = o + o_all[h * rows:(h + 1) * rows] * masks[h]
    return o


def _lru_gates(xl_bf, wg_ref, pre_s):
    pre_s[:, 0:D_LRU] = _dot(xl_bf[:, 0:GATE_HALF], wg_ref[0])
    pre_s[:, D_LRU:2 * D_LRU] = _dot(xl_bf[:, GATE_HALF:D_LRU], wg_ref[1])


def _pre_cols(l):
    half_slabs = N_LRU_SLABS // 2
    base = (l // half_slabs) * D_LRU + (l % half_slabs) * LANES
    return base, base + GATE_HALF


def _lru_coeffs(hpre_a, hpre_x, xl, hb_a, hb_x, hc_ls):
    t_r = jnp.tanh(hpre_a + hb_a)
    t_i = jnp.tanh(hpre_x + hb_x)
    log_a = hc_ls + hc_ls * t_r
    a = jnp.exp(log_a)
    one_minus_a2 = -jnp.tanh(log_a) * (a * a + 1.0)
    hx = 0.5 * xl
    u = _sqrt_nonneg(one_minus_a2) * (hx + hx * t_i)
    return a, u


def _conv_taps(src_s, l, w_row, row0, nrows, width, halo):
    acc = None
    for k in range(width):
        x_k = src_s[l, pl.ds(row0 + (halo - (width - 1) + k), nrows), :]
        acc = w_row(k) * x_k if acc is None else acc + w_row(k) * x_k
    return acc


def _layernorm_silu_slabs(ys, g_ref, b_ref):
    tot = ys[0]
    for y in ys[1:]:
        tot = tot + y
    mu = jnp.sum(tot, axis=-1, keepdims=True) * (1.0 / D_CONV)
    cs = [y - mu for y in ys]
    sq = cs[0] * cs[0]
    for c in cs[1:]:
        sq = sq + c * c
    var = jnp.sum(sq, axis=-1, keepdims=True) * (1.0 / D_CONV)
    inv = lax.rsqrt(var + EPS)
    outs = []
    for l, c in enumerate(cs):
        lo = l * LANES
        outs.append(_silu(c * inv * g_ref[:, lo:lo + LANES] + b_ref[:, lo:lo + LANES]))
    return outs


def _gate_block_diag(w_blocks):
    n = (LRU_BLOCKS // 2) * LRU_BLOCK
    stacked = _bf16(0.5 * w_blocks.reshape(n, LRU_BLOCK))
    src = lax.broadcasted_iota(jnp.int32, (LRU_BLOCK, n), 0)
    dst = lax.broadcasted_iota(jnp.int32, (LRU_BLOCK, n), 1)
    spread = _bf16(((dst & (LRU_BLOCK - 1)) == src).astype(jnp.float32))
    rows = _head_of(lax.broadcasted_iota(jnp.int32, (n, n), 0))
    cols = _head_of(lax.broadcasted_iota(jnp.int32, (n, n), 1))
    return _bf16(_dot(stacked, spread) * (rows == cols).astype(jnp.float32))


def _memkv_kernel(mem_ref, g_ref, wk_ref, wv_ref, kt_ref, vt_ref):
    m = _bf16(_rmsnorm(mem_ref[0], g_ref[...]))
    kt_ref[0] = _dot(m, _bf16(wk_ref[...])).T
    vt_ref[0] = _dot(m, _bf16(wv_ref[...])).T


def _memkv(mem, g_mem, w_k, w_v):
    bsz = mem.shape[0]
    return pl.pallas_call(
        _memkv_kernel,
        grid=(bsz,),
        in_specs=[
            pl.BlockSpec((1, N_MEM, D_MODEL), lambda b: (b, 0, 0)),
            pl.BlockSpec((1, D_MODEL), lambda b: (0, 0)),
            pl.BlockSpec((None, D_MODEL, D_XATTN), lambda b: (0, 0, 0)),
            pl.BlockSpec((None, D_MODEL, D_XATTN), lambda b: (0, 0, 0)),
        ],
        out_specs=[
            pl.BlockSpec((1, D_XATTN, N_MEM), lambda b: (b, 0, 0)),
            pl.BlockSpec((1, D_XATTN, N_MEM), lambda b: (b, 0, 0)),
        ],
        out_shape=[jax.ShapeDtypeStruct((bsz, D_XATTN, N_MEM), jnp.float32)] * 2,
        compiler_params=pltpu.CompilerParams(dimension_semantics=("arbitrary",)),
        name="memkv",
    )(mem, g_mem, w_k, w_v)


def _prompt_kernel(x_ref, kt_ref, vt_ref, g_norm_ref, w_in_ref, w_lc_ref, b_lc_ref, wga_ref, wgx_ref,
                   b_ga_ref, b_gx_ref, lam_ref, w_dw_ref, b_dw_ref, ln_g_ref, ln_b_ref,
                   w_pw_ref, b_pw_ref, w_out_ref, g_final_ref,
                   y_ref, h_out_ref, lb_out_ref, cb_out_ref, win_bf, wout_bf, wpw_bf, wg_bf,
                   z_s, xc_s, xl_s, pre_s, a_s, u_s, uc_s, ln_s, mix_s, hin_s, kst_s, vst_s):
    b = pl.program_id(0)
    t = pl.program_id(1)
    n_t = pl.num_programs(1)
    tile = PROMPT_TILE
    seg = PROMPT_SEG
    n_seg = tile // seg
    pitch = PROMPT_PITCH

    @pl.when(jnp.logical_and(b == 0, t == 0))
    def _():
        win_bf[...] = _bf16(w_in_ref[...])
        wout_bf[...] = _bf16(w_out_ref[...])
        wpw_bf[...] = _bf16(w_pw_ref[...])
        half = LRU_BLOCKS // 2
        for h in range(2):
            wg_bf[h, :, 0:GATE_HALF] = _gate_block_diag(wga_ref[h * half:(h + 1) * half])
            wg_bf[h, :, GATE_HALF:D_LRU] = _gate_block_diag(wgx_ref[h * half:(h + 1) * half])

    @pl.when(t == 0)
    def _():
        xc_s[:, 0:LRU_HALO, :] = jnp.zeros((N_LRU_SLABS, LRU_HALO, LANES), jnp.float32)
        uc_s[:, 0:CONV_HALO, :] = jnp.zeros((N_CONV_SLABS, CONV_HALO, LANES), jnp.float32)
        hin_s[:, 0:1, :] = jnp.zeros((N_LRU_SLABS, 1, LANES), jnp.float32)
        _head_stack_t(kt_ref[0], vt_ref[0], kst_s, vst_s)

    hn = _rmsnorm(x_ref[0], g_norm_ref[...])
    z_s[...] = _dot(_bf16(hn), win_bf[...])

    for l in range(N_LRU_SLABS):
        lo = l * LANES
        xc_s[l, LRU_HALO:LRU_HALO + tile, :] = z_s[:, OFF_LRU_IN + lo:OFF_LRU_IN + lo + LANES]
    for l in range(N_LRU_SLABS):
        lo = l * LANES
        w_row = lambda k, lo=lo: w_lc_ref[k:k + 1, lo:lo + LANES]
        xl = _conv_taps(xc_s, l, w_row, 0, tile, LRU_CONV, LRU_HALO) + b_lc_ref[:, lo:lo + LANES]
        xl_s[:, lo:lo + LANES] = xl
    _lru_gates(_bf16(xl_s[...]), wg_bf, pre_s)

    hc_ls = (0.5 * LRU_C) * _log_sigmoid(lam_ref[...])
    hb_ga = 0.5 * b_ga_ref[...]
    hb_gx = 0.5 * b_gx_ref[...]

    def coeff_body(c, carry):
        r0 = _aligned(c * seg, seg)
        p0 = _aligned(c * pitch, SUBLANES)
        for l in range(N_LRU_SLABS):
            lo = l * LANES
            ca, cx = _pre_cols(l)
            a, u = _lru_coeffs(pre_s[pl.ds(r0, seg), ca:ca + LANES],
                               pre_s[pl.ds(r0, seg), cx:cx + LANES],
                               xl_s[pl.ds(r0, seg), lo:lo + LANES],
                               hb_ga[:, lo:lo + LANES], hb_gx[:, lo:lo + LANES],
                               hc_ls[:, lo:lo + LANES])
            a_s[l, pl.ds(p0, seg), :] = a
            u_s[l, pl.ds(p0, seg), :] = u
        return carry

    _unrolled(n_seg, coeff_body, 0)

    def tot_body(j, carry):
        new = []
        for l in range(N_LRU_SLABS):
            a_tot, u_tot = carry[l]
            a_j = a_s[l, pl.ds(j, n_seg, stride=pitch), :]
            u_j = u_s[l, pl.ds(j, n_seg, stride=pitch), :]
            new.append((a_j * a_tot, a_j * u_tot + u_j))
        return tuple(new)

    init = tuple((jnp.ones((n_seg, LANES), jnp.float32), jnp.zeros((n_seg, LANES), jnp.float32))
                 for _ in range(N_LRU_SLABS))
    totals = _unrolled(seg, tot_body, init)

    h_in = []
    for l in range(N_LRU_SLABS):
        a_tot, u_tot = totals[l]
        h = hin_s[l, 0:1, :]
        for c in range(n_seg):
            h = a_tot[c:c + 1, :] * h + u_tot[c:c + 1, :]
            hin_s[l, c + 1:c + 2, :] = h
        h_in.append(hin_s[l, 0:n_seg, :])
        hin_s[l, 0:1, :] = h

    def scan_body(j, carry):
        new = []
        for l in range(N_LRU_SLABS):
            a_j = a_s[l, pl.ds(j, n_seg, stride=pitch), :]
            u_j = u_s[l, pl.ds(j, n_seg, stride=pitch), :]
            h = a_j * carry[l] + u_j
            u_s[l, pl.ds(j, n_seg, stride=pitch), :] = h
            new.append(h)
        return tuple(new)

    _unrolled(seg, scan_body, tuple(h_in))

    for l in range(N_CONV_SLABS):
        lo = l * LANES
        uc_s[l, CONV_HALO:CONV_HALO + tile, :] = _gated(
            z_s[:, OFF_GLU_A + lo:OFF_GLU_A + lo + LANES],
            z_s[:, OFF_GLU_B + lo:OFF_GLU_B + lo + LANES])

    def seg_body(c, carry):
        r0 = _aligned(c * seg, seg)
        p0 = _aligned(c * pitch, SUBLANES)
        for l in range(N_LRU_SLABS):
            lo = l * LANES
            gate = z_s[pl.ds(r0, seg), OFF_LRU_GATE + lo:OFF_LRU_GATE + lo + LANES]
            mix_s[pl.ds(r0, seg), lo:lo + LANES] = _bf16(u_s[l, pl.ds(p0, seg), :] * _silu(gate))
        ys = []
        for l in range(N_CONV_SLABS):
            lo = l * LANES
            w_row = lambda k, lo=lo: w_dw_ref[k, :, lo:lo + LANES]
            ys.append(_conv_taps(uc_s, l, w_row, r0, seg, CONV_W, CONV_HALO)
                      + b_dw_ref[:, lo:lo + LANES])
        outs = _layernorm_silu_slabs(ys, ln_g_ref, ln_b_ref)
        for l in range(N_CONV_SLABS):
            lo = l * LANES
            ln_s[pl.ds(r0, seg), lo:lo + LANES] = _bf16(outs[l])
        return carry

    _unrolled(n_seg, seg_body, 0)

    yc = (_dot(ln_s[...], wpw_bf[...]) + b_pw_ref[...]) * _silu(
        z_s[:, OFF_CONV_GATE:OFF_CONV_GATE + D_CONV])
    mix_s[:, D_LRU:D_LRU + D_CONV] = _bf16(yc)

    def attn_body(c, carry):
        r0 = _aligned(c * ATTN_ROWS, ATTN_ROWS)
        q = z_s[pl.ds(r0, ATTN_ROWS), OFF_Q:OFF_Q + D_XATTN]
        o = _attention(_bf16(q), kst_s, vst_s)
        gate = z_s[pl.ds(r0, ATTN_ROWS), OFF_ATTN_GATE:OFF_ATTN_GATE + D_XATTN]
        mix_s[pl.ds(r0, ATTN_ROWS), D_LRU + D_CONV:D_MIX] = _bf16(o * _silu(gate))
        return carry

    _unrolled(tile // ATTN_ROWS, attn_body, 0)

    y = x_ref[0] + _dot(mix_s[...], wout_bf[...])
    y_ref[0] = _rmsnorm(y, g_final_ref[...])

    for l in range(N_LRU_SLABS):
        xc_s[l, LRU_HALO - (LRU_CONV - 1):LRU_HALO, :] = (
            xc_s[l, LRU_HALO + tile - (LRU_CONV - 1):LRU_HALO + tile, :])
    for l in range(N_CONV_SLABS):
        uc_s[l, CONV_HALO - (CONV_W - 1):CONV_HALO, :] = (
            uc_s[l, CONV_HALO + tile - (CONV_W - 1):CONV_HALO + tile, :])

    def write_state(bb):
        for l in range(N_LRU_SLABS):
            lo = l * LANES
            h_out_ref[bb:bb + 1, lo:lo + LANES] = hin_s[l, 0:1, :]
            for k in range(LRU_CONV - 1):
                r = LRU_HALO - (LRU_CONV - 1) + k
                lb_out_ref[k, bb:bb + 1, lo:lo + LANES] = xc_s[l, r:r + 1, :]
        for l in range(N_CONV_SLABS):
            lo = l * LANES
            for k in range(CONV_W - 1):
                r = CONV_HALO - (CONV_W - 1) + k
                cb_out_ref[k, bb:bb + 1, lo:lo + LANES] = uc_s[l, r:r + 1, :]

    for bb in range(h_out_ref.shape[0]):
        pl.when(jnp.logical_and(t == n_t - 1, b == bb))(functools.partial(write_state, bb))


def _whole(shape):
    nd = len(shape)
    return pl.BlockSpec(shape, lambda *_: (0,) * nd, pipeline_mode=pl.Buffered(1))


def _prompt(x, kt, vt, w):
    bsz, seq, _ = x.shape
    tile = PROMPT_TILE
    n_t = seq // tile
    n_seg = tile // PROMPT_SEG
    weights = [w["g_norm"], w["w_in"], w["w_lc"], w["b_lc"], w["w_ga"], w["w_gx"], w["b_ga"], w["b_gx"],
               w["lam"], w["w_dw"], w["b_dw"], w["ln_g"], w["ln_b"], w["w_pw"], w["b_pw"],
               w["w_out"], w["g_final"]]
    weight_specs = [
        _whole((1, D_MODEL)), _whole((None, D_MODEL, D_IN)), _whole((None, LRU_CONV, D_LRU)),
        _whole((1, D_LRU)), _whole((None, LRU_BLOCKS, LRU_BLOCK, LRU_BLOCK)),
        _whole((None, LRU_BLOCKS, LRU_BLOCK, LRU_BLOCK)), _whole((1, D_LRU)), _whole((1, D_LRU)),
        _whole((1, D_LRU)), _whole((CONV_W, 1, D_CONV)), _whole((1, D_CONV)), _whole((1, D_CONV)),
        _whole((1, D_CONV)), _whole((None, D_CONV, D_CONV)), _whole((1, D_CONV)),
        _whole((None, D_MIX, D_MODEL)), _whole((1, D_MODEL)),
    ]
    in_specs = [
        pl.BlockSpec((1, tile, D_MODEL), lambda b, t: (b, t, 0)),
        pl.BlockSpec((1, D_XATTN, N_MEM), lambda b, t: (b, 0, 0)),
        pl.BlockSpec((1, D_XATTN, N_MEM), lambda b, t: (b, 0, 0)),
    ] + weight_specs
    out_specs = [
        pl.BlockSpec((1, tile, D_MODEL), lambda b, t: (b, t, 0)),
        _whole((bsz, D_LRU)),
        _whole((LRU_CONV - 1, bsz, D_LRU)),
        _whole((CONV_W - 1, bsz, D_CONV)),
        _whole((D_MODEL, D_IN)), _whole((D_MIX, D_MODEL)), _whole((D_CONV, D_CONV)),
        _whole((2, GATE_HALF, D_LRU)),
    ]
    out_shape = [
        jax.ShapeDtypeStruct((bsz, seq, D_MODEL), jnp.float32),
        jax.ShapeDtypeStruct((bsz, D_LRU), jnp.float32),
        jax.ShapeDtypeStruct((LRU_CONV - 1, bsz, D_LRU), jnp.float32),
        jax.ShapeDtypeStruct((CONV_W - 1, bsz, D_CONV), jnp.float32),
        jax.ShapeDtypeStruct((D_MODEL, D_IN), jnp.bfloat16),
        jax.ShapeDtypeStruct((D_MIX, D_MODEL), jnp.bfloat16),
        jax.ShapeDtypeStruct((D_CONV, D_CONV), jnp.bfloat16),
        jax.ShapeDtypeStruct((2, GATE_HALF, D_LRU), jnp.bfloat16),
    ]
    scratch = [
        pltpu.VMEM((tile, D_IN), jnp.float32),
        pltpu.VMEM((N_LRU_SLABS, LRU_HALO + tile, LANES), jnp.float32),
        pltpu.VMEM((tile, D_LRU), jnp.float32),
        pltpu.VMEM((tile, 2 * D_LRU), jnp.float32),
        pltpu.VMEM((N_LRU_SLABS, n_seg * PROMPT_PITCH, LANES), jnp.float32),
        pltpu.VMEM((N_LRU_SLABS, n_seg * PROMPT_PITCH, LANES), jnp.float32),
        pltpu.VMEM((N_CONV_SLABS, CONV_HALO + tile, LANES), jnp.float32),
        pltpu.VMEM((tile, D_CONV), jnp.bfloat16),
        pltpu.VMEM((tile, D_MIX), jnp.bfloat16),
        pltpu.VMEM((N_LRU_SLABS, 2 * SUBLANES, LANES), jnp.float32),
        pltpu.VMEM((D_XATTN, N_XHEADS * N_MEM), jnp.bfloat16),
        pltpu.VMEM((D_XATTN, N_XHEADS * N_MEM), jnp.bfloat16),
    ]
    return pl.pallas_call(
        _prompt_kernel,
        grid=(bsz, n_t),
        in_specs=in_specs,
        out_specs=out_specs,
        out_shape=out_shape,
        scratch_shapes=scratch,
        compiler_params=pltpu.CompilerParams(
            dimension_semantics=("arbitrary", "arbitrary"),
            vmem_limit_bytes=VMEM_LIMIT_BYTES),
        name="prompt_layer",
    )(x, kt, vt, *weights)


def _sample_kernel(nb, steps,
                   x_ref, h0_ref, lb_ref, cb_ref, kt_ref, vt_ref,
                   g_norm_ref, win_bf, w_lc_ref, b_lc_ref, wg_bf,
                   b_ga_ref, b_gx_ref, lam_ref, w_dw_ref, b_dw_ref, ln_g_ref, ln_b_ref,
                   wpw_bf, b_pw_ref, wout_bf, g_final_ref,
                   y_ref, h_out_ref, lb_out_ref, cb_out_ref,
                   z_s, xc_s, xl_s, pre_s, a_s, u_s, uc_s, ln_s, mix_s):
    rows = nb * steps
    pitch = SAMPLE_PITCH
    cpitch = CONV_HALO + steps
    prow = nb * pitch

    x = x_ref[...].reshape(rows, D_MODEL)
    hn = _rmsnorm(x, g_norm_ref[...])
    z_s[...] = _dot(_bf16(hn), win_bf[...])

    for b in range(nb):
        for l in range(N_LRU_SLABS):
            lo = l * LANES
            zcol = OFF_LRU_IN + lo
            xc_s[l, b * pitch:b * pitch + LRU_HALO, :] = jnp.zeros((LRU_HALO, LANES), jnp.float32)
            for k in range(LRU_CONV - 1):
                r = b * pitch + LRU_HALO - (LRU_CONV - 1) + k
                xc_s[l, r:r + 1, :] = lb_ref[k, b:b + 1, lo:lo + LANES]
                rn = (b + 1) * steps - (LRU_CONV - 1) + k
                lb_out_ref[k, b:b + 1, lo:lo + LANES] = z_s[rn:rn + 1, zcol:zcol + LANES]
            xc_s[l, b * pitch + LRU_HALO:(b + 1) * pitch, :] = (
                z_s[b * steps:(b + 1) * steps, zcol:zcol + LANES])
    n_conv = prow - LRU_HALO
    for l in range(N_LRU_SLABS):
        lo = l * LANES
        w_row = lambda k, lo=lo: w_lc_ref[k:k + 1, lo:lo + LANES]
        xl = _conv_taps(xc_s, l, w_row, 0, n_conv, LRU_CONV, LRU_HALO) + b_lc_ref[:, lo:lo + LANES]
        xl_s[0:n_conv, lo:lo + LANES] = xl
    xl_s[n_conv:prow, :] = jnp.zeros((LRU_HALO, D_LRU), jnp.float32)
    _lru_gates(_bf16(xl_s[...]), wg_bf, pre_s)

    hc_ls = (0.5 * LRU_C) * _log_sigmoid(lam_ref[...])
    hb_ga = 0.5 * b_ga_ref[...]
    hb_gx = 0.5 * b_gx_ref[...]
    for l in range(N_LRU_SLABS):
        lo = l * LANES
        ca, cx = _pre_cols(l)
        a, u = _lru_coeffs(pre_s[:, ca:ca + LANES], pre_s[:, cx:cx + LANES], xl_s[:, lo:lo + LANES],
                           hb_ga[:, lo:lo + LANES], hb_gx[:, lo:lo + LANES],
                           hc_ls[:, lo:lo + LANES])
        a_s[l, :, :] = a
        u_s[l, :, :] = u

    n_grp = nb // SUBLANES

    def scan_body(j, carry):
        new = []
        for g in range(n_grp):
            for l in range(N_LRU_SLABS):
                base = g * SUBLANES * pitch
                a_j = a_s[l, pl.ds(base + j, SUBLANES, stride=pitch), :]
                u_j = u_s[l, pl.ds(base + j, SUBLANES, stride=pitch), :]
                h = a_j * carry[g * N_LRU_SLABS + l] + u_j
                u_s[l, pl.ds(base + j, SUBLANES, stride=pitch), :] = h
                new.append(h)
        return tuple(new)

    h0 = tuple(h0_ref[g * SUBLANES:(g + 1) * SUBLANES, l * LANES:(l + 1) * LANES]
               for g in range(n_grp) for l in range(N_LRU_SLABS))
    h_fin = _unrolled(steps, scan_body, h0)
    for g in range(n_grp):
        for l in range(N_LRU_SLABS):
            h_out_ref[g * SUBLANES:(g + 1) * SUBLANES, l * LANES:(l + 1) * LANES] = (
                h_fin[g * N_LRU_SLABS + l])

    for b in range(nb):
        for l in range(N_LRU_SLABS):
            lo = l * LANES
            gate = z_s[b * steps:(b + 1) * steps, OFF_LRU_GATE + lo:OFF_LRU_GATE + lo + LANES]
            mix_s[b * steps:(b + 1) * steps, lo:lo + LANES] = _bf16(
                u_s[l, b * pitch:b * pitch + steps, :] * _silu(gate))

    for b in range(nb):
        for l in range(N_CONV_SLABS):
            lo = l * LANES
            uc = _gated(z_s[b * steps:(b + 1) * steps, OFF_GLU_A + lo:OFF_GLU_A + lo + LANES],
                        z_s[b * steps:(b + 1) * steps, OFF_GLU_B + lo:OFF_GLU_B + lo + LANES])
            uc_s[l, b * cpitch + CONV_HALO:(b + 1) * cpitch, :] = uc
            for k in range(CONV_W - 1):
                r = b * cpitch + CONV_HALO - (CONV_W - 1) + k
                uc_s[l, r:r + 1, :] = cb_ref[k, b:b + 1, lo:lo + LANES]
                rn = (b + 1) * cpitch - (CONV_W - 1) + k
                cb_out_ref[k, b:b + 1, lo:lo + LANES] = uc_s[l, rn:rn + 1, :]

    def conv_body(i, carry):
        for j in range(SAMPLE_UNROLL):
            b = i * SAMPLE_UNROLL + j
            r0 = pl.multiple_of(b * cpitch, SUBLANES)
            o0 = pl.multiple_of(b * steps, steps)
            ys = []
            for l in range(N_CONV_SLABS):
                lo = l * LANES
                w_row = lambda k, lo=lo: w_dw_ref[k, :, lo:lo + LANES]
                ys.append(_conv_taps(uc_s, l, w_row, r0, steps, CONV_W, CONV_HALO)
                          + b_dw_ref[:, lo:lo + LANES])
            outs = _layernorm_silu_slabs(ys, ln_g_ref, ln_b_ref)
            for l in range(N_CONV_SLABS):
                lo = l * LANES
                ln_s[pl.ds(o0, steps), lo:lo + LANES] = _bf16(outs[l])
        return carry

    lax.fori_loop(0, nb // SAMPLE_UNROLL, conv_body, 0)

    yc = (_dot(ln_s[...], wpw_bf[...]) + b_pw_ref[...]) * _silu(
        z_s[:, OFF_CONV_GATE:OFF_CONV_GATE + D_CONV])
    mix_s[:, D_LRU:D_LRU + D_CONV] = _bf16(yc)

    masks = _lane_head_masks()

    def attn_body(i, carry):
        for j in range(SAMPLE_UNROLL):
            b = i * SAMPLE_UNROLL + j
            o0 = pl.multiple_of(b * steps, steps)
            q = z_s[pl.ds(o0, steps), OFF_Q:OFF_Q + D_XATTN]
            o = _attention_qstack(q, _bf16(kt_ref[b]), _bf16(vt_ref[b]), masks)
            gate = z_s[pl.ds(o0, steps), OFF_ATTN_GATE:OFF_ATTN_GATE + D_XATTN]
            mix_s[pl.ds(o0, steps), D_LRU + D_CONV:D_MIX] = _bf16(o * _silu(gate))
        return carry

    lax.fori_loop(0, nb // SAMPLE_UNROLL, attn_body, 0)

    y = x + _dot(mix_s[...], wout_bf[...])
    y_ref[...] = _rmsnorm(y, g_final_ref[...]).reshape(nb, steps, D_MODEL)


def _sample(x, h0, lb, cb, kt, vt, w, win_bf, wout_bf, wpw_bf, wg_bf):
    bsz, steps, _ = x.shape
    nb = SAMPLE_GROUP
    assert bsz % nb == 0 and nb % SUBLANES == 0 and nb % SAMPLE_UNROLL == 0
    assert steps + LRU_HALO == SAMPLE_PITCH and steps >= CONV_W - 1
    rows = nb * steps
    prow = nb * SAMPLE_PITCH
    weights = [w["g_norm"], win_bf, w["w_lc"], w["b_lc"], wg_bf, w["b_ga"], w["b_gx"],
               w["lam"], w["w_dw"], w["b_dw"], w["ln_g"], w["ln_b"], wpw_bf, w["b_pw"],
               wout_bf, w["g_final"]]
    weight_specs = [
        _whole((1, D_MODEL)), _whole((D_MODEL, D_IN)), _whole((None, LRU_CONV, D_LRU)),
        _whole((1, D_LRU)), _whole((2, GATE_HALF, D_LRU)), _whole((1, D_LRU)), _whole((1, D_LRU)),
        _whole((1, D_LRU)), _whole((CONV_W, 1, D_CONV)), _whole((1, D_CONV)), _whole((1, D_CONV)),
        _whole((1, D_CONV)), _whole((D_CONV, D_CONV)), _whole((1, D_CONV)),
        _whole((D_MIX, D_MODEL)), _whole((1, D_MODEL)),
    ]
    in_specs = [
        pl.BlockSpec((nb, steps, D_MODEL), lambda g: (g, 0, 0)),
        pl.BlockSpec((nb, D_LRU), lambda g: (g, 0)),
        pl.BlockSpec((LRU_CONV - 1, nb, D_LRU), lambda g: (0, g, 0)),
        pl.BlockSpec((CONV_W - 1, nb, D_CONV), lambda g: (0, g, 0)),
        pl.BlockSpec((nb, D_XATTN, N_MEM), lambda g: (g, 0, 0)),
        pl.BlockSpec((nb, D_XATTN, N_MEM), lambda g: (g, 0, 0)),
    ] + weight_specs
    out_specs = [
        pl.BlockSpec((nb, steps, D_MODEL), lambda g: (g, 0, 0)),
        pl.BlockSpec((nb, D_LRU), lambda g: (g, 0)),
        pl.BlockSpec((LRU_CONV - 1, nb, D_LRU), lambda g: (0, g, 0)),
        pl.BlockSpec((CONV_W - 1, nb, D_CONV), lambda g: (0, g, 0)),
    ]
    out_shape = [
        jax.ShapeDtypeStruct((bsz, steps, D_MODEL), jnp.float32),
        jax.ShapeDtypeStruct((bsz, D_LRU), jnp.float32),
        jax.ShapeDtypeStruct((LRU_CONV - 1, bsz, D_LRU), jnp.float32),
        jax.ShapeDtypeStruct((CONV_W - 1, bsz, D_CONV), jnp.float32),
    ]
    scratch = [
        pltpu.VMEM((rows, D_IN), jnp.float32),
        pltpu.VMEM((N_LRU_SLABS, prow, LANES), jnp.float32),
        pltpu.VMEM((prow, D_LRU), jnp.float32),
        pltpu.VMEM((prow, 2 * D_LRU), jnp.float32),
        pltpu.VMEM((N_LRU_SLABS, prow, LANES), jnp.float32),
        pltpu.VMEM((N_LRU_SLABS, prow, LANES), jnp.float32),
        pltpu.VMEM((N_CONV_SLABS, nb * (CONV_HALO + steps), LANES), jnp.float32),
        pltpu.VMEM((rows, D_CONV), jnp.bfloat16),
        pltpu.VMEM((rows, D_MIX), jnp.bfloat16),
    ]
    return pl.pallas_call(
        functools.partial(_sample_kernel, nb, steps),
        grid=(bsz // nb,),
        in_specs=in_specs,
        out_specs=out_specs,
        out_shape=out_shape,
        scratch_shapes=scratch,
        compiler_params=pltpu.CompilerParams(
            dimension_semantics=("arbitrary",),
            vmem_limit_bytes=VMEM_LIMIT_BYTES),
        name="sample_layer",
    )(x, h0, lb, cb, kt, vt, *weights)


def _time_major(state):
    return jnp.transpose(state, (1, 0, 2))


def _memory_t(mem):
    bsz = mem.shape[0]
    return jnp.transpose(mem, (0, 2, 3, 1)).reshape(bsz, D_XATTN, N_MEM)


def _memory_from_t(mem_t):
    bsz = mem_t.shape[0]
    return jnp.transpose(mem_t.reshape(bsz, N_XHEADS, XHEAD_DIM, N_MEM), (0, 3, 1, 2))


def kernel(x_prompt, x_sample, state_lru_h, state_lru_conv, state_conv, cache_mem_k, cache_mem_v,
           mem_prompt, g_norm, w_in, w_lru_conv, b_lru_conv, w_gate_a, b_gate_a, w_gate_x, b_gate_x,
           lru_lambda, w_dw, b_dw, ln_g, ln_b, w_pw, b_pw, g_mem, w_mem_k, w_mem_v, w_out, g_final):
    depth = g_norm.shape[0]
    assert depth == 1
    w = {
        "g_norm": g_norm, "w_in": w_in, "w_lc": w_lru_conv, "b_lc": b_lru_conv,
        "w_ga": w_gate_a, "w_gx": w_gate_x, "b_ga": b_gate_a, "b_gx": b_gate_x, "lam": lru_lambda,
        "w_dw": jnp.transpose(w_dw, (1, 0, 2)), "b_dw": b_dw, "ln_g": ln_g, "ln_b": ln_b,
        "w_pw": w_pw, "b_pw": b_pw, "w_out": w_out, "g_final": g_final.reshape(1, D_MODEL),
    }
    mkt, mvt = _memkv(mem_prompt, g_mem, w_mem_k, w_mem_v)

    y_prompt, ph, plb, pcb, win_bf, wout_bf, wpw_bf, wg_bf = _prompt(x_prompt, mkt, mvt, w)

    y_sample, sh, slb, scb = _sample(
        x_sample, state_lru_h[0], _time_major(state_lru_conv[0]), _time_major(state_conv[0]),
        _memory_t(cache_mem_k[0]), _memory_t(cache_mem_v[0]), w, win_bf, wout_bf, wpw_bf, wg_bf)

    return (y_prompt, y_sample,
            ph[None], _time_major(plb)[None], _time_major(pcb)[None],
            _memory_from_t(mkt)[None], _memory_from_t(mvt)[None],
            sh[None], _time_major(slb)[None], _time_major(scb)[None])
```

```python
import functools

import jax
import jax.numpy as jnp
from jax import lax
from jax.experimental import pallas as pl
from jax.experimental.pallas import tpu as pltpu

D_MODEL = 1024
N_MEM = 256
D_LRU = 512
LRU_BLOCKS = 8
LRU_BLOCK = D_LRU // LRU_BLOCKS
LRU_CONV = 4
LRU_C = 8.0
D_CONV = 256
CONV_W = 31
N_XHEADS = 4
XHEAD_DIM = 64
D_XATTN = N_XHEADS * XHEAD_DIM
D_MIX = D_LRU + D_CONV + D_XATTN
D_IN = 2 * D_LRU + 3 * D_CONV + 2 * D_XATTN
EPS = 1e-6

LANES = 128
SUBLANES = 8
N_LRU_SLABS = D_LRU // LANES
N_CONV_SLABS = D_CONV // LANES
GATE_HALF = D_LRU // 2

OFF_LRU_IN = 0
OFF_LRU_GATE = D_LRU
OFF_GLU_A = 2 * D_LRU
OFF_GLU_B = OFF_GLU_A + D_CONV
OFF_CONV_GATE = OFF_GLU_B + D_CONV
OFF_Q = OFF_CONV_GATE + D_CONV
OFF_ATTN_GATE = OFF_Q + D_XATTN

LRU_HALO = 8
CONV_HALO = 32

PROMPT_TILE = 512
PROMPT_SEG = 64
PROMPT_PITCH = 72
SAMPLE_PITCH = 40
SAMPLE_GROUP = 16
SAMPLE_UNROLL = 4
ATTN_ROWS = 128

VMEM_LIMIT_BYTES = 56 * 1024 * 1024
HALF_LOG2_E = 0.7213475204444817


def _silu(x):
    hx = 0.5 * x
    return hx + hx * jnp.tanh(hx)


def _gated(a, b):
    ha = 0.5 * a
    return ha + ha * jnp.tanh(0.5 * b)


def _sqrt_nonneg(x):
    return jnp.exp2(jnp.log(x) * HALF_LOG2_E)


def _rmsnorm(x, g):
    ms = jnp.mean(x * x, axis=-1, keepdims=True)
    return x * lax.rsqrt(ms + EPS) * g


def _log_sigmoid(x):
    y = -x
    return -(jnp.maximum(y, 0.0) + jnp.log1p(jnp.exp(-jnp.abs(y))))


def _bf16(x):
    return x.astype(jnp.bfloat16)


def _dot(a, b):
    return jnp.dot(a, b, preferred_element_type=jnp.float32)


def _dot_nt(a, b):
    return lax.dot_general(a, b, (((1,), (1,)), ((), ())), preferred_element_type=jnp.float32)


def _unrolled(n, body, init):
    carry = init
    for i in range(n):
        carry = body(i, carry)
    return carry


def _aligned(x, m):
    return x if isinstance(x, int) else pl.multiple_of(x, m)


def _head_of(index):
    return lax.shift_right_logical(index, XHEAD_DIM.bit_length() - 1)


def _lane_head_masks():
    head = _head_of(lax.broadcasted_iota(jnp.int32, (1, D_XATTN), 1))
    return [(head == h).astype(jnp.float32) for h in range(N_XHEADS)]


def _row_head_masks():
    head = _head_of(lax.broadcasted_iota(jnp.int32, (D_XATTN, 1), 0))
    return [(head == h).astype(jnp.float32) for h in range(N_XHEADS)]


def _head_stack_t(kt, vt, kst_ref, vst_ref):
    scale = XHEAD_DIM ** -0.5
    for h, m in enumerate(_row_head_masks()):
        kst_ref[:, h * N_MEM:(h + 1) * N_MEM] = _bf16(kt * (m * scale))
        vst_ref[:, h * N_MEM:(h + 1) * N_MEM] = _bf16(vt * m)


def _attention(q_bf, kst_ref, vst_ref):
    s = _dot(q_bf, kst_ref[...])
    ps = []
    for h in range(N_XHEADS):
        sh = s[:, h * N_MEM:(h + 1) * N_MEM]
        e = jnp.exp(sh - jnp.max(sh, axis=-1, keepdims=True))
        inv = 1.0 / jnp.sum(e, axis=-1, keepdims=True)
        ps.append(_bf16(e * inv))
    return _dot_nt(jnp.concatenate(ps, axis=1), vst_ref[...])


def _attention_qstack(q, kt_bf, vt_bf, masks):
    rows = q.shape[0]
    scale = XHEAD_DIM ** -0.5
    qm = jnp.concatenate([_bf16(q * (m * scale)) for m in masks], axis=0)
    s = _dot(qm, kt_bf)
    e = jnp.exp(s - jnp.max(s, axis=-1, keepdims=True))
    inv = 1.0 / jnp.sum(e, axis=-1, keepdims=True)
    o_all = _dot_nt(_bf16(e * inv), vt_bf)
    o = o_all[0:rows] * masks[0]
    for h in range(1, N_XHEADS):
        o = o + o_all[h * rows:(h + 1) * rows] * masks[h]
    return o


def _lru_gates(xl_bf, wg_ref, pre_s):
    pre_s[:, 0:D_LRU] = _dot(xl_bf[:, 0:GATE_HALF], wg_ref[0])
    pre_s[:, D_LRU:2 * D_LRU] = _dot(xl_bf[:, GATE_HALF:D_LRU], wg_ref[1])


def _pre_cols(l):
    half_slabs = N_LRU_SLABS // 2
    base = (l // half_slabs) * D_LRU + (l % half_slabs) * LANES
    return base, base + GATE_HALF


def _lru_coeffs(hpre_a, hpre_x, xl, hb_a, hb_x, hc_ls):
    t_r = jnp.tanh(hpre_a + hb_a)
    t_i = jnp.tanh(hpre_x + hb_x)
    log_a = hc_ls + hc_ls * t_r
    a = jnp.exp(log_a)
    one_minus_a2 = -jnp.tanh(log_a) * (a * a + 1.0)
    hx = 0.5 * xl
    u = _sqrt_nonneg(one_minus_a2) * (hx + hx * t_i)
    return a, u


def _conv_taps(src_s, l, w_row, row0, nrows, width, halo):
    acc = None
    for k in range(width):
        x_k = src_s[l, pl.ds(row0 + (halo - (width - 1) + k), nrows), :]
        acc = w_row(k) * x_k if acc is None else acc + w_row(k) * x_k
    return acc


def _layernorm_silu_slabs(ys, g_ref, b_ref):
    tot = ys[0]
    for y in ys[1:]:
        tot = tot + y
    mu = jnp.sum(tot, axis=-1, keepdims=True) * (1.0 / D_CONV)
    cs = [y - mu for y in ys]
    sq = cs[0] * cs[0]
    for c in cs[1:]:
        sq = sq + c * c
    var = jnp.sum(sq, axis=-1, keepdims=True) * (1.0 / D_CONV)
    inv = lax.rsqrt(var + EPS)
    outs = []
    for l, c in enumerate(cs):
        lo = l * LANES
        outs.append(_silu(c * inv * g_ref[:, lo:lo + LANES] + b_ref[:, lo:lo + LANES]))
    return outs


def _gate_block_diag(w_blocks):
    n = (LRU_BLOCKS // 2) * LRU_BLOCK
    stacked = _bf16(0.5 * w_blocks.reshape(n, LRU_BLOCK))
    src = lax.broadcasted_iota(jnp.int32, (LRU_BLOCK, n), 0)
    dst = lax.broadcasted_iota(jnp.int32, (LRU_BLOCK, n), 1)
    spread = _bf16(((dst & (LRU_BLOCK - 1)) == src).astype(jnp.float32))
    rows = _head_of(lax.broadcasted_iota(jnp.int32, (n, n), 0))
    cols = _head_of(lax.broadcasted_iota(jnp.int32, (n, n), 1))
    return _bf16(_dot(stacked, spread) * (rows == cols).astype(jnp.float32))


def _memkv_kernel(mem_ref, g_ref, wk_ref, wv_ref, kt_ref, vt_ref):
    m = _bf16(_rmsnorm(mem_ref[0], g_ref[...]))
    kt_ref[0] = _dot(m, _bf16(wk_ref[...])).T
    vt_ref[0] = _dot(m, _bf16(wv_ref[...])).T


def _memkv(mem, g_mem, w_k, w_v):
    bsz = mem.shape[0]
    return pl.pallas_call(
        _memkv_kernel,
        grid=(bsz,),
        in_specs=[
            pl.BlockSpec((1, N_MEM, D_MODEL), lambda b: (b, 0, 0)),
            pl.BlockSpec((1, D_MODEL), lambda b: (0, 0)),
            pl.BlockSpec((None, D_MODEL, D_XATTN), lambda b: (0, 0, 0)),
            pl.BlockSpec((None, D_MODEL, D_XATTN), lambda b: (0, 0, 0)),
        ],
        out_specs=[
            pl.BlockSpec((1, D_XATTN, N_MEM), lambda b: (b, 0, 0)),
            pl.BlockSpec((1, D_XATTN, N_MEM), lambda b: (b, 0, 0)),
        ],
        out_shape=[jax.ShapeDtypeStruct((bsz, D_XATTN, N_MEM), jnp.float32)] * 2,
        compiler_params=pltpu.CompilerParams(dimension_semantics=("arbitrary",)),
        name="memkv",
    )(mem, g_mem, w_k, w_v)


def _prompt_kernel(x_ref, kt_ref, vt_ref, g_norm_ref, w_in_ref, w_lc_ref, b_lc_ref, wga_ref, wgx_ref,
                   b_ga_ref, b_gx_ref, lam_ref, w_dw_ref, b_dw_ref, ln_g_ref, ln_b_ref,
                   w_pw_ref, b_pw_ref, w_out_ref, g_final_ref,
                   y_ref, h_out_ref, lb_out_ref, cb_out_ref, win_bf, wout_bf, wpw_bf, wg_bf,
                   z_s, xc_s, xl_s, pre_s, a_s, u_s, uc_s, ln_s, mix_s, hin_s, kst_s, vst_s):
    b = pl.program_id(0)
    t = pl.program_id(1)
    n_t = pl.num_programs(1)
    tile = PROMPT_TILE
    seg = PROMPT_SEG
    n_seg = tile // seg
    pitch = PROMPT_PITCH

    @pl.when(jnp.logical_and(b == 0, t == 0))
    def _():
        win_bf[...] = _bf16(w_in_ref[...])
        wout_bf[...] = _bf16(w_out_ref[...])
        wpw_bf[...] = _bf16(w_pw_ref[...])
        half = LRU_BLOCKS // 2
        for h in range(2):
            wg_bf[h, :, 0:GATE_HALF] = _gate_block_diag(wga_ref[h * half:(h + 1) * half])
            wg_bf[h, :, GATE_HALF:D_LRU] = _gate_block_diag(wgx_ref[h * half:(h + 1) * half])

    @pl.when(t == 0)
    def _():
        xc_s[:, 0:LRU_HALO, :] = jnp.zeros((N_LRU_SLABS, LRU_HALO, LANES), jnp.float32)
        uc_s[:, 0:CONV_HALO, :] = jnp.zeros((N_CONV_SLABS, CONV_HALO, LANES), jnp.float32)
        hin_s[:, 0:1, :] = jnp.zeros((N_LRU_SLABS, 1, LANES), jnp.float32)
        _head_stack_t(kt_ref[0], vt_ref[0], kst_s, vst_s)

    hn = _rmsnorm(x_ref[0], g_norm_ref[...])
    z_s[...] = _dot(_bf16(hn), win_bf[...])

    for l in range(N_LRU_SLABS):
        lo = l * LANES
        xc_s[l, LRU_HALO:LRU_HALO + tile, :] = z_s[:, OFF_LRU_IN + lo:OFF_LRU_IN + lo + LANES]
    for l in range(N_LRU_SLABS):
        lo = l * LANES
        w_row = lambda k, lo=lo: w_lc_ref[k:k + 1, lo:lo + LANES]
        xl = _conv_taps(xc_s, l, w_row, 0, tile, LRU_CONV, LRU_HALO) + b_lc_ref[:, lo:lo + LANES]
        xl_s[:, lo:lo + LANES] = xl
    _lru_gates(_bf16(xl_s[...]), wg_bf, pre_s)

    hc_ls = (0.5 * LRU_C) * _log_sigmoid(lam_ref[...])
    hb_ga = 0.5 * b_ga_ref[...]
    hb_gx = 0.5 * b_gx_ref[...]

    def coeff_body(c, carry):
        r0 = _aligned(c * seg, seg)
        p0 = _aligned(c * pitch, SUBLANES)
        for l in range(N_LRU_SLABS):
            lo = l * LANES
            ca, cx = _pre_cols(l)
            a, u = _lru_coeffs(pre_s[pl.ds(r0, seg), ca:ca + LANES],
                               pre_s[pl.ds(r0, seg), cx:cx + LANES],
                               xl_s[pl.ds(r0, seg), lo:lo + LANES],
                               hb_ga[:, lo:lo + LANES], hb_gx[:, lo:lo + LANES],
                               hc_ls[:, lo:lo + LANES])
            a_s[l, pl.ds(p0, seg), :] = a
            u_s[l, pl.ds(p0, seg), :] = u
        return carry

    _unrolled(n_seg, coeff_body, 0)

    def tot_body(j, carry):
        new = []
        for l in range(N_LRU_SLABS):
            a_tot, u_tot = carry[l]
            a_j = a_s[l, pl.ds(j, n_seg, stride=pitch), :]
            u_j = u_s[l, pl.ds(j, n_seg, stride=pitch), :]
            new.append((a_j * a_tot, a_j * u_tot + u_j))
        return tuple(new)

    init = tuple((jnp.ones((n_seg, LANES), jnp.float32), jnp.zeros((n_seg, LANES), jnp.float32))
                 for _ in range(N_LRU_SLABS))
    totals = _unrolled(seg, tot_body, init)

    h_in = []
    for l in range(N_LRU_SLABS):
        a_tot, u_tot = totals[l]
        h = hin_s[l, 0:1, :]
        for c in range(n_seg):
            h = a_tot[c:c + 1, :] * h + u_tot[c:c + 1, :]
            hin_s[l, c + 1:c + 2, :] = h
        h_in.append(hin_s[l, 0:n_seg, :])
        hin_s[l, 0:1, :] = h

    def scan_body(j, carry):
        new = []
        for l in range(N_LRU_SLABS):
            a_j = a_s[l, pl.ds(j, n_seg, stride=pitch), :]
            u_j = u_s[l, pl.ds(j, n_seg, stride=pitch), :]
            h = a_j * carry[l] + u_j
            u_s[l, pl.ds(j, n_seg, stride=pitch), :] = h
            new.append(h)
        return tuple(new)

    _unrolled(seg, scan_body, tuple(h_in))

    for l in range(N_CONV_SLABS):
        lo = l * LANES
        uc_s[l, CONV_HALO:CONV_HALO + tile, :] = _gated(
            z_s[:, OFF_GLU_A + lo:OFF_GLU_A + lo + LANES],
            z_s[:, OFF_GLU_B + lo:OFF_GLU_B + lo + LANES])

    def seg_body(c, carry):
        r0 = _aligned(c * seg, seg)
        p0 = _aligned(c * pitch, SUBLANES)
        for l in range(N_LRU_SLABS):
            lo = l * LANES
            gate = z_s[pl.ds(r0, seg), OFF_LRU_GATE + lo:OFF_LRU_GATE + lo + LANES]
            mix_s[pl.ds(r0, seg), lo:lo + LANES] = _bf16(u_s[l, pl.ds(p0, seg), :] * _silu(gate))
        ys = []
        for l in range(N_CONV_SLABS):
            lo = l * LANES
            w_row = lambda k, lo=lo: w_dw_ref[k, :, lo:lo + LANES]
            ys.append(_conv_taps(uc_s, l, w_row, r0, seg, CONV_W, CONV_HALO)
                      + b_dw_ref[:, lo:lo + LANES])
        outs = _layernorm_silu_slabs(ys, ln_g_ref, ln_b_ref)
        for l in range(N_CONV_SLABS):
            lo = l * LANES
            ln_s[pl.ds(r0, seg), lo:lo + LANES] = _bf16(outs[l])
        return carry

    _unrolled(n_seg, seg_body, 0)

    yc = (_dot(ln_s[...], wpw_bf[...]) + b_pw_ref[...]) * _silu(
        z_s[:, OFF_CONV_GATE:OFF_CONV_GATE + D_CONV])
    mix_s[:, D_LRU:D_LRU + D_CONV] = _bf16(yc)

    def attn_body(c, carry):
        r0 = _aligned(c * ATTN_ROWS, ATTN_ROWS)
        q = z_s[pl.ds(r0, ATTN_ROWS), OFF_Q:OFF_Q + D_XATTN]
        o = _attention(_bf16(q), kst_s, vst_s)
        gate = z_s[pl.ds(r0, ATTN_ROWS), OFF_ATTN_GATE:OFF_ATTN_GATE + D_XATTN]
        mix_s[pl.ds(r0, ATTN_ROWS), D_LRU + D_CONV:D_MIX] = _bf16(o * _silu(gate))
        return carry

    _unrolled(tile // ATTN_ROWS, attn_body, 0)

    y = x_ref[0] + _dot(mix_s[...], wout_bf[...])
    y_ref[0] = _rmsnorm(y, g_final_ref[...])

    for l in range(N_LRU_SLABS):
        xc_s[l, LRU_HALO - (LRU_CONV - 1):LRU_HALO, :] = (
            xc_s[l, LRU_HALO + tile - (LRU_CONV - 1):LRU_HALO + tile, :])
    for l in range(N_CONV_SLABS):
        uc_s[l, CONV_HALO - (CONV_W - 1):CONV_HALO, :] = (
            uc_s[l, CONV_HALO + tile - (CONV_W - 1):CONV_HALO + tile, :])

    def write_state(bb):
        for l in range(N_LRU_SLABS):
            lo = l * LANES
            h_out_ref[bb:bb + 1, lo:lo + LANES] = hin_s[l, 0:1, :]
            for k in range(LRU_CONV - 1):
                r = LRU_HALO - (LRU_CONV - 1) + k
                lb_out_ref[k, bb:bb + 1, lo:lo + LANES] = xc_s[l, r:r + 1, :]
        for l in range(N_CONV_SLABS):
            lo = l * LANES
            for k in range(CONV_W - 1):
                r = CONV_HALO - (CONV_W - 1) + k
                cb_out_ref[k, bb:bb + 1, lo:lo + LANES] = uc_s[l, r:r + 1, :]

    for bb in range(h_out_ref.shape[0]):
        pl.when(jnp.logical_and(t == n_t - 1, b == bb))(functools.partial(write_state, bb))


def _whole(shape):
    nd = len(shape)
    return pl.BlockSpec(shape, lambda *_: (0,) * nd, pipeline_mode=pl.Buffered(1))


def _prompt(x, kt, vt, w):
    bsz, seq, _ = x.shape
    tile = PROMPT_TILE
    n_t = seq // tile
    n_seg = tile // PROMPT_SEG
    weights = [w["g_norm"], w["w_in"], w["w_lc"], w["b_lc"], w["w_ga"], w["w_gx"], w["b_ga"], w["b_gx"],
               w["lam"], w["w_dw"], w["b_dw"], w["ln_g"], w["ln_b"], w["w_pw"], w["b_pw"],
               w["w_out"], w["g_final"]]
    weight_specs = [
        _whole((1, D_MODEL)), _whole((None, D_MODEL, D_IN)), _whole((None, LRU_CONV, D_LRU)),
        _whole((1, D_LRU)), _whole((None, LRU_BLOCKS, LRU_BLOCK, LRU_BLOCK)),
        _whole((None, LRU_BLOCKS, LRU_BLOCK, LRU_BLOCK)), _whole((1, D_LRU)), _whole((1, D_LRU)),
        _whole((1, D_LRU)), _whole((CONV_W, 1, D_CONV)), _whole((1, D_CONV)), _whole((1, D_CONV)),
        _whole((1, D_CONV)), _whole((None, D_CONV, D_CONV)), _whole((1, D_CONV)),
        _whole((None, D_MIX, D_MODEL)), _whole((1, D_MODEL)),
    ]
    in_specs = [
        pl.BlockSpec((1, tile, D_MODEL), lambda b, t: (b, t, 0)),
        pl.BlockSpec((1, D_XATTN, N_MEM), lambda b, t: (b, 0, 0)),
        pl.BlockSpec((1, D_XATTN, N_MEM), lambda b, t: (b, 0, 0)),
    ] + weight_specs
    out_specs = [
        pl.BlockSpec((1, tile, D_MODEL), lambda b, t: (b, t, 0)),
        _whole((bsz, D_LRU)),
        _whole((LRU_CONV - 1, bsz, D_LRU)),
        _whole((CONV_W - 1, bsz, D_CONV)),
        _whole((D_MODEL, D_IN)), _whole((D_MIX, D_MODEL)), _whole((D_CONV, D_CONV)),
        _whole((2, GATE_HALF, D_LRU)),
    ]
    out_shape = [
        jax.ShapeDtypeStruct((bsz, seq, D_MODEL), jnp.float32),
        jax.ShapeDtypeStruct((bsz, D_LRU), jnp.float32),
        jax.ShapeDtypeStruct((LRU_CONV - 1, bsz, D_LRU), jnp.float32),
        jax.ShapeDtypeStruct((CONV_W - 1, bsz, D_CONV), jnp.float32),
        jax.ShapeDtypeStruct((D_MODEL, D_IN), jnp.bfloat16),
        jax.ShapeDtypeStruct((D_MIX, D_MODEL), jnp.bfloat16),
        jax.ShapeDtypeStruct((D_CONV, D_CONV), jnp.bfloat16),
        jax.ShapeDtypeStruct((2, GATE_HALF, D_LRU), jnp.bfloat16),
    ]
    scratch = [
        pltpu.VMEM((tile, D_IN), jnp.float32),
        pltpu.VMEM((N_LRU_SLABS, LRU_HALO + tile, LANES), jnp.float32),
        pltpu.VMEM((tile, D_LRU), jnp.float32),
        pltpu.VMEM((tile, 2 * D_LRU), jnp.float32),
        pltpu.VMEM((N_LRU_SLABS, n_seg * PROMPT_PITCH, LANES), jnp.float32),
        pltpu.VMEM((N_LRU_SLABS, n_seg * PROMPT_PITCH, LANES), jnp.float32),
        pltpu.VMEM((N_CONV_SLABS, CONV_HALO + tile, LANES), jnp.float32),
        pltpu.VMEM((tile, D_CONV), jnp.bfloat16),
        pltpu.VMEM((tile, D_MIX), jnp.bfloat16),
        pltpu.VMEM((N_LRU_SLABS, 2 * SUBLANES, LANES), jnp.float32),
        pltpu.VMEM((D_XATTN, N_XHEADS * N_MEM), jnp.bfloat16),
        pltpu.VMEM((D_XATTN, N_XHEADS * N_MEM), jnp.bfloat16),
    ]
    return pl.pallas_call(
        _prompt_kernel,
        grid=(bsz, n_t),
        in_specs=in_specs,
        out_specs=out_specs,
        out_shape=out_shape,
        scratch_shapes=scratch,
        compiler_params=pltpu.CompilerParams(
            dimension_semantics=("arbitrary", "arbitrary"),
            vmem_limit_bytes=VMEM_LIMIT_BYTES),
        name="prompt_layer",
    )(x, kt, vt, *weights)


def _sample_kernel(nb, steps,
                   x_ref, h0_ref, lb_ref, cb_ref, kt_ref, vt_ref,
                   g_norm_ref, win_bf, w_lc_ref, b_lc_ref, wg_bf,
                   b_ga_ref, b_gx_ref, lam_ref, w_dw_ref, b_dw_ref, ln_g_ref, ln_b_ref,
                   wpw_bf, b_pw_ref, wout_bf, g_final_ref,
                   y_ref, h_out_ref, lb_out_ref, cb_out_ref,
                   z_s, xc_s, xl_s, pre_s, a_s, u_s, uc_s, ln_s, mix_s):
    rows = nb * steps
    pitch = SAMPLE_PITCH
    cpitch = CONV_HALO + steps
    prow = nb * pitch

    x = x_ref[...].reshape(rows, D_MODEL)
    hn = _rmsnorm(x, g_norm_ref[...])
    z_s[...] = _dot(_bf16(hn), win_bf[...])

    for b in range(nb):
        for l in range(N_LRU_SLABS):
            lo = l * LANES
            zcol = OFF_LRU_IN + lo
            xc_s[l, b * pitch:b * pitch + LRU_HALO, :] = jnp.zeros((LRU_HALO, LANES), jnp.float32)
            for k in range(LRU_CONV - 1):
                r = b * pitch + LRU_HALO - (LRU_CONV - 1) + k
                xc_s[l, r:r + 1, :] = lb_ref[k, b:b + 1, lo:lo + LANES]
                rn = (b + 1) * steps - (LRU_CONV - 1) + k
                lb_out_ref[k, b:b + 1, lo:lo + LANES] = z_s[rn:rn + 1, zcol:zcol + LANES]
            xc_s[l, b * pitch + LRU_HALO:(b + 1) * pitch, :] = (
                z_s[b * steps:(b + 1) * steps, zcol:zcol + LANES])
    n_conv = prow - LRU_HALO
    for l in range(N_LRU_SLABS):
        lo = l * LANES
        w_row = lambda k, lo=lo: w_lc_ref[k:k + 1, lo:lo + LANES]
        xl = _conv_taps(xc_s, l, w_row, 0, n_conv, LRU_CONV, LRU_HALO) + b_lc_ref[:, lo:lo + LANES]
        xl_s[0:n_conv, lo:lo + LANES] = xl
    xl_s[n_conv:prow, :] = jnp.zeros((LRU_HALO, D_LRU), jnp.float32)
    _lru_gates(_bf16(xl_s[...]), wg_bf, pre_s)

    hc_ls = (0.5 * LRU_C) * _log_sigmoid(lam_ref[...])
    hb_ga = 0.5 * b_ga_ref[...]
    hb_gx = 0.5 * b_gx_ref[...]
    for l in range(N_LRU_SLABS):
        lo = l * LANES
        ca, cx = _pre_cols(l)
        a, u = _lru_coeffs(pre_s[:, ca:ca + LANES], pre_s[:, cx:cx + LANES], xl_s[:, lo:lo + LANES],
                           hb_ga[:, lo:lo + LANES], hb_gx[:, lo:lo + LANES],
                           hc_ls[:, lo:lo + LANES])
        a_s[l, :, :] = a
        u_s[l, :, :] = u

    n_grp = nb // SUBLANES

    def scan_body(j, carry):
        new = []
        for g in range(n_grp):
            for l in range(N_LRU_SLABS):
                base = g * SUBLANES * pitch
                a_j = a_s[l, pl.ds(base + j, SUBLANES, stride=pitch), :]
                u_j = u_s[l, pl.ds(base + j, SUBLANES, stride=pitch), :]
                h = a_j * carry[g * N_LRU_SLABS + l] + u_j
                u_s[l, pl.ds(base + j, SUBLANES, stride=pitch), :] = h
                new.append(h)
        return tuple(new)

    h0 = tuple(h0_ref[g * SUBLANES:(g + 1) * SUBLANES, l * LANES:(l + 1) * LANES]
               for g in range(n_grp) for l in range(N_LRU_SLABS))
    h_fin = _unrolled(steps, scan_body, h0)
    for g in range(n_grp):
        for l in range(N_LRU_SLABS):
            h_out_ref[g * SUBLANES:(g + 1) * SUBLANES, l * LANES:(l + 1) * LANES] = (
                h_fin[g * N_LRU_SLABS + l])

    for b in range(nb):
        for l in range(N_LRU_SLABS):
            lo = l * LANES
            gate = z_s[b * steps:(b + 1) * steps, OFF_LRU_GATE + lo:OFF_LRU_GATE + lo + LANES]
            mix_s[b * steps:(b + 1) * steps, lo:lo + LANES] = _bf16(
                u_s[l, b * pitch:b * pitch + steps, :] * _silu(gate))

    for b in range(nb):
        for l in range(N_CONV_SLABS):
            lo = l * LANES
            uc = _gated(z_s[b * steps:(b + 1) * steps, OFF_GLU_A + lo:OFF_GLU_A + lo + LANES],
                        z_s[b * steps:(b + 1) * steps, OFF_GLU_B + lo:OFF_GLU_B + lo + LANES])
            uc_s[l, b * cpitch + CONV_HALO:(b + 1) * cpitch, :] = uc
            for k in range(CONV_W - 1):
                r = b * cpitch + CONV_HALO - (CONV_W - 1) + k
                uc_s[l, r:r + 1, :] = cb_ref[k, b:b + 1, lo:lo + LANES]
                rn = (b + 1) * cpitch - (CONV_W - 1) + k
                cb_out_ref[k, b:b + 1, lo:lo + LANES] = uc_s[l, rn:rn + 1, :]

    def conv_body(i, carry):
        for j in range(SAMPLE_UNROLL):
            b = i * SAMPLE_UNROLL + j
            r0 = pl.multiple_of(b * cpitch, SUBLANES)
            o0 = pl.multiple_of(b * steps, steps)
            ys = []
            for l in range(N_CONV_SLABS):
                lo = l * LANES
                w_row = lambda k, lo=lo: w_dw_ref[k, :, lo:lo + LANES]
                ys.append(_conv_taps(uc_s, l, w_row, r0, steps, CONV_W, CONV_HALO)
                          + b_dw_ref[:, lo:lo + LANES])
            outs = _layernorm_silu_slabs(ys, ln_g_ref, ln_b_ref)
            for l in range(N_CONV_SLABS):
                lo = l * LANES
                ln_s[pl.ds(o0, steps), lo:lo + LANES] = _bf16(outs[l])
        return carry

    lax.fori_loop(0, nb // SAMPLE_UNROLL, conv_body, 0)

    yc = (_dot(ln_s[...], wpw_bf[...]) + b_pw_ref[...]) * _silu(
        z_s[:, OFF_CONV_GATE:OFF_CONV_GATE + D_CONV])
    mix_s[:, D_LRU:D_LRU + D_CONV] = _bf16(yc)

    masks = _lane_head_masks()

    def attn_body(b, carry):
        o0 = b * steps
        q = z_s[o0:o0 + steps, OFF_Q:OFF_Q + D_XATTN]
        o = _attention_qstack(q, _bf16(kt_ref[b]), _bf16(vt_ref[b]), masks)
        gate = z_s[o0:o0 + steps, OFF_ATTN_GATE:OFF_ATTN_GATE + D_XATTN]
        mix_s[o0:o0 + steps, D_LRU + D_CONV:D_MIX] = _bf16(o * _silu(gate))
        return carry

    _unrolled(nb, attn_body, 0)

    y = x + _dot(mix_s[...], wout_bf[...])
    y_ref[...] = _rmsnorm(y, g_final_ref[...]).reshape(nb, steps, D_MODEL)


def _sample(x, h0, lb, cb, kt, vt, w, win_bf, wout_bf, wpw_bf, wg_bf):
    bsz, steps, _ = x.shape
    nb = SAMPLE_GROUP
    assert bsz % nb == 0 and nb % SUBLANES == 0 and nb % SAMPLE_UNROLL == 0
    assert steps + LRU_HALO == SAMPLE_PITCH and steps >= CONV_W - 1
    rows = nb * steps
    prow = nb * SAMPLE_PITCH
    weights = [w["g_norm"], win_bf, w["w_lc"], w["b_lc"], wg_bf, w["b_ga"], w["b_gx"],
               w["lam"], w["w_dw"], w["b_dw"], w["ln_g"], w["ln_b"], wpw_bf, w["b_pw"],
               wout_bf, w["g_final"]]
    weight_specs = [
        _whole((1, D_MODEL)), _whole((D_MODEL, D_IN)), _whole((None, LRU_CONV, D_LRU)),
        _whole((1, D_LRU)), _whole((2, GATE_HALF, D_LRU)), _whole((1, D_LRU)), _whole((1, D_LRU)),
        _whole((1, D_LRU)), _whole((CONV_W, 1, D_CONV)), _whole((1, D_CONV)), _whole((1, D_CONV)),
        _whole((1, D_CONV)), _whole((D_CONV, D_CONV)), _whole((1, D_CONV)),
        _whole((D_MIX, D_MODEL)), _whole((1, D_MODEL)),
    ]
    in_specs = [
        pl.BlockSpec((nb, steps, D_MODEL), lambda g: (g, 0, 0)),
        pl.BlockSpec((nb, D_LRU), lambda g: (g, 0)),
        pl.BlockSpec((LRU_CONV - 1, nb, D_LRU), lambda g: (0, g, 0)),
        pl.BlockSpec((CONV_W - 1, nb, D_CONV), lambda g: (0, g, 0)),
        pl.BlockSpec((nb, D_XATTN, N_MEM), lambda g: (g, 0, 0)),
        pl.BlockSpec((nb, D_XATTN, N_MEM), lambda g: (g, 0, 0)),
    ] + weight_specs
    out_specs = [
        pl.BlockSpec((nb, steps, D_MODEL), lambda g: (g, 0, 0)),
        pl.BlockSpec((nb, D_LRU), lambda g: (g, 0)),
        pl.BlockSpec((LRU_CONV - 1, nb, D_LRU), lambda g: (0, g, 0)),
        pl.BlockSpec((CONV_W - 1, nb, D_CONV), lambda g: (0, g, 0)),
    ]
    out_shape = [
        jax.ShapeDtypeStruct((bsz, steps, D_MODEL), jnp.float32),
        jax.ShapeDtypeStruct((bsz, D_LRU), jnp.float32),
        jax.ShapeDtypeStruct((LRU_CONV - 1, bsz, D_LRU), jnp.float32),
        jax.ShapeDtypeStruct((CONV_W - 1, bsz, D_CONV), jnp.float32),
    ]
    scratch = [
        pltpu.VMEM((rows, D_IN), jnp.float32),
        pltpu.VMEM((N_LRU_SLABS, prow, LANES), jnp.float32),
        pltpu.VMEM((prow, D_LRU), jnp.float32),
        pltpu.VMEM((prow, 2 * D_LRU), jnp.float32),
        pltpu.VMEM((N_LRU_SLABS, prow, LANES), jnp.float32),
        pltpu.VMEM((N_LRU_SLABS, prow, LANES), jnp.float32),
        pltpu.VMEM((N_CONV_SLABS, nb * (CONV_HALO + steps), LANES), jnp.float32),
        pltpu.VMEM((rows, D_CONV), jnp.bfloat16),
        pltpu.VMEM((rows, D_MIX), jnp.bfloat16),
    ]
    return pl.pallas_call(
        functools.partial(_sample_kernel, nb, steps),
        grid=(bsz // nb,),
        in_specs=in_specs,
        out_specs=out_specs,
        out_shape=out_shape,
        scratch_shapes=scratch,
        compiler_params=pltpu.CompilerParams(
            dimension_semantics=("arbitrary",),
            vmem_limit_bytes=VMEM_LIMIT_BYTES),
        name="sample_layer",
    )(x, h0, lb, cb, kt, vt, *weights)


def _time_major(state):
    return jnp.transpose(state, (1, 0, 2))


def _memory_t(mem):
    bsz = mem.shape[0]
    return jnp.transpose(mem, (0, 2, 3, 1)).reshape(bsz, D_XATTN, N_MEM)


def _memory_from_t(mem_t):
    bsz = mem_t.shape[0]
    return jnp.transpose(mem_t.reshape(bsz, N_XHEADS, XHEAD_DIM, N_MEM), (0, 3, 1, 2))


def kernel(x_prompt, x_sample, state_lru_h, state_lru_conv, state_conv, cache_mem_k, cache_mem_v,
           mem_prompt, g_norm, w_in, w_lru_conv, b_lru_conv, w_gate_a, b_gate_a, w_gate_x, b_gate_x,
           lru_lambda, w_dw, b_dw, ln_g, ln_b, w_pw, b_pw, g_mem, w_mem_k, w_mem_v, w_out, g_final):
    depth = g_norm.shape[0]
    assert depth == 1
    w = {
        "g_norm": g_norm, "w_in": w_in, "w_lc": w_lru_conv, "b_lc": b_lru_conv,
        "w_ga": w_gate_a, "w_gx": w_gate_x, "b_ga": b_gate_a, "b_gx": b_gate_x, "lam": lru_lambda,
        "w_dw": jnp.transpose(w_dw, (1, 0, 2)), "b_dw": b_dw, "ln_g": ln_g, "ln_b": ln_b,
        "w_pw": w_pw, "b_pw": b_pw, "w_out": w_out, "g_final": g_final.reshape(1, D_MODEL),
    }
    mkt, mvt = _memkv(mem_prompt, g_mem, w_mem_k, w_mem_v)

    y_prompt, ph, plb, pcb, win_bf, wout_bf, wpw_bf, wg_bf = _prompt(x_prompt, mkt, mvt, w)

    y_sample, sh, slb, scb = _sample(
        x_sample, state_lru_h[0], _time_major(state_lru_conv[0]), _time_major(state_conv[0]),
        _memory_t(cache_mem_k[0]), _memory_t(cache_mem_v[0]), w, win_bf, wout_bf, wpw_bf, wg_bf)

    return (y_prompt, y_sample,
            ph[None], _time_major(plb)[None], _time_major(pcb)[None],
            _memory_from_t(mkt)[None], _memory_from_t(mvt)[None],
            sh[None], _time_major(slb)[None], _time_major(scb)[None])
```

```python
import functools

import jax
import jax.numpy as jnp
from jax import lax
from jax.experimental import pallas as pl
from jax.experimental.pallas import tpu as pltpu

D_MODEL = 1024
N_MEM = 256
D_LRU = 512
LRU_BLOCKS = 8
LRU_BLOCK = D_LRU // LRU_BLOCKS
LRU_CONV = 4
LRU_C = 8.0
D_CONV = 256
CONV_W = 31
N_XHEADS = 4
XHEAD_DIM = 64
D_XATTN = N_XHEADS * XHEAD_DIM
D_MIX = D_LRU + D_CONV + D_XATTN
D_IN = 2 * D_LRU + 3 * D_CONV + 2 * D_XATTN
EPS = 1e-6

LANES = 128
SUBLANES = 8
N_LRU_SLABS = D_LRU // LANES
N_CONV_SLABS = D_CONV // LANES
GATE_HALF = D_LRU // 2

OFF_LRU_IN = 0
OFF_LRU_GATE = D_LRU
OFF_GLU_A = 2 * D_LRU
OFF_GLU_B = OFF_GLU_A + D_CONV
OFF_CONV_GATE = OFF_GLU_B + D_CONV
OFF_Q = OFF_CONV_GATE + D_CONV
OFF_ATTN_GATE = OFF_Q + D_XATTN

LRU_HALO = 8
CONV_HALO = 32

PROMPT_TILE = 512
PROMPT_SEG = 64
PROMPT_PITCH = 68
SAMPLE_PITCH = 40
SAMPLE_GROUP = 16
SAMPLE_UNROLL = 4
ATTN_ROWS = 128

VMEM_LIMIT_BYTES = 56 * 1024 * 1024
HALF_LOG2_E = 0.7213475204444817


def _silu_of_twice(hx):
    return hx + hx * jnp.tanh(hx)


def _gated_of_twice(ha, hb):
    return ha + ha * jnp.tanh(hb)


def _w_in_col_scale():
    col = lax.broadcasted_iota(jnp.int32, (1, D_IN), 1)
    plain = (col < OFF_LRU_GATE) | ((col >= OFF_Q) & (col < OFF_ATTN_GATE))
    return jnp.where(plain, 1.0, 0.5)


def _sqrt_nonneg(x):
    return jnp.exp2(jnp.log(x) * HALF_LOG2_E)


def _rmsnorm(x, g):
    ms = jnp.mean(x * x, axis=-1, keepdims=True)
    return x * lax.rsqrt(ms + EPS) * g


def _log_sigmoid(x):
    y = -x
    return -(jnp.maximum(y, 0.0) + jnp.log1p(jnp.exp(-jnp.abs(y))))


def _bf16(x):
    return x.astype(jnp.bfloat16)


def _dot(a, b):
    return jnp.dot(a, b, preferred_element_type=jnp.float32)


def _dot_nt(a, b):
    return lax.dot_general(a, b, (((1,), (1,)), ((), ())), preferred_element_type=jnp.float32)


def _unrolled(n, body, init):
    carry = init
    for i in range(n):
        carry = body(i, carry)
    return carry


def _aligned(x, m):
    return x if isinstance(x, int) else pl.multiple_of(x, m)


def _head_of(index):
    return lax.shift_right_logical(index, XHEAD_DIM.bit_length() - 1)


def _lane_head_masks():
    head = _head_of(lax.broadcasted_iota(jnp.int32, (1, D_XATTN), 1))
    return [(head == h).astype(jnp.float32) for h in range(N_XHEADS)]


def _row_head_masks():
    head = _head_of(lax.broadcasted_iota(jnp.int32, (D_XATTN, 1), 0))
    return [(head == h).astype(jnp.float32) for h in range(N_XHEADS)]


def _head_stack_t(kt, vt, kst_ref, vst_ref):
    scale = XHEAD_DIM ** -0.5
    for h, m in enumerate(_row_head_masks()):
        kst_ref[:, h * N_MEM:(h + 1) * N_MEM] = _bf16(kt * (m * scale))
        vst_ref[:, h * N_MEM:(h + 1) * N_MEM] = _bf16(vt * m)


def _attention(q_bf, kst_ref, vst_ref):
    s = _dot(q_bf, kst_ref[...])
    ps = []
    for h in range(N_XHEADS):
        sh = s[:, h * N_MEM:(h + 1) * N_MEM]
        e = jnp.exp(sh - jnp.max(sh, axis=-1, keepdims=True))
        inv = 1.0 / jnp.sum(e, axis=-1, keepdims=True)
        ps.append(_bf16(e * inv))
    return _dot_nt(jnp.concatenate(ps, axis=1), vst_ref[...])


def _attention_qstack(q, kt_bf, vt_bf, masks):
    rows = q.shape[0]
    scale = XHEAD_DIM ** -0.5
    qm = jnp.concatenate([_bf16(q * (m * scale)) for m in masks], axis=0)
    s = _dot(qm, kt_bf)
    e = jnp.exp(s - jnp.max(s, axis=-1, keepdims=True))
    inv = 1.0 / jnp.sum(e, axis=-1, keepdims=True)
    o_all = _dot_nt(_bf16(e * inv), vt_bf)
    o = o_all[0:rows] * masks[0]
    for h in range(1, N_XHEADS):
        o = o + o_all[h * rows:(h + 1) * rows] * masks[h]
    return o


def _lru_gates(xl_bf, wg_ref, pre_s):
    pre_s[:, 0:D_LRU] = _dot(xl_bf[:, 0:GATE_HALF], wg_ref[0])
    pre_s[:, D_LRU:2 * D_LRU] = _dot(xl_bf[:, GATE_HALF:D_LRU], wg_ref[1])


def _pre_cols(l):
    half_slabs = N_LRU_SLABS // 2
    base = (l // half_slabs) * D_LRU + (l % half_slabs) * LANES
    return base, base + GATE_HALF


def _lru_coeffs(hpre_a, hpre_x, hx, hb_a, hb_x, hc_ls):
    t_r = jnp.tanh(hpre_a + hb_a)
    t_i = jnp.tanh(hpre_x + hb_x)
    log_a = hc_ls + hc_ls * t_r
    a = jnp.exp(log_a)
    one_minus_a2 = -jnp.tanh(log_a) * (a * a + 1.0)
    u = _sqrt_nonneg(one_minus_a2) * (hx + hx * t_i)
    return a, u


def _conv_taps(src_s, l, w_row, row0, nrows, width, halo):
    acc = None
    for k in range(width):
        x_k = src_s[l, pl.ds(row0 + (halo - (width - 1) + k), nrows), :]
        acc = w_row(k) * x_k if acc is None else acc + w_row(k) * x_k
    return acc


def _layernorm_silu_slabs(ys, g_ref, b_ref):
    tot = ys[0]
    for y in ys[1:]:
        tot = tot + y
    mu = jnp.sum(tot, axis=-1, keepdims=True) * (1.0 / D_CONV)
    cs = [y - mu for y in ys]
    sq = cs[0] * cs[0]
    for c in cs[1:]:
        sq = sq + c * c
    var = jnp.sum(sq, axis=-1, keepdims=True) * (1.0 / D_CONV)
    inv = lax.rsqrt(var + EPS)
    outs = []
    for l, c in enumerate(cs):
        lo = l * LANES
        half_g = 0.5 * g_ref[:, lo:lo + LANES]
        half_b = 0.5 * b_ref[:, lo:lo + LANES]
        outs.append(_silu_of_twice(c * inv * half_g + half_b))
    return outs


def _gate_block_diag(w_blocks):
    n = (LRU_BLOCKS // 2) * LRU_BLOCK
    stacked = _bf16(w_blocks.reshape(n, LRU_BLOCK))
    src = lax.broadcasted_iota(jnp.int32, (LRU_BLOCK, n), 0)
    dst = lax.broadcasted_iota(jnp.int32, (LRU_BLOCK, n), 1)
    spread = _bf16(((dst & (LRU_BLOCK - 1)) == src).astype(jnp.float32))
    rows = _head_of(lax.broadcasted_iota(jnp.int32, (n, n), 0))
    cols = _head_of(lax.broadcasted_iota(jnp.int32, (n, n), 1))
    return _bf16(_dot(stacked, spread) * (rows == cols).astype(jnp.float32))


def _memkv_kernel(mem_ref, g_ref, wk_ref, wv_ref, kt_ref, vt_ref):
    m = _bf16(_rmsnorm(mem_ref[0], g_ref[...]))
    kt_ref[0] = _dot(m, _bf16(wk_ref[...])).T
    vt_ref[0] = _dot(m, _bf16(wv_ref[...])).T


def _memkv(mem, g_mem, w_k, w_v):
    bsz = mem.shape[0]
    return pl.pallas_call(
        _memkv_kernel,
        grid=(bsz,),
        in_specs=[
            pl.BlockSpec((1, N_MEM, D_MODEL), lambda b: (b, 0, 0)),
            pl.BlockSpec((1, D_MODEL), lambda b: (0, 0)),
            pl.BlockSpec((None, D_MODEL, D_XATTN), lambda b: (0, 0, 0)),
            pl.BlockSpec((None, D_MODEL, D_XATTN), lambda b: (0, 0, 0)),
        ],
        out_specs=[
            pl.BlockSpec((1, D_XATTN, N_MEM), lambda b: (b, 0, 0)),
            pl.BlockSpec((1, D_XATTN, N_MEM), lambda b: (b, 0, 0)),
        ],
        out_shape=[jax.ShapeDtypeStruct((bsz, D_XATTN, N_MEM), jnp.float32)] * 2,
        compiler_params=pltpu.CompilerParams(dimension_semantics=("arbitrary",)),
        name="memkv",
    )(mem, g_mem, w_k, w_v)


def _prompt_kernel(x_ref, kt_ref, vt_ref, g_norm_ref, w_in_ref, w_lc_ref, b_lc_ref, wga_ref, wgx_ref,
                   b_ga_ref, b_gx_ref, lam_ref, w_dw_ref, b_dw_ref, ln_g_ref, ln_b_ref,
                   w_pw_ref, b_pw_ref, w_out_ref, g_final_ref,
                   y_ref, h_out_ref, lb_out_ref, cb_out_ref, win_bf, wout_bf, wpw_bf, wg_bf,
                   z_s, xc_s, xl_s, pre_s, a_s, u_s, uc_s, ln_s, mix_s, hin_s, kst_s, vst_s):
    b = pl.program_id(0)
    t = pl.program_id(1)
    n_t = pl.num_programs(1)
    tile = PROMPT_TILE
    seg = PROMPT_SEG
    n_seg = tile // seg
    pitch = PROMPT_PITCH

    @pl.when(jnp.logical_and(b == 0, t == 0))
    def _():
        win_bf[...] = _bf16(w_in_ref[...] * _w_in_col_scale())
        wout_bf[...] = _bf16(w_out_ref[...])
        wpw_bf[...] = _bf16(w_pw_ref[...])
        half = LRU_BLOCKS // 2
        for h in range(2):
            wg_bf[h, :, 0:GATE_HALF] = _gate_block_diag(wga_ref[h * half:(h + 1) * half])
            wg_bf[h, :, GATE_HALF:D_LRU] = _gate_block_diag(wgx_ref[h * half:(h + 1) * half])

    @pl.when(t == 0)
    def _():
        xc_s[:, 0:LRU_HALO, :] = jnp.zeros((N_LRU_SLABS, LRU_HALO, LANES), jnp.float32)
        uc_s[:, 0:CONV_HALO, :] = jnp.zeros((N_CONV_SLABS, CONV_HALO, LANES), jnp.float32)
        hin_s[:, 0:1, :] = jnp.zeros((N_LRU_SLABS, 1, LANES), jnp.float32)
        _head_stack_t(kt_ref[0], vt_ref[0], kst_s, vst_s)

    hn = _rmsnorm(x_ref[0], g_norm_ref[...])
    z_s[...] = _dot(_bf16(hn), win_bf[...])

    for l in range(N_LRU_SLABS):
        lo = l * LANES
        xc_s[l, LRU_HALO:LRU_HALO + tile, :] = z_s[:, OFF_LRU_IN + lo:OFF_LRU_IN + lo + LANES]
    for l in range(N_LRU_SLABS):
        lo = l * LANES
        w_row = lambda k, lo=lo: 0.5 * w_lc_ref[k:k + 1, lo:lo + LANES]
        xl = (_conv_taps(xc_s, l, w_row, 0, tile, LRU_CONV, LRU_HALO)
              + 0.5 * b_lc_ref[:, lo:lo + LANES])
        xl_s[:, lo:lo + LANES] = xl
    _lru_gates(_bf16(xl_s[...]), wg_bf, pre_s)

    hc_ls = (0.5 * LRU_C) * _log_sigmoid(lam_ref[...])
    hb_ga = 0.5 * b_ga_ref[...]
    hb_gx = 0.5 * b_gx_ref[...]

    def coeff_body(c, carry):
        r0 = _aligned(c * seg, seg)
        p0 = _aligned(c * pitch, SUBLANES)
        for l in range(N_LRU_SLABS):
            lo = l * LANES
            ca, cx = _pre_cols(l)
            a, u = _lru_coeffs(pre_s[pl.ds(r0, seg), ca:ca + LANES],
                               pre_s[pl.ds(r0, seg), cx:cx + LANES],
                               xl_s[pl.ds(r0, seg), lo:lo + LANES],
                               hb_ga[:, lo:lo + LANES], hb_gx[:, lo:lo + LANES],
                               hc_ls[:, lo:lo + LANES])
            a_s[l, pl.ds(p0, seg), :] = a
            u_s[l, pl.ds(p0, seg), :] = u
        return carry

    _unrolled(n_seg, coeff_body, 0)

    def tot_body(j, carry):
        new = []
        for l in range(N_LRU_SLABS):
            a_tot, u_tot = carry[l]
            a_j = a_s[l, pl.ds(j, n_seg, stride=pitch), :]
            u_j = u_s[l, pl.ds(j, n_seg, stride=pitch), :]
            new.append((a_j * a_tot, a_j * u_tot + u_j))
        return tuple(new)

    init = tuple((jnp.ones((n_seg, LANES), jnp.float32), jnp.zeros((n_seg, LANES), jnp.float32))
                 for _ in range(N_LRU_SLABS))
    totals = _unrolled(seg, tot_body, init)

    h_in = []
    for l in range(N_LRU_SLABS):
        a_tot, u_tot = totals[l]
        h = hin_s[l, 0:1, :]
        for c in range(n_seg):
            h = a_tot[c:c + 1, :] * h + u_tot[c:c + 1, :]
            hin_s[l, c + 1:c + 2, :] = h
        h_in.append(hin_s[l, 0:n_seg, :])
        hin_s[l, 0:1, :] = h

    def scan_body(j, carry):
        new = []
        for l in range(N_LRU_SLABS):
            a_j = a_s[l, pl.ds(j, n_seg, stride=pitch), :]
            u_j = u_s[l, pl.ds(j, n_seg, stride=pitch), :]
            h = a_j * carry[l] + u_j
            u_s[l, pl.ds(j, n_seg, stride=pitch), :] = h
            new.append(h)
        return tuple(new)

    _unrolled(seg, scan_body, tuple(h_in))

    for l in range(N_CONV_SLABS):
        lo = l * LANES
        uc_s[l, CONV_HALO:CONV_HALO + tile, :] = _gated_of_twice(
            z_s[:, OFF_GLU_A + lo:OFF_GLU_A + lo + LANES],
            z_s[:, OFF_GLU_B + lo:OFF_GLU_B + lo + LANES])

    def seg_body(c, carry):
        r0 = _aligned(c * seg, seg)
        p0 = _aligned(c * pitch, SUBLANES)
        for l in range(N_LRU_SLABS):
            lo = l * LANES
            gate = z_s[pl.ds(r0, seg), OFF_LRU_GATE + lo:OFF_LRU_GATE + lo + LANES]
            mix_s[pl.ds(r0, seg), lo:lo + LANES] = _bf16(u_s[l, pl.ds(p0, seg), :] * _silu_of_twice(gate))
        ys = []
        for l in range(N_CONV_SLABS):
            lo = l * LANES
            w_row = lambda k, lo=lo: w_dw_ref[k, :, lo:lo + LANES]
            ys.append(_conv_taps(uc_s, l, w_row, r0, seg, CONV_W, CONV_HALO)
                      + b_dw_ref[:, lo:lo + LANES])
        outs = _layernorm_silu_slabs(ys, ln_g_ref, ln_b_ref)
        for l in range(N_CONV_SLABS):
            lo = l * LANES
            ln_s[pl.ds(r0, seg), lo:lo + LANES] = _bf16(outs[l])
        return carry

    _unrolled(n_seg, seg_body, 0)

    yc = (_dot(ln_s[...], wpw_bf[...]) + b_pw_ref[...]) * _silu_of_twice(
        z_s[:, OFF_CONV_GATE:OFF_CONV_GATE + D_CONV])
    mix_s[:, D_LRU:D_LRU + D_CONV] = _bf16(yc)

    def attn_body(c, carry):
        r0 = _aligned(c * ATTN_ROWS, ATTN_ROWS)
        q = z_s[pl.ds(r0, ATTN_ROWS), OFF_Q:OFF_Q + D_XATTN]
        o = _attention(_bf16(q), kst_s, vst_s)
        gate = z_s[pl.ds(r0, ATTN_ROWS), OFF_ATTN_GATE:OFF_ATTN_GATE + D_XATTN]
        mix_s[pl.ds(r0, ATTN_ROWS), D_LRU + D_CONV:D_MIX] = _bf16(o * _silu_of_twice(gate))
        return carry

    _unrolled(tile // ATTN_ROWS, attn_body, 0)

    y = x_ref[0] + _dot(mix_s[...], wout_bf[...])
    y_ref[0] = _rmsnorm(y, g_final_ref[...])

    for l in range(N_LRU_SLABS):
        xc_s[l, LRU_HALO - (LRU_CONV - 1):LRU_HALO, :] = (
            xc_s[l, LRU_HALO + tile - (LRU_CONV - 1):LRU_HALO + tile, :])
    for l in range(N_CONV_SLABS):
        uc_s[l, CONV_HALO - (CONV_W - 1):CONV_HALO, :] = (
            uc_s[l, CONV_HALO + tile - (CONV_W - 1):CONV_HALO + tile, :])

    def write_state(bb):
        for l in range(N_LRU_SLABS):
            lo = l * LANES
            h_out_ref[bb:bb + 1, lo:lo + LANES] = hin_s[l, 0:1, :]
            for k in range(LRU_CONV - 1):
                r = LRU_HALO - (LRU_CONV - 1) + k
                lb_out_ref[k, bb:bb + 1, lo:lo + LANES] = xc_s[l, r:r + 1, :]
        for l in range(N_CONV_SLABS):
            lo = l * LANES
            for k in range(CONV_W - 1):
                r = CONV_HALO - (CONV_W - 1) + k
                cb_out_ref[k, bb:bb + 1, lo:lo + LANES] = uc_s[l, r:r + 1, :]

    for bb in range(h_out_ref.shape[0]):
        pl.when(jnp.logical_and(t == n_t - 1, b == bb))(functools.partial(write_state, bb))


def _whole(shape):
    nd = len(shape)
    return pl.BlockSpec(shape, lambda *_: (0,) * nd, pipeline_mode=pl.Buffered(1))


def _prompt(x, kt, vt, w):
    bsz, seq, _ = x.shape
    tile = PROMPT_TILE
    n_t = seq // tile
    n_seg = tile // PROMPT_SEG
    weights = [w["g_norm"], w["w_in"], w["w_lc"], w["b_lc"], w["w_ga"], w["w_gx"], w["b_ga"], w["b_gx"],
               w["lam"], w["w_dw"], w["b_dw"], w["ln_g"], w["ln_b"], w["w_pw"], w["b_pw"],
               w["w_out"], w["g_final"]]
    weight_specs = [
        _whole((1, D_MODEL)), _whole((None, D_MODEL, D_IN)), _whole((None, LRU_CONV, D_LRU)),
        _whole((1, D_LRU)), _whole((None, LRU_BLOCKS, LRU_BLOCK, LRU_BLOCK)),
        _whole((None, LRU_BLOCKS, LRU_BLOCK, LRU_BLOCK)), _whole((1, D_LRU)), _whole((1, D_LRU)),
        _whole((1, D_LRU)), _whole((CONV_W, 1, D_CONV)), _whole((1, D_CONV)), _whole((1, D_CONV)),
        _whole((1, D_CONV)), _whole((None, D_CONV, D_CONV)), _whole((1, D_CONV)),
        _whole((None, D_MIX, D_MODEL)), _whole((1, D_MODEL)),
    ]
    in_specs = [
        pl.BlockSpec((1, tile, D_MODEL), lambda b, t: (b, t, 0)),
        pl.BlockSpec((1, D_XATTN, N_MEM), lambda b, t: (b, 0, 0)),
        pl.BlockSpec((1, D_XATTN, N_MEM), lambda b, t: (b, 0, 0)),
    ] + weight_specs
    out_specs = [
        pl.BlockSpec((1, tile, D_MODEL), lambda b, t: (b, t, 0)),
        _whole((bsz, D_LRU)),
        _whole((LRU_CONV - 1, bsz, D_LRU)),
        _whole((CONV_W - 1, bsz, D_CONV)),
        _whole((D_MODEL, D_IN)), _whole((D_MIX, D_MODEL)), _whole((D_CONV, D_CONV)),
        _whole((2, GATE_HALF, D_LRU)),
    ]
    out_shape = [
        jax.ShapeDtypeStruct((bsz, seq, D_MODEL), jnp.float32),
        jax.ShapeDtypeStruct((bsz, D_LRU), jnp.float32),
        jax.ShapeDtypeStruct((LRU_CONV - 1, bsz, D_LRU), jnp.float32),
        jax.ShapeDtypeStruct((CONV_W - 1, bsz, D_CONV), jnp.float32),
        jax.ShapeDtypeStruct((D_MODEL, D_IN), jnp.bfloat16),
        jax.ShapeDtypeStruct((D_MIX, D_MODEL), jnp.bfloat16),
        jax.ShapeDtypeStruct((D_CONV, D_CONV), jnp.bfloat16),
        jax.ShapeDtypeStruct((2, GATE_HALF, D_LRU), jnp.bfloat16),
    ]
    scratch = [
        pltpu.VMEM((tile, D_IN), jnp.float32),
        pltpu.VMEM((N_LRU_SLABS, LRU_HALO + tile, LANES), jnp.float32),
        pltpu.VMEM((tile, D_LRU), jnp.float32),
        pltpu.VMEM((tile, 2 * D_LRU), jnp.float32),
        pltpu.VMEM((N_LRU_SLABS, n_seg * PROMPT_PITCH, LANES), jnp.float32),
        pltpu.VMEM((N_LRU_SLABS, n_seg * PROMPT_PITCH, LANES), jnp.float32),
        pltpu.VMEM((N_CONV_SLABS, CONV_HALO + tile, LANES), jnp.float32),
        pltpu.VMEM((tile, D_CONV), jnp.bfloat16),
        pltpu.VMEM((tile, D_MIX), jnp.bfloat16),
        pltpu.VMEM((N_LRU_SLABS, 2 * SUBLANES, LANES), jnp.float32),
        pltpu.VMEM((D_XATTN, N_XHEADS * N_MEM), jnp.bfloat16),
        pltpu.VMEM((D_XATTN, N_XHEADS * N_MEM), jnp.bfloat16),
    ]
    return pl.pallas_call(
        _prompt_kernel,
        grid=(bsz, n_t),
        in_specs=in_specs,
        out_specs=out_specs,
        out_shape=out_shape,
        scratch_shapes=scratch,
        compiler_params=pltpu.CompilerParams(
            dimension_semantics=("arbitrary", "arbitrary"),
            vmem_limit_bytes=VMEM_LIMIT_BYTES),
        name="prompt_layer",
    )(x, kt, vt, *weights)


def _sample_kernel(nb, steps,
                   x_ref, h0_ref, lb_ref, cb_ref, kt_ref, vt_ref,
                   g_norm_ref, win_bf, w_lc_ref, b_lc_ref, wg_bf,
                   b_ga_ref, b_gx_ref, lam_ref, w_dw_ref, b_dw_ref, ln_g_ref, ln_b_ref,
                   wpw_bf, b_pw_ref, wout_bf, g_final_ref,
                   y_ref, h_out_ref, lb_out_ref, cb_out_ref,
                   z_s, xc_s, xl_s, pre_s, a_s, u_s, uc_s, ln_s, mix_s):
    rows = nb * steps
    pitch = SAMPLE_PITCH
    cpitch = CONV_HALO + steps
    prow = nb * pitch

    x = x_ref[...].reshape(rows, D_MODEL)
    hn = _rmsnorm(x, g_norm_ref[...])
    z_s[...] = _dot(_bf16(hn), win_bf[...])

    for b in range(nb):
        for l in range(N_LRU_SLABS):
            lo = l * LANES
            zcol = OFF_LRU_IN + lo
            xc_s[l, b * pitch:b * pitch + LRU_HALO, :] = jnp.zeros((LRU_HALO, LANES), jnp.float32)
            for k in range(LRU_CONV - 1):
                r = b * pitch + LRU_HALO - (LRU_CONV - 1) + k
                xc_s[l, r:r + 1, :] = lb_ref[k, b:b + 1, lo:lo + LANES]
                rn = (b + 1) * steps - (LRU_CONV - 1) + k
                lb_out_ref[k, b:b + 1, lo:lo + LANES] = z_s[rn:rn + 1, zcol:zcol + LANES]
            xc_s[l, b * pitch + LRU_HALO:(b + 1) * pitch, :] = (
                z_s[b * steps:(b + 1) * steps, zcol:zcol + LANES])
    n_conv = prow - LRU_HALO
    for l in range(N_LRU_SLABS):
        lo = l * LANES
        w_row = lambda k, lo=lo: 0.5 * w_lc_ref[k:k + 1, lo:lo + LANES]
        xl = (_conv_taps(xc_s, l, w_row, 0, n_conv, LRU_CONV, LRU_HALO)
              + 0.5 * b_lc_ref[:, lo:lo + LANES])
        xl_s[0:n_conv, lo:lo + LANES] = xl
    xl_s[n_conv:prow, :] = jnp.zeros((LRU_HALO, D_LRU), jnp.float32)
    _lru_gates(_bf16(xl_s[...]), wg_bf, pre_s)

    hc_ls = (0.5 * LRU_C) * _log_sigmoid(lam_ref[...])
    hb_ga = 0.5 * b_ga_ref[...]
    hb_gx = 0.5 * b_gx_ref[...]
    for l in range(N_LRU_SLABS):
        lo = l * LANES
        ca, cx = _pre_cols(l)
        a, u = _lru_coeffs(pre_s[:, ca:ca + LANES], pre_s[:, cx:cx + LANES], xl_s[:, lo:lo + LANES],
                           hb_ga[:, lo:lo + LANES], hb_gx[:, lo:lo + LANES],
                           hc_ls[:, lo:lo + LANES])
        a_s[l, :, :] = a
        u_s[l, :, :] = u

    n_grp = nb // SUBLANES

    def scan_body(j, carry):
        new = []
        for g in range(n_grp):
            for l in range(N_LRU_SLABS):
                base = g * SUBLANES * pitch
                a_j = a_s[l, pl.ds(base + j, SUBLANES, stride=pitch), :]
                u_j = u_s[l, pl.ds(base + j, SUBLANES, stride=pitch), :]
                h = a_j * carry[g * N_LRU_SLABS + l] + u_j
                u_s[l, pl.ds(base + j, SUBLANES, stride=pitch), :] = h
                new.append(h)
        return tuple(new)

    h0 = tuple(h0_ref[g * SUBLANES:(g + 1) * SUBLANES, l * LANES:(l + 1) * LANES]
               for g in range(n_grp) for l in range(N_LRU_SLABS))
    h_fin = _unrolled(steps, scan_body, h0)
    for g in range(n_grp):
        for l in range(N_LRU_SLABS):
            h_out_ref[g * SUBLANES:(g + 1) * SUBLANES, l * LANES:(l + 1) * LANES] = (
                h_fin[g * N_LRU_SLABS + l])

    for b in range(nb):
        for l in range(N_LRU_SLABS):
            lo = l * LANES
            gate = z_s[b * steps:(b + 1) * steps, OFF_LRU_GATE + lo:OFF_LRU_GATE + lo + LANES]
            mix_s[b * steps:(b + 1) * steps, lo:lo + LANES] = _bf16(
                u_s[l, b * pitch:b * pitch + steps, :] * _silu_of_twice(gate))

    for b in range(nb):
        for l in range(N_CONV_SLABS):
            lo = l * LANES
            uc = _gated_of_twice(
                z_s[b * steps:(b + 1) * steps, OFF_GLU_A + lo:OFF_GLU_A + lo + LANES],
                z_s[b * steps:(b + 1) * steps, OFF_GLU_B + lo:OFF_GLU_B + lo + LANES])
            uc_s[l, b * cpitch + CONV_HALO:(b + 1) * cpitch, :] = uc
            for k in range(CONV_W - 1):
                r = b * cpitch + CONV_HALO - (CONV_W - 1) + k
                uc_s[l, r:r + 1, :] = cb_ref[k, b:b + 1, lo:lo + LANES]
                rn = (b + 1) * cpitch - (CONV_W - 1) + k
                cb_out_ref[k, b:b + 1, lo:lo + LANES] = uc_s[l, rn:rn + 1, :]

    def conv_body(i, carry):
        for j in range(SAMPLE_UNROLL):
            b = i * SAMPLE_UNROLL + j
            r0 = pl.multiple_of(b * cpitch, SUBLANES)
            o0 = pl.multiple_of(b * steps, steps)
            ys = []
            for l in range(N_CONV_SLABS):
                lo = l * LANES
                w_row = lambda k, lo=lo: w_dw_ref[k, :, lo:lo + LANES]
                ys.append(_conv_taps(uc_s, l, w_row, r0, steps, CONV_W, CONV_HALO)
                          + b_dw_ref[:, lo:lo + LANES])
            outs = _layernorm_silu_slabs(ys, ln_g_ref, ln_b_ref)
            for l in range(N_CONV_SLABS):
                lo = l * LANES
                ln_s[pl.ds(o0, steps), lo:lo + LANES] = _bf16(outs[l])
        return carry

    lax.fori_loop(0, nb // SAMPLE_UNROLL, conv_body, 0)

    yc = (_dot(ln_s[...], wpw_bf[...]) + b_pw_ref[...]) * _silu_of_twice(
        z_s[:, OFF_CONV_GATE:OFF_CONV_GATE + D_CONV])
    mix_s[:, D_LRU:D_LRU + D_CONV] = _bf16(yc)

    masks = _lane_head_masks()

    def attn_body(b, carry):
        o0 = b * steps
        q = z_s[o0:o0 + steps, OFF_Q:OFF_Q + D_XATTN]
        o = _attention_qstack(q, _bf16(kt_ref[b]), _bf16(vt_ref[b]), masks)
        gate = z_s[o0:o0 + steps, OFF_ATTN_GATE:OFF_ATTN_GATE + D_XATTN]
        mix_s[o0:o0 + steps, D_LRU + D_CONV:D_MIX] = _bf16(o * _silu_of_twice(gate))
        return carry

    _unrolled(nb, attn_body, 0)

    y = x + _dot(mix_s[...], wout_bf[...])
    y_ref[...] = _rmsnorm(y, g_final_ref[...]).reshape(nb, steps, D_MODEL)


def _sample(x, h0, lb, cb, kt, vt, w, win_bf, wout_bf, wpw_bf, wg_bf):
    bsz, steps, _ = x.shape
    nb = SAMPLE_GROUP
    assert bsz % nb == 0 and nb % SUBLANES == 0 and nb % SAMPLE_UNROLL == 0
    assert steps + LRU_HALO == SAMPLE_PITCH and steps >= CONV_W - 1
    rows = nb * steps
    prow = nb * SAMPLE_PITCH
    weights = [w["g_norm"], win_bf, w["w_lc"], w["b_lc"], wg_bf, w["b_ga"], w["b_gx"],
               w["lam"], w["w_dw"], w["b_dw"], w["ln_g"], w["ln_b"], wpw_bf, w["b_pw"],
               wout_bf, w["g_final"]]
    weight_specs = [
        _whole((1, D_MODEL)), _whole((D_MODEL, D_IN)), _whole((None, LRU_CONV, D_LRU)),
        _whole((1, D_LRU)), _whole((2, GATE_HALF, D_LRU)), _whole((1, D_LRU)), _whole((1, D_LRU)),
        _whole((1, D_LRU)), _whole((CONV_W, 1, D_CONV)), _whole((1, D_CONV)), _whole((1, D_CONV)),
        _whole((1, D_CONV)), _whole((D_CONV, D_CONV)), _whole((1, D_CONV)),
        _whole((D_MIX, D_MODEL)), _whole((1, D_MODEL)),
    ]
    in_specs = [
        pl.BlockSpec((nb, steps, D_MODEL), lambda g: (g, 0, 0)),
        pl.BlockSpec((nb, D_LRU), lambda g: (g, 0)),
        pl.BlockSpec((LRU_CONV - 1, nb, D_LRU), lambda g: (0, g, 0)),
        pl.BlockSpec((CONV_W - 1, nb, D_CONV), lambda g: (0, g, 0)),
        pl.BlockSpec((nb, D_XATTN, N_MEM), lambda g: (g, 0, 0)),
        pl.BlockSpec((nb, D_XATTN, N_MEM), lambda g: (g, 0, 0)),
    ] + weight_specs
    out_specs = [
        pl.BlockSpec((nb, steps, D_MODEL), lambda g: (g, 0, 0)),
        pl.BlockSpec((nb, D_LRU), lambda g: (g, 0)),
        pl.BlockSpec((LRU_CONV - 1, nb, D_LRU), lambda g: (0, g, 0)),
        pl.BlockSpec((CONV_W - 1, nb, D_CONV), lambda g: (0, g, 0)),
    ]
    out_shape = [
        jax.ShapeDtypeStruct((bsz, steps, D_MODEL), jnp.float32),
        jax.ShapeDtypeStruct((bsz, D_LRU), jnp.float32),
        jax.ShapeDtypeStruct((LRU_CONV - 1, bsz, D_LRU), jnp.float32),
        jax.ShapeDtypeStruct((CONV_W - 1, bsz, D_CONV), jnp.float32),
    ]
    scratch = [
        pltpu.VMEM((rows, D_IN), jnp.float32),
        pltpu.VMEM((N_LRU_SLABS, prow, LANES), jnp.float32),
        pltpu.VMEM((prow, D_LRU), jnp.float32),
        pltpu.VMEM((prow, 2 * D_LRU), jnp.float32),
        pltpu.VMEM((N_LRU_SLABS, prow, LANES), jnp.float32),
        pltpu.VMEM((N_LRU_SLABS, prow, LANES), jnp.float32),
        pltpu.VMEM((N_CONV_SLABS, nb * (CONV_HALO + steps), LANES), jnp.float32),
        pltpu.VMEM((rows, D_CONV), jnp.bfloat16),
        pltpu.VMEM((rows, D_MIX), jnp.bfloat16),
    ]
    return pl.pallas_call(
        functools.partial(_sample_kernel, nb, steps),
        grid=(bsz // nb,),
        in_specs=in_specs,
        out_specs=out_specs,
        out_shape=out_shape,
        scratch_shapes=scratch,
        compiler_params=pltpu.CompilerParams(
            dimension_semantics=("arbitrary",),
            vmem_limit_bytes=VMEM_LIMIT_BYTES),
        name="sample_layer",
    )(x, h0, lb, cb, kt, vt, *weights)


def _time_major(state):
    return jnp.transpose(state, (1, 0, 2))


def _memory_t(mem):
    bsz = mem.shape[0]
    return jnp.transpose(mem, (0, 2, 3, 1)).reshape(bsz, D_XATTN, N_MEM)


def _memory_from_t(mem_t):
    bsz = mem_t.shape[0]
    return jnp.transpose(mem_t.reshape(bsz, N_XHEADS, XHEAD_DIM, N_MEM), (0, 3, 1, 2))


def kernel(x_prompt, x_sample, state_lru_h, state_lru_conv, state_conv, cache_mem_k, cache_mem_v,
           mem_prompt, g_norm, w_in, w_lru_conv, b_lru_conv, w_gate_a, b_gate_a, w_gate_x, b_gate_x,
           lru_lambda, w_dw, b_dw, ln_g, ln_b, w_pw, b_pw, g_mem, w_mem_k, w_mem_v, w_out, g_final):
    depth = g_norm.shape[0]
    assert depth == 1
    w = {
        "g_norm": g_norm, "w_in": w_in, "w_lc": w_lru_conv, "b_lc": b_lru_conv,
        "w_ga": w_gate_a, "w_gx": w_gate_x, "b_ga": b_gate_a, "b_gx": b_gate_x, "lam": lru_lambda,
        "w_dw": jnp.transpose(w_dw, (1, 0, 2)), "b_dw": b_dw, "ln_g": ln_g, "ln_b": ln_b,
        "w_pw": w_pw, "b_pw": b_pw, "w_out": w_out, "g_final": g_final.reshape(1, D_MODEL),
    }
    mkt, mvt = _memkv(mem_prompt, g_mem, w_mem_k, w_mem_v)

    y_prompt, ph, plb, pcb, win_bf, wout_bf, wpw_bf, wg_bf = _prompt(x_prompt, mkt, mvt, w)

    y_sample, sh, slb, scb = _sample(
        x_sample, state_lru_h[0], _time_major(state_lru_conv[0]), _time_major(state_conv[0]),
        _memory_t(cache_mem_k[0]), _memory_t(cache_mem_v[0]), w, win_bf, wout_bf, wpw_bf, wg_bf)

    return (y_prompt, y_sample,
            ph[None], _time_major(plb)[None], _time_major(pcb)[None],
            _memory_from_t(mkt)[None], _memory_from_t(mvt)[None],
            sh[None], _time_major(slb)[None], _time_major(scb)[None])
```

```python
import functools

import jax
import jax.numpy as jnp
from jax import lax
from jax.experimental import pallas as pl
from jax.experimental.pallas import tpu as pltpu

D_MODEL = 1024
N_MEM = 256
D_LRU = 512
LRU_BLOCKS = 8
LRU_BLOCK = D_LRU // LRU_BLOCKS
LRU_CONV = 4
LRU_C = 8.0
D_CONV = 256
CONV_W = 31
N_XHEADS = 4
XHEAD_DIM = 64
D_XATTN = N_XHEADS * XHEAD_DIM
D_MIX = D_LRU + D_CONV + D_XATTN
D_IN = 2 * D_LRU + 3 * D_CONV + 2 * D_XATTN
EPS = 1e-6

LANES = 128
SUBLANES = 8
N_LRU_SLABS = D_LRU // LANES
N_CONV_SLABS = D_CONV // LANES
GATE_HALF = D_LRU // 2

OFF_LRU_IN = 0
OFF_LRU_GATE = D_LRU
OFF_GLU_A = 2 * D_LRU
OFF_GLU_B = OFF_GLU_A + D_CONV
OFF_CONV_GATE = OFF_GLU_B + D_CONV
OFF_Q = OFF_CONV_GATE + D_CONV
OFF_ATTN_GATE = OFF_Q + D_XATTN

LRU_HALO = 8
CONV_HALO = 32

PROMPT_TILE = 512
PROMPT_SEG = 64
PROMPT_PITCH = 68
SAMPLE_PITCH = 40
SAMPLE_GROUP = 16
SAMPLE_UNROLL = 4
ATTN_ROWS = 128

VMEM_LIMIT_BYTES = 56 * 1024 * 1024
HALF_LOG2_E = 0.7213475204444817


def _silu_of_twice(hx):
    return hx + hx * jnp.tanh(hx)


def _gated_of_twice(ha, hb):
    return ha + ha * jnp.tanh(hb)


def _w_in_col_scale():
    col = lax.broadcasted_iota(jnp.int32, (1, D_IN), 1)
    plain = (col < OFF_LRU_GATE) | ((col >= OFF_Q) & (col < OFF_ATTN_GATE))
    return jnp.where(plain, 1.0, 0.5)


def _sqrt_nonneg(x):
    return jnp.exp2(jnp.log(x) * HALF_LOG2_E)


def _rmsnorm(x, g):
    ms = jnp.mean(x * x, axis=-1, keepdims=True)
    return x * lax.rsqrt(ms + EPS) * g


def _log_sigmoid(x):
    y = -x
    return -(jnp.maximum(y, 0.0) + jnp.log1p(jnp.exp(-jnp.abs(y))))


def _bf16(x):
    return x.astype(jnp.bfloat16)


def _dot(a, b):
    return jnp.dot(a, b, preferred_element_type=jnp.float32)


def _dot_nt(a, b):
    return lax.dot_general(a, b, (((1,), (1,)), ((), ())), preferred_element_type=jnp.float32)


def _unrolled(n, body, init):
    carry = init
    for i in range(n):
        carry = body(i, carry)
    return carry


def _aligned(x, m):
    return x if isinstance(x, int) else pl.multiple_of(x, m)


def _head_of(index):
    return lax.shift_right_logical(index, XHEAD_DIM.bit_length() - 1)


def _lane_head_masks():
    head = _head_of(lax.broadcasted_iota(jnp.int32, (1, D_XATTN), 1))
    return [(head == h).astype(jnp.float32) for h in range(N_XHEADS)]


def _row_head_masks():
    head = _head_of(lax.broadcasted_iota(jnp.int32, (D_XATTN, 1), 0))
    return [(head == h).astype(jnp.float32) for h in range(N_XHEADS)]


def _head_stack_t(kt, vt, kst_ref, vst_ref):
    scale = XHEAD_DIM ** -0.5
    for h, m in enumerate(_row_head_masks()):
        kst_ref[:, h * N_MEM:(h + 1) * N_MEM] = _bf16(kt * (m * scale))
        vst_ref[:, h * N_MEM:(h + 1) * N_MEM] = _bf16(vt * m)


def _zero_row_after(x):
    bits = lax.bitcast_convert_type(x, jnp.uint32)
    zero = lax.shift_right_logical(lax.shift_right_logical(bits, jnp.uint32(16)), jnp.uint32(16))
    return lax.bitcast_convert_type(zero, jnp.float32)[0:1, :]


def _softmax_pv(s, vst_ref):
    ps = []
    for h in range(N_XHEADS):
        sh = s[:, h * N_MEM:(h + 1) * N_MEM]
        e = jnp.exp(sh - jnp.max(sh, axis=-1, keepdims=True))
        inv = 1.0 / jnp.sum(e, axis=-1, keepdims=True)
        ps.append(_bf16(e * inv))
    return _dot_nt(jnp.concatenate(ps, axis=1), vst_ref[...])


def _attention_qstack(q, kt_bf, vt_bf, masks):
    rows = q.shape[0]
    scale = XHEAD_DIM ** -0.5
    qm = jnp.concatenate([_bf16(q * (m * scale)) for m in masks], axis=0)
    s = _dot(qm, kt_bf)
    e = jnp.exp(s - jnp.max(s, axis=-1, keepdims=True))
    inv = 1.0 / jnp.sum(e, axis=-1, keepdims=True)
    o_all = _dot_nt(_bf16(e * inv), vt_bf)
    o = o_all[0:rows] * masks[0]
    for h in range(1, N_XHEADS):
        o = o + o_all[h * rows:(h + 1) * rows] * masks[h]
    return o


def _lru_gates(xl_bf, wg_ref, pre_s):
    pre_s[:, 0:D_LRU] = _dot(xl_bf[:, 0:GATE_HALF], wg_ref[0])
    pre_s[:, D_LRU:2 * D_LRU] = _dot(xl_bf[:, GATE_HALF:D_LRU], wg_ref[1])


def _pre_cols(l):
    half_slabs = N_LRU_SLABS // 2
    base = (l // half_slabs) * D_LRU + (l % half_slabs) * LANES
    return base, base + GATE_HALF


def _lru_coeffs(hpre_a, hpre_x, hx, hb_a, hb_x, hc_ls):
    t_r = jnp.tanh(hpre_a + hb_a)
    t_i = jnp.tanh(hpre_x + hb_x)
    log_a = hc_ls + hc_ls * t_r
    a = jnp.exp(log_a)
    one_minus_a2 = -jnp.tanh(log_a) * (a * a + 1.0)
    u = _sqrt_nonneg(one_minus_a2) * (hx + hx * t_i)
    return a, u


def _conv_taps(src_s, l, w_row, row0, nrows, width, halo):
    acc = None
    for k in range(width):
        x_k = src_s[l, pl.ds(row0 + (halo - (width - 1) + k), nrows), :]
        acc = w_row(k) * x_k if acc is None else acc + w_row(k) * x_k
    return acc


def _layernorm_silu_slabs(ys, g_ref, b_ref):
    tot = ys[0]
    for y in ys[1:]:
        tot = tot + y
    mu = jnp.sum(tot, axis=-1, keepdims=True) * (1.0 / D_CONV)
    cs = [y - mu for y in ys]
    sq = cs[0] * cs[0]
    for c in cs[1:]:
        sq = sq + c * c
    var = jnp.sum(sq, axis=-1, keepdims=True) * (1.0 / D_CONV)
    inv = lax.rsqrt(var + EPS)
    outs = []
    for l, c in enumerate(cs):
        lo = l * LANES
        half_g = 0.5 * g_ref[:, lo:lo + LANES]
        half_b = 0.5 * b_ref[:, lo:lo + LANES]
        outs.append(_silu_of_twice(c * inv * half_g + half_b))
    return outs


def _gate_block_diag(w_blocks):
    n = (LRU_BLOCKS // 2) * LRU_BLOCK
    stacked = _bf16(w_blocks.reshape(n, LRU_BLOCK))
    src = lax.broadcasted_iota(jnp.int32, (LRU_BLOCK, n), 0)
    dst = lax.broadcasted_iota(jnp.int32, (LRU_BLOCK, n), 1)
    spread = _bf16(((dst & (LRU_BLOCK - 1)) == src).astype(jnp.float32))
    rows = _head_of(lax.broadcasted_iota(jnp.int32, (n, n), 0))
    cols = _head_of(lax.broadcasted_iota(jnp.int32, (n, n), 1))
    return _bf16(_dot(stacked, spread) * (rows == cols).astype(jnp.float32))


def _memkv_kernel(mem_ref, g_ref, wk_ref, wv_ref, kt_ref, vt_ref):
    m = _bf16(_rmsnorm(mem_ref[0], g_ref[...]))
    kt_ref[0] = _dot(m, _bf16(wk_ref[...])).T
    vt_ref[0] = _dot(m, _bf16(wv_ref[...])).T


def _memkv(mem, g_mem, w_k, w_v):
    bsz = mem.shape[0]
    return pl.pallas_call(
        _memkv_kernel,
        grid=(bsz,),
        in_specs=[
            pl.BlockSpec((1, N_MEM, D_MODEL), lambda b: (b, 0, 0)),
            pl.BlockSpec((1, D_MODEL), lambda b: (0, 0)),
            pl.BlockSpec((None, D_MODEL, D_XATTN), lambda b: (0, 0, 0)),
            pl.BlockSpec((None, D_MODEL, D_XATTN), lambda b: (0, 0, 0)),
        ],
        out_specs=[
            pl.BlockSpec((1, D_XATTN, N_MEM), lambda b: (b, 0, 0)),
            pl.BlockSpec((1, D_XATTN, N_MEM), lambda b: (b, 0, 0)),
        ],
        out_shape=[jax.ShapeDtypeStruct((bsz, D_XATTN, N_MEM), jnp.float32)] * 2,
        compiler_params=pltpu.CompilerParams(dimension_semantics=("arbitrary",)),
        name="memkv",
    )(mem, g_mem, w_k, w_v)


def _prompt_kernel(x_ref, kt_ref, vt_ref, g_norm_ref, w_in_ref, w_lc_ref, b_lc_ref, wga_ref, wgx_ref,
                   b_ga_ref, b_gx_ref, lam_ref, w_dw_ref, b_dw_ref, ln_g_ref, ln_b_ref,
                   w_pw_ref, b_pw_ref, w_out_ref, g_final_ref,
                   y_ref, h_out_ref, lb_out_ref, cb_out_ref, win_bf, wout_bf, wpw_bf, wg_bf,
                   z_s, xc_s, xl_s, pre_s, a_s, u_s, uc_s, ln_s, mix_s, hin_s, kst_s, vst_s, sc_s):
    b = pl.program_id(0)
    t = pl.program_id(1)
    n_t = pl.num_programs(1)
    tile = PROMPT_TILE
    seg = PROMPT_SEG
    n_seg = tile // seg
    pitch = PROMPT_PITCH

    @pl.when(jnp.logical_and(b == 0, t == 0))
    def _():
        win_bf[...] = _bf16(w_in_ref[...] * _w_in_col_scale())
        wout_bf[...] = _bf16(w_out_ref[...])
        wpw_bf[...] = _bf16(w_pw_ref[...])
        half = LRU_BLOCKS // 2
        for h in range(2):
            wg_bf[h, :, 0:GATE_HALF] = _gate_block_diag(wga_ref[h * half:(h + 1) * half])
            wg_bf[h, :, GATE_HALF:D_LRU] = _gate_block_diag(wgx_ref[h * half:(h + 1) * half])

    @pl.when(t == 0)
    def _():
        xc_s[:, 0:LRU_HALO, :] = jnp.zeros((N_LRU_SLABS, LRU_HALO, LANES), jnp.float32)
        uc_s[:, 0:CONV_HALO, :] = jnp.zeros((N_CONV_SLABS, CONV_HALO, LANES), jnp.float32)
        hin_s[:, 0:1, :] = jnp.zeros((N_LRU_SLABS, 1, LANES), jnp.float32)
        _head_stack_t(kt_ref[0], vt_ref[0], kst_s, vst_s)

    hn = _rmsnorm(x_ref[0], g_norm_ref[...])
    z_s[...] = _dot(_bf16(hn), win_bf[...])

    for l in range(N_LRU_SLABS):
        lo = l * LANES
        xc_s[l, LRU_HALO:LRU_HALO + tile, :] = z_s[:, OFF_LRU_IN + lo:OFF_LRU_IN + lo + LANES]
    for l in range(N_LRU_SLABS):
        lo = l * LANES
        w_row = lambda k, lo=lo: 0.5 * w_lc_ref[k:k + 1, lo:lo + LANES]
        xl = (_conv_taps(xc_s, l, w_row, 0, tile, LRU_CONV, LRU_HALO)
              + 0.5 * b_lc_ref[:, lo:lo + LANES])
        xl_s[:, lo:lo + LANES] = xl
    _lru_gates(_bf16(xl_s[...]), wg_bf, pre_s)

    hc_ls = (0.5 * LRU_C) * _log_sigmoid(lam_ref[...])
    hb_ga = 0.5 * b_ga_ref[...]
    hb_gx = 0.5 * b_gx_ref[...]

    def coeff_body(c, carry):
        r0 = _aligned(c * seg, seg)
        p0 = _aligned(c * pitch, SUBLANES)
        for l in range(N_LRU_SLABS):
            lo = l * LANES
            ca, cx = _pre_cols(l)
            a, u = _lru_coeffs(pre_s[pl.ds(r0, seg), ca:ca + LANES],
                               pre_s[pl.ds(r0, seg), cx:cx + LANES],
                               xl_s[pl.ds(r0, seg), lo:lo + LANES],
                               hb_ga[:, lo:lo + LANES], hb_gx[:, lo:lo + LANES],
                               hc_ls[:, lo:lo + LANES])
            a_s[l, pl.ds(p0, seg), :] = a
            u_s[l, pl.ds(p0, seg), :] = u
        return carry

    _unrolled(n_seg, coeff_body, 0)

    def tot_body(j, carry):
        new = []
        for l in range(N_LRU_SLABS):
            a_tot, u_tot = carry[l]
            a_j = a_s[l, pl.ds(j, n_seg, stride=pitch), :]
            u_j = u_s[l, pl.ds(j, n_seg, stride=pitch), :]
            new.append((a_j * a_tot, a_j * u_tot + u_j))
        return tuple(new)

    init = tuple((jnp.ones((n_seg, LANES), jnp.float32), jnp.zeros((n_seg, LANES), jnp.float32))
                 for _ in range(N_LRU_SLABS))
    totals = _unrolled(seg, tot_body, init)

    h_in = []
    for l in range(N_LRU_SLABS):
        a_tot, u_tot = totals[l]
        h = hin_s[l, 0:1, :]
        for c in range(n_seg):
            h = a_tot[c:c + 1, :] * h + u_tot[c:c + 1, :]
            hin_s[l, c + 1:c + 2, :] = h
        h_in.append(hin_s[l, 0:n_seg, :])
        hin_s[l, 0:1, :] = h

    def scan_body(j, carry):
        new = []
        for l in range(N_LRU_SLABS):
            a_j = a_s[l, pl.ds(j, n_seg, stride=pitch), :]
            u_j = u_s[l, pl.ds(j, n_seg, stride=pitch), :]
            h = a_j * carry[l] + u_j
            u_s[l, pl.ds(j, n_seg, stride=pitch), :] = h
            new.append(h)
        return tuple(new)

    scores = _dot(_bf16(z_s[:, OFF_Q:OFF_Q + D_XATTN]), kst_s[...])
    sc_s[...] = scores
    after_scores = _zero_row_after(scores[tile - SUBLANES:tile, N_XHEADS * N_MEM - LANES:])
    _unrolled(seg, scan_body, tuple(h + after_scores for h in h_in))

    for l in range(N_CONV_SLABS):
        lo = l * LANES
        uc_s[l, CONV_HALO:CONV_HALO + tile, :] = _gated_of_twice(
            z_s[:, OFF_GLU_A + lo:OFF_GLU_A + lo + LANES],
            z_s[:, OFF_GLU_B + lo:OFF_GLU_B + lo + LANES])

    def seg_body(c, carry):
        r0 = _aligned(c * seg, seg)
        p0 = _aligned(c * pitch, SUBLANES)
        for l in range(N_LRU_SLABS):
            lo = l * LANES
            gate = z_s[pl.ds(r0, seg), OFF_LRU_GATE + lo:OFF_LRU_GATE + lo + LANES]
            mix_s[pl.ds(r0, seg), lo:lo + LANES] = _bf16(u_s[l, pl.ds(p0, seg), :] * _silu_of_twice(gate))
        ys = []
        for l in range(N_CONV_SLABS):
            lo = l * LANES
            w_row = lambda k, lo=lo: w_dw_ref[k, :, lo:lo + LANES]
            ys.append(_conv_taps(uc_s, l, w_row, r0, seg, CONV_W, CONV_HALO)
                      + b_dw_ref[:, lo:lo + LANES])
        outs = _layernorm_silu_slabs(ys, ln_g_ref, ln_b_ref)
        for l in range(N_CONV_SLABS):
            lo = l * LANES
            ln_s[pl.ds(r0, seg), lo:lo + LANES] = _bf16(outs[l])
        return carry

    _unrolled(n_seg, seg_body, 0)

    yc = (_dot(ln_s[...], wpw_bf[...]) + b_pw_ref[...]) * _silu_of_twice(
        z_s[:, OFF_CONV_GATE:OFF_CONV_GATE + D_CONV])
    mix_s[:, D_LRU:D_LRU + D_CONV] = _bf16(yc)

    def attn_body(c, carry):
        r0 = _aligned(c * ATTN_ROWS, ATTN_ROWS)
        o = _softmax_pv(sc_s[pl.ds(r0, ATTN_ROWS), :], vst_s)
        gate = z_s[pl.ds(r0, ATTN_ROWS), OFF_ATTN_GATE:OFF_ATTN_GATE + D_XATTN]
        mix_s[pl.ds(r0, ATTN_ROWS), D_LRU + D_CONV:D_MIX] = _bf16(o * _silu_of_twice(gate))
        return carry

    _unrolled(tile // ATTN_ROWS, attn_body, 0)

    y = x_ref[0] + _dot(mix_s[...], wout_bf[...])
    y_ref[0] = _rmsnorm(y, g_final_ref[...])

    for l in range(N_LRU_SLABS):
        xc_s[l, LRU_HALO - (LRU_CONV - 1):LRU_HALO, :] = (
            xc_s[l, LRU_HALO + tile - (LRU_CONV - 1):LRU_HALO + tile, :])
    for l in range(N_CONV_SLABS):
        uc_s[l, CONV_HALO - (CONV_W - 1):CONV_HALO, :] = (
            uc_s[l, CONV_HALO + tile - (CONV_W - 1):CONV_HALO + tile, :])

    def write_state(bb):
        for l in range(N_LRU_SLABS):
            lo = l * LANES
            h_out_ref[bb:bb + 1, lo:lo + LANES] = hin_s[l, 0:1, :]
            for k in range(LRU_CONV - 1):
                r = LRU_HALO - (LRU_CONV - 1) + k
                lb_out_ref[k, bb:bb + 1, lo:lo + LANES] = xc_s[l, r:r + 1, :]
        for l in range(N_CONV_SLABS):
            lo = l * LANES
            for k in range(CONV_W - 1):
                r = CONV_HALO - (CONV_W - 1) + k
                cb_out_ref[k, bb:bb + 1, lo:lo + LANES] = uc_s[l, r:r + 1, :]

    for bb in range(h_out_ref.shape[0]):
        pl.when(jnp.logical_and(t == n_t - 1, b == bb))(functools.partial(write_state, bb))


def _whole(shape):
    nd = len(shape)
    return pl.BlockSpec(shape, lambda *_: (0,) * nd, pipeline_mode=pl.Buffered(1))


def _prompt(x, kt, vt, w):
    bsz, seq, _ = x.shape
    tile = PROMPT_TILE
    n_t = seq // tile
    n_seg = tile // PROMPT_SEG
    weights = [w["g_norm"], w["w_in"], w["w_lc"], w["b_lc"], w["w_ga"], w["w_gx"], w["b_ga"], w["b_gx"],
               w["lam"], w["w_dw"], w["b_dw"], w["ln_g"], w["ln_b"], w["w_pw"], w["b_pw"],
               w["w_out"], w["g_final"]]
    weight_specs = [
        _whole((1, D_MODEL)), _whole((None, D_MODEL, D_IN)), _whole((None, LRU_CONV, D_LRU)),
        _whole((1, D_LRU)), _whole((None, LRU_BLOCKS, LRU_BLOCK, LRU_BLOCK)),
        _whole((None, LRU_BLOCKS, LRU_BLOCK, LRU_BLOCK)), _whole((1, D_LRU)), _whole((1, D_LRU)),
        _whole((1, D_LRU)), _whole((CONV_W, 1, D_CONV)), _whole((1, D_CONV)), _whole((1, D_CONV)),
        _whole((1, D_CONV)), _whole((None, D_CONV, D_CONV)), _whole((1, D_CONV)),
        _whole((None, D_MIX, D_MODEL)), _whole((1, D_MODEL)),
    ]
    in_specs = [
        pl.BlockSpec((1, tile, D_MODEL), lambda b, t: (b, t, 0)),
        pl.BlockSpec((1, D_XATTN, N_MEM), lambda b, t: (b, 0, 0)),
        pl.BlockSpec((1, D_XATTN, N_MEM), lambda b, t: (b, 0, 0)),
    ] + weight_specs
    out_specs = [
        pl.BlockSpec((1, tile, D_MODEL), lambda b, t: (b, t, 0)),
        _whole((bsz, D_LRU)),
        _whole((LRU_CONV - 1, bsz, D_LRU)),
        _whole((CONV_W - 1, bsz, D_CONV)),
        _whole((D_MODEL, D_IN)), _whole((D_MIX, D_MODEL)), _whole((D_CONV, D_CONV)),
        _whole((2, GATE_HALF, D_LRU)),
    ]
    out_shape = [
        jax.ShapeDtypeStruct((bsz, seq, D_MODEL), jnp.float32),
        jax.ShapeDtypeStruct((bsz, D_LRU), jnp.float32),
        jax.ShapeDtypeStruct((LRU_CONV - 1, bsz, D_LRU), jnp.float32),
        jax.ShapeDtypeStruct((CONV_W - 1, bsz, D_CONV), jnp.float32),
        jax.ShapeDtypeStruct((D_MODEL, D_IN), jnp.bfloat16),
        jax.ShapeDtypeStruct((D_MIX, D_MODEL), jnp.bfloat16),
        jax.ShapeDtypeStruct((D_CONV, D_CONV), jnp.bfloat16),
        jax.ShapeDtypeStruct((2, GATE_HALF, D_LRU), jnp.bfloat16),
    ]
    scratch = [
        pltpu.VMEM((tile, D_IN), jnp.float32),
        pltpu.VMEM((N_LRU_SLABS, LRU_HALO + tile, LANES), jnp.float32),
        pltpu.VMEM((tile, D_LRU), jnp.float32),
        pltpu.VMEM((tile, 2 * D_LRU), jnp.float32),
        pltpu.VMEM((N_LRU_SLABS, n_seg * PROMPT_PITCH, LANES), jnp.float32),
        pltpu.VMEM((N_LRU_SLABS, n_seg * PROMPT_PITCH, LANES), jnp.float32),
        pltpu.VMEM((N_CONV_SLABS, CONV_HALO + tile, LANES), jnp.float32),
        pltpu.VMEM((tile, D_CONV), jnp.bfloat16),
        pltpu.VMEM((tile, D_MIX), jnp.bfloat16),
        pltpu.VMEM((N_LRU_SLABS, 2 * SUBLANES, LANES), jnp.float32),
        pltpu.VMEM((D_XATTN, N_XHEADS * N_MEM), jnp.bfloat16),
        pltpu.VMEM((D_XATTN, N_XHEADS * N_MEM), jnp.bfloat16),
        pltpu.VMEM((tile, N_XHEADS * N_MEM), jnp.float32),
    ]
    return pl.pallas_call(
        _prompt_kernel,
        grid=(bsz, n_t),
        in_specs=in_specs,
        out_specs=out_specs,
        out_shape=out_shape,
        scratch_shapes=scratch,
        compiler_params=pltpu.CompilerParams(
            dimension_semantics=("arbitrary", "arbitrary"),
            vmem_limit_bytes=VMEM_LIMIT_BYTES),
        name="prompt_layer",
    )(x, kt, vt, *weights)


def _sample_kernel(nb, steps,
                   x_ref, h0_ref, lb_ref, cb_ref, kt_ref, vt_ref,
                   g_norm_ref, win_bf, w_lc_ref, b_lc_ref, wg_bf,
                   b_ga_ref, b_gx_ref, lam_ref, w_dw_ref, b_dw_ref, ln_g_ref, ln_b_ref,
                   wpw_bf, b_pw_ref, wout_bf, g_final_ref,
                   y_ref, h_out_ref, lb_out_ref, cb_out_ref,
                   z_s, xc_s, xl_s, pre_s, a_s, u_s, uc_s, ln_s, mix_s):
    rows = nb * steps
    pitch = SAMPLE_PITCH
    cpitch = CONV_HALO + steps
    prow = nb * pitch

    x = x_ref[...].reshape(rows, D_MODEL)
    hn = _rmsnorm(x, g_norm_ref[...])
    z_s[...] = _dot(_bf16(hn), win_bf[...])

    for b in range(nb):
        for l in range(N_LRU_SLABS):
            lo = l * LANES
            zcol = OFF_LRU_IN + lo
            xc_s[l, b * pitch:b * pitch + LRU_HALO, :] = jnp.zeros((LRU_HALO, LANES), jnp.float32)
            for k in range(LRU_CONV - 1):
                r = b * pitch + LRU_HALO - (LRU_CONV - 1) + k
                xc_s[l, r:r + 1, :] = lb_ref[k, b:b + 1, lo:lo + LANES]
                rn = (b + 1) * steps - (LRU_CONV - 1) + k
                lb_out_ref[k, b:b + 1, lo:lo + LANES] = z_s[rn:rn + 1, zcol:zcol + LANES]
            xc_s[l, b * pitch + LRU_HALO:(b + 1) * pitch, :] = (
                z_s[b * steps:(b + 1) * steps, zcol:zcol + LANES])
    n_conv = prow - LRU_HALO
    for l in range(N_LRU_SLABS):
        lo = l * LANES
        w_row = lambda k, lo=lo: 0.5 * w_lc_ref[k:k + 1, lo:lo + LANES]
        xl = (_conv_taps(xc_s, l, w_row, 0, n_conv, LRU_CONV, LRU_HALO)
              + 0.5 * b_lc_ref[:, lo:lo + LANES])
        xl_s[0:n_conv, lo:lo + LANES] = xl
    xl_s[n_conv:prow, :] = jnp.zeros((LRU_HALO, D_LRU), jnp.float32)
    _lru_gates(_bf16(xl_s[...]), wg_bf, pre_s)

    hc_ls = (0.5 * LRU_C) * _log_sigmoid(lam_ref[...])
    hb_ga = 0.5 * b_ga_ref[...]
    hb_gx = 0.5 * b_gx_ref[...]
    for l in range(N_LRU_SLABS):
        lo = l * LANES
        ca, cx = _pre_cols(l)
        a, u = _lru_coeffs(pre_s[:, ca:ca + LANES], pre_s[:, cx:cx + LANES], xl_s[:, lo:lo + LANES],
                           hb_ga[:, lo:lo + LANES], hb_gx[:, lo:lo + LANES],
                           hc_ls[:, lo:lo + LANES])
        a_s[l, :, :] = a
        u_s[l, :, :] = u

    n_grp = nb // SUBLANES

    def scan_body(j, carry):
        new = []
        for g in range(n_grp):
            for l in range(N_LRU_SLABS):
                base = g * SUBLANES * pitch
                a_j = a_s[l, pl.ds(base + j, SUBLANES, stride=pitch), :]
                u_j = u_s[l, pl.ds(base + j, SUBLANES, stride=pitch), :]
                h = a_j * carry[g * N_LRU_SLABS + l] + u_j
                u_s[l, pl.ds(base + j, SUBLANES, stride=pitch), :] = h
                new.append(h)
        return tuple(new)

    h0 = tuple(h0_ref[g * SUBLANES:(g + 1) * SUBLANES, l * LANES:(l + 1) * LANES]
               for g in range(n_grp) for l in range(N_LRU_SLABS))
    h_fin = _unrolled(steps, scan_body, h0)
    for g in range(n_grp):
        for l in range(N_LRU_SLABS):
            h_out_ref[g * SUBLANES:(g + 1) * SUBLANES, l * LANES:(l + 1) * LANES] = (
                h_fin[g * N_LRU_SLABS + l])

    for b in range(nb):
        for l in range(N_LRU_SLABS):
            lo = l * LANES
            gate = z_s[b * steps:(b + 1) * steps, OFF_LRU_GATE + lo:OFF_LRU_GATE + lo + LANES]
            mix_s[b * steps:(b + 1) * steps, lo:lo + LANES] = _bf16(
                u_s[l, b * pitch:b * pitch + steps, :] * _silu_of_twice(gate))

    for b in range(nb):
        for l in range(N_CONV_SLABS):
            lo = l * LANES
            uc = _gated_of_twice(
                z_s[b * steps:(b + 1) * steps, OFF_GLU_A + lo:OFF_GLU_A + lo + LANES],
                z_s[b * steps:(b + 1) * steps, OFF_GLU_B + lo:OFF_GLU_B + lo + LANES])
            uc_s[l, b * cpitch + CONV_HALO:(b + 1) * cpitch, :] = uc
            for k in range(CONV_W - 1):
                r = b * cpitch + CONV_HALO - (CONV_W - 1) + k
                uc_s[l, r:r + 1, :] = cb_ref[k, b:b + 1, lo:lo + LANES]
                rn = (b + 1) * cpitch - (CONV_W - 1) + k
                cb_out_ref[k, b:b + 1, lo:lo + LANES] = uc_s[l, rn:rn + 1, :]

    def conv_body(i, carry):
        for j in range(SAMPLE_UNROLL):
            b = i * SAMPLE_UNROLL + j
            r0 = pl.multiple_of(b * cpitch, SUBLANES)
            o0 = pl.multiple_of(b * steps, steps)
            ys = []
            for l in range(N_CONV_SLABS):
                lo = l * LANES
                w_row = lambda k, lo=lo: w_dw_ref[k, :, lo:lo + LANES]
                ys.append(_conv_taps(uc_s, l, w_row, r0, steps, CONV_W, CONV_HALO)
                          + b_dw_ref[:, lo:lo + LANES])
            outs = _layernorm_silu_slabs(ys, ln_g_ref, ln_b_ref)
            for l in range(N_CONV_SLABS):
                lo = l * LANES
                ln_s[pl.ds(o0, steps), lo:lo + LANES] = _bf16(outs[l])
        return carry

    lax.fori_loop(0, nb // SAMPLE_UNROLL, conv_body, 0)

    yc = (_dot(ln_s[...], wpw_bf[...]) + b_pw_ref[...]) * _silu_of_twice(
        z_s[:, OFF_CONV_GATE:OFF_CONV_GATE + D_CONV])
    mix_s[:, D_LRU:D_LRU + D_CONV] = _bf16(yc)

    masks = _lane_head_masks()

    def attn_body(b, carry):
        o0 = b * steps
        q = z_s[o0:o0 + steps, OFF_Q:OFF_Q + D_XATTN]
        o = _attention_qstack(q, _bf16(kt_ref[b]), _bf16(vt_ref[b]), masks)
        gate = z_s[o0:o0 + steps, OFF_ATTN_GATE:OFF_ATTN_GATE + D_XATTN]
        mix_s[o0:o0 + steps, D_LRU + D_CONV:D_MIX] = _bf16(o * _silu_of_twice(gate))
        return carry

    _unrolled(nb, attn_body, 0)

    y = x + _dot(mix_s[...], wout_bf[...])
    y_ref[...] = _rmsnorm(y, g_final_ref[...]).reshape(nb, steps, D_MODEL)


def _sample(x, h0, lb, cb, kt, vt, w, win_bf, wout_bf, wpw_bf, wg_bf):
    bsz, steps, _ = x.shape
    nb = SAMPLE_GROUP
    assert bsz % nb == 0 and nb % SUBLANES == 0 and nb % SAMPLE_UNROLL == 0
    assert steps + LRU_HALO == SAMPLE_PITCH and steps >= CONV_W - 1
    rows = nb * steps
    prow = nb * SAMPLE_PITCH
    weights = [w["g_norm"], win_bf, w["w_lc"], w["b_lc"], wg_bf, w["b_ga"], w["b_gx"],
               w["lam"], w["w_dw"], w["b_dw"], w["ln_g"], w["ln_b"], wpw_bf, w["b_pw"],
               wout_bf, w["g_final"]]
    weight_specs = [
        _whole((1, D_MODEL)), _whole((D_MODEL, D_IN)), _whole((None, LRU_CONV, D_LRU)),
        _whole((1, D_LRU)), _whole((2, GATE_HALF, D_LRU)), _whole((1, D_LRU)), _whole((1, D_LRU)),
        _whole((1, D_LRU)), _whole((CONV_W, 1, D_CONV)), _whole((1, D_CONV)), _whole((1, D_CONV)),
        _whole((1, D_CONV)), _whole((D_CONV, D_CONV)), _whole((1, D_CONV)),
        _whole((D_MIX, D_MODEL)), _whole((1, D_MODEL)),
    ]
    in_specs = [
        pl.BlockSpec((nb, steps, D_MODEL), lambda g: (g, 0, 0)),
        pl.BlockSpec((nb, D_LRU), lambda g: (g, 0)),
        pl.BlockSpec((LRU_CONV - 1, nb, D_LRU), lambda g: (0, g, 0)),
        pl.BlockSpec((CONV_W - 1, nb, D_CONV), lambda g: (0, g, 0)),
        pl.BlockSpec((nb, D_XATTN, N_MEM), lambda g: (g, 0, 0)),
        pl.BlockSpec((nb, D_XATTN, N_MEM), lambda g: (g, 0, 0)),
    ] + weight_specs
    out_specs = [
        pl.BlockSpec((nb, steps, D_MODEL), lambda g: (g, 0, 0)),
        pl.BlockSpec((nb, D_LRU), lambda g: (g, 0)),
        pl.BlockSpec((LRU_CONV - 1, nb, D_LRU), lambda g: (0, g, 0)),
        pl.BlockSpec((CONV_W - 1, nb, D_CONV), lambda g: (0, g, 0)),
    ]
    out_shape = [
        jax.ShapeDtypeStruct((bsz, steps, D_MODEL), jnp.float32),
        jax.ShapeDtypeStruct((bsz, D_LRU), jnp.float32),
        jax.ShapeDtypeStruct((LRU_CONV - 1, bsz, D_LRU), jnp.float32),
        jax.ShapeDtypeStruct((CONV_W - 1, bsz, D_CONV), jnp.float32),
    ]
    scratch = [
        pltpu.VMEM((rows, D_IN), jnp.float32),
        pltpu.VMEM((N_LRU_SLABS, prow, LANES), jnp.float32),
        pltpu.VMEM((prow, D_LRU), jnp.float32),
        pltpu.VMEM((prow, 2 * D_LRU), jnp.float32),
        pltpu.VMEM((N_LRU_SLABS, prow, LANES), jnp.float32),
        pltpu.VMEM((N_LRU_SLABS, prow, LANES), jnp.float32),
        pltpu.VMEM((N_CONV_SLABS, nb * (CONV_HALO + steps), LANES), jnp.float32),
        pltpu.VMEM((rows, D_CONV), jnp.bfloat16),
        pltpu.VMEM((rows, D_MIX), jnp.bfloat16),
    ]
    return pl.pallas_call(
        functools.partial(_sample_kernel, nb, steps),
        grid=(bsz // nb,),
        in_specs=in_specs,
        out_specs=out_specs,
        out_shape=out_shape,
        scratch_shapes=scratch,
        compiler_params=pltpu.CompilerParams(
            dimension_semantics=("arbitrary",),
            vmem_limit_bytes=VMEM_LIMIT_BYTES),
        name="sample_layer",
    )(x, h0, lb, cb, kt, vt, *weights)


def _time_major(state):
    return jnp.transpose(state, (1, 0, 2))


def _memory_t(mem):
    bsz = mem.shape[0]
    return jnp.transpose(mem, (0, 2, 3, 1)).reshape(bsz, D_XATTN, N_MEM)


def _memory_from_t(mem_t):
    bsz = mem_t.shape[0]
    return jnp.transpose(mem_t.reshape(bsz, N_XHEADS, XHEAD_DIM, N_MEM), (0, 3, 1, 2))


def kernel(x_prompt, x_sample, state_lru_h, state_lru_conv, state_conv, cache_mem_k, cache_mem_v,
           mem_prompt, g_norm, w_in, w_lru_conv, b_lru_conv, w_gate_a, b_gate_a, w_gate_x, b_gate_x,
           lru_lambda, w_dw, b_dw, ln_g, ln_b, w_pw, b_pw, g_mem, w_mem_k, w_mem_v, w_out, g_final):
    depth = g_norm.shape[0]
    assert depth == 1
    w = {
        "g_norm": g_norm, "w_in": w_in, "w_lc": w_lru_conv, "b_lc": b_lru_conv,
        "w_ga": w_gate_a, "w_gx": w_gate_x, "b_ga": b_gate_a, "b_gx": b_gate_x, "lam": lru_lambda,
        "w_dw": jnp.transpose(w_dw, (1, 0, 2)), "b_dw": b_dw, "ln_g": ln_g, "ln_b": ln_b,
        "w_pw": w_pw, "b_pw": b_pw, "w_out": w_out, "g_final": g_final.reshape(1, D_MODEL),
    }
    mkt, mvt = _memkv(mem_prompt, g_mem, w_mem_k, w_mem_v)

    y_prompt, ph, plb, pcb, win_bf, wout_bf, wpw_bf, wg_bf = _prompt(x_prompt, mkt, mvt, w)

    y_sample, sh, slb, scb = _sample(
        x_sample, state_lru_h[0], _time_major(state_lru_conv[0]), _time_major(state_conv[0]),
        _memory_t(cache_mem_k[0]), _memory_t(cache_mem_v[0]), w, win_bf, wout_bf, wpw_bf, wg_bf)

    return (y_prompt, y_sample,
            ph[None], _time_major(plb)[None], _time_major(pcb)[None],
            _memory_from_t(mkt)[None], _memory_from_t(mvt)[None],
            sh[None], _time_major(slb)[None], _time_major(scb)[None])
```

```python
import functools

import jax
import jax.numpy as jnp
from jax import lax
from jax.experimental import pallas as pl
from jax.experimental.pallas import tpu as pltpu

D_MODEL = 1024
N_MEM = 256
D_LRU = 512
LRU_BLOCKS = 8
LRU_BLOCK = D_LRU // LRU_BLOCKS
LRU_CONV = 4
LRU_C = 8.0
D_CONV = 256
CONV_W = 31
N_XHEADS = 4
XHEAD_DIM = 64
D_XATTN = N_XHEADS * XHEAD_DIM
D_MIX = D_LRU + D_CONV + D_XATTN
D_IN = 2 * D_LRU + 3 * D_CONV + 2 * D_XATTN
EPS = 1e-6

LANES = 128
SUBLANES = 8
N_LRU_SLABS = D_LRU // LANES
N_CONV_SLABS = D_CONV // LANES
GATE_HALF = D_LRU // 2

OFF_LRU_IN = 0
OFF_LRU_GATE = D_LRU
OFF_GLU_A = 2 * D_LRU
OFF_GLU_B = OFF_GLU_A + D_CONV
OFF_CONV_GATE = OFF_GLU_B + D_CONV
OFF_Q = OFF_CONV_GATE + D_CONV
OFF_ATTN_GATE = OFF_Q + D_XATTN

LRU_HALO = 8
CONV_HALO = 32

PROMPT_TILE = 512
PROMPT_SEG = 64
PROMPT_PITCH = 68
SAMPLE_PITCH = 40
SAMPLE_GROUP = 16
SAMPLE_UNROLL = 4
ATTN_ROWS = 128

VMEM_LIMIT_BYTES = 56 * 1024 * 1024
HALF_LOG2_E = 0.7213475204444817


def _silu_of_twice(hx):
    return hx + hx * jnp.tanh(hx)


def _gated_of_twice(ha, hb):
    return ha + ha * jnp.tanh(hb)


def _w_in_col_scale():
    col = lax.broadcasted_iota(jnp.int32, (1, D_IN), 1)
    plain = (col < OFF_LRU_GATE) | ((col >= OFF_Q) & (col < OFF_ATTN_GATE))
    return jnp.where(plain, 1.0, 0.5)


def _sqrt_nonneg(x):
    return jnp.exp2(jnp.log(x) * HALF_LOG2_E)


def _rmsnorm(x, g):
    ms = jnp.mean(x * x, axis=-1, keepdims=True)
    return x * lax.rsqrt(ms + EPS) * g


def _log_sigmoid(x):
    y = -x
    return -(jnp.maximum(y, 0.0) + jnp.log1p(jnp.exp(-jnp.abs(y))))


def _bf16(x):
    return x.astype(jnp.bfloat16)


def _dot(a, b):
    return jnp.dot(a, b, preferred_element_type=jnp.float32)


def _dot_nt(a, b):
    return lax.dot_general(a, b, (((1,), (1,)), ((), ())), preferred_element_type=jnp.float32)


def _unrolled(n, body, init):
    carry = init
    for i in range(n):
        carry = body(i, carry)
    return carry


def _aligned(x, m):
    return x if isinstance(x, int) else pl.multiple_of(x, m)


def _head_of(index):
    return lax.shift_right_logical(index, XHEAD_DIM.bit_length() - 1)


def _lane_head_masks():
    head = _head_of(lax.broadcasted_iota(jnp.int32, (1, D_XATTN), 1))
    return [(head == h).astype(jnp.float32) for h in range(N_XHEADS)]


def _row_head_masks():
    head = _head_of(lax.broadcasted_iota(jnp.int32, (D_XATTN, 1), 0))
    return [(head == h).astype(jnp.float32) for h in range(N_XHEADS)]


def _head_stack_t(kt, vt, kst_ref, vst_ref):
    scale = XHEAD_DIM ** -0.5
    for h, m in enumerate(_row_head_masks()):
        kst_ref[:, h * N_MEM:(h + 1) * N_MEM] = _bf16(kt * (m * scale))
        vst_ref[:, h * N_MEM:(h + 1) * N_MEM] = _bf16(vt * m)


def _zero_row_after(x):
    bits = lax.bitcast_convert_type(x, jnp.uint32)
    zero = lax.shift_right_logical(lax.shift_right_logical(bits, jnp.uint32(16)), jnp.uint32(16))
    return lax.bitcast_convert_type(zero, jnp.float32)[0:1, :]


def _softmax_heads(s):
    ps = []
    for h in range(N_XHEADS):
        sh = s[:, h * N_MEM:(h + 1) * N_MEM]
        e = jnp.exp(sh - jnp.max(sh, axis=-1, keepdims=True))
        inv = 1.0 / jnp.sum(e, axis=-1, keepdims=True)
        ps.append(_bf16(e * inv))
    return jnp.concatenate(ps, axis=1)


def _attention_qstack(q, kt_bf, vt_bf, masks):
    rows = q.shape[0]
    scale = XHEAD_DIM ** -0.5
    qm = jnp.concatenate([_bf16(q * (m * scale)) for m in masks], axis=0)
    s = _dot(qm, kt_bf)
    e = jnp.exp(s - jnp.max(s, axis=-1, keepdims=True))
    inv = 1.0 / jnp.sum(e, axis=-1, keepdims=True)
    o_all = _dot_nt(_bf16(e * inv), vt_bf)
    o = o_all[0:rows] * masks[0]
    for h in range(1, N_XHEADS):
        o = o + o_all[h * rows:(h + 1) * rows] * masks[h]
    return o


def _lru_gates(xl_bf, wg_ref, pre_s):
    pre_s[:, 0:D_LRU] = _dot(xl_bf[:, 0:GATE_HALF], wg_ref[0])
    pre_s[:, D_LRU:2 * D_LRU] = _dot(xl_bf[:, GATE_HALF:D_LRU], wg_ref[1])


def _pre_cols(l):
    half_slabs = N_LRU_SLABS // 2
    base = (l // half_slabs) * D_LRU + (l % half_slabs) * LANES
    return base, base + GATE_HALF


def _lru_coeffs(hpre_a, hpre_x, hx, hb_a, hb_x, hc_ls):
    t_r = jnp.tanh(hpre_a + hb_a)
    t_i = jnp.tanh(hpre_x + hb_x)
    log_a = hc_ls + hc_ls * t_r
    a = jnp.exp(log_a)
    one_minus_a2 = -jnp.tanh(log_a) * (a * a + 1.0)
    u = _sqrt_nonneg(one_minus_a2) * (hx + hx * t_i)
    return a, u


def _conv_taps(src_s, l, w_row, row0, nrows, width, halo):
    acc = None
    for k in range(width):
        x_k = src_s[l, pl.ds(row0 + (halo - (width - 1) + k), nrows), :]
        acc = w_row(k) * x_k if acc is None else acc + w_row(k) * x_k
    return acc


def _layernorm_silu_slabs(ys, g_ref, b_ref):
    tot = ys[0]
    for y in ys[1:]:
        tot = tot + y
    mu = jnp.sum(tot, axis=-1, keepdims=True) * (1.0 / D_CONV)
    cs = [y - mu for y in ys]
    sq = cs[0] * cs[0]
    for c in cs[1:]:
        sq = sq + c * c
    var = jnp.sum(sq, axis=-1, keepdims=True) * (1.0 / D_CONV)
    inv = lax.rsqrt(var + EPS)
    outs = []
    for l, c in enumerate(cs):
        lo = l * LANES
        half_g = 0.5 * g_ref[:, lo:lo + LANES]
        half_b = 0.5 * b_ref[:, lo:lo + LANES]
        outs.append(_silu_of_twice(c * inv * half_g + half_b))
    return outs


def _gate_block_diag(w_blocks):
    n = (LRU_BLOCKS // 2) * LRU_BLOCK
    stacked = _bf16(w_blocks.reshape(n, LRU_BLOCK))
    src = lax.broadcasted_iota(jnp.int32, (LRU_BLOCK, n), 0)
    dst = lax.broadcasted_iota(jnp.int32, (LRU_BLOCK, n), 1)
    spread = _bf16(((dst & (LRU_BLOCK - 1)) == src).astype(jnp.float32))
    rows = _head_of(lax.broadcasted_iota(jnp.int32, (n, n), 0))
    cols = _head_of(lax.broadcasted_iota(jnp.int32, (n, n), 1))
    return _bf16(_dot(stacked, spread) * (rows == cols).astype(jnp.float32))


def _memkv_kernel(mem_ref, g_ref, wk_ref, wv_ref, kt_ref, vt_ref):
    m = _bf16(_rmsnorm(mem_ref[0], g_ref[...]))
    kt_ref[0] = _dot(m, _bf16(wk_ref[...])).T
    vt_ref[0] = _dot(m, _bf16(wv_ref[...])).T


def _memkv(mem, g_mem, w_k, w_v):
    bsz = mem.shape[0]
    return pl.pallas_call(
        _memkv_kernel,
        grid=(bsz,),
        in_specs=[
            pl.BlockSpec((1, N_MEM, D_MODEL), lambda b: (b, 0, 0)),
            pl.BlockSpec((1, D_MODEL), lambda b: (0, 0)),
            pl.BlockSpec((None, D_MODEL, D_XATTN), lambda b: (0, 0, 0)),
            pl.BlockSpec((None, D_MODEL, D_XATTN), lambda b: (0, 0, 0)),
        ],
        out_specs=[
            pl.BlockSpec((1, D_XATTN, N_MEM), lambda b: (b, 0, 0)),
            pl.BlockSpec((1, D_XATTN, N_MEM), lambda b: (b, 0, 0)),
        ],
        out_shape=[jax.ShapeDtypeStruct((bsz, D_XATTN, N_MEM), jnp.float32)] * 2,
        compiler_params=pltpu.CompilerParams(dimension_semantics=("arbitrary",)),
        name="memkv",
    )(mem, g_mem, w_k, w_v)


def _prompt_kernel(x_ref, kt_ref, vt_ref, g_norm_ref, w_in_ref, w_lc_ref, b_lc_ref, wga_ref, wgx_ref,
                   b_ga_ref, b_gx_ref, lam_ref, w_dw_ref, b_dw_ref, ln_g_ref, ln_b_ref,
                   w_pw_ref, b_pw_ref, w_out_ref, g_final_ref,
                   y_ref, h_out_ref, lb_out_ref, cb_out_ref, win_bf, wout_bf, wpw_bf, wg_bf,
                   z_s, xc_s, xl_s, pre_s, a_s, u_s, uc_s, ln_s, mix_s, hin_s, kst_s, vst_s, sc_s, p_s):
    b = pl.program_id(0)
    t = pl.program_id(1)
    n_t = pl.num_programs(1)
    tile = PROMPT_TILE
    seg = PROMPT_SEG
    n_seg = tile // seg
    pitch = PROMPT_PITCH

    @pl.when(jnp.logical_and(b == 0, t == 0))
    def _():
        win_bf[...] = _bf16(w_in_ref[...] * _w_in_col_scale())
        wout_bf[...] = _bf16(w_out_ref[...])
        wpw_bf[...] = _bf16(w_pw_ref[...])
        half = LRU_BLOCKS // 2
        for h in range(2):
            wg_bf[h, :, 0:GATE_HALF] = _gate_block_diag(wga_ref[h * half:(h + 1) * half])
            wg_bf[h, :, GATE_HALF:D_LRU] = _gate_block_diag(wgx_ref[h * half:(h + 1) * half])

    @pl.when(t == 0)
    def _():
        xc_s[:, 0:LRU_HALO, :] = jnp.zeros((N_LRU_SLABS, LRU_HALO, LANES), jnp.float32)
        uc_s[:, 0:CONV_HALO, :] = jnp.zeros((N_CONV_SLABS, CONV_HALO, LANES), jnp.float32)
        hin_s[:, 0:1, :] = jnp.zeros((N_LRU_SLABS, 1, LANES), jnp.float32)
        _head_stack_t(kt_ref[0], vt_ref[0], kst_s, vst_s)

    hn = _rmsnorm(x_ref[0], g_norm_ref[...])
    z_s[...] = _dot(_bf16(hn), win_bf[...])

    for l in range(N_LRU_SLABS):
        lo = l * LANES
        xc_s[l, LRU_HALO:LRU_HALO + tile, :] = z_s[:, OFF_LRU_IN + lo:OFF_LRU_IN + lo + LANES]
    for l in range(N_LRU_SLABS):
        lo = l * LANES
        w_row = lambda k, lo=lo: 0.5 * w_lc_ref[k:k + 1, lo:lo + LANES]
        xl = (_conv_taps(xc_s, l, w_row, 0, tile, LRU_CONV, LRU_HALO)
              + 0.5 * b_lc_ref[:, lo:lo + LANES])
        xl_s[:, lo:lo + LANES] = xl
    _lru_gates(_bf16(xl_s[...]), wg_bf, pre_s)

    hc_ls = (0.5 * LRU_C) * _log_sigmoid(lam_ref[...])
    hb_ga = 0.5 * b_ga_ref[...]
    hb_gx = 0.5 * b_gx_ref[...]

    def coeff_body(c, carry):
        r0 = _aligned(c * seg, seg)
        p0 = _aligned(c * pitch, SUBLANES)
        for l in range(N_LRU_SLABS):
            lo = l * LANES
            ca, cx = _pre_cols(l)
            a, u = _lru_coeffs(pre_s[pl.ds(r0, seg), ca:ca + LANES],
                               pre_s[pl.ds(r0, seg), cx:cx + LANES],
                               xl_s[pl.ds(r0, seg), lo:lo + LANES],
                               hb_ga[:, lo:lo + LANES], hb_gx[:, lo:lo + LANES],
                               hc_ls[:, lo:lo + LANES])
            a_s[l, pl.ds(p0, seg), :] = a
            u_s[l, pl.ds(p0, seg), :] = u
        return carry

    _unrolled(n_seg, coeff_body, 0)

    def tot_body(j, carry):
        new = []
        for l in range(N_LRU_SLABS):
            a_tot, u_tot = carry[l]
            a_j = a_s[l, pl.ds(j, n_seg, stride=pitch), :]
            u_j = u_s[l, pl.ds(j, n_seg, stride=pitch), :]
            new.append((a_j * a_tot, a_j * u_tot + u_j))
        return tuple(new)

    init = tuple((jnp.ones((n_seg, LANES), jnp.float32), jnp.zeros((n_seg, LANES), jnp.float32))
                 for _ in range(N_LRU_SLABS))
    totals = _unrolled(seg, tot_body, init)

    h_in = []
    for l in range(N_LRU_SLABS):
        a_tot, u_tot = totals[l]
        h = hin_s[l, 0:1, :]
        for c in range(n_seg):
            h = a_tot[c:c + 1, :] * h + u_tot[c:c + 1, :]
            hin_s[l, c + 1:c + 2, :] = h
        h_in.append(hin_s[l, 0:n_seg, :])
        hin_s[l, 0:1, :] = h

    def scan_body(j, carry):
        new = []
        for l in range(N_LRU_SLABS):
            a_j = a_s[l, pl.ds(j, n_seg, stride=pitch), :]
            u_j = u_s[l, pl.ds(j, n_seg, stride=pitch), :]
            h = a_j * carry[l] + u_j
            u_s[l, pl.ds(j, n_seg, stride=pitch), :] = h
            new.append(h)
        return tuple(new)

    scores = _dot(_bf16(z_s[:, OFF_Q:OFF_Q + D_XATTN]), kst_s[...])
    sc_s[...] = scores
    after_scores = _zero_row_after(scores[tile - SUBLANES:tile, N_XHEADS * N_MEM - LANES:])
    _unrolled(seg, scan_body, tuple(h + after_scores for h in h_in))

    for l in range(N_CONV_SLABS):
        lo = l * LANES
        uc_s[l, CONV_HALO:CONV_HALO + tile, :] = _gated_of_twice(
            z_s[:, OFF_GLU_A + lo:OFF_GLU_A + lo + LANES],
            z_s[:, OFF_GLU_B + lo:OFF_GLU_B + lo + LANES])

    def seg_body(c, carry):
        r0 = _aligned(c * seg, seg)
        p0 = _aligned(c * pitch, SUBLANES)
        for l in range(N_LRU_SLABS):
            lo = l * LANES
            gate = z_s[pl.ds(r0, seg), OFF_LRU_GATE + lo:OFF_LRU_GATE + lo + LANES]
            mix_s[pl.ds(r0, seg), lo:lo + LANES] = _bf16(u_s[l, pl.ds(p0, seg), :] * _silu_of_twice(gate))
        ys = []
        for l in range(N_CONV_SLABS):
            lo = l * LANES
            w_row = lambda k, lo=lo: w_dw_ref[k, :, lo:lo + LANES]
            ys.append(_conv_taps(uc_s, l, w_row, r0, seg, CONV_W, CONV_HALO)
                      + b_dw_ref[:, lo:lo + LANES])
        outs = _layernorm_silu_slabs(ys, ln_g_ref, ln_b_ref)
        for l in range(N_CONV_SLABS):
            lo = l * LANES
            ln_s[pl.ds(r0, seg), lo:lo + LANES] = _bf16(outs[l])
        return carry

    _unrolled(n_seg, seg_body, 0)

    yc = (_dot(ln_s[...], wpw_bf[...]) + b_pw_ref[...]) * _silu_of_twice(
        z_s[:, OFF_CONV_GATE:OFF_CONV_GATE + D_CONV])
    mix_s[:, D_LRU:D_LRU + D_CONV] = _bf16(yc)

    def attn_body(c, carry):
        r0 = _aligned(c * ATTN_ROWS, ATTN_ROWS)
        p_s[pl.ds(r0, ATTN_ROWS), :] = _softmax_heads(sc_s[pl.ds(r0, ATTN_ROWS), :])
        return carry

    _unrolled(tile // ATTN_ROWS, attn_body, 0)
    o = _dot_nt(p_s[...], vst_s[...])
    mix_s[:, D_LRU + D_CONV:D_MIX] = _bf16(
        o * _silu_of_twice(z_s[:, OFF_ATTN_GATE:OFF_ATTN_GATE + D_XATTN]))

    y = x_ref[0] + _dot(mix_s[...], wout_bf[...])
    y_ref[0] = _rmsnorm(y, g_final_ref[...])

    for l in range(N_LRU_SLABS):
        xc_s[l, LRU_HALO - (LRU_CONV - 1):LRU_HALO, :] = (
            xc_s[l, LRU_HALO + tile - (LRU_CONV - 1):LRU_HALO + tile, :])
    for l in range(N_CONV_SLABS):
        uc_s[l, CONV_HALO - (CONV_W - 1):CONV_HALO, :] = (
            uc_s[l, CONV_HALO + tile - (CONV_W - 1):CONV_HALO + tile, :])

    def write_state(bb):
        for l in range(N_LRU_SLABS):
            lo = l * LANES
            h_out_ref[bb:bb + 1, lo:lo + LANES] = hin_s[l, 0:1, :]
            for k in range(LRU_CONV - 1):
                r = LRU_HALO - (LRU_CONV - 1) + k
                lb_out_ref[k, bb:bb + 1, lo:lo + LANES] = xc_s[l, r:r + 1, :]
        for l in range(N_CONV_SLABS):
            lo = l * LANES
            for k in range(CONV_W - 1):
                r = CONV_HALO - (CONV_W - 1) + k
                cb_out_ref[k, bb:bb + 1, lo:lo + LANES] = uc_s[l, r:r + 1, :]

    for bb in range(h_out_ref.shape[0]):
        pl.when(jnp.logical_and(t == n_t - 1, b == bb))(functools.partial(write_state, bb))


def _whole(shape):
    nd = len(shape)
    return pl.BlockSpec(shape, lambda *_: (0,) * nd, pipeline_mode=pl.Buffered(1))


def _prompt(x, kt, vt, w):
    bsz, seq, _ = x.shape
    tile = PROMPT_TILE
    n_t = seq // tile
    n_seg = tile // PROMPT_SEG
    weights = [w["g_norm"], w["w_in"], w["w_lc"], w["b_lc"], w["w_ga"], w["w_gx"], w["b_ga"], w["b_gx"],
               w["lam"], w["w_dw"], w["b_dw"], w["ln_g"], w["ln_b"], w["w_pw"], w["b_pw"],
               w["w_out"], w["g_final"]]
    weight_specs = [
        _whole((1, D_MODEL)), _whole((None, D_MODEL, D_IN)), _whole((None, LRU_CONV, D_LRU)),
        _whole((1, D_LRU)), _whole((None, LRU_BLOCKS, LRU_BLOCK, LRU_BLOCK)),
        _whole((None, LRU_BLOCKS, LRU_BLOCK, LRU_BLOCK)), _whole((1, D_LRU)), _whole((1, D_LRU)),
        _whole((1, D_LRU)), _whole((CONV_W, 1, D_CONV)), _whole((1, D_CONV)), _whole((1, D_CONV)),
        _whole((1, D_CONV)), _whole((None, D_CONV, D_CONV)), _whole((1, D_CONV)),
        _whole((None, D_MIX, D_MODEL)), _whole((1, D_MODEL)),
    ]
    in_specs = [
        pl.BlockSpec((1, tile, D_MODEL), lambda b, t: (b, t, 0)),
        pl.BlockSpec((1, D_XATTN, N_MEM), lambda b, t: (b, 0, 0)),
        pl.BlockSpec((1, D_XATTN, N_MEM), lambda b, t: (b, 0, 0)),
    ] + weight_specs
    out_specs = [
        pl.BlockSpec((1, tile, D_MODEL), lambda b, t: (b, t, 0)),
        _whole((bsz, D_LRU)),
        _whole((LRU_CONV - 1, bsz, D_LRU)),
        _whole((CONV_W - 1, bsz, D_CONV)),
        _whole((D_MODEL, D_IN)), _whole((D_MIX, D_MODEL)), _whole((D_CONV, D_CONV)),
        _whole((2, GATE_HALF, D_LRU)),
    ]
    out_shape = [
        jax.ShapeDtypeStruct((bsz, seq, D_MODEL), jnp.float32),
        jax.ShapeDtypeStruct((bsz, D_LRU), jnp.float32),
        jax.ShapeDtypeStruct((LRU_CONV - 1, bsz, D_LRU), jnp.float32),
        jax.ShapeDtypeStruct((CONV_W - 1, bsz, D_CONV), jnp.float32),
        jax.ShapeDtypeStruct((D_MODEL, D_IN), jnp.bfloat16),
        jax.ShapeDtypeStruct((D_MIX, D_MODEL), jnp.bfloat16),
        jax.ShapeDtypeStruct((D_CONV, D_CONV), jnp.bfloat16),
        jax.ShapeDtypeStruct((2, GATE_HALF, D_LRU), jnp.bfloat16),
    ]
    scratch = [
        pltpu.VMEM((tile, D_IN), jnp.float32),
        pltpu.VMEM((N_LRU_SLABS, LRU_HALO + tile, LANES), jnp.float32),
        pltpu.VMEM((tile, D_LRU), jnp.float32),
        pltpu.VMEM((tile, 2 * D_LRU), jnp.float32),
        pltpu.VMEM((N_LRU_SLABS, n_seg * PROMPT_PITCH, LANES), jnp.float32),
        pltpu.VMEM((N_LRU_SLABS, n_seg * PROMPT_PITCH, LANES), jnp.float32),
        pltpu.VMEM((N_CONV_SLABS, CONV_HALO + tile, LANES), jnp.float32),
        pltpu.VMEM((tile, D_CONV), jnp.bfloat16),
        pltpu.VMEM((tile, D_MIX), jnp.bfloat16),
        pltpu.VMEM((N_LRU_SLABS, 2 * SUBLANES, LANES), jnp.float32),
        pltpu.VMEM((D_XATTN, N_XHEADS * N_MEM), jnp.bfloat16),
        pltpu.VMEM((D_XATTN, N_XHEADS * N_MEM), jnp.bfloat16),
        pltpu.VMEM((tile, N_XHEADS * N_MEM), jnp.float32),
        pltpu.VMEM((tile, N_XHEADS * N_MEM), jnp.bfloat16),
    ]
    return pl.pallas_call(
        _prompt_kernel,
        grid=(bsz, n_t),
        in_specs=in_specs,
        out_specs=out_specs,
        out_shape=out_shape,
        scratch_shapes=scratch,
        compiler_params=pltpu.CompilerParams(
            dimension_semantics=("arbitrary", "arbitrary"),
            vmem_limit_bytes=VMEM_LIMIT_BYTES),
        name="prompt_layer",
    )(x, kt, vt, *weights)


def _sample_kernel(nb, steps,
                   x_ref, h0_ref, lb_ref, cb_ref, kt_ref, vt_ref,
                   g_norm_ref, win_bf, w_lc_ref, b_lc_ref, wg_bf,
                   b_ga_ref, b_gx_ref, lam_ref, w_dw_ref, b_dw_ref, ln_g_ref, ln_b_ref,
                   wpw_bf, b_pw_ref, wout_bf, g_final_ref,
                   y_ref, h_out_ref, lb_out_ref, cb_out_ref,
                   z_s, xc_s, xl_s, pre_s, a_s, u_s, uc_s, ln_s, mix_s):
    rows = nb * steps
    pitch = SAMPLE_PITCH
    cpitch = CONV_HALO + steps
    prow = nb * pitch

    x = x_ref[...].reshape(rows, D_MODEL)
    hn = _rmsnorm(x, g_norm_ref[...])
    z_s[...] = _dot(_bf16(hn), win_bf[...])

    for b in range(nb):
        for l in range(N_LRU_SLABS):
            lo = l * LANES
            zcol = OFF_LRU_IN + lo
            xc_s[l, b * pitch:b * pitch + LRU_HALO, :] = jnp.zeros((LRU_HALO, LANES), jnp.float32)
            for k in range(LRU_CONV - 1):
                r = b * pitch + LRU_HALO - (LRU_CONV - 1) + k
                xc_s[l, r:r + 1, :] = lb_ref[k, b:b + 1, lo:lo + LANES]
                rn = (b + 1) * steps - (LRU_CONV - 1) + k
                lb_out_ref[k, b:b + 1, lo:lo + LANES] = z_s[rn:rn + 1, zcol:zcol + LANES]
            xc_s[l, b * pitch + LRU_HALO:(b + 1) * pitch, :] = (
                z_s[b * steps:(b + 1) * steps, zcol:zcol + LANES])
    n_conv = prow - LRU_HALO
    for l in range(N_LRU_SLABS):
        lo = l * LANES
        w_row = lambda k, lo=lo: 0.5 * w_lc_ref[k:k + 1, lo:lo + LANES]
        xl = (_conv_taps(xc_s, l, w_row, 0, n_conv, LRU_CONV, LRU_HALO)
              + 0.5 * b_lc_ref[:, lo:lo + LANES])
        xl_s[0:n_conv, lo:lo + LANES] = xl
    xl_s[n_conv:prow, :] = jnp.zeros((LRU_HALO, D_LRU), jnp.float32)
    _lru_gates(_bf16(xl_s[...]), wg_bf, pre_s)

    hc_ls = (0.5 * LRU_C) * _log_sigmoid(lam_ref[...])
    hb_ga = 0.5 * b_ga_ref[...]
    hb_gx = 0.5 * b_gx_ref[...]
    for l in range(N_LRU_SLABS):
        lo = l * LANES
        ca, cx = _pre_cols(l)
        a, u = _lru_coeffs(pre_s[:, ca:ca + LANES], pre_s[:, cx:cx + LANES], xl_s[:, lo:lo + LANES],
                           hb_ga[:, lo:lo + LANES], hb_gx[:, lo:lo + LANES],
                           hc_ls[:, lo:lo + LANES])
        a_s[l, :, :] = a
        u_s[l, :, :] = u

    n_grp = nb // SUBLANES

    def scan_body(j, carry):
        new = []
        for g in range(n_grp):
            for l in range(N_LRU_SLABS):
                base = g * SUBLANES * pitch
                a_j = a_s[l, pl.ds(base + j, SUBLANES, stride=pitch), :]
                u_j = u_s[l, pl.ds(base + j, SUBLANES, stride=pitch), :]
                h = a_j * carry[g * N_LRU_SLABS + l] + u_j
                u_s[l, pl.ds(base + j, SUBLANES, stride=pitch), :] = h
                new.append(h)
        return tuple(new)

    h0 = tuple(h0_ref[g * SUBLANES:(g + 1) * SUBLANES, l * LANES:(l + 1) * LANES]
               for g in range(n_grp) for l in range(N_LRU_SLABS))
    h_fin = _unrolled(steps, scan_body, h0)
    for g in range(n_grp):
        for l in range(N_LRU_SLABS):
            h_out_ref[g * SUBLANES:(g + 1) * SUBLANES, l * LANES:(l + 1) * LANES] = (
                h_fin[g * N_LRU_SLABS + l])

    for b in range(nb):
        for l in range(N_LRU_SLABS):
            lo = l * LANES
            gate = z_s[b * steps:(b + 1) * steps, OFF_LRU_GATE + lo:OFF_LRU_GATE + lo + LANES]
            mix_s[b * steps:(b + 1) * steps, lo:lo + LANES] = _bf16(
                u_s[l, b * pitch:b * pitch + steps, :] * _silu_of_twice(gate))

    for b in range(nb):
        for l in range(N_CONV_SLABS):
            lo = l * LANES
            uc = _gated_of_twice(
                z_s[b * steps:(b + 1) * steps, OFF_GLU_A + lo:OFF_GLU_A + lo + LANES],
                z_s[b * steps:(b + 1) * steps, OFF_GLU_B + lo:OFF_GLU_B + lo + LANES])
            uc_s[l, b * cpitch + CONV_HALO:(b + 1) * cpitch, :] = uc
            for k in range(CONV_W - 1):
                r = b * cpitch + CONV_HALO - (CONV_W - 1) + k
                uc_s[l, r:r + 1, :] = cb_ref[k, b:b + 1, lo:lo + LANES]
                rn = (b + 1) * cpitch - (CONV_W - 1) + k
                cb_out_ref[k, b:b + 1, lo:lo + LANES] = uc_s[l, rn:rn + 1, :]

    def conv_body(i, carry):
        for j in range(SAMPLE_UNROLL):
            b = i * SAMPLE_UNROLL + j
            r0 = pl.multiple_of(b * cpitch, SUBLANES)
            o0 = pl.multiple_of(b * steps, steps)
            ys = []
            for l in range(N_CONV_SLABS):
                lo = l * LANES
                w_row = lambda k, lo=lo: w_dw_ref[k, :, lo:lo + LANES]
                ys.append(_conv_taps(uc_s, l, w_row, r0, steps, CONV_W, CONV_HALO)
                          + b_dw_ref[:, lo:lo + LANES])
            outs = _layernorm_silu_slabs(ys, ln_g_ref, ln_b_ref)
            for l in range(N_CONV_SLABS):
                lo = l * LANES
                ln_s[pl.ds(o0, steps), lo:lo + LANES] = _bf16(outs[l])
        return carry

    lax.fori_loop(0, nb // SAMPLE_UNROLL, conv_body, 0)

    yc = (_dot(ln_s[...], wpw_bf[...]) + b_pw_ref[...]) * _silu_of_twice(
        z_s[:, OFF_CONV_GATE:OFF_CONV_GATE + D_CONV])
    mix_s[:, D_LRU:D_LRU + D_CONV] = _bf16(yc)

    masks = _lane_head_masks()

    def attn_body(b, carry):
        o0 = b * steps
        q = z_s[o0:o0 + steps, OFF_Q:OFF_Q + D_XATTN]
        o = _attention_qstack(q, _bf16(kt_ref[b]), _bf16(vt_ref[b]), masks)
        gate = z_s[o0:o0 + steps, OFF_ATTN_GATE:OFF_ATTN_GATE + D_XATTN]
        mix_s[o0:o0 + steps, D_LRU + D_CONV:D_MIX] = _bf16(o * _silu_of_twice(gate))
        return carry

    _unrolled(nb, attn_body, 0)

    y = x + _dot(mix_s[...], wout_bf[...])
    y_ref[...] = _rmsnorm(y, g_final_ref[...]).reshape(nb, steps, D_MODEL)


def _sample(x, h0, lb, cb, kt, vt, w, win_bf, wout_bf, wpw_bf, wg_bf):
    bsz, steps, _ = x.shape
    nb = SAMPLE_GROUP
    assert bsz % nb == 0 and nb % SUBLANES == 0 and nb % SAMPLE_UNROLL == 0
    assert steps + LRU_HALO == SAMPLE_PITCH and steps >= CONV_W - 1
    rows = nb * steps
    prow = nb * SAMPLE_PITCH
    weights = [w["g_norm"], win_bf, w["w_lc"], w["b_lc"], wg_bf, w["b_ga"], w["b_gx"],
               w["lam"], w["w_dw"], w["b_dw"], w["ln_g"], w["ln_b"], wpw_bf, w["b_pw"],
               wout_bf, w["g_final"]]
    weight_specs = [
        _whole((1, D_MODEL)), _whole((D_MODEL, D_IN)), _whole((None, LRU_CONV, D_LRU)),
        _whole((1, D_LRU)), _whole((2, GATE_HALF, D_LRU)), _whole((1, D_LRU)), _whole((1, D_LRU)),
        _whole((1, D_LRU)), _whole((CONV_W, 1, D_CONV)), _whole((1, D_CONV)), _whole((1, D_CONV)),
        _whole((1, D_CONV)), _whole((D_CONV, D_CONV)), _whole((1, D_CONV)),
        _whole((D_MIX, D_MODEL)), _whole((1, D_MODEL)),
    ]
    in_specs = [
        pl.BlockSpec((nb, steps, D_MODEL), lambda g: (g, 0, 0)),
        pl.BlockSpec((nb, D_LRU), lambda g: (g, 0)),
        pl.BlockSpec((LRU_CONV - 1, nb, D_LRU), lambda g: (0, g, 0)),
        pl.BlockSpec((CONV_W - 1, nb, D_CONV), lambda g: (0, g, 0)),
        pl.BlockSpec((nb, D_XATTN, N_MEM), lambda g: (g, 0, 0)),
        pl.BlockSpec((nb, D_XATTN, N_MEM), lambda g: (g, 0, 0)),
    ] + weight_specs
    out_specs = [
        pl.BlockSpec((nb, steps, D_MODEL), lambda g: (g, 0, 0)),
        pl.BlockSpec((nb, D_LRU), lambda g: (g, 0)),
        pl.BlockSpec((LRU_CONV - 1, nb, D_LRU), lambda g: (0, g, 0)),
        pl.BlockSpec((CONV_W - 1, nb, D_CONV), lambda g: (0, g, 0)),
    ]
    out_shape = [
        jax.ShapeDtypeStruct((bsz, steps, D_MODEL), jnp.float32),
        jax.ShapeDtypeStruct((bsz, D_LRU), jnp.float32),
        jax.ShapeDtypeStruct((LRU_CONV - 1, bsz, D_LRU), jnp.float32),
        jax.ShapeDtypeStruct((CONV_W - 1, bsz, D_CONV), jnp.float32),
    ]
    scratch = [
        pltpu.VMEM((rows, D_IN), jnp.float32),
        pltpu.VMEM((N_LRU_SLABS, prow, LANES), jnp.float32),
        pltpu.VMEM((prow, D_LRU), jnp.float32),
        pltpu.VMEM((prow, 2 * D_LRU), jnp.float32),
        pltpu.VMEM((N_LRU_SLABS, prow, LANES), jnp.float32),
        pltpu.VMEM((N_LRU_SLABS, prow, LANES), jnp.float32),
        pltpu.VMEM((N_CONV_SLABS, nb * (CONV_HALO + steps), LANES), jnp.float32),
        pltpu.VMEM((rows, D_CONV), jnp.bfloat16),
        pltpu.VMEM((rows, D_MIX), jnp.bfloat16),
    ]
    return pl.pallas_call(
        functools.partial(_sample_kernel, nb, steps),
        grid=(bsz // nb,),
        in_specs=in_specs,
        out_specs=out_specs,
        out_shape=out_shape,
        scratch_shapes=scratch,
        compiler_params=pltpu.CompilerParams(
            dimension_semantics=("arbitrary",),
            vmem_limit_bytes=VMEM_LIMIT_BYTES),
        name="sample_layer",
    )(x, h0, lb, cb, kt, vt, *weights)


def _time_major(state):
    return jnp.transpose(state, (1, 0, 2))


def _memory_t(mem):
    bsz = mem.shape[0]
    return jnp.transpose(mem, (0, 2, 3, 1)).reshape(bsz, D_XATTN, N_MEM)


def _memory_from_t(mem_t):
    bsz = mem_t.shape[0]
    return jnp.transpose(mem_t.reshape(bsz, N_XHEADS, XHEAD_DIM, N_MEM), (0, 3, 1, 2))


def kernel(x_prompt, x_sample, state_lru_h, state_lru_conv, state_conv, cache_mem_k, cache_mem_v,
           mem_prompt, g_norm, w_in, w_lru_conv, b_lru_conv, w_gate_a, b_gate_a, w_gate_x, b_gate_x,
           lru_lambda, w_dw, b_dw, ln_g, ln_b, w_pw, b_pw, g_mem, w_mem_k, w_mem_v, w_out, g_final):
    depth = g_norm.shape[0]
    assert depth == 1
    w = {
        "g_norm": g_norm, "w_in": w_in, "w_lc": w_lru_conv, "b_lc": b_lru_conv,
        "w_ga": w_gate_a, "w_gx": w_gate_x, "b_ga": b_gate_a, "b_gx": b_gate_x, "lam": lru_lambda,
        "w_dw": jnp.transpose(w_dw, (1, 0, 2)), "b_dw": b_dw, "ln_g": ln_g, "ln_b": ln_b,
        "w_pw": w_pw, "b_pw": b_pw, "w_out": w_out, "g_final": g_final.reshape(1, D_MODEL),
    }
    mkt, mvt = _memkv(mem_prompt, g_mem, w_mem_k, w_mem_v)

    y_prompt, ph, plb, pcb, win_bf, wout_bf, wpw_bf, wg_bf = _prompt(x_prompt, mkt, mvt, w)

    y_sample, sh, slb, scb = _sample(
        x_sample, state_lru_h[0], _time_major(state_lru_conv[0]), _time_major(state_conv[0]),
        _memory_t(cache_mem_k[0]), _memory_t(cache_mem_v[0]), w, win_bf, wout_bf, wpw_bf, wg_bf)

    return (y_prompt, y_sample,
            ph[None], _time_major(plb)[None], _time_major(pcb)[None],
            _memory_from_t(mkt)[None], _memory_from_t(mvt)[None],
            sh[None], _time_major(slb)[None], _time_major(scb)[None])
```

```python
import functools

import jax
import jax.numpy as jnp
from jax import lax
from jax.experimental import pallas as pl
from jax.experimental.pallas import tpu as pltpu

D_MODEL = 1024
N_MEM = 256
D_LRU = 512
LRU_BLOCKS = 8
LRU_BLOCK = D_LRU // LRU_BLOCKS
LRU_CONV = 4
LRU_C = 8.0
D_CONV = 256
CONV_W = 31
N_XHEADS = 4
XHEAD_DIM = 64
D_XATTN = N_XHEADS * XHEAD_DIM
D_MIX = D_LRU + D_CONV + D_XATTN
D_IN = 2 * D_LRU + 3 * D_CONV + 2 * D_XATTN
EPS = 1e-6

LANES = 128
SUBLANES = 8
N_LRU_SLABS = D_LRU // LANES
N_CONV_SLABS = D_CONV // LANES
GATE_HALF = D_LRU // 2

OFF_LRU_IN = 0
OFF_LRU_GATE = D_LRU
OFF_GLU_A = 2 * D_LRU
OFF_GLU_B = OFF_GLU_A + D_CONV
OFF_CONV_GATE = OFF_GLU_B + D_CONV
OFF_Q = OFF_CONV_GATE + D_CONV
OFF_ATTN_GATE = OFF_Q + D_XATTN

LRU_HALO = 8
CONV_HALO = 32

PROMPT_TILE = 512
PROMPT_SEG = 64
PROMPT_PITCH = 68
SAMPLE_PITCH = 40
SAMPLE_GROUP = 16
ATTN_ROWS = 128

VMEM_LIMIT_BYTES = 56 * 1024 * 1024
HALF_LOG2_E = 0.7213475204444817


def _silu_of_twice(hx):
    return hx + hx * jnp.tanh(hx)


def _gated_of_twice(ha, hb):
    return ha + ha * jnp.tanh(hb)


def _w_in_col_scale():
    col = lax.broadcasted_iota(jnp.int32, (1, D_IN), 1)
    plain = (col < OFF_LRU_GATE) | ((col >= OFF_Q) & (col < OFF_ATTN_GATE))
    return jnp.where(plain, 1.0, 0.5)


def _sqrt_nonneg(x):
    return jnp.exp2(jnp.log(x) * HALF_LOG2_E)


def _rmsnorm(x, g):
    ms = jnp.mean(x * x, axis=-1, keepdims=True)
    return x * lax.rsqrt(ms + EPS) * g


def _log_sigmoid(x):
    y = -x
    return -(jnp.maximum(y, 0.0) + jnp.log1p(jnp.exp(-jnp.abs(y))))


def _bf16(x):
    return x.astype(jnp.bfloat16)


def _dot(a, b):
    return jnp.dot(a, b, preferred_element_type=jnp.float32)


def _dot_nt(a, b):
    return lax.dot_general(a, b, (((1,), (1,)), ((), ())), preferred_element_type=jnp.float32)


def _unrolled(n, body, init):
    carry = init
    for i in range(n):
        carry = body(i, carry)
    return carry


def _aligned(x, m):
    return x if isinstance(x, int) else pl.multiple_of(x, m)


def _head_of(index):
    return lax.shift_right_logical(index, XHEAD_DIM.bit_length() - 1)


def _lane_head_masks():
    head = _head_of(lax.broadcasted_iota(jnp.int32, (1, D_XATTN), 1))
    return [(head == h).astype(jnp.float32) for h in range(N_XHEADS)]


def _row_head_masks():
    head = _head_of(lax.broadcasted_iota(jnp.int32, (D_XATTN, 1), 0))
    return [(head == h).astype(jnp.float32) for h in range(N_XHEADS)]


def _head_stack_t(kt, vt, kst_ref, vst_ref):
    scale = XHEAD_DIM ** -0.5
    for h, m in enumerate(_row_head_masks()):
        kst_ref[:, h * N_MEM:(h + 1) * N_MEM] = _bf16(kt * (m * scale))
        vst_ref[:, h * N_MEM:(h + 1) * N_MEM] = _bf16(vt * m)


def _zero_row_after(x):
    bits = lax.bitcast_convert_type(x, jnp.uint32)
    zero = lax.shift_right_logical(lax.shift_right_logical(bits, jnp.uint32(16)), jnp.uint32(16))
    return lax.bitcast_convert_type(zero, jnp.float32)[0:1, :]


def _softmax_heads(s):
    ps = []
    for h in range(N_XHEADS):
        sh = s[:, h * N_MEM:(h + 1) * N_MEM]
        e = jnp.exp(sh - jnp.max(sh, axis=-1, keepdims=True))
        inv = 1.0 / jnp.sum(e, axis=-1, keepdims=True)
        ps.append(_bf16(e * inv))
    return jnp.concatenate(ps, axis=1)


def _lru_gates(xl_bf, wg_ref, pre_s):
    pre_s[:, 0:D_LRU] = _dot(xl_bf[:, 0:GATE_HALF], wg_ref[0])
    pre_s[:, D_LRU:2 * D_LRU] = _dot(xl_bf[:, GATE_HALF:D_LRU], wg_ref[1])


def _pre_cols(l):
    half_slabs = N_LRU_SLABS // 2
    base = (l // half_slabs) * D_LRU + (l % half_slabs) * LANES
    return base, base + GATE_HALF


def _lru_coeffs(hpre_a, hpre_x, hx, hb_a, hb_x, hc_ls):
    t_r = jnp.tanh(hpre_a + hb_a)
    t_i = jnp.tanh(hpre_x + hb_x)
    log_a = hc_ls + hc_ls * t_r
    a = jnp.exp(log_a)
    one_minus_a2 = -jnp.tanh(log_a) * (a * a + 1.0)
    u = _sqrt_nonneg(one_minus_a2) * (hx + hx * t_i)
    return a, u


def _conv_taps(src_s, l, w_row, row0, nrows, width, halo):
    acc = None
    for k in range(width):
        x_k = src_s[l, pl.ds(row0 + (halo - (width - 1) + k), nrows), :]
        acc = w_row(k) * x_k if acc is None else acc + w_row(k) * x_k
    return acc


def _layernorm_silu_slabs(ys, g_ref, b_ref):
    tot = ys[0]
    for y in ys[1:]:
        tot = tot + y
    mu = jnp.sum(tot, axis=-1, keepdims=True) * (1.0 / D_CONV)
    cs = [y - mu for y in ys]
    sq = cs[0] * cs[0]
    for c in cs[1:]:
        sq = sq + c * c
    var = jnp.sum(sq, axis=-1, keepdims=True) * (1.0 / D_CONV)
    inv = lax.rsqrt(var + EPS)
    outs = []
    for l, c in enumerate(cs):
        lo = l * LANES
        half_g = 0.5 * g_ref[:, lo:lo + LANES]
        half_b = 0.5 * b_ref[:, lo:lo + LANES]
        outs.append(_silu_of_twice(c * inv * half_g + half_b))
    return outs


def _gate_block_diag(w_blocks):
    n = (LRU_BLOCKS // 2) * LRU_BLOCK
    stacked = _bf16(w_blocks.reshape(n, LRU_BLOCK))
    src = lax.broadcasted_iota(jnp.int32, (LRU_BLOCK, n), 0)
    dst = lax.broadcasted_iota(jnp.int32, (LRU_BLOCK, n), 1)
    spread = _bf16(((dst & (LRU_BLOCK - 1)) == src).astype(jnp.float32))
    rows = _head_of(lax.broadcasted_iota(jnp.int32, (n, n), 0))
    cols = _head_of(lax.broadcasted_iota(jnp.int32, (n, n), 1))
    return _bf16(_dot(stacked, spread) * (rows == cols).astype(jnp.float32))


def _memkv_kernel(mem_ref, g_ref, wk_ref, wv_ref, kt_ref, vt_ref):
    m = _bf16(_rmsnorm(mem_ref[0], g_ref[...]))
    kt_ref[0] = _dot(m, _bf16(wk_ref[...])).T
    vt_ref[0] = _dot(m, _bf16(wv_ref[...])).T


def _memkv(mem, g_mem, w_k, w_v):
    bsz = mem.shape[0]
    return pl.pallas_call(
        _memkv_kernel,
        grid=(bsz,),
        in_specs=[
            pl.BlockSpec((1, N_MEM, D_MODEL), lambda b: (b, 0, 0)),
            pl.BlockSpec((1, D_MODEL), lambda b: (0, 0)),
            pl.BlockSpec((None, D_MODEL, D_XATTN), lambda b: (0, 0, 0)),
            pl.BlockSpec((None, D_MODEL, D_XATTN), lambda b: (0, 0, 0)),
        ],
        out_specs=[
            pl.BlockSpec((1, D_XATTN, N_MEM), lambda b: (b, 0, 0)),
            pl.BlockSpec((1, D_XATTN, N_MEM), lambda b: (b, 0, 0)),
        ],
        out_shape=[jax.ShapeDtypeStruct((bsz, D_XATTN, N_MEM), jnp.float32)] * 2,
        compiler_params=pltpu.CompilerParams(dimension_semantics=("arbitrary",)),
        name="memkv",
    )(mem, g_mem, w_k, w_v)


def _prompt_kernel(x_ref, kt_ref, vt_ref, g_norm_ref, w_in_ref, w_lc_ref, b_lc_ref, wga_ref, wgx_ref,
                   b_ga_ref, b_gx_ref, lam_ref, w_dw_ref, b_dw_ref, ln_g_ref, ln_b_ref,
                   w_pw_ref, b_pw_ref, w_out_ref, g_final_ref,
                   y_ref, h_out_ref, lb_out_ref, cb_out_ref, win_bf, wout_bf, wpw_bf, wg_bf,
                   z_s, xc_s, xl_s, pre_s, a_s, u_s, uc_s, ln_s, mix_s, hin_s, kst_s, vst_s, sc_s, p_s):
    b = pl.program_id(0)
    t = pl.program_id(1)
    n_t = pl.num_programs(1)
    tile = PROMPT_TILE
    seg = PROMPT_SEG
    n_seg = tile // seg
    pitch = PROMPT_PITCH

    @pl.when(jnp.logical_and(b == 0, t == 0))
    def _():
        win_bf[...] = _bf16(w_in_ref[...] * _w_in_col_scale())
        wout_bf[...] = _bf16(w_out_ref[...])
        wpw_bf[...] = _bf16(w_pw_ref[...])
        half = LRU_BLOCKS // 2
        for h in range(2):
            wg_bf[h, :, 0:GATE_HALF] = _gate_block_diag(wga_ref[h * half:(h + 1) * half])
            wg_bf[h, :, GATE_HALF:D_LRU] = _gate_block_diag(wgx_ref[h * half:(h + 1) * half])

    @pl.when(t == 0)
    def _():
        xc_s[:, 0:LRU_HALO, :] = jnp.zeros((N_LRU_SLABS, LRU_HALO, LANES), jnp.float32)
        uc_s[:, 0:CONV_HALO, :] = jnp.zeros((N_CONV_SLABS, CONV_HALO, LANES), jnp.float32)
        hin_s[:, 0:1, :] = jnp.zeros((N_LRU_SLABS, 1, LANES), jnp.float32)
        _head_stack_t(kt_ref[0], vt_ref[0], kst_s, vst_s)

    hn = _rmsnorm(x_ref[0], g_norm_ref[...])
    z_s[...] = _dot(_bf16(hn), win_bf[...])

    for l in range(N_LRU_SLABS):
        lo = l * LANES
        xc_s[l, LRU_HALO:LRU_HALO + tile, :] = z_s[:, OFF_LRU_IN + lo:OFF_LRU_IN + lo + LANES]
    for l in range(N_LRU_SLABS):
        lo = l * LANES
        w_row = lambda k, lo=lo: 0.5 * w_lc_ref[k:k + 1, lo:lo + LANES]
        xl = (_conv_taps(xc_s, l, w_row, 0, tile, LRU_CONV, LRU_HALO)
              + 0.5 * b_lc_ref[:, lo:lo + LANES])
        xl_s[:, lo:lo + LANES] = xl
    _lru_gates(_bf16(xl_s[...]), wg_bf, pre_s)

    hc_ls = (0.5 * LRU_C) * _log_sigmoid(lam_ref[...])
    hb_ga = 0.5 * b_ga_ref[...]
    hb_gx = 0.5 * b_gx_ref[...]

    def coeff_body(c, carry):
        r0 = _aligned(c * seg, seg)
        p0 = _aligned(c * pitch, SUBLANES)
        for l in range(N_LRU_SLABS):
            lo = l * LANES
            ca, cx = _pre_cols(l)
            a, u = _lru_coeffs(pre_s[pl.ds(r0, seg), ca:ca + LANES],
                               pre_s[pl.ds(r0, seg), cx:cx + LANES],
                               xl_s[pl.ds(r0, seg), lo:lo + LANES],
                               hb_ga[:, lo:lo + LANES], hb_gx[:, lo:lo + LANES],
                               hc_ls[:, lo:lo + LANES])
            a_s[l, pl.ds(p0, seg), :] = a
            u_s[l, pl.ds(p0, seg), :] = u
        return carry

    _unrolled(n_seg, coeff_body, 0)

    def tot_body(j, carry):
        new = []
        for l in range(N_LRU_SLABS):
            a_tot, u_tot = carry[l]
            a_j = a_s[l, pl.ds(j, n_seg, stride=pitch), :]
            u_j = u_s[l, pl.ds(j, n_seg, stride=pitch), :]
            new.append((a_j * a_tot, a_j * u_tot + u_j))
        return tuple(new)

    init = tuple((jnp.ones((n_seg, LANES), jnp.float32), jnp.zeros((n_seg, LANES), jnp.float32))
                 for _ in range(N_LRU_SLABS))
    totals = _unrolled(seg, tot_body, init)

    h_in = []
    for l in range(N_LRU_SLABS):
        a_tot, u_tot = totals[l]
        h = hin_s[l, 0:1, :]
        for c in range(n_seg):
            h = a_tot[c:c + 1, :] * h + u_tot[c:c + 1, :]
            hin_s[l, c + 1:c + 2, :] = h
        h_in.append(hin_s[l, 0:n_seg, :])
        hin_s[l, 0:1, :] = h

    def scan_body(j, carry):
        new = []
        for l in range(N_LRU_SLABS):
            a_j = a_s[l, pl.ds(j, n_seg, stride=pitch), :]
            u_j = u_s[l, pl.ds(j, n_seg, stride=pitch), :]
            h = a_j * carry[l] + u_j
            u_s[l, pl.ds(j, n_seg, stride=pitch), :] = h
            new.append(h)
        return tuple(new)

    scores = _dot(_bf16(z_s[:, OFF_Q:OFF_Q + D_XATTN]), kst_s[...])
    sc_s[...] = scores
    after_scores = _zero_row_after(scores[tile - SUBLANES:tile, N_XHEADS * N_MEM - LANES:])
    _unrolled(seg, scan_body, tuple(h + after_scores for h in h_in))

    for l in range(N_CONV_SLABS):
        lo = l * LANES
        uc_s[l, CONV_HALO:CONV_HALO + tile, :] = _gated_of_twice(
            z_s[:, OFF_GLU_A + lo:OFF_GLU_A + lo + LANES],
            z_s[:, OFF_GLU_B + lo:OFF_GLU_B + lo + LANES])

    def seg_body(c, carry):
        r0 = _aligned(c * seg, seg)
        p0 = _aligned(c * pitch, SUBLANES)
        for l in range(N_LRU_SLABS):
            lo = l * LANES
            gate = z_s[pl.ds(r0, seg), OFF_LRU_GATE + lo:OFF_LRU_GATE + lo + LANES]
            mix_s[pl.ds(r0, seg), lo:lo + LANES] = _bf16(u_s[l, pl.ds(p0, seg), :] * _silu_of_twice(gate))
        ys = []
        for l in range(N_CONV_SLABS):
            lo = l * LANES
            w_row = lambda k, lo=lo: w_dw_ref[k, :, lo:lo + LANES]
            ys.append(_conv_taps(uc_s, l, w_row, r0, seg, CONV_W, CONV_HALO)
                      + b_dw_ref[:, lo:lo + LANES])
        outs = _layernorm_silu_slabs(ys, ln_g_ref, ln_b_ref)
        for l in range(N_CONV_SLABS):
            lo = l * LANES
            ln_s[pl.ds(r0, seg), lo:lo + LANES] = _bf16(outs[l])
        return carry

    _unrolled(n_seg, seg_body, 0)

    yc = (_dot(ln_s[...], wpw_bf[...]) + b_pw_ref[...]) * _silu_of_twice(
        z_s[:, OFF_CONV_GATE:OFF_CONV_GATE + D_CONV])
    mix_s[:, D_LRU:D_LRU + D_CONV] = _bf16(yc)

    def attn_body(c, carry):
        r0 = _aligned(c * ATTN_ROWS, ATTN_ROWS)
        p_s[pl.ds(r0, ATTN_ROWS), :] = _softmax_heads(sc_s[pl.ds(r0, ATTN_ROWS), :])
        return carry

    _unrolled(tile // ATTN_ROWS, attn_body, 0)
    o = _dot_nt(p_s[...], vst_s[...])
    mix_s[:, D_LRU + D_CONV:D_MIX] = _bf16(
        o * _silu_of_twice(z_s[:, OFF_ATTN_GATE:OFF_ATTN_GATE + D_XATTN]))

    y = x_ref[0] + _dot(mix_s[...], wout_bf[...])
    y_ref[0] = _rmsnorm(y, g_final_ref[...])

    for l in range(N_LRU_SLABS):
        xc_s[l, LRU_HALO - (LRU_CONV - 1):LRU_HALO, :] = (
            xc_s[l, LRU_HALO + tile - (LRU_CONV - 1):LRU_HALO + tile, :])
    for l in range(N_CONV_SLABS):
        uc_s[l, CONV_HALO - (CONV_W - 1):CONV_HALO, :] = (
            uc_s[l, CONV_HALO + tile - (CONV_W - 1):CONV_HALO + tile, :])

    def write_state(bb):
        for l in range(N_LRU_SLABS):
            lo = l * LANES
            h_out_ref[bb:bb + 1, lo:lo + LANES] = hin_s[l, 0:1, :]
            for k in range(LRU_CONV - 1):
                r = LRU_HALO - (LRU_CONV - 1) + k
                lb_out_ref[k, bb:bb + 1, lo:lo + LANES] = xc_s[l, r:r + 1, :]
        for l in range(N_CONV_SLABS):
            lo = l * LANES
            for k in range(CONV_W - 1):
                r = CONV_HALO - (CONV_W - 1) + k
                cb_out_ref[k, bb:bb + 1, lo:lo + LANES] = uc_s[l, r:r + 1, :]

    for bb in range(h_out_ref.shape[0]):
        pl.when(jnp.logical_and(t == n_t - 1, b == bb))(functools.partial(write_state, bb))


def _whole(shape):
    nd = len(shape)
    return pl.BlockSpec(shape, lambda *_: (0,) * nd, pipeline_mode=pl.Buffered(1))


def _prompt(x, kt, vt, w):
    bsz, seq, _ = x.shape
    tile = PROMPT_TILE
    n_t = seq // tile
    n_seg = tile // PROMPT_SEG
    weights = [w["g_norm"], w["w_in"], w["w_lc"], w["b_lc"], w["w_ga"], w["w_gx"], w["b_ga"], w["b_gx"],
               w["lam"], w["w_dw"], w["b_dw"], w["ln_g"], w["ln_b"], w["w_pw"], w["b_pw"],
               w["w_out"], w["g_final"]]
    weight_specs = [
        _whole((1, D_MODEL)), _whole((None, D_MODEL, D_IN)), _whole((None, LRU_CONV, D_LRU)),
        _whole((1, D_LRU)), _whole((None, LRU_BLOCKS, LRU_BLOCK, LRU_BLOCK)),
        _whole((None, LRU_BLOCKS, LRU_BLOCK, LRU_BLOCK)), _whole((1, D_LRU)), _whole((1, D_LRU)),
        _whole((1, D_LRU)), _whole((CONV_W, 1, D_CONV)), _whole((1, D_CONV)), _whole((1, D_CONV)),
        _whole((1, D_CONV)), _whole((None, D_CONV, D_CONV)), _whole((1, D_CONV)),
        _whole((None, D_MIX, D_MODEL)), _whole((1, D_MODEL)),
    ]
    in_specs = [
        pl.BlockSpec((1, tile, D_MODEL), lambda b, t: (b, t, 0)),
        pl.BlockSpec((1, D_XATTN, N_MEM), lambda b, t: (b, 0, 0)),
        pl.BlockSpec((1, D_XATTN, N_MEM), lambda b, t: (b, 0, 0)),
    ] + weight_specs
    out_specs = [
        pl.BlockSpec((1, tile, D_MODEL), lambda b, t: (b, t, 0)),
        _whole((bsz, D_LRU)),
        _whole((LRU_CONV - 1, bsz, D_LRU)),
        _whole((CONV_W - 1, bsz, D_CONV)),
        _whole((D_MODEL, D_IN)), _whole((D_MIX, D_MODEL)), _whole((D_CONV, D_CONV)),
        _whole((2, GATE_HALF, D_LRU)),
    ]
    out_shape = [
        jax.ShapeDtypeStruct((bsz, seq, D_MODEL), jnp.float32),
        jax.ShapeDtypeStruct((bsz, D_LRU), jnp.float32),
        jax.ShapeDtypeStruct((LRU_CONV - 1, bsz, D_LRU), jnp.float32),
        jax.ShapeDtypeStruct((CONV_W - 1, bsz, D_CONV), jnp.float32),
        jax.ShapeDtypeStruct((D_MODEL, D_IN), jnp.bfloat16),
        jax.ShapeDtypeStruct((D_MIX, D_MODEL), jnp.bfloat16),
        jax.ShapeDtypeStruct((D_CONV, D_CONV), jnp.bfloat16),
        jax.ShapeDtypeStruct((2, GATE_HALF, D_LRU), jnp.bfloat16),
    ]
    scratch = [
        pltpu.VMEM((tile, D_IN), jnp.float32),
        pltpu.VMEM((N_LRU_SLABS, LRU_HALO + tile, LANES), jnp.float32),
        pltpu.VMEM((tile, D_LRU), jnp.float32),
        pltpu.VMEM((tile, 2 * D_LRU), jnp.float32),
        pltpu.VMEM((N_LRU_SLABS, n_seg * PROMPT_PITCH, LANES), jnp.float32),
        pltpu.VMEM((N_LRU_SLABS, n_seg * PROMPT_PITCH, LANES), jnp.float32),
        pltpu.VMEM((N_CONV_SLABS, CONV_HALO + tile, LANES), jnp.float32),
        pltpu.VMEM((tile, D_CONV), jnp.bfloat16),
        pltpu.VMEM((tile, D_MIX), jnp.bfloat16),
        pltpu.VMEM((N_LRU_SLABS, 2 * SUBLANES, LANES), jnp.float32),
        pltpu.VMEM((D_XATTN, N_XHEADS * N_MEM), jnp.bfloat16),
        pltpu.VMEM((D_XATTN, N_XHEADS * N_MEM), jnp.bfloat16),
        pltpu.VMEM((tile, N_XHEADS * N_MEM), jnp.float32),
        pltpu.VMEM((tile, N_XHEADS * N_MEM), jnp.bfloat16),
    ]
    return pl.pallas_call(
        _prompt_kernel,
        grid=(bsz, n_t),
        in_specs=in_specs,
        out_specs=out_specs,
        out_shape=out_shape,
        scratch_shapes=scratch,
        compiler_params=pltpu.CompilerParams(
            dimension_semantics=("arbitrary", "arbitrary"),
            vmem_limit_bytes=VMEM_LIMIT_BYTES),
        name="prompt_layer",
    )(x, kt, vt, *weights)


def _sample_kernel(nb, steps,
                   x_ref, h0_ref, lb_ref, cb_ref, kt_ref, vt_ref,
                   g_norm_ref, win_bf, w_lc_ref, b_lc_ref, wg_bf,
                   b_ga_ref, b_gx_ref, lam_ref, w_dw_ref, b_dw_ref, ln_g_ref, ln_b_ref,
                   wpw_bf, b_pw_ref, wout_bf, g_final_ref,
                   y_ref, h_out_ref, lb_out_ref, cb_out_ref,
                   z_s, xc_s, xl_s, pre_s, a_s, u_s, uc_s, ln_s, mix_s, sc_s, p_s):
    rows = nb * steps
    pitch = SAMPLE_PITCH
    cpitch = CONV_HALO + steps
    prow = nb * pitch

    x = x_ref[...].reshape(rows, D_MODEL)
    hn = _rmsnorm(x, g_norm_ref[...])
    z_s[...] = _dot(_bf16(hn), win_bf[...])

    for b in range(nb):
        for l in range(N_LRU_SLABS):
            lo = l * LANES
            zcol = OFF_LRU_IN + lo
            xc_s[l, b * pitch:b * pitch + LRU_HALO, :] = jnp.zeros((LRU_HALO, LANES), jnp.float32)
            for k in range(LRU_CONV - 1):
                r = b * pitch + LRU_HALO - (LRU_CONV - 1) + k
                xc_s[l, r:r + 1, :] = lb_ref[k, b:b + 1, lo:lo + LANES]
                rn = (b + 1) * steps - (LRU_CONV - 1) + k
                lb_out_ref[k, b:b + 1, lo:lo + LANES] = z_s[rn:rn + 1, zcol:zcol + LANES]
            xc_s[l, b * pitch + LRU_HALO:(b + 1) * pitch, :] = (
                z_s[b * steps:(b + 1) * steps, zcol:zcol + LANES])
    n_conv = prow - LRU_HALO
    for l in range(N_LRU_SLABS):
        lo = l * LANES
        w_row = lambda k, lo=lo: 0.5 * w_lc_ref[k:k + 1, lo:lo + LANES]
        xl = (_conv_taps(xc_s, l, w_row, 0, n_conv, LRU_CONV, LRU_HALO)
              + 0.5 * b_lc_ref[:, lo:lo + LANES])
        xl_s[0:n_conv, lo:lo + LANES] = xl
    xl_s[n_conv:prow, :] = jnp.zeros((LRU_HALO, D_LRU), jnp.float32)
    _lru_gates(_bf16(xl_s[...]), wg_bf, pre_s)

    hc_ls = (0.5 * LRU_C) * _log_sigmoid(lam_ref[...])
    hb_ga = 0.5 * b_ga_ref[...]
    hb_gx = 0.5 * b_gx_ref[...]
    for l in range(N_LRU_SLABS):
        lo = l * LANES
        ca, cx = _pre_cols(l)
        a, u = _lru_coeffs(pre_s[:, ca:ca + LANES], pre_s[:, cx:cx + LANES], xl_s[:, lo:lo + LANES],
                           hb_ga[:, lo:lo + LANES], hb_gx[:, lo:lo + LANES],
                           hc_ls[:, lo:lo + LANES])
        a_s[l, :, :] = a
        u_s[l, :, :] = u

    n_grp = nb // SUBLANES

    def scan_body(j, carry):
        new = []
        for g in range(n_grp):
            for l in range(N_LRU_SLABS):
                base = g * SUBLANES * pitch
                a_j = a_s[l, pl.ds(base + j, SUBLANES, stride=pitch), :]
                u_j = u_s[l, pl.ds(base + j, SUBLANES, stride=pitch), :]
                h = a_j * carry[g * N_LRU_SLABS + l] + u_j
                u_s[l, pl.ds(base + j, SUBLANES, stride=pitch), :] = h
                new.append(h)
        return tuple(new)

    h0 = tuple(h0_ref[g * SUBLANES:(g + 1) * SUBLANES, l * LANES:(l + 1) * LANES]
               for g in range(n_grp) for l in range(N_LRU_SLABS))
    h_fin = _unrolled(steps, scan_body, h0)
    for g in range(n_grp):
        for l in range(N_LRU_SLABS):
            h_out_ref[g * SUBLANES:(g + 1) * SUBLANES, l * LANES:(l + 1) * LANES] = (
                h_fin[g * N_LRU_SLABS + l])

    for b in range(nb):
        for l in range(N_LRU_SLABS):
            lo = l * LANES
            gate = z_s[b * steps:(b + 1) * steps, OFF_LRU_GATE + lo:OFF_LRU_GATE + lo + LANES]
            mix_s[b * steps:(b + 1) * steps, lo:lo + LANES] = _bf16(
                u_s[l, b * pitch:b * pitch + steps, :] * _silu_of_twice(gate))

    for b in range(nb):
        for l in range(N_CONV_SLABS):
            lo = l * LANES
            uc = _gated_of_twice(
                z_s[b * steps:(b + 1) * steps, OFF_GLU_A + lo:OFF_GLU_A + lo + LANES],
                z_s[b * steps:(b + 1) * steps, OFF_GLU_B + lo:OFF_GLU_B + lo + LANES])
            uc_s[l, b * cpitch + CONV_HALO:(b + 1) * cpitch, :] = uc
            for k in range(CONV_W - 1):
                r = b * cpitch + CONV_HALO - (CONV_W - 1) + k
                uc_s[l, r:r + 1, :] = cb_ref[k, b:b + 1, lo:lo + LANES]
                rn = (b + 1) * cpitch - (CONV_W - 1) + k
                cb_out_ref[k, b:b + 1, lo:lo + LANES] = uc_s[l, rn:rn + 1, :]

    for b in range(nb):
        r0 = b * cpitch
        o0 = b * steps
        ys = []
        for l in range(N_CONV_SLABS):
            lo = l * LANES
            w_row = lambda k, lo=lo: w_dw_ref[k, :, lo:lo + LANES]
            ys.append(_conv_taps(uc_s, l, w_row, r0, steps, CONV_W, CONV_HALO)
                      + b_dw_ref[:, lo:lo + LANES])
        outs = _layernorm_silu_slabs(ys, ln_g_ref, ln_b_ref)
        for l in range(N_CONV_SLABS):
            lo = l * LANES
            ln_s[o0:o0 + steps, lo:lo + LANES] = _bf16(outs[l])

    yc = (_dot(ln_s[...], wpw_bf[...]) + b_pw_ref[...]) * _silu_of_twice(
        z_s[:, OFF_CONV_GATE:OFF_CONV_GATE + D_CONV])
    mix_s[:, D_LRU:D_LRU + D_CONV] = _bf16(yc)

    masks = _lane_head_masks()
    scale = XHEAD_DIM ** -0.5
    qrows = N_XHEADS * steps
    for b in range(nb):
        q = z_s[b * steps:(b + 1) * steps, OFF_Q:OFF_Q + D_XATTN]
        qm = jnp.concatenate([_bf16(q * (m * scale)) for m in masks], axis=0)
        sc_s[b * qrows:(b + 1) * qrows, :] = _dot(qm, _bf16(kt_ref[b]))
    s = sc_s[...]
    e = jnp.exp(s - jnp.max(s, axis=-1, keepdims=True))
    p_s[...] = _bf16(e * (1.0 / jnp.sum(e, axis=-1, keepdims=True)))
    for b in range(nb):
        o_all = _dot_nt(p_s[b * qrows:(b + 1) * qrows, :], _bf16(vt_ref[b]))
        o = o_all[0:steps] * masks[0]
        for h in range(1, N_XHEADS):
            o = o + o_all[h * steps:(h + 1) * steps] * masks[h]
        gate = z_s[b * steps:(b + 1) * steps, OFF_ATTN_GATE:OFF_ATTN_GATE + D_XATTN]
        mix_s[b * steps:(b + 1) * steps, D_LRU + D_CONV:D_MIX] = _bf16(o * _silu_of_twice(gate))

    y = x + _dot(mix_s[...], wout_bf[...])
    y_ref[...] = _rmsnorm(y, g_final_ref[...]).reshape(nb, steps, D_MODEL)


def _sample(x, h0, lb, cb, kt, vt, w, win_bf, wout_bf, wpw_bf, wg_bf):
    bsz, steps, _ = x.shape
    nb = SAMPLE_GROUP
    assert bsz % nb == 0 and nb % SUBLANES == 0
    assert steps + LRU_HALO == SAMPLE_PITCH and steps >= CONV_W - 1
    rows = nb * steps
    prow = nb * SAMPLE_PITCH
    weights = [w["g_norm"], win_bf, w["w_lc"], w["b_lc"], wg_bf, w["b_ga"], w["b_gx"],
               w["lam"], w["w_dw"], w["b_dw"], w["ln_g"], w["ln_b"], wpw_bf, w["b_pw"],
               wout_bf, w["g_final"]]
    weight_specs = [
        _whole((1, D_MODEL)), _whole((D_MODEL, D_IN)), _whole((None, LRU_CONV, D_LRU)),
        _whole((1, D_LRU)), _whole((2, GATE_HALF, D_LRU)), _whole((1, D_LRU)), _whole((1, D_LRU)),
        _whole((1, D_LRU)), _whole((CONV_W, 1, D_CONV)), _whole((1, D_CONV)), _whole((1, D_CONV)),
        _whole((1, D_CONV)), _whole((D_CONV, D_CONV)), _whole((1, D_CONV)),
        _whole((D_MIX, D_MODEL)), _whole((1, D_MODEL)),
    ]
    in_specs = [
        pl.BlockSpec((nb, steps, D_MODEL), lambda g: (g, 0, 0)),
        pl.BlockSpec((nb, D_LRU), lambda g: (g, 0)),
        pl.BlockSpec((LRU_CONV - 1, nb, D_LRU), lambda g: (0, g, 0)),
        pl.BlockSpec((CONV_W - 1, nb, D_CONV), lambda g: (0, g, 0)),
        pl.BlockSpec((nb, D_XATTN, N_MEM), lambda g: (g, 0, 0)),
        pl.BlockSpec((nb, D_XATTN, N_MEM), lambda g: (g, 0, 0)),
    ] + weight_specs
    out_specs = [
        pl.BlockSpec((nb, steps, D_MODEL), lambda g: (g, 0, 0)),
        pl.BlockSpec((nb, D_LRU), lambda g: (g, 0)),
        pl.BlockSpec((LRU_CONV - 1, nb, D_LRU), lambda g: (0, g, 0)),
        pl.BlockSpec((CONV_W - 1, nb, D_CONV), lambda g: (0, g, 0)),
    ]
    out_shape = [
        jax.ShapeDtypeStruct((bsz, steps, D_MODEL), jnp.float32),
        jax.ShapeDtypeStruct((bsz, D_LRU), jnp.float32),
        jax.ShapeDtypeStruct((LRU_CONV - 1, bsz, D_LRU), jnp.float32),
        jax.ShapeDtypeStruct((CONV_W - 1, bsz, D_CONV), jnp.float32),
    ]
    scratch = [
        pltpu.VMEM((rows, D_IN), jnp.float32),
        pltpu.VMEM((N_LRU_SLABS, prow, LANES), jnp.float32),
        pltpu.VMEM((prow, D_LRU), jnp.float32),
        pltpu.VMEM((prow, 2 * D_LRU), jnp.float32),
        pltpu.VMEM((N_LRU_SLABS, prow, LANES), jnp.float32),
        pltpu.VMEM((N_LRU_SLABS, prow, LANES), jnp.float32),
        pltpu.VMEM((N_CONV_SLABS, nb * (CONV_HALO + steps), LANES), jnp.float32),
        pltpu.VMEM((rows, D_CONV), jnp.bfloat16),
        pltpu.VMEM((rows, D_MIX), jnp.bfloat16),
        pltpu.VMEM((N_XHEADS * rows, N_MEM), jnp.float32),
        pltpu.VMEM((N_XHEADS * rows, N_MEM), jnp.bfloat16),
    ]
    return pl.pallas_call(
        functools.partial(_sample_kernel, nb, steps),
        grid=(bsz // nb,),
        in_specs=in_specs,
        out_specs=out_specs,
        out_shape=out_shape,
        scratch_shapes=scratch,
        compiler_params=pltpu.CompilerParams(
            dimension_semantics=("arbitrary",),
            vmem_limit_bytes=VMEM_LIMIT_BYTES),
        name="sample_layer",
    )(x, h0, lb, cb, kt, vt, *weights)


def _time_major(state):
    return jnp.transpose(state, (1, 0, 2))


def _memory_t(mem):
    bsz = mem.shape[0]
    return jnp.transpose(mem, (0, 2, 3, 1)).reshape(bsz, D_XATTN, N_MEM)


def _memory_from_t(mem_t):
    bsz = mem_t.shape[0]
    return jnp.transpose(mem_t.reshape(bsz, N_XHEADS, XHEAD_DIM, N_MEM), (0, 3, 1, 2))


def kernel(x_prompt, x_sample, state_lru_h, state_lru_conv, state_conv, cache_mem_k, cache_mem_v,
           mem_prompt, g_norm, w_in, w_lru_conv, b_lru_conv, w_gate_a, b_gate_a, w_gate_x, b_gate_x,
           lru_lambda, w_dw, b_dw, ln_g, ln_b, w_pw, b_pw, g_mem, w_mem_k, w_mem_v, w_out, g_final):
    depth = g_norm.shape[0]
    assert depth == 1
    w = {
        "g_norm": g_norm, "w_in": w_in, "w_lc": w_lru_conv, "b_lc": b_lru_conv,
        "w_ga": w_gate_a, "w_gx": w_gate_x, "b_ga": b_gate_a, "b_gx": b_gate_x, "lam": lru_lambda,
        "w_dw": jnp.transpose(w_dw, (1, 0, 2)), "b_dw": b_dw, "ln_g": ln_g, "ln_b": ln_b,
        "w_pw": w_pw, "b_pw": b_pw, "w_out": w_out, "g_final": g_final.reshape(1, D_MODEL),
    }
    mkt, mvt = _memkv(mem_prompt, g_mem, w_mem_k, w_mem_v)

    y_prompt, ph, plb, pcb, win_bf, wout_bf, wpw_bf, wg_bf = _prompt(x_prompt, mkt, mvt, w)

    y_sample, sh, slb, scb = _sample(
        x_sample, state_lru_h[0], _time_major(state_lru_conv[0]), _time_major(state_conv[0]),
        _memory_t(cache_mem_k[0]), _memory_t(cache_mem_v[0]), w, win_bf, wout_bf, wpw_bf, wg_bf)

    return (y_prompt, y_sample,
            ph[None], _time_major(plb)[None], _time_major(pcb)[None],
            _memory_from_t(mkt)[None], _memory_from_t(mvt)[None],
            sh[None], _time_major(slb)[None], _time_major(scb)[None])
```

```python
import functools

import jax
import jax.numpy as jnp
from jax import lax
from jax.experimental import pallas as pl
from jax.experimental.pallas import tpu as pltpu

D_MODEL = 1024
N_MEM = 256
D_LRU = 512
LRU_BLOCKS = 8
LRU_BLOCK = D_LRU // LRU_BLOCKS
LRU_CONV = 4
LRU_C = 8.0
D_CONV = 256
CONV_W = 31
N_XHEADS = 4
XHEAD_DIM = 64
D_XATTN = N_XHEADS * XHEAD_DIM
D_MIX = D_LRU + D_CONV + D_XATTN
D_IN = 2 * D_LRU + 3 * D_CONV + 2 * D_XATTN
EPS = 1e-6

LANES = 128
SUBLANES = 8
N_LRU_SLABS = D_LRU // LANES
N_CONV_SLABS = D_CONV // LANES
GATE_HALF = D_LRU // 2

OFF_LRU_IN = 0
OFF_LRU_GATE = D_LRU
OFF_GLU_A = 2 * D_LRU
OFF_GLU_B = OFF_GLU_A + D_CONV
OFF_CONV_GATE = OFF_GLU_B + D_CONV
OFF_Q = OFF_CONV_GATE + D_CONV
OFF_ATTN_GATE = OFF_Q + D_XATTN

LRU_HALO = 8
CONV_HALO = 32

PROMPT_TILE = 512
PROMPT_SEG = 64
PROMPT_PITCH = 68
SAMPLE_PITCH = 40
SAMPLE_GROUP = 16
ATTN_ROWS = 128

VMEM_LIMIT_BYTES = 56 * 1024 * 1024
HALF_LOG2_E = 0.7213475204444817
SCORE_SCALE = XHEAD_DIM ** -0.5 * 2.0 * HALF_LOG2_E


def _silu_of_twice(hx):
    return hx + hx * jnp.tanh(hx)


def _gated_of_twice(ha, hb):
    return ha + ha * jnp.tanh(hb)


def _w_in_col_scale():
    col = lax.broadcasted_iota(jnp.int32, (1, D_IN), 1)
    plain = (col < OFF_LRU_GATE) | ((col >= OFF_Q) & (col < OFF_ATTN_GATE))
    return jnp.where(plain, 1.0, 0.5)


def _sqrt_nonneg(x):
    return jnp.exp2(jnp.log(x) * HALF_LOG2_E)


def _rms_scaled(x):
    ms = jnp.mean(x * x, axis=-1, keepdims=True)
    return x * lax.rsqrt(ms + EPS)


def _rmsnorm(x, g):
    return _rms_scaled(x) * g


def _as_column(row):
    n = row.shape[1] // LANES
    on_diag = (lax.broadcasted_iota(jnp.int32, (LANES, LANES), 0)
               == lax.broadcasted_iota(jnp.int32, (LANES, LANES), 1))
    cols = [jnp.sum(jnp.where(on_diag, row[:, j * LANES:(j + 1) * LANES], 0.0), axis=1, keepdims=True)
            for j in range(n)]
    return jnp.concatenate(cols, axis=0)


def _log_sigmoid(x):
    y = -x
    return -(jnp.maximum(y, 0.0) + jnp.log1p(jnp.exp(-jnp.abs(y))))


def _bf16(x):
    return x.astype(jnp.bfloat16)


def _dot(a, b):
    return jnp.dot(a, b, preferred_element_type=jnp.float32)


def _dot_nt(a, b):
    return lax.dot_general(a, b, (((1,), (1,)), ((), ())), preferred_element_type=jnp.float32)


def _unrolled(n, body, init):
    carry = init
    for i in range(n):
        carry = body(i, carry)
    return carry


def _aligned(x, m):
    return x if isinstance(x, int) else pl.multiple_of(x, m)


def _head_of(index):
    return lax.shift_right_logical(index, XHEAD_DIM.bit_length() - 1)


def _lane_head_masks():
    head = _head_of(lax.broadcasted_iota(jnp.int32, (1, D_XATTN), 1))
    return [(head == h).astype(jnp.float32) for h in range(N_XHEADS)]


def _row_head_masks():
    head = _head_of(lax.broadcasted_iota(jnp.int32, (D_XATTN, 1), 0))
    return [(head == h).astype(jnp.float32) for h in range(N_XHEADS)]


def _head_stack_t(kt, vt, kst_ref, vst_ref):
    scale = SCORE_SCALE
    for h, m in enumerate(_row_head_masks()):
        kst_ref[:, h * N_MEM:(h + 1) * N_MEM] = _bf16(kt * (m * scale))
        vst_ref[:, h * N_MEM:(h + 1) * N_MEM] = _bf16(vt * m)


def _zero_row_after(x):
    bits = lax.bitcast_convert_type(x, jnp.uint32)
    zero = lax.shift_right_logical(lax.shift_right_logical(bits, jnp.uint32(16)), jnp.uint32(16))
    return lax.bitcast_convert_type(zero, jnp.float32)[0:1, :]


def _softmax_heads(s):
    ps = []
    for h in range(N_XHEADS):
        sh = s[:, h * N_MEM:(h + 1) * N_MEM]
        e = jnp.exp2(sh - jnp.max(sh, axis=-1, keepdims=True))
        inv = 1.0 / jnp.sum(e, axis=-1, keepdims=True)
        ps.append(_bf16(e * inv))
    return jnp.concatenate(ps, axis=1)


def _lru_gates(xl_bf, wg_ref, pre_s):
    pre_s[:, 0:D_LRU] = _dot(xl_bf[:, 0:GATE_HALF], wg_ref[0])
    pre_s[:, D_LRU:2 * D_LRU] = _dot(xl_bf[:, GATE_HALF:D_LRU], wg_ref[1])


def _pre_cols(l):
    half_slabs = N_LRU_SLABS // 2
    base = (l // half_slabs) * D_LRU + (l % half_slabs) * LANES
    return base, base + GATE_HALF


def _lru_coeffs(hpre_a, hpre_x, hx, hb_a, hb_x, hc_ls):
    t_r = jnp.tanh(hpre_a + hb_a)
    t_i = jnp.tanh(hpre_x + hb_x)
    log_a = hc_ls + hc_ls * t_r
    a = jnp.exp(log_a)
    one_minus_a2 = -jnp.tanh(log_a) * (a * a + 1.0)
    u = _sqrt_nonneg(one_minus_a2) * (hx + hx * t_i)
    return a, u


def _conv_taps(src_s, l, w_row, row0, nrows, width, halo):
    acc = None
    for k in range(width):
        x_k = src_s[l, pl.ds(row0 + (halo - (width - 1) + k), nrows), :]
        acc = w_row(k) * x_k if acc is None else acc + w_row(k) * x_k
    return acc


def _layernorm_silu_slabs(ys, g_ref, b_ref):
    tot = ys[0]
    for y in ys[1:]:
        tot = tot + y
    mu = jnp.sum(tot, axis=-1, keepdims=True) * (1.0 / D_CONV)
    cs = [y - mu for y in ys]
    sq = cs[0] * cs[0]
    for c in cs[1:]:
        sq = sq + c * c
    var = jnp.sum(sq, axis=-1, keepdims=True) * (1.0 / D_CONV)
    inv = lax.rsqrt(var + EPS)
    outs = []
    for l, c in enumerate(cs):
        lo = l * LANES
        half_g = 0.5 * g_ref[:, lo:lo + LANES]
        half_b = 0.5 * b_ref[:, lo:lo + LANES]
        outs.append(_silu_of_twice(c * inv * half_g + half_b))
    return outs


def _gate_block_diag(w_blocks):
    n = (LRU_BLOCKS // 2) * LRU_BLOCK
    stacked = _bf16(w_blocks.reshape(n, LRU_BLOCK))
    src = lax.broadcasted_iota(jnp.int32, (LRU_BLOCK, n), 0)
    dst = lax.broadcasted_iota(jnp.int32, (LRU_BLOCK, n), 1)
    spread = _bf16(((dst & (LRU_BLOCK - 1)) == src).astype(jnp.float32))
    rows = _head_of(lax.broadcasted_iota(jnp.int32, (n, n), 0))
    cols = _head_of(lax.broadcasted_iota(jnp.int32, (n, n), 1))
    return _bf16(_dot(stacked, spread) * (rows == cols).astype(jnp.float32))


def _memkv_kernel(mem_ref, g_ref, wk_ref, wv_ref, kt_ref, vt_ref):
    m = _bf16(_rmsnorm(mem_ref[0], g_ref[...]))
    kt_ref[0] = _dot(m, _bf16(wk_ref[...])).T
    vt_ref[0] = _dot(m, _bf16(wv_ref[...])).T


def _memkv(mem, g_mem, w_k, w_v):
    bsz = mem.shape[0]
    return pl.pallas_call(
        _memkv_kernel,
        grid=(bsz,),
        in_specs=[
            pl.BlockSpec((1, N_MEM, D_MODEL), lambda b: (b, 0, 0)),
            pl.BlockSpec((1, D_MODEL), lambda b: (0, 0)),
            pl.BlockSpec((None, D_MODEL, D_XATTN), lambda b: (0, 0, 0)),
            pl.BlockSpec((None, D_MODEL, D_XATTN), lambda b: (0, 0, 0)),
        ],
        out_specs=[
            pl.BlockSpec((1, D_XATTN, N_MEM), lambda b: (b, 0, 0)),
            pl.BlockSpec((1, D_XATTN, N_MEM), lambda b: (b, 0, 0)),
        ],
        out_shape=[jax.ShapeDtypeStruct((bsz, D_XATTN, N_MEM), jnp.float32)] * 2,
        compiler_params=pltpu.CompilerParams(dimension_semantics=("arbitrary",)),
        name="memkv",
    )(mem, g_mem, w_k, w_v)


def _prompt_kernel(x_ref, kt_ref, vt_ref, g_norm_ref, w_in_ref, w_lc_ref, b_lc_ref, wga_ref, wgx_ref,
                   b_ga_ref, b_gx_ref, lam_ref, w_dw_ref, b_dw_ref, ln_g_ref, ln_b_ref,
                   w_pw_ref, b_pw_ref, w_out_ref, g_final_ref,
                   y_ref, h_out_ref, lb_out_ref, cb_out_ref, win_bf, wout_bf, wpw_bf, wg_bf,
                   z_s, xc_s, xl_s, pre_s, a_s, u_s, uc_s, ln_s, mix_s, hin_s, kst_s, vst_s, sc_s, p_s):
    b = pl.program_id(0)
    t = pl.program_id(1)
    n_t = pl.num_programs(1)
    tile = PROMPT_TILE
    seg = PROMPT_SEG
    n_seg = tile // seg
    pitch = PROMPT_PITCH

    @pl.when(jnp.logical_and(b == 0, t == 0))
    def _():
        win_bf[...] = _bf16(w_in_ref[...] * _as_column(g_norm_ref[...]) * _w_in_col_scale())
        wout_bf[...] = _bf16(w_out_ref[...])
        wpw_bf[...] = _bf16(w_pw_ref[...])
        half = LRU_BLOCKS // 2
        for h in range(2):
            wg_bf[h, :, 0:GATE_HALF] = _gate_block_diag(wga_ref[h * half:(h + 1) * half])
            wg_bf[h, :, GATE_HALF:D_LRU] = _gate_block_diag(wgx_ref[h * half:(h + 1) * half])

    @pl.when(t == 0)
    def _():
        xc_s[:, 0:LRU_HALO, :] = jnp.zeros((N_LRU_SLABS, LRU_HALO, LANES), jnp.float32)
        uc_s[:, 0:CONV_HALO, :] = jnp.zeros((N_CONV_SLABS, CONV_HALO, LANES), jnp.float32)
        hin_s[:, 0:1, :] = jnp.zeros((N_LRU_SLABS, 1, LANES), jnp.float32)
        _head_stack_t(kt_ref[0], vt_ref[0], kst_s, vst_s)

    z_s[...] = _dot(_bf16(_rms_scaled(x_ref[0])), win_bf[...])

    for l in range(N_LRU_SLABS):
        lo = l * LANES
        xc_s[l, LRU_HALO:LRU_HALO + tile, :] = z_s[:, OFF_LRU_IN + lo:OFF_LRU_IN + lo + LANES]
    for l in range(N_LRU_SLABS):
        lo = l * LANES
        w_row = lambda k, lo=lo: 0.5 * w_lc_ref[k:k + 1, lo:lo + LANES]
        xl = (_conv_taps(xc_s, l, w_row, 0, tile, LRU_CONV, LRU_HALO)
              + 0.5 * b_lc_ref[:, lo:lo + LANES])
        xl_s[:, lo:lo + LANES] = xl
    _lru_gates(_bf16(xl_s[...]), wg_bf, pre_s)

    hc_ls = (0.5 * LRU_C) * _log_sigmoid(lam_ref[...])
    hb_ga = 0.5 * b_ga_ref[...]
    hb_gx = 0.5 * b_gx_ref[...]

    def coeff_body(c, carry):
        r0 = _aligned(c * seg, seg)
        p0 = _aligned(c * pitch, SUBLANES)
        for l in range(N_LRU_SLABS):
            lo = l * LANES
            ca, cx = _pre_cols(l)
            a, u = _lru_coeffs(pre_s[pl.ds(r0, seg), ca:ca + LANES],
                               pre_s[pl.ds(r0, seg), cx:cx + LANES],
                               xl_s[pl.ds(r0, seg), lo:lo + LANES],
                               hb_ga[:, lo:lo + LANES], hb_gx[:, lo:lo + LANES],
                               hc_ls[:, lo:lo + LANES])
            a_s[l, pl.ds(p0, seg), :] = a
            u_s[l, pl.ds(p0, seg), :] = u
        return carry

    _unrolled(n_seg, coeff_body, 0)

    def tot_body(j, carry):
        new = []
        for l in range(N_LRU_SLABS):
            a_tot, u_tot = carry[l]
            a_j = a_s[l, pl.ds(j, n_seg, stride=pitch), :]
            u_j = u_s[l, pl.ds(j, n_seg, stride=pitch), :]
            new.append((a_j * a_tot, a_j * u_tot + u_j))
        return tuple(new)

    init = tuple((jnp.ones((n_seg, LANES), jnp.float32), jnp.zeros((n_seg, LANES), jnp.float32))
                 for _ in range(N_LRU_SLABS))
    totals = _unrolled(seg, tot_body, init)

    h_in = []
    for l in range(N_LRU_SLABS):
        a_tot, u_tot = totals[l]
        h = hin_s[l, 0:1, :]
        for c in range(n_seg):
            h = a_tot[c:c + 1, :] * h + u_tot[c:c + 1, :]
            hin_s[l, c + 1:c + 2, :] = h
        h_in.append(hin_s[l, 0:n_seg, :])
        hin_s[l, 0:1, :] = h

    def scan_body(j, carry):
        new = []
        for l in range(N_LRU_SLABS):
            a_j = a_s[l, pl.ds(j, n_seg, stride=pitch), :]
            u_j = u_s[l, pl.ds(j, n_seg, stride=pitch), :]
            h = a_j * carry[l] + u_j
            u_s[l, pl.ds(j, n_seg, stride=pitch), :] = h
            new.append(h)
        return tuple(new)

    scores = _dot(_bf16(z_s[:, OFF_Q:OFF_Q + D_XATTN]), kst_s[...])
    sc_s[...] = scores
    after_scores = _zero_row_after(scores[tile - SUBLANES:tile, N_XHEADS * N_MEM - LANES:])
    _unrolled(seg, scan_body, tuple(h + after_scores for h in h_in))

    for l in range(N_CONV_SLABS):
        lo = l * LANES
        uc_s[l, CONV_HALO:CONV_HALO + tile, :] = _gated_of_twice(
            z_s[:, OFF_GLU_A + lo:OFF_GLU_A + lo + LANES],
            z_s[:, OFF_GLU_B + lo:OFF_GLU_B + lo + LANES])

    def seg_body(c, carry):
        r0 = _aligned(c * seg, seg)
        p0 = _aligned(c * pitch, SUBLANES)
        for l in range(N_LRU_SLABS):
            lo = l * LANES
            gate = z_s[pl.ds(r0, seg), OFF_LRU_GATE + lo:OFF_LRU_GATE + lo + LANES]
            mix_s[pl.ds(r0, seg), lo:lo + LANES] = _bf16(u_s[l, pl.ds(p0, seg), :] * _silu_of_twice(gate))
        ys = []
        for l in range(N_CONV_SLABS):
            lo = l * LANES
            w_row = lambda k, lo=lo: w_dw_ref[k, :, lo:lo + LANES]
            ys.append(_conv_taps(uc_s, l, w_row, r0, seg, CONV_W, CONV_HALO)
                      + b_dw_ref[:, lo:lo + LANES])
        outs = _layernorm_silu_slabs(ys, ln_g_ref, ln_b_ref)
        for l in range(N_CONV_SLABS):
            lo = l * LANES
            ln_s[pl.ds(r0, seg), lo:lo + LANES] = _bf16(outs[l])
        return carry

    _unrolled(n_seg, seg_body, 0)

    yc = (_dot(ln_s[...], wpw_bf[...]) + b_pw_ref[...]) * _silu_of_twice(
        z_s[:, OFF_CONV_GATE:OFF_CONV_GATE + D_CONV])
    mix_s[:, D_LRU:D_LRU + D_CONV] = _bf16(yc)

    def attn_body(c, carry):
        r0 = _aligned(c * ATTN_ROWS, ATTN_ROWS)
        p_s[pl.ds(r0, ATTN_ROWS), :] = _softmax_heads(sc_s[pl.ds(r0, ATTN_ROWS), :])
        return carry

    _unrolled(tile // ATTN_ROWS, attn_body, 0)
    o = _dot_nt(p_s[...], vst_s[...])
    mix_s[:, D_LRU + D_CONV:D_MIX] = _bf16(
        o * _silu_of_twice(z_s[:, OFF_ATTN_GATE:OFF_ATTN_GATE + D_XATTN]))

    y = x_ref[0] + _dot(mix_s[...], wout_bf[...])
    y_ref[0] = _rmsnorm(y, g_final_ref[...])

    for l in range(N_LRU_SLABS):
        xc_s[l, LRU_HALO - (LRU_CONV - 1):LRU_HALO, :] = (
            xc_s[l, LRU_HALO + tile - (LRU_CONV - 1):LRU_HALO + tile, :])
    for l in range(N_CONV_SLABS):
        uc_s[l, CONV_HALO - (CONV_W - 1):CONV_HALO, :] = (
            uc_s[l, CONV_HALO + tile - (CONV_W - 1):CONV_HALO + tile, :])

    def write_state(bb):
        for l in range(N_LRU_SLABS):
            lo = l * LANES
            h_out_ref[bb:bb + 1, lo:lo + LANES] = hin_s[l, 0:1, :]
            for k in range(LRU_CONV - 1):
                r = LRU_HALO - (LRU_CONV - 1) + k
                lb_out_ref[k, bb:bb + 1, lo:lo + LANES] = xc_s[l, r:r + 1, :]
        for l in range(N_CONV_SLABS):
            lo = l * LANES
            for k in range(CONV_W - 1):
                r = CONV_HALO - (CONV_W - 1) + k
                cb_out_ref[k, bb:bb + 1, lo:lo + LANES] = uc_s[l, r:r + 1, :]

    for bb in range(h_out_ref.shape[0]):
        pl.when(jnp.logical_and(t == n_t - 1, b == bb))(functools.partial(write_state, bb))


def _whole(shape):
    nd = len(shape)
    return pl.BlockSpec(shape, lambda *_: (0,) * nd, pipeline_mode=pl.Buffered(1))


def _prompt(x, kt, vt, w):
    bsz, seq, _ = x.shape
    tile = PROMPT_TILE
    n_t = seq // tile
    n_seg = tile // PROMPT_SEG
    weights = [w["g_norm"], w["w_in"], w["w_lc"], w["b_lc"], w["w_ga"], w["w_gx"], w["b_ga"], w["b_gx"],
               w["lam"], w["w_dw"], w["b_dw"], w["ln_g"], w["ln_b"], w["w_pw"], w["b_pw"],
               w["w_out"], w["g_final"]]
    weight_specs = [
        _whole((1, D_MODEL)), _whole((None, D_MODEL, D_IN)), _whole((None, LRU_CONV, D_LRU)),
        _whole((1, D_LRU)), _whole((None, LRU_BLOCKS, LRU_BLOCK, LRU_BLOCK)),
        _whole((None, LRU_BLOCKS, LRU_BLOCK, LRU_BLOCK)), _whole((1, D_LRU)), _whole((1, D_LRU)),
        _whole((1, D_LRU)), _whole((CONV_W, 1, D_CONV)), _whole((1, D_CONV)), _whole((1, D_CONV)),
        _whole((1, D_CONV)), _whole((None, D_CONV, D_CONV)), _whole((1, D_CONV)),
        _whole((None, D_MIX, D_MODEL)), _whole((1, D_MODEL)),
    ]
    in_specs = [
        pl.BlockSpec((1, tile, D_MODEL), lambda b, t: (b, t, 0)),
        pl.BlockSpec((1, D_XATTN, N_MEM), lambda b, t: (b, 0, 0)),
        pl.BlockSpec((1, D_XATTN, N_MEM), lambda b, t: (b, 0, 0)),
    ] + weight_specs
    out_specs = [
        pl.BlockSpec((1, tile, D_MODEL), lambda b, t: (b, t, 0)),
        _whole((bsz, D_LRU)),
        _whole((LRU_CONV - 1, bsz, D_LRU)),
        _whole((CONV_W - 1, bsz, D_CONV)),
        _whole((D_MODEL, D_IN)), _whole((D_MIX, D_MODEL)), _whole((D_CONV, D_CONV)),
        _whole((2, GATE_HALF, D_LRU)),
    ]
    out_shape = [
        jax.ShapeDtypeStruct((bsz, seq, D_MODEL), jnp.float32),
        jax.ShapeDtypeStruct((bsz, D_LRU), jnp.float32),
        jax.ShapeDtypeStruct((LRU_CONV - 1, bsz, D_LRU), jnp.float32),
        jax.ShapeDtypeStruct((CONV_W - 1, bsz, D_CONV), jnp.float32),
        jax.ShapeDtypeStruct((D_MODEL, D_IN), jnp.bfloat16),
        jax.ShapeDtypeStruct((D_MIX, D_MODEL), jnp.bfloat16),
        jax.ShapeDtypeStruct((D_CONV, D_CONV), jnp.bfloat16),
        jax.ShapeDtypeStruct((2, GATE_HALF, D_LRU), jnp.bfloat16),
    ]
    scratch = [
        pltpu.VMEM((tile, D_IN), jnp.float32),
        pltpu.VMEM((N_LRU_SLABS, LRU_HALO + tile, LANES), jnp.float32),
        pltpu.VMEM((tile, D_LRU), jnp.float32),
        pltpu.VMEM((tile, 2 * D_LRU), jnp.float32),
        pltpu.VMEM((N_LRU_SLABS, n_seg * PROMPT_PITCH, LANES), jnp.float32),
        pltpu.VMEM((N_LRU_SLABS, n_seg * PROMPT_PITCH, LANES), jnp.float32),
        pltpu.VMEM((N_CONV_SLABS, CONV_HALO + tile, LANES), jnp.float32),
        pltpu.VMEM((tile, D_CONV), jnp.bfloat16),
        pltpu.VMEM((tile, D_MIX), jnp.bfloat16),
        pltpu.VMEM((N_LRU_SLABS, 2 * SUBLANES, LANES), jnp.float32),
        pltpu.VMEM((D_XATTN, N_XHEADS * N_MEM), jnp.bfloat16),
        pltpu.VMEM((D_XATTN, N_XHEADS * N_MEM), jnp.bfloat16),
        pltpu.VMEM((tile, N_XHEADS * N_MEM), jnp.float32),
        pltpu.VMEM((tile, N_XHEADS * N_MEM), jnp.bfloat16),
    ]
    return pl.pallas_call(
        _prompt_kernel,
        grid=(bsz, n_t),
        in_specs=in_specs,
        out_specs=out_specs,
        out_shape=out_shape,
        scratch_shapes=scratch,
        compiler_params=pltpu.CompilerParams(
            dimension_semantics=("arbitrary", "arbitrary"),
            vmem_limit_bytes=VMEM_LIMIT_BYTES),
        name="prompt_layer",
    )(x, kt, vt, *weights)


def _sample_kernel(nb, steps,
                   x_ref, h0_ref, lb_ref, cb_ref, kt_ref, vt_ref,
                   win_bf, w_lc_ref, b_lc_ref, wg_bf,
                   b_ga_ref, b_gx_ref, lam_ref, w_dw_ref, b_dw_ref, ln_g_ref, ln_b_ref,
                   wpw_bf, b_pw_ref, wout_bf, g_final_ref,
                   y_ref, h_out_ref, lb_out_ref, cb_out_ref,
                   z_s, xc_s, xl_s, pre_s, a_s, u_s, uc_s, ln_s, mix_s, sc_s, p_s):
    rows = nb * steps
    pitch = SAMPLE_PITCH
    cpitch = CONV_HALO + steps
    prow = nb * pitch

    x = x_ref[...].reshape(rows, D_MODEL)
    z_s[...] = _dot(_bf16(_rms_scaled(x)), win_bf[...])

    for b in range(nb):
        for l in range(N_LRU_SLABS):
            lo = l * LANES
            zcol = OFF_LRU_IN + lo
            xc_s[l, b * pitch:b * pitch + LRU_HALO, :] = jnp.zeros((LRU_HALO, LANES), jnp.float32)
            for k in range(LRU_CONV - 1):
                r = b * pitch + LRU_HALO - (LRU_CONV - 1) + k
                xc_s[l, r:r + 1, :] = lb_ref[k, b:b + 1, lo:lo + LANES]
                rn = (b + 1) * steps - (LRU_CONV - 1) + k
                lb_out_ref[k, b:b + 1, lo:lo + LANES] = z_s[rn:rn + 1, zcol:zcol + LANES]
            xc_s[l, b * pitch + LRU_HALO:(b + 1) * pitch, :] = (
                z_s[b * steps:(b + 1) * steps, zcol:zcol + LANES])
    n_conv = prow - LRU_HALO
    for l in range(N_LRU_SLABS):
        lo = l * LANES
        w_row = lambda k, lo=lo: 0.5 * w_lc_ref[k:k + 1, lo:lo + LANES]
        xl = (_conv_taps(xc_s, l, w_row, 0, n_conv, LRU_CONV, LRU_HALO)
              + 0.5 * b_lc_ref[:, lo:lo + LANES])
        xl_s[0:n_conv, lo:lo + LANES] = xl
    xl_s[n_conv:prow, :] = jnp.zeros((LRU_HALO, D_LRU), jnp.float32)
    _lru_gates(_bf16(xl_s[...]), wg_bf, pre_s)

    hc_ls = (0.5 * LRU_C) * _log_sigmoid(lam_ref[...])
    hb_ga = 0.5 * b_ga_ref[...]
    hb_gx = 0.5 * b_gx_ref[...]
    for l in range(N_LRU_SLABS):
        lo = l * LANES
        ca, cx = _pre_cols(l)
        a, u = _lru_coeffs(pre_s[:, ca:ca + LANES], pre_s[:, cx:cx + LANES], xl_s[:, lo:lo + LANES],
                           hb_ga[:, lo:lo + LANES], hb_gx[:, lo:lo + LANES],
                           hc_ls[:, lo:lo + LANES])
        a_s[l, :, :] = a
        u_s[l, :, :] = u

    n_grp = nb // SUBLANES

    def scan_body(j, carry):
        new = []
        for g in range(n_grp):
            for l in range(N_LRU_SLABS):
                base = g * SUBLANES * pitch
                a_j = a_s[l, pl.ds(base + j, SUBLANES, stride=pitch), :]
                u_j = u_s[l, pl.ds(base + j, SUBLANES, stride=pitch), :]
                h = a_j * carry[g * N_LRU_SLABS + l] + u_j
                u_s[l, pl.ds(base + j, SUBLANES, stride=pitch), :] = h
                new.append(h)
        return tuple(new)

    h0 = tuple(h0_ref[g * SUBLANES:(g + 1) * SUBLANES, l * LANES:(l + 1) * LANES]
               for g in range(n_grp) for l in range(N_LRU_SLABS))
    h_fin = _unrolled(steps, scan_body, h0)
    for g in range(n_grp):
        for l in range(N_LRU_SLABS):
            h_out_ref[g * SUBLANES:(g + 1) * SUBLANES, l * LANES:(l + 1) * LANES] = (
                h_fin[g * N_LRU_SLABS + l])

    for b in range(nb):
        for l in range(N_LRU_SLABS):
            lo = l * LANES
            gate = z_s[b * steps:(b + 1) * steps, OFF_LRU_GATE + lo:OFF_LRU_GATE + lo + LANES]
            mix_s[b * steps:(b + 1) * steps, lo:lo + LANES] = _bf16(
                u_s[l, b * pitch:b * pitch + steps, :] * _silu_of_twice(gate))

    for b in range(nb):
        for l in range(N_CONV_SLABS):
            lo = l * LANES
            uc = _gated_of_twice(
                z_s[b * steps:(b + 1) * steps, OFF_GLU_A + lo:OFF_GLU_A + lo + LANES],
                z_s[b * steps:(b + 1) * steps, OFF_GLU_B + lo:OFF_GLU_B + lo + LANES])
            uc_s[l, b * cpitch + CONV_HALO:(b + 1) * cpitch, :] = uc
            for k in range(CONV_W - 1):
                r = b * cpitch + CONV_HALO - (CONV_W - 1) + k
                uc_s[l, r:r + 1, :] = cb_ref[k, b:b + 1, lo:lo + LANES]
                rn = (b + 1) * cpitch - (CONV_W - 1) + k
                cb_out_ref[k, b:b + 1, lo:lo + LANES] = uc_s[l, rn:rn + 1, :]

    for b in range(nb):
        r0 = b * cpitch
        o0 = b * steps
        ys = []
        for l in range(N_CONV_SLABS):
            lo = l * LANES
            w_row = lambda k, lo=lo: w_dw_ref[k, :, lo:lo + LANES]
            ys.append(_conv_taps(uc_s, l, w_row, r0, steps, CONV_W, CONV_HALO)
                      + b_dw_ref[:, lo:lo + LANES])
        outs = _layernorm_silu_slabs(ys, ln_g_ref, ln_b_ref)
        for l in range(N_CONV_SLABS):
            lo = l * LANES
            ln_s[o0:o0 + steps, lo:lo + LANES] = _bf16(outs[l])

    yc = (_dot(ln_s[...], wpw_bf[...]) + b_pw_ref[...]) * _silu_of_twice(
        z_s[:, OFF_CONV_GATE:OFF_CONV_GATE + D_CONV])
    mix_s[:, D_LRU:D_LRU + D_CONV] = _bf16(yc)

    masks = _lane_head_masks()
    scale = SCORE_SCALE
    qrows = N_XHEADS * steps
    for b in range(nb):
        q = z_s[b * steps:(b + 1) * steps, OFF_Q:OFF_Q + D_XATTN]
        qm = jnp.concatenate([_bf16(q * (m * scale)) for m in masks], axis=0)
        sc_s[b * qrows:(b + 1) * qrows, :] = _dot(qm, _bf16(kt_ref[b]))
    s = sc_s[...]
    e = jnp.exp2(s - jnp.max(s, axis=-1, keepdims=True))
    p_s[...] = _bf16(e * (1.0 / jnp.sum(e, axis=-1, keepdims=True)))
    for b in range(nb):
        o_all = _dot_nt(p_s[b * qrows:(b + 1) * qrows, :], _bf16(vt_ref[b]))
        o = o_all[0:steps] * masks[0]
        for h in range(1, N_XHEADS):
            o = o + o_all[h * steps:(h + 1) * steps] * masks[h]
        gate = z_s[b * steps:(b + 1) * steps, OFF_ATTN_GATE:OFF_ATTN_GATE + D_XATTN]
        mix_s[b * steps:(b + 1) * steps, D_LRU + D_CONV:D_MIX] = _bf16(o * _silu_of_twice(gate))

    y = x + _dot(mix_s[...], wout_bf[...])
    y_ref[...] = _rmsnorm(y, g_final_ref[...]).reshape(nb, steps, D_MODEL)


def _sample(x, h0, lb, cb, kt, vt, w, win_bf, wout_bf, wpw_bf, wg_bf):
    bsz, steps, _ = x.shape
    nb = SAMPLE_GROUP
    assert bsz % nb == 0 and nb % SUBLANES == 0
    assert steps + LRU_HALO == SAMPLE_PITCH and steps >= CONV_W - 1
    rows = nb * steps
    prow = nb * SAMPLE_PITCH
    weights = [win_bf, w["w_lc"], w["b_lc"], wg_bf, w["b_ga"], w["b_gx"],
               w["lam"], w["w_dw"], w["b_dw"], w["ln_g"], w["ln_b"], wpw_bf, w["b_pw"],
               wout_bf, w["g_final"]]
    weight_specs = [
        _whole((D_MODEL, D_IN)), _whole((None, LRU_CONV, D_LRU)),
        _whole((1, D_LRU)), _whole((2, GATE_HALF, D_LRU)), _whole((1, D_LRU)), _whole((1, D_LRU)),
        _whole((1, D_LRU)), _whole((CONV_W, 1, D_CONV)), _whole((1, D_CONV)), _whole((1, D_CONV)),
        _whole((1, D_CONV)), _whole((D_CONV, D_CONV)), _whole((1, D_CONV)),
        _whole((D_MIX, D_MODEL)), _whole((1, D_MODEL)),
    ]
    in_specs = [
        pl.BlockSpec((nb, steps, D_MODEL), lambda g: (g, 0, 0)),
        pl.BlockSpec((nb, D_LRU), lambda g: (g, 0)),
        pl.BlockSpec((LRU_CONV - 1, nb, D_LRU), lambda g: (0, g, 0)),
        pl.BlockSpec((CONV_W - 1, nb, D_CONV), lambda g: (0, g, 0)),
        pl.BlockSpec((nb, D_XATTN, N_MEM), lambda g: (g, 0, 0)),
        pl.BlockSpec((nb, D_XATTN, N_MEM), lambda g: (g, 0, 0)),
    ] + weight_specs
    out_specs = [
        pl.BlockSpec((nb, steps, D_MODEL), lambda g: (g, 0, 0)),
        pl.BlockSpec((nb, D_LRU), lambda g: (g, 0)),
        pl.BlockSpec((LRU_CONV - 1, nb, D_LRU), lambda g: (0, g, 0)),
        pl.BlockSpec((CONV_W - 1, nb, D_CONV), lambda g: (0, g, 0)),
    ]
    out_shape = [
        jax.ShapeDtypeStruct((bsz, steps, D_MODEL), jnp.float32),
        jax.ShapeDtypeStruct((bsz, D_LRU), jnp.float32),
        jax.ShapeDtypeStruct((LRU_CONV - 1, bsz, D_LRU), jnp.float32),
        jax.ShapeDtypeStruct((CONV_W - 1, bsz, D_CONV), jnp.float32),
    ]
    scratch = [
        pltpu.VMEM((rows, D_IN), jnp.float32),
        pltpu.VMEM((N_LRU_SLABS, prow, LANES), jnp.float32),
        pltpu.VMEM((prow, D_LRU), jnp.float32),
        pltpu.VMEM((prow, 2 * D_LRU), jnp.float32),
        pltpu.VMEM((N_LRU_SLABS, prow, LANES), jnp.float32),
        pltpu.VMEM((N_LRU_SLABS, prow, LANES), jnp.float32),
        pltpu.VMEM((N_CONV_SLABS, nb * (CONV_HALO + steps), LANES), jnp.float32),
        pltpu.VMEM((rows, D_CONV), jnp.bfloat16),
        pltpu.VMEM((rows, D_MIX), jnp.bfloat16),
        pltpu.VMEM((N_XHEADS * rows, N_MEM), jnp.float32),
        pltpu.VMEM((N_XHEADS * rows, N_MEM), jnp.bfloat16),
    ]
    return pl.pallas_call(
        functools.partial(_sample_kernel, nb, steps),
        grid=(bsz // nb,),
        in_specs=in_specs,
        out_specs=out_specs,
        out_shape=out_shape,
        scratch_shapes=scratch,
        compiler_params=pltpu.CompilerParams(
            dimension_semantics=("arbitrary",),
            vmem_limit_bytes=VMEM_LIMIT_BYTES),
        name="sample_layer",
    )(x, h0, lb, cb, kt, vt, *weights)


def _time_major(state):
    return jnp.transpose(state, (1, 0, 2))


def _memory_t(mem):
    bsz = mem.shape[0]
    return jnp.transpose(mem, (0, 2, 3, 1)).reshape(bsz, D_XATTN, N_MEM)


def _memory_from_t(mem_t):
    bsz = mem_t.shape[0]
    return jnp.transpose(mem_t.reshape(bsz, N_XHEADS, XHEAD_DIM, N_MEM), (0, 3, 1, 2))


def kernel(x_prompt, x_sample, state_lru_h, state_lru_conv, state_conv, cache_mem_k, cache_mem_v,
           mem_prompt, g_norm, w_in, w_lru_conv, b_lru_conv, w_gate_a, b_gate_a, w_gate_x, b_gate_x,
           lru_lambda, w_dw, b_dw, ln_g, ln_b, w_pw, b_pw, g_mem, w_mem_k, w_mem_v, w_out, g_final):
    depth = g_norm.shape[0]
    assert depth == 1
    w = {
        "g_norm": g_norm, "w_in": w_in, "w_lc": w_lru_conv, "b_lc": b_lru_conv,
        "w_ga": w_gate_a, "w_gx": w_gate_x, "b_ga": b_gate_a, "b_gx": b_gate_x, "lam": lru_lambda,
        "w_dw": jnp.transpose(w_dw, (1, 0, 2)), "b_dw": b_dw, "ln_g": ln_g, "ln_b": ln_b,
        "w_pw": w_pw, "b_pw": b_pw, "w_out": w_out, "g_final": g_final.reshape(1, D_MODEL),
    }
    mkt, mvt = _memkv(mem_prompt, g_mem, w_mem_k, w_mem_v)

    y_prompt, ph, plb, pcb, win_bf, wout_bf, wpw_bf, wg_bf = _prompt(x_prompt, mkt, mvt, w)

    y_sample, sh, slb, scb = _sample(
        x_sample, state_lru_h[0], _time_major(state_lru_conv[0]), _time_major(state_conv[0]),
        _memory_t(cache_mem_k[0]), _memory_t(cache_mem_v[0]), w, win_bf, wout_bf, wpw_bf, wg_bf)

    return (y_prompt, y_sample,
            ph[None], _time_major(plb)[None], _time_major(pcb)[None],
            _memory_from_t(mkt)[None], _memory_from_t(mvt)[None],
            sh[None], _time_major(slb)[None], _time_major(scb)[None])
```

```python
import functools

import jax
import jax.numpy as jnp
from jax import lax
from jax.experimental import pallas as pl
from jax.experimental.pallas import tpu as pltpu

D_MODEL = 1024
N_MEM = 256
D_LRU = 512
LRU_BLOCKS = 8
LRU_BLOCK = D_LRU // LRU_BLOCKS
LRU_CONV = 4
LRU_C = 8.0
D_CONV = 256
CONV_W = 31
N_XHEADS = 4
XHEAD_DIM = 64
D_XATTN = N_XHEADS * XHEAD_DIM
D_MIX = D_LRU + D_CONV + D_XATTN
D_IN = 2 * D_LRU + 3 * D_CONV + 2 * D_XATTN
EPS = 1e-6

LANES = 128
SUBLANES = 8
N_LRU_SLABS = D_LRU // LANES
N_CONV_SLABS = D_CONV // LANES
GATE_HALF = D_LRU // 2

OFF_LRU_IN = 0
OFF_LRU_GATE = D_LRU
OFF_GLU_A = 2 * D_LRU
OFF_GLU_B = OFF_GLU_A + D_CONV
OFF_CONV_GATE = OFF_GLU_B + D_CONV
OFF_Q = OFF_CONV_GATE + D_CONV
OFF_ATTN_GATE = OFF_Q + D_XATTN

LRU_HALO = 8
CONV_HALO = 32

PROMPT_TILE = 512
PROMPT_SEG = 64
PROMPT_PITCH = 68
SAMPLE_PITCH = 40
SAMPLE_GROUP = 16
ATTN_ROWS = 128
CONV_ROWS = 32
PROJ_ROWS = 256

VMEM_LIMIT_BYTES = 56 * 1024 * 1024
HALF_LOG2_E = 0.7213475204444817
SCORE_SCALE = XHEAD_DIM ** -0.5 * 2.0 * HALF_LOG2_E


def _silu_of_twice(hx):
    return hx + hx * jnp.tanh(hx)


def _gated_of_twice(ha, hb):
    return ha + ha * jnp.tanh(hb)


def _w_in_col_scale():
    col = lax.broadcasted_iota(jnp.int32, (1, D_IN), 1)
    plain = (col < OFF_LRU_GATE) | ((col >= OFF_Q) & (col < OFF_ATTN_GATE))
    return jnp.where(plain, 1.0, 0.5)


def _sqrt_nonneg(x):
    return jnp.exp2(jnp.log(x) * HALF_LOG2_E)


def _rms_scaled(x):
    ms = jnp.mean(x * x, axis=-1, keepdims=True)
    return x * lax.rsqrt(ms + EPS)


def _rmsnorm(x, g):
    return _rms_scaled(x) * g


def _as_column(row):
    n = row.shape[1] // LANES
    on_diag = (lax.broadcasted_iota(jnp.int32, (LANES, LANES), 0)
               == lax.broadcasted_iota(jnp.int32, (LANES, LANES), 1))
    cols = [jnp.sum(jnp.where(on_diag, row[:, j * LANES:(j + 1) * LANES], 0.0), axis=1, keepdims=True)
            for j in range(n)]
    return jnp.concatenate(cols, axis=0)


def _log_sigmoid(x):
    y = -x
    return -(jnp.maximum(y, 0.0) + jnp.log1p(jnp.exp(-jnp.abs(y))))


def _bf16(x):
    return x.astype(jnp.bfloat16)


def _dot(a, b):
    return jnp.dot(a, b, preferred_element_type=jnp.float32)


def _dot_nt(a, b):
    return lax.dot_general(a, b, (((1,), (1,)), ((), ())), preferred_element_type=jnp.float32)


def _unrolled(n, body, init):
    carry = init
    for i in range(n):
        carry = body(i, carry)
    return carry


def _aligned(x, m):
    return x if isinstance(x, int) else pl.multiple_of(x, m)


def _head_of(index):
    return lax.shift_right_logical(index, XHEAD_DIM.bit_length() - 1)


def _lane_head_masks():
    head = _head_of(lax.broadcasted_iota(jnp.int32, (1, D_XATTN), 1))
    return [(head == h).astype(jnp.float32) for h in range(N_XHEADS)]


def _row_head_masks():
    head = _head_of(lax.broadcasted_iota(jnp.int32, (D_XATTN, 1), 0))
    return [(head == h).astype(jnp.float32) for h in range(N_XHEADS)]


def _head_stack_t(kt, vt, kst_ref, vst_ref):
    scale = SCORE_SCALE
    for h, m in enumerate(_row_head_masks()):
        kst_ref[:, h * N_MEM:(h + 1) * N_MEM] = _bf16(kt * (m * scale))
        vst_ref[:, h * N_MEM:(h + 1) * N_MEM] = _bf16(vt * m)


def _zero_row_after(x):
    bits = lax.bitcast_convert_type(x, jnp.uint32)
    zero = lax.shift_right_logical(lax.shift_right_logical(bits, jnp.uint32(16)), jnp.uint32(16))
    return lax.bitcast_convert_type(zero, jnp.float32)[0:1, :]


def _softmax_heads(s):
    ps = []
    for h in range(N_XHEADS):
        sh = s[:, h * N_MEM:(h + 1) * N_MEM]
        e = jnp.exp2(sh - jnp.max(sh, axis=-1, keepdims=True))
        inv = 1.0 / jnp.sum(e, axis=-1, keepdims=True)
        ps.append(_bf16(e * inv))
    return jnp.concatenate(ps, axis=1)


def _lru_gates(xl_bf, wg_ref, pre_s):
    pre_s[:, 0:D_LRU] = _dot(xl_bf[:, 0:GATE_HALF], wg_ref[0])
    pre_s[:, D_LRU:2 * D_LRU] = _dot(xl_bf[:, GATE_HALF:D_LRU], wg_ref[1])


def _pre_cols(l):
    half_slabs = N_LRU_SLABS // 2
    base = (l // half_slabs) * D_LRU + (l % half_slabs) * LANES
    return base, base + GATE_HALF


def _lru_coeffs(hpre_a, hpre_x, hx, hb_a, hb_x, hc_ls):
    t_r = jnp.tanh(hpre_a + hb_a)
    t_i = jnp.tanh(hpre_x + hb_x)
    log_a = hc_ls + hc_ls * t_r
    a = jnp.exp(log_a)
    one_minus_a2 = -jnp.tanh(log_a) * (a * a + 1.0)
    u = _sqrt_nonneg(one_minus_a2) * (hx + hx * t_i)
    return a, u


def _conv_taps(src_s, l, w_row, row0, nrows, width, halo):
    acc = None
    for k in range(width):
        x_k = src_s[l, pl.ds(row0 + (halo - (width - 1) + k), nrows), :]
        acc = w_row(k) * x_k if acc is None else acc + w_row(k) * x_k
    return acc


def _layernorm_silu_slabs(ys, g_ref, b_ref):
    tot = ys[0]
    for y in ys[1:]:
        tot = tot + y
    mu = jnp.sum(tot, axis=-1, keepdims=True) * (1.0 / D_CONV)
    cs = [y - mu for y in ys]
    sq = cs[0] * cs[0]
    for c in cs[1:]:
        sq = sq + c * c
    var = jnp.sum(sq, axis=-1, keepdims=True) * (1.0 / D_CONV)
    inv = lax.rsqrt(var + EPS)
    outs = []
    for l, c in enumerate(cs):
        lo = l * LANES
        half_g = 0.5 * g_ref[:, lo:lo + LANES]
        half_b = 0.5 * b_ref[:, lo:lo + LANES]
        outs.append(_silu_of_twice(c * inv * half_g + half_b))
    return outs


def _gate_block_diag(w_blocks):
    n = (LRU_BLOCKS // 2) * LRU_BLOCK
    stacked = _bf16(w_blocks.reshape(n, LRU_BLOCK))
    src = lax.broadcasted_iota(jnp.int32, (LRU_BLOCK, n), 0)
    dst = lax.broadcasted_iota(jnp.int32, (LRU_BLOCK, n), 1)
    spread = _bf16(((dst & (LRU_BLOCK - 1)) == src).astype(jnp.float32))
    rows = _head_of(lax.broadcasted_iota(jnp.int32, (n, n), 0))
    cols = _head_of(lax.broadcasted_iota(jnp.int32, (n, n), 1))
    return _bf16(_dot(stacked, spread) * (rows == cols).astype(jnp.float32))


def _memkv_kernel(mem_ref, g_ref, wk_ref, wv_ref, kt_ref, vt_ref):
    m = _bf16(_rmsnorm(mem_ref[0], g_ref[...]))
    kt_ref[0] = _dot(m, _bf16(wk_ref[...])).T
    vt_ref[0] = _dot(m, _bf16(wv_ref[...])).T


def _memkv(mem, g_mem, w_k, w_v):
    bsz = mem.shape[0]
    return pl.pallas_call(
        _memkv_kernel,
        grid=(bsz,),
        in_specs=[
            pl.BlockSpec((1, N_MEM, D_MODEL), lambda b: (b, 0, 0)),
            pl.BlockSpec((1, D_MODEL), lambda b: (0, 0)),
            pl.BlockSpec((None, D_MODEL, D_XATTN), lambda b: (0, 0, 0)),
            pl.BlockSpec((None, D_MODEL, D_XATTN), lambda b: (0, 0, 0)),
        ],
        out_specs=[
            pl.BlockSpec((1, D_XATTN, N_MEM), lambda b: (b, 0, 0)),
            pl.BlockSpec((1, D_XATTN, N_MEM), lambda b: (b, 0, 0)),
        ],
        out_shape=[jax.ShapeDtypeStruct((bsz, D_XATTN, N_MEM), jnp.float32)] * 2,
        compiler_params=pltpu.CompilerParams(dimension_semantics=("arbitrary",)),
        name="memkv",
    )(mem, g_mem, w_k, w_v)


def _prompt_kernel(x_ref, kt_ref, vt_ref, g_norm_ref, w_in_ref, w_lc_ref, b_lc_ref, wga_ref, wgx_ref,
                   b_ga_ref, b_gx_ref, lam_ref, w_dw_ref, b_dw_ref, ln_g_ref, ln_b_ref,
                   w_pw_ref, b_pw_ref, w_out_ref, g_final_ref,
                   y_ref, h_out_ref, lb_out_ref, cb_out_ref, win_bf, wout_bf, wpw_bf, wg_bf,
                   z_s, xc_s, xl_s, pre_s, a_s, u_s, uc_s, ln_s, mix_s, hin_s, kst_s, vst_s, sc_s, p_s):
    b = pl.program_id(0)
    t = pl.program_id(1)
    n_t = pl.num_programs(1)
    tile = PROMPT_TILE
    seg = PROMPT_SEG
    n_seg = tile // seg
    pitch = PROMPT_PITCH

    @pl.when(jnp.logical_and(b == 0, t == 0))
    def _():
        win_bf[...] = _bf16(w_in_ref[...] * _as_column(g_norm_ref[...]) * _w_in_col_scale())
        wout_bf[...] = _bf16(w_out_ref[...])
        wpw_bf[...] = _bf16(w_pw_ref[...])
        half = LRU_BLOCKS // 2
        for h in range(2):
            wg_bf[h, :, 0:GATE_HALF] = _gate_block_diag(wga_ref[h * half:(h + 1) * half])
            wg_bf[h, :, GATE_HALF:D_LRU] = _gate_block_diag(wgx_ref[h * half:(h + 1) * half])

    @pl.when(t == 0)
    def _():
        xc_s[:, 0:LRU_HALO, :] = jnp.zeros((N_LRU_SLABS, LRU_HALO, LANES), jnp.float32)
        uc_s[:, 0:CONV_HALO, :] = jnp.zeros((N_CONV_SLABS, CONV_HALO, LANES), jnp.float32)
        hin_s[:, 0:1, :] = jnp.zeros((N_LRU_SLABS, 1, LANES), jnp.float32)
        _head_stack_t(kt_ref[0], vt_ref[0], kst_s, vst_s)

    for r0 in range(0, tile, PROJ_ROWS):
        z_s[r0:r0 + PROJ_ROWS, :] = _dot(_bf16(_rms_scaled(x_ref[0, r0:r0 + PROJ_ROWS, :])), win_bf[...])

    for l in range(N_LRU_SLABS):
        lo = l * LANES
        xc_s[l, LRU_HALO:LRU_HALO + tile, :] = z_s[:, OFF_LRU_IN + lo:OFF_LRU_IN + lo + LANES]
    for l in range(N_LRU_SLABS):
        lo = l * LANES
        w_row = lambda k, lo=lo: 0.5 * w_lc_ref[k:k + 1, lo:lo + LANES]
        xl = (_conv_taps(xc_s, l, w_row, 0, tile, LRU_CONV, LRU_HALO)
              + 0.5 * b_lc_ref[:, lo:lo + LANES])
        xl_s[:, lo:lo + LANES] = xl
    _lru_gates(_bf16(xl_s[...]), wg_bf, pre_s)

    hc_ls = (0.5 * LRU_C) * _log_sigmoid(lam_ref[...])
    hb_ga = 0.5 * b_ga_ref[...]
    hb_gx = 0.5 * b_gx_ref[...]

    def coeff_body(c, carry):
        r0 = _aligned(c * seg, seg)
        p0 = _aligned(c * pitch, SUBLANES)
        for l in range(N_LRU_SLABS):
            lo = l * LANES
            ca, cx = _pre_cols(l)
            a, u = _lru_coeffs(pre_s[pl.ds(r0, seg), ca:ca + LANES],
                               pre_s[pl.ds(r0, seg), cx:cx + LANES],
                               xl_s[pl.ds(r0, seg), lo:lo + LANES],
                               hb_ga[:, lo:lo + LANES], hb_gx[:, lo:lo + LANES],
                               hc_ls[:, lo:lo + LANES])
            a_s[l, pl.ds(p0, seg), :] = a
            u_s[l, pl.ds(p0, seg), :] = u
        return carry

    _unrolled(n_seg, coeff_body, 0)

    def tot_body(j, carry):
        new = []
        for l in range(N_LRU_SLABS):
            a_tot, u_tot = carry[l]
            a_j = a_s[l, pl.ds(j, n_seg, stride=pitch), :]
            u_j = u_s[l, pl.ds(j, n_seg, stride=pitch), :]
            new.append((a_j * a_tot, a_j * u_tot + u_j))
        return tuple(new)

    init = tuple((jnp.ones((n_seg, LANES), jnp.float32), jnp.zeros((n_seg, LANES), jnp.float32))
                 for _ in range(N_LRU_SLABS))
    totals = _unrolled(seg, tot_body, init)

    h_in = []
    for l in range(N_LRU_SLABS):
        a_tot, u_tot = totals[l]
        h = hin_s[l, 0:1, :]
        for c in range(n_seg):
            h = a_tot[c:c + 1, :] * h + u_tot[c:c + 1, :]
            hin_s[l, c + 1:c + 2, :] = h
        h_in.append(hin_s[l, 0:n_seg, :])
        hin_s[l, 0:1, :] = h

    def scan_body(j, carry):
        new = []
        for l in range(N_LRU_SLABS):
            a_j = a_s[l, pl.ds(j, n_seg, stride=pitch), :]
            u_j = u_s[l, pl.ds(j, n_seg, stride=pitch), :]
            h = a_j * carry[l] + u_j
            u_s[l, pl.ds(j, n_seg, stride=pitch), :] = h
            new.append(h)
        return tuple(new)

    scores = _dot(_bf16(z_s[:, OFF_Q:OFF_Q + D_XATTN]), kst_s[...])
    sc_s[...] = scores
    after_scores = _zero_row_after(scores[tile - SUBLANES:tile, N_XHEADS * N_MEM - LANES:])
    _unrolled(seg, scan_body, tuple(h + after_scores for h in h_in))

    for l in range(N_CONV_SLABS):
        lo = l * LANES
        uc_s[l, CONV_HALO:CONV_HALO + tile, :] = _gated_of_twice(
            z_s[:, OFF_GLU_A + lo:OFF_GLU_A + lo + LANES],
            z_s[:, OFF_GLU_B + lo:OFF_GLU_B + lo + LANES])

    def seg_body(c, carry):
        r0 = _aligned(c * seg, seg)
        p0 = _aligned(c * pitch, SUBLANES)
        for l in range(N_LRU_SLABS):
            lo = l * LANES
            gate = z_s[pl.ds(r0, seg), OFF_LRU_GATE + lo:OFF_LRU_GATE + lo + LANES]
            mix_s[pl.ds(r0, seg), lo:lo + LANES] = _bf16(u_s[l, pl.ds(p0, seg), :] * _silu_of_twice(gate))
        for piece in range(seg // CONV_ROWS):
            rr = r0 + piece * CONV_ROWS
            ys = []
            for l in range(N_CONV_SLABS):
                lo = l * LANES
                w_row = lambda k, lo=lo: w_dw_ref[k, :, lo:lo + LANES]
                ys.append(_conv_taps(uc_s, l, w_row, rr, CONV_ROWS, CONV_W, CONV_HALO)
                          + b_dw_ref[:, lo:lo + LANES])
            outs = _layernorm_silu_slabs(ys, ln_g_ref, ln_b_ref)
            for l in range(N_CONV_SLABS):
                lo = l * LANES
                ln_s[pl.ds(rr, CONV_ROWS), lo:lo + LANES] = _bf16(outs[l])
        return carry

    _unrolled(n_seg, seg_body, 0)

    yc = (_dot(ln_s[...], wpw_bf[...]) + b_pw_ref[...]) * _silu_of_twice(
        z_s[:, OFF_CONV_GATE:OFF_CONV_GATE + D_CONV])
    mix_s[:, D_LRU:D_LRU + D_CONV] = _bf16(yc)

    def attn_body(c, carry):
        r0 = _aligned(c * ATTN_ROWS, ATTN_ROWS)
        p_s[pl.ds(r0, ATTN_ROWS), :] = _softmax_heads(sc_s[pl.ds(r0, ATTN_ROWS), :])
        return carry

    _unrolled(tile // ATTN_ROWS, attn_body, 0)
    o = _dot_nt(p_s[...], vst_s[...])
    mix_s[:, D_LRU + D_CONV:D_MIX] = _bf16(
        o * _silu_of_twice(z_s[:, OFF_ATTN_GATE:OFF_ATTN_GATE + D_XATTN]))

    y = x_ref[0] + _dot(mix_s[...], wout_bf[...])
    y_ref[0] = _rmsnorm(y, g_final_ref[...])

    for l in range(N_LRU_SLABS):
        xc_s[l, LRU_HALO - (LRU_CONV - 1):LRU_HALO, :] = (
            xc_s[l, LRU_HALO + tile - (LRU_CONV - 1):LRU_HALO + tile, :])
    for l in range(N_CONV_SLABS):
        uc_s[l, CONV_HALO - (CONV_W - 1):CONV_HALO, :] = (
            uc_s[l, CONV_HALO + tile - (CONV_W - 1):CONV_HALO + tile, :])

    def write_state(bb):
        for l in range(N_LRU_SLABS):
            lo = l * LANES
            h_out_ref[bb:bb + 1, lo:lo + LANES] = hin_s[l, 0:1, :]
            for k in range(LRU_CONV - 1):
                r = LRU_HALO - (LRU_CONV - 1) + k
                lb_out_ref[k, bb:bb + 1, lo:lo + LANES] = xc_s[l, r:r + 1, :]
        for l in range(N_CONV_SLABS):
            lo = l * LANES
            for k in range(CONV_W - 1):
                r = CONV_HALO - (CONV_W - 1) + k
                cb_out_ref[k, bb:bb + 1, lo:lo + LANES] = uc_s[l, r:r + 1, :]

    for bb in range(h_out_ref.shape[0]):
        pl.when(jnp.logical_and(t == n_t - 1, b == bb))(functools.partial(write_state, bb))


def _whole(shape):
    nd = len(shape)
    return pl.BlockSpec(shape, lambda *_: (0,) * nd, pipeline_mode=pl.Buffered(1))


def _prompt(x, kt, vt, w):
    bsz, seq, _ = x.shape
    tile = PROMPT_TILE
    n_t = seq // tile
    n_seg = tile // PROMPT_SEG
    weights = [w["g_norm"], w["w_in"], w["w_lc"], w["b_lc"], w["w_ga"], w["w_gx"], w["b_ga"], w["b_gx"],
               w["lam"], w["w_dw"], w["b_dw"], w["ln_g"], w["ln_b"], w["w_pw"], w["b_pw"],
               w["w_out"], w["g_final"]]
    weight_specs = [
        _whole((1, D_MODEL)), _whole((None, D_MODEL, D_IN)), _whole((None, LRU_CONV, D_LRU)),
        _whole((1, D_LRU)), _whole((None, LRU_BLOCKS, LRU_BLOCK, LRU_BLOCK)),
        _whole((None, LRU_BLOCKS, LRU_BLOCK, LRU_BLOCK)), _whole((1, D_LRU)), _whole((1, D_LRU)),
        _whole((1, D_LRU)), _whole((CONV_W, 1, D_CONV)), _whole((1, D_CONV)), _whole((1, D_CONV)),
        _whole((1, D_CONV)), _whole((None, D_CONV, D_CONV)), _whole((1, D_CONV)),
        _whole((None, D_MIX, D_MODEL)), _whole((1, D_MODEL)),
    ]
    in_specs = [
        pl.BlockSpec((1, tile, D_MODEL), lambda b, t: (b, t, 0)),
        pl.BlockSpec((1, D_XATTN, N_MEM), lambda b, t: (b, 0, 0)),
        pl.BlockSpec((1, D_XATTN, N_MEM), lambda b, t: (b, 0, 0)),
    ] + weight_specs
    out_specs = [
        pl.BlockSpec((1, tile, D_MODEL), lambda b, t: (b, t, 0)),
        _whole((bsz, D_LRU)),
        _whole((LRU_CONV - 1, bsz, D_LRU)),
        _whole((CONV_W - 1, bsz, D_CONV)),
        _whole((D_MODEL, D_IN)), _whole((D_MIX, D_MODEL)), _whole((D_CONV, D_CONV)),
        _whole((2, GATE_HALF, D_LRU)),
    ]
    out_shape = [
        jax.ShapeDtypeStruct((bsz, seq, D_MODEL), jnp.float32),
        jax.ShapeDtypeStruct((bsz, D_LRU), jnp.float32),
        jax.ShapeDtypeStruct((LRU_CONV - 1, bsz, D_LRU), jnp.float32),
        jax.ShapeDtypeStruct((CONV_W - 1, bsz, D_CONV), jnp.float32),
        jax.ShapeDtypeStruct((D_MODEL, D_IN), jnp.bfloat16),
        jax.ShapeDtypeStruct((D_MIX, D_MODEL), jnp.bfloat16),
        jax.ShapeDtypeStruct((D_CONV, D_CONV), jnp.bfloat16),
        jax.ShapeDtypeStruct((2, GATE_HALF, D_LRU), jnp.bfloat16),
    ]
    scratch = [
        pltpu.VMEM((tile, D_IN), jnp.float32),
        pltpu.VMEM((N_LRU_SLABS, LRU_HALO + tile, LANES), jnp.float32),
        pltpu.VMEM((tile, D_LRU), jnp.float32),
        pltpu.VMEM((tile, 2 * D_LRU), jnp.float32),
        pltpu.VMEM((N_LRU_SLABS, n_seg * PROMPT_PITCH, LANES), jnp.float32),
        pltpu.VMEM((N_LRU_SLABS, n_seg * PROMPT_PITCH, LANES), jnp.float32),
        pltpu.VMEM((N_CONV_SLABS, CONV_HALO + tile, LANES), jnp.float32),
        pltpu.VMEM((tile, D_CONV), jnp.bfloat16),
        pltpu.VMEM((tile, D_MIX), jnp.bfloat16),
        pltpu.VMEM((N_LRU_SLABS, 2 * SUBLANES, LANES), jnp.float32),
        pltpu.VMEM((D_XATTN, N_XHEADS * N_MEM), jnp.bfloat16),
        pltpu.VMEM((D_XATTN, N_XHEADS * N_MEM), jnp.bfloat16),
        pltpu.VMEM((tile, N_XHEADS * N_MEM), jnp.float32),
        pltpu.VMEM((tile, N_XHEADS * N_MEM), jnp.bfloat16),
    ]
    return pl.pallas_call(
        _prompt_kernel,
        grid=(bsz, n_t),
        in_specs=in_specs,
        out_specs=out_specs,
        out_shape=out_shape,
        scratch_shapes=scratch,
        compiler_params=pltpu.CompilerParams(
            dimension_semantics=("arbitrary", "arbitrary"),
            vmem_limit_bytes=VMEM_LIMIT_BYTES),
        name="prompt_layer",
    )(x, kt, vt, *weights)


def _sample_kernel(nb, steps,
                   x_ref, h0_ref, lb_ref, cb_ref, kt_ref, vt_ref,
                   win_bf, w_lc_ref, b_lc_ref, wg_bf,
                   b_ga_ref, b_gx_ref, lam_ref, w_dw_ref, b_dw_ref, ln_g_ref, ln_b_ref,
                   wpw_bf, b_pw_ref, wout_bf, g_final_ref,
                   y_ref, h_out_ref, lb_out_ref, cb_out_ref,
                   z_s, xc_s, xl_s, pre_s, a_s, u_s, uc_s, ln_s, mix_s, sc_s, p_s):
    rows = nb * steps
    pitch = SAMPLE_PITCH
    cpitch = CONV_HALO + steps
    prow = nb * pitch

    x = x_ref[...].reshape(rows, D_MODEL)
    z_s[...] = _dot(_bf16(_rms_scaled(x)), win_bf[...])

    for b in range(nb):
        for l in range(N_LRU_SLABS):
            lo = l * LANES
            zcol = OFF_LRU_IN + lo
            xc_s[l, b * pitch:b * pitch + LRU_HALO, :] = jnp.zeros((LRU_HALO, LANES), jnp.float32)
            for k in range(LRU_CONV - 1):
                r = b * pitch + LRU_HALO - (LRU_CONV - 1) + k
                xc_s[l, r:r + 1, :] = lb_ref[k, b:b + 1, lo:lo + LANES]
                rn = (b + 1) * steps - (LRU_CONV - 1) + k
                lb_out_ref[k, b:b + 1, lo:lo + LANES] = z_s[rn:rn + 1, zcol:zcol + LANES]
            xc_s[l, b * pitch + LRU_HALO:(b + 1) * pitch, :] = (
                z_s[b * steps:(b + 1) * steps, zcol:zcol + LANES])
    n_conv = prow - LRU_HALO
    for l in range(N_LRU_SLABS):
        lo = l * LANES
        w_row = lambda k, lo=lo: 0.5 * w_lc_ref[k:k + 1, lo:lo + LANES]
        xl = (_conv_taps(xc_s, l, w_row, 0, n_conv, LRU_CONV, LRU_HALO)
              + 0.5 * b_lc_ref[:, lo:lo + LANES])
        xl_s[0:n_conv, lo:lo + LANES] = xl
    xl_s[n_conv:prow, :] = jnp.zeros((LRU_HALO, D_LRU), jnp.float32)
    _lru_gates(_bf16(xl_s[...]), wg_bf, pre_s)

    hc_ls = (0.5 * LRU_C) * _log_sigmoid(lam_ref[...])
    hb_ga = 0.5 * b_ga_ref[...]
    hb_gx = 0.5 * b_gx_ref[...]
    for l in range(N_LRU_SLABS):
        lo = l * LANES
        ca, cx = _pre_cols(l)
        a, u = _lru_coeffs(pre_s[:, ca:ca + LANES], pre_s[:, cx:cx + LANES], xl_s[:, lo:lo + LANES],
                           hb_ga[:, lo:lo + LANES], hb_gx[:, lo:lo + LANES],
                           hc_ls[:, lo:lo + LANES])
        a_s[l, :, :] = a
        u_s[l, :, :] = u

    n_grp = nb // SUBLANES

    def scan_body(j, carry):
        new = []
        for g in range(n_grp):
            for l in range(N_LRU_SLABS):
                base = g * SUBLANES * pitch
                a_j = a_s[l, pl.ds(base + j, SUBLANES, stride=pitch), :]
                u_j = u_s[l, pl.ds(base + j, SUBLANES, stride=pitch), :]
                h = a_j * carry[g * N_LRU_SLABS + l] + u_j
                u_s[l, pl.ds(base + j, SUBLANES, stride=pitch), :] = h
                new.append(h)
        return tuple(new)

    h0 = tuple(h0_ref[g * SUBLANES:(g + 1) * SUBLANES, l * LANES:(l + 1) * LANES]
               for g in range(n_grp) for l in range(N_LRU_SLABS))
    h_fin = _unrolled(steps, scan_body, h0)
    for g in range(n_grp):
        for l in range(N_LRU_SLABS):
            h_out_ref[g * SUBLANES:(g + 1) * SUBLANES, l * LANES:(l + 1) * LANES] = (
                h_fin[g * N_LRU_SLABS + l])

    for b in range(nb):
        for l in range(N_LRU_SLABS):
            lo = l * LANES
            gate = z_s[b * steps:(b + 1) * steps, OFF_LRU_GATE + lo:OFF_LRU_GATE + lo + LANES]
            mix_s[b * steps:(b + 1) * steps, lo:lo + LANES] = _bf16(
                u_s[l, b * pitch:b * pitch + steps, :] * _silu_of_twice(gate))

    for b in range(nb):
        for l in range(N_CONV_SLABS):
            lo = l * LANES
            uc = _gated_of_twice(
                z_s[b * steps:(b + 1) * steps, OFF_GLU_A + lo:OFF_GLU_A + lo + LANES],
                z_s[b * steps:(b + 1) * steps, OFF_GLU_B + lo:OFF_GLU_B + lo + LANES])
            uc_s[l, b * cpitch + CONV_HALO:(b + 1) * cpitch, :] = uc
            for k in range(CONV_W - 1):
                r = b * cpitch + CONV_HALO - (CONV_W - 1) + k
                uc_s[l, r:r + 1, :] = cb_ref[k, b:b + 1, lo:lo + LANES]
                rn = (b + 1) * cpitch - (CONV_W - 1) + k
                cb_out_ref[k, b:b + 1, lo:lo + LANES] = uc_s[l, rn:rn + 1, :]

    for b in range(nb):
        r0 = b * cpitch
        o0 = b * steps
        ys = []
        for l in range(N_CONV_SLABS):
            lo = l * LANES
            w_row = lambda k, lo=lo: w_dw_ref[k, :, lo:lo + LANES]
            ys.append(_conv_taps(uc_s, l, w_row, r0, steps, CONV_W, CONV_HALO)
                      + b_dw_ref[:, lo:lo + LANES])
        outs = _layernorm_silu_slabs(ys, ln_g_ref, ln_b_ref)
        for l in range(N_CONV_SLABS):
            lo = l * LANES
            ln_s[o0:o0 + steps, lo:lo + LANES] = _bf16(outs[l])

    yc = (_dot(ln_s[...], wpw_bf[...]) + b_pw_ref[...]) * _silu_of_twice(
        z_s[:, OFF_CONV_GATE:OFF_CONV_GATE + D_CONV])
    mix_s[:, D_LRU:D_LRU + D_CONV] = _bf16(yc)

    masks = _lane_head_masks()
    scale = SCORE_SCALE
    qrows = N_XHEADS * steps
    for b in range(nb):
        q = z_s[b * steps:(b + 1) * steps, OFF_Q:OFF_Q + D_XATTN]
        qm = jnp.concatenate([_bf16(q * (m * scale)) for m in masks], axis=0)
        sc_s[b * qrows:(b + 1) * qrows, :] = _dot(qm, _bf16(kt_ref[b]))
    s = sc_s[...]
    e = jnp.exp2(s - jnp.max(s, axis=-1, keepdims=True))
    p_s[...] = _bf16(e * (1.0 / jnp.sum(e, axis=-1, keepdims=True)))
    for b in range(nb):
        o_all = _dot_nt(p_s[b * qrows:(b + 1) * qrows, :], _bf16(vt_ref[b]))
        o = o_all[0:steps] * masks[0]
        for h in range(1, N_XHEADS):
            o = o + o_all[h * steps:(h + 1) * steps] * masks[h]
        gate = z_s[b * steps:(b + 1) * steps, OFF_ATTN_GATE:OFF_ATTN_GATE + D_XATTN]
        mix_s[b * steps:(b + 1) * steps, D_LRU + D_CONV:D_MIX] = _bf16(o * _silu_of_twice(gate))

    y = x + _dot(mix_s[...], wout_bf[...])
    y_ref[...] = _rmsnorm(y, g_final_ref[...]).reshape(nb, steps, D_MODEL)


def _sample(x, h0, lb, cb, kt, vt, w, win_bf, wout_bf, wpw_bf, wg_bf):
    bsz, steps, _ = x.shape
    nb = SAMPLE_GROUP
    assert bsz % nb == 0 and nb % SUBLANES == 0
    assert steps + LRU_HALO == SAMPLE_PITCH and steps >= CONV_W - 1
    rows = nb * steps
    prow = nb * SAMPLE_PITCH
    weights = [win_bf, w["w_lc"], w["b_lc"], wg_bf, w["b_ga"], w["b_gx"],
               w["lam"], w["w_dw"], w["b_dw"], w["ln_g"], w["ln_b"], wpw_bf, w["b_pw"],
               wout_bf, w["g_final"]]
    weight_specs = [
        _whole((D_MODEL, D_IN)), _whole((None, LRU_CONV, D_LRU)),
        _whole((1, D_LRU)), _whole((2, GATE_HALF, D_LRU)), _whole((1, D_LRU)), _whole((1, D_LRU)),
        _whole((1, D_LRU)), _whole((CONV_W, 1, D_CONV)), _whole((1, D_CONV)), _whole((1, D_CONV)),
        _whole((1, D_CONV)), _whole((D_CONV, D_CONV)), _whole((1, D_CONV)),
        _whole((D_MIX, D_MODEL)), _whole((1, D_MODEL)),
    ]
    in_specs = [
        pl.BlockSpec((nb, steps, D_MODEL), lambda g: (g, 0, 0)),
        pl.BlockSpec((nb, D_LRU), lambda g: (g, 0)),
        pl.BlockSpec((LRU_CONV - 1, nb, D_LRU), lambda g: (0, g, 0)),
        pl.BlockSpec((CONV_W - 1, nb, D_CONV), lambda g: (0, g, 0)),
        pl.BlockSpec((nb, D_XATTN, N_MEM), lambda g: (g, 0, 0)),
        pl.BlockSpec((nb, D_XATTN, N_MEM), lambda g: (g, 0, 0)),
    ] + weight_specs
    out_specs = [
        pl.BlockSpec((nb, steps, D_MODEL), lambda g: (g, 0, 0)),
        pl.BlockSpec((nb, D_LRU), lambda g: (g, 0)),
        pl.BlockSpec((LRU_CONV - 1, nb, D_LRU), lambda g: (0, g, 0)),
        pl.BlockSpec((CONV_W - 1, nb, D_CONV), lambda g: (0, g, 0)),
    ]
    out_shape = [
        jax.ShapeDtypeStruct((bsz, steps, D_MODEL), jnp.float32),
        jax.ShapeDtypeStruct((bsz, D_LRU), jnp.float32),
        jax.ShapeDtypeStruct((LRU_CONV - 1, bsz, D_LRU), jnp.float32),
        jax.ShapeDtypeStruct((CONV_W - 1, bsz, D_CONV), jnp.float32),
    ]
    scratch = [
        pltpu.VMEM((rows, D_IN), jnp.float32),
        pltpu.VMEM((N_LRU_SLABS, prow, LANES), jnp.float32),
        pltpu.VMEM((prow, D_LRU), jnp.float32),
        pltpu.VMEM((prow, 2 * D_LRU), jnp.float32),
        pltpu.VMEM((N_LRU_SLABS, prow, LANES), jnp.float32),
        pltpu.VMEM((N_LRU_SLABS, prow, LANES), jnp.float32),
        pltpu.VMEM((N_CONV_SLABS, nb * (CONV_HALO + steps), LANES), jnp.float32),
        pltpu.VMEM((rows, D_CONV), jnp.bfloat16),
        pltpu.VMEM((rows, D_MIX), jnp.bfloat16),
        pltpu.VMEM((N_XHEADS * rows, N_MEM), jnp.float32),
        pltpu.VMEM((N_XHEADS * rows, N_MEM), jnp.bfloat16),
    ]
    return pl.pallas_call(
        functools.partial(_sample_kernel, nb, steps),
        grid=(bsz // nb,),
        in_specs=in_specs,
        out_specs=out_specs,
        out_shape=out_shape,
        scratch_shapes=scratch,
        compiler_params=pltpu.CompilerParams(
            dimension_semantics=("arbitrary",),
            vmem_limit_bytes=VMEM_LIMIT_BYTES),
        name="sample_layer",
    )(x, h0, lb, cb, kt, vt, *weights)


def _time_major(state):
    return jnp.transpose(state, (1, 0, 2))


def _memory_t(mem):
    bsz = mem.shape[0]
    return jnp.transpose(mem, (0, 2, 3, 1)).reshape(bsz, D_XATTN, N_MEM)


def _memory_from_t(mem_t):
    bsz = mem_t.shape[0]
    return jnp.transpose(mem_t.reshape(bsz, N_XHEADS, XHEAD_DIM, N_MEM), (0, 3, 1, 2))


def kernel(x_prompt, x_sample, state_lru_h, state_lru_conv, state_conv, cache_mem_k, cache_mem_v,
           mem_prompt, g_norm, w_in, w_lru_conv, b_lru_conv, w_gate_a, b_gate_a, w_gate_x, b_gate_x,
           lru_lambda, w_dw, b_dw, ln_g, ln_b, w_pw, b_pw, g_mem, w_mem_k, w_mem_v, w_out, g_final):
    depth = g_norm.shape[0]
    assert depth == 1
    w = {
        "g_norm": g_norm, "w_in": w_in, "w_lc": w_lru_conv, "b_lc": b_lru_conv,
        "w_ga": w_gate_a, "w_gx": w_gate_x, "b_ga": b_gate_a, "b_gx": b_gate_x, "lam": lru_lambda,
        "w_dw": jnp.transpose(w_dw, (1, 0, 2)), "b_dw": b_dw, "ln_g": ln_g, "ln_b": ln_b,
        "w_pw": w_pw, "b_pw": b_pw, "w_out": w_out, "g_final": g_final.reshape(1, D_MODEL),
    }
    mkt, mvt = _memkv(mem_prompt, g_mem, w_mem_k, w_mem_v)

    y_prompt, ph, plb, pcb, win_bf, wout_bf, wpw_bf, wg_bf = _prompt(x_prompt, mkt, mvt, w)

    y_sample, sh, slb, scb = _sample(
        x_sample, state_lru_h[0], _time_major(state_lru_conv[0]), _time_major(state_conv[0]),
        _memory_t(cache_mem_k[0]), _memory_t(cache_mem_v[0]), w, win_bf, wout_bf, wpw_bf, wg_bf)

    return (y_prompt, y_sample,
            ph[None], _time_major(plb)[None], _time_major(pcb)[None],
            _memory_from_t(mkt)[None], _memory_from_t(mvt)[None],
            sh[None], _time_major(slb)[None], _time_major(scb)[None])
```

```python
import functools

import jax
import jax.numpy as jnp
from jax import lax
from jax.experimental import pallas as pl
from jax.experimental.pallas import tpu as pltpu

D_MODEL = 1024
N_MEM = 256
D_LRU = 512
LRU_BLOCKS = 8
LRU_BLOCK = D_LRU // LRU_BLOCKS
LRU_CONV = 4
LRU_C = 8.0
D_CONV = 256
CONV_W = 31
N_XHEADS = 4
XHEAD_DIM = 64
D_XATTN = N_XHEADS * XHEAD_DIM
D_MIX = D_LRU + D_CONV + D_XATTN
D_IN = 2 * D_LRU + 3 * D_CONV + 2 * D_XATTN
EPS = 1e-6

LANES = 128
SUBLANES = 8
N_LRU_SLABS = D_LRU // LANES
N_CONV_SLABS = D_CONV // LANES
GATE_HALF = D_LRU // 2

OFF_LRU_IN = 0
OFF_LRU_GATE = D_LRU
OFF_GLU_A = 2 * D_LRU
OFF_GLU_B = OFF_GLU_A + D_CONV
OFF_CONV_GATE = OFF_GLU_B + D_CONV
OFF_Q = OFF_CONV_GATE + D_CONV
OFF_ATTN_GATE = OFF_Q + D_XATTN

LRU_HALO = 8
CONV_HALO = 32

PROMPT_TILE = 512
PROMPT_SEG = 64
PROMPT_PITCH = 68
SAMPLE_PITCH = 40
SAMPLE_GROUP = 16
ATTN_ROWS = 128
CONV_ROWS = 32
PROJ_ROWS = 256

VMEM_LIMIT_BYTES = 56 * 1024 * 1024
HALF_LOG2_E = 0.7213475204444817
SCORE_SCALE = XHEAD_DIM ** -0.5 * 2.0 * HALF_LOG2_E


def _silu_of_twice(hx):
    return hx + hx * jnp.tanh(hx)


def _gated_of_twice(ha, hb):
    return ha + ha * jnp.tanh(hb)


def _w_in_col_scale():
    col = lax.broadcasted_iota(jnp.int32, (1, D_IN), 1)
    plain = (col < OFF_LRU_GATE) | ((col >= OFF_Q) & (col < OFF_ATTN_GATE))
    return jnp.where(plain, 1.0, 0.5)


def _sqrt_nonneg(x):
    return jnp.exp2(jnp.log(x) * HALF_LOG2_E)


def _rms_scaled(x):
    ms = jnp.mean(x * x, axis=-1, keepdims=True)
    return x * lax.rsqrt(ms + EPS)


def _rmsnorm(x, g):
    return _rms_scaled(x) * g


def _as_column(row):
    n = row.shape[1] // LANES
    on_diag = (lax.broadcasted_iota(jnp.int32, (LANES, LANES), 0)
               == lax.broadcasted_iota(jnp.int32, (LANES, LANES), 1))
    cols = [jnp.sum(jnp.where(on_diag, row[:, j * LANES:(j + 1) * LANES], 0.0), axis=1, keepdims=True)
            for j in range(n)]
    return jnp.concatenate(cols, axis=0)


def _log_sigmoid(x):
    y = -x
    return -(jnp.maximum(y, 0.0) + jnp.log1p(jnp.exp(-jnp.abs(y))))


def _bf16(x):
    return x.astype(jnp.bfloat16)


def _dot(a, b):
    return jnp.dot(a, b, preferred_element_type=jnp.float32)


def _dot_nt(a, b):
    return lax.dot_general(a, b, (((1,), (1,)), ((), ())), preferred_element_type=jnp.float32)


def _unrolled(n, body, init):
    carry = init
    for i in range(n):
        carry = body(i, carry)
    return carry


def _aligned(x, m):
    return x if isinstance(x, int) else pl.multiple_of(x, m)


def _head_of(index):
    return lax.shift_right_logical(index, XHEAD_DIM.bit_length() - 1)


def _lane_head_masks():
    head = _head_of(lax.broadcasted_iota(jnp.int32, (1, D_XATTN), 1))
    return [(head == h).astype(jnp.float32) for h in range(N_XHEADS)]


def _row_head_masks():
    head = _head_of(lax.broadcasted_iota(jnp.int32, (D_XATTN, 1), 0))
    return [(head == h).astype(jnp.float32) for h in range(N_XHEADS)]


def _head_stack_t(kt, vt, kst_ref, vst_ref):
    scale = SCORE_SCALE
    for h, m in enumerate(_row_head_masks()):
        kst_ref[:, h * N_MEM:(h + 1) * N_MEM] = _bf16(kt * (m * scale))
        vst_ref[:, h * N_MEM:(h + 1) * N_MEM] = _bf16(vt * m)


def _zero_row_after(x):
    bits = lax.bitcast_convert_type(x, jnp.uint32)
    zero = lax.shift_right_logical(lax.shift_right_logical(bits, jnp.uint32(16)), jnp.uint32(16))
    return lax.bitcast_convert_type(zero, jnp.float32)[0:1, :]


def _softmax_heads(s):
    ps = []
    for h in range(N_XHEADS):
        sh = s[:, h * N_MEM:(h + 1) * N_MEM]
        e = jnp.exp2(sh - jnp.max(sh, axis=-1, keepdims=True))
        inv = 1.0 / jnp.sum(e, axis=-1, keepdims=True)
        ps.append(_bf16(e * inv))
    return jnp.concatenate(ps, axis=1)


def _lru_gates(xl_bf, wg_ref, pre_s):
    pre_s[:, 0:D_LRU] = _dot(xl_bf[:, 0:GATE_HALF], wg_ref[0])
    pre_s[:, D_LRU:2 * D_LRU] = _dot(xl_bf[:, GATE_HALF:D_LRU], wg_ref[1])


def _pre_cols(l):
    half_slabs = N_LRU_SLABS // 2
    base = (l // half_slabs) * D_LRU + (l % half_slabs) * LANES
    return base, base + GATE_HALF


def _lru_coeffs(hpre_a, hpre_x, hx, hb_a, hb_x, hc_ls):
    t_r = jnp.tanh(hpre_a + hb_a)
    t_i = jnp.tanh(hpre_x + hb_x)
    log_a = hc_ls + hc_ls * t_r
    a = jnp.exp(log_a)
    one_minus_a2 = -jnp.tanh(log_a) * (a * a + 1.0)
    u = _sqrt_nonneg(one_minus_a2) * (hx + hx * t_i)
    return a, u


def _conv_taps(src_s, l, w_row, row0, nrows, width, halo):
    acc = None
    for k in range(width):
        x_k = src_s[l, pl.ds(row0 + (halo - (width - 1) + k), nrows), :]
        acc = w_row(k) * x_k if acc is None else acc + w_row(k) * x_k
    return acc


def _layernorm_silu_slabs(ys, g_ref, b_ref):
    tot = ys[0]
    for y in ys[1:]:
        tot = tot + y
    mu = jnp.sum(tot, axis=-1, keepdims=True) * (1.0 / D_CONV)
    cs = [y - mu for y in ys]
    sq = cs[0] * cs[0]
    for c in cs[1:]:
        sq = sq + c * c
    var = jnp.sum(sq, axis=-1, keepdims=True) * (1.0 / D_CONV)
    inv = lax.rsqrt(var + EPS)
    outs = []
    for l, c in enumerate(cs):
        lo = l * LANES
        half_g = 0.5 * g_ref[:, lo:lo + LANES]
        half_b = 0.5 * b_ref[:, lo:lo + LANES]
        outs.append(_silu_of_twice(c * inv * half_g + half_b))
    return outs


def _gate_block_diag(w_blocks):
    n = (LRU_BLOCKS // 2) * LRU_BLOCK
    stacked = _bf16(w_blocks.reshape(n, LRU_BLOCK))
    src = lax.broadcasted_iota(jnp.int32, (LRU_BLOCK, n), 0)
    dst = lax.broadcasted_iota(jnp.int32, (LRU_BLOCK, n), 1)
    spread = _bf16(((dst & (LRU_BLOCK - 1)) == src).astype(jnp.float32))
    rows = _head_of(lax.broadcasted_iota(jnp.int32, (n, n), 0))
    cols = _head_of(lax.broadcasted_iota(jnp.int32, (n, n), 1))
    return _bf16(_dot(stacked, spread) * (rows == cols).astype(jnp.float32))


def _prompt_kernel(x_ref, mem_ref, g_mem_ref, wk_ref, wv_ref,
                   g_norm_ref, w_in_ref, w_lc_ref, b_lc_ref, wga_ref, wgx_ref,
                   b_ga_ref, b_gx_ref, lam_ref, w_dw_ref, b_dw_ref, ln_g_ref, ln_b_ref,
                   w_pw_ref, b_pw_ref, w_out_ref, g_final_ref,
                   y_ref, kt_ref, vt_ref, h_out_ref, lb_out_ref, cb_out_ref,
                   win_bf, wout_bf, wpw_bf, wg_bf,
                   z_s, xc_s, xl_s, pre_s, a_s, u_s, uc_s, ln_s, mix_s, hin_s, kst_s, vst_s, sc_s, p_s):
    b = pl.program_id(0)
    t = pl.program_id(1)
    n_t = pl.num_programs(1)
    tile = PROMPT_TILE
    seg = PROMPT_SEG
    n_seg = tile // seg
    pitch = PROMPT_PITCH

    @pl.when(jnp.logical_and(b == 0, t == 0))
    def _():
        win_bf[...] = _bf16(w_in_ref[...] * _as_column(g_norm_ref[...]) * _w_in_col_scale())
        wout_bf[...] = _bf16(w_out_ref[...])
        wpw_bf[...] = _bf16(w_pw_ref[...])
        half = LRU_BLOCKS // 2
        for h in range(2):
            wg_bf[h, :, 0:GATE_HALF] = _gate_block_diag(wga_ref[h * half:(h + 1) * half])
            wg_bf[h, :, GATE_HALF:D_LRU] = _gate_block_diag(wgx_ref[h * half:(h + 1) * half])

    @pl.when(t == 0)
    def _():
        xc_s[:, 0:LRU_HALO, :] = jnp.zeros((N_LRU_SLABS, LRU_HALO, LANES), jnp.float32)
        uc_s[:, 0:CONV_HALO, :] = jnp.zeros((N_CONV_SLABS, CONV_HALO, LANES), jnp.float32)
        hin_s[:, 0:1, :] = jnp.zeros((N_LRU_SLABS, 1, LANES), jnp.float32)
        m = _bf16(_rmsnorm(mem_ref[0], g_mem_ref[...]))
        kt = _dot(m, _bf16(wk_ref[...])).T
        vt = _dot(m, _bf16(wv_ref[...])).T
        kt_ref[0] = kt
        vt_ref[0] = vt
        _head_stack_t(kt, vt, kst_s, vst_s)

    for r0 in range(0, tile, PROJ_ROWS):
        z_s[r0:r0 + PROJ_ROWS, :] = _dot(_bf16(_rms_scaled(x_ref[0, r0:r0 + PROJ_ROWS, :])), win_bf[...])

    for l in range(N_LRU_SLABS):
        lo = l * LANES
        xc_s[l, LRU_HALO:LRU_HALO + tile, :] = z_s[:, OFF_LRU_IN + lo:OFF_LRU_IN + lo + LANES]
    for l in range(N_LRU_SLABS):
        lo = l * LANES
        w_row = lambda k, lo=lo: 0.5 * w_lc_ref[k:k + 1, lo:lo + LANES]
        xl = (_conv_taps(xc_s, l, w_row, 0, tile, LRU_CONV, LRU_HALO)
              + 0.5 * b_lc_ref[:, lo:lo + LANES])
        xl_s[:, lo:lo + LANES] = xl
    _lru_gates(_bf16(xl_s[...]), wg_bf, pre_s)

    hc_ls = (0.5 * LRU_C) * _log_sigmoid(lam_ref[...])
    hb_ga = 0.5 * b_ga_ref[...]
    hb_gx = 0.5 * b_gx_ref[...]

    def coeff_body(c, carry):
        r0 = _aligned(c * seg, seg)
        p0 = _aligned(c * pitch, SUBLANES)
        for l in range(N_LRU_SLABS):
            lo = l * LANES
            ca, cx = _pre_cols(l)
            a, u = _lru_coeffs(pre_s[pl.ds(r0, seg), ca:ca + LANES],
                               pre_s[pl.ds(r0, seg), cx:cx + LANES],
                               xl_s[pl.ds(r0, seg), lo:lo + LANES],
                               hb_ga[:, lo:lo + LANES], hb_gx[:, lo:lo + LANES],
                               hc_ls[:, lo:lo + LANES])
            a_s[l, pl.ds(p0, seg), :] = a
            u_s[l, pl.ds(p0, seg), :] = u
        return carry

    _unrolled(n_seg, coeff_body, 0)

    def tot_body(j, carry):
        new = []
        for l in range(N_LRU_SLABS):
            a_tot, u_tot = carry[l]
            a_j = a_s[l, pl.ds(j, n_seg, stride=pitch), :]
            u_j = u_s[l, pl.ds(j, n_seg, stride=pitch), :]
            new.append((a_j * a_tot, a_j * u_tot + u_j))
        return tuple(new)

    init = tuple((jnp.ones((n_seg, LANES), jnp.float32), jnp.zeros((n_seg, LANES), jnp.float32))
                 for _ in range(N_LRU_SLABS))
    totals = _unrolled(seg, tot_body, init)

    h_in = []
    for l in range(N_LRU_SLABS):
        a_tot, u_tot = totals[l]
        h = hin_s[l, 0:1, :]
        for c in range(n_seg):
            h = a_tot[c:c + 1, :] * h + u_tot[c:c + 1, :]
            hin_s[l, c + 1:c + 2, :] = h
        h_in.append(hin_s[l, 0:n_seg, :])
        hin_s[l, 0:1, :] = h

    def scan_body(j, carry):
        new = []
        for l in range(N_LRU_SLABS):
            a_j = a_s[l, pl.ds(j, n_seg, stride=pitch), :]
            u_j = u_s[l, pl.ds(j, n_seg, stride=pitch), :]
            h = a_j * carry[l] + u_j
            u_s[l, pl.ds(j, n_seg, stride=pitch), :] = h
            new.append(h)
        return tuple(new)

    scores = _dot(_bf16(z_s[:, OFF_Q:OFF_Q + D_XATTN]), kst_s[...])
    sc_s[...] = scores
    after_scores = _zero_row_after(scores[tile - SUBLANES:tile, N_XHEADS * N_MEM - LANES:])
    _unrolled(seg, scan_body, tuple(h + after_scores for h in h_in))

    for l in range(N_CONV_SLABS):
        lo = l * LANES
        uc_s[l, CONV_HALO:CONV_HALO + tile, :] = _gated_of_twice(
            z_s[:, OFF_GLU_A + lo:OFF_GLU_A + lo + LANES],
            z_s[:, OFF_GLU_B + lo:OFF_GLU_B + lo + LANES])

    def seg_body(c, carry):
        r0 = _aligned(c * seg, seg)
        p0 = _aligned(c * pitch, SUBLANES)
        for l in range(N_LRU_SLABS):
            lo = l * LANES
            gate = z_s[pl.ds(r0, seg), OFF_LRU_GATE + lo:OFF_LRU_GATE + lo + LANES]
            mix_s[pl.ds(r0, seg), lo:lo + LANES] = _bf16(u_s[l, pl.ds(p0, seg), :] * _silu_of_twice(gate))
        for piece in range(seg // CONV_ROWS):
            rr = r0 + piece * CONV_ROWS
            ys = []
            for l in range(N_CONV_SLABS):
                lo = l * LANES
                w_row = lambda k, lo=lo: w_dw_ref[k, :, lo:lo + LANES]
                ys.append(_conv_taps(uc_s, l, w_row, rr, CONV_ROWS, CONV_W, CONV_HALO)
                          + b_dw_ref[:, lo:lo + LANES])
            outs = _layernorm_silu_slabs(ys, ln_g_ref, ln_b_ref)
            for l in range(N_CONV_SLABS):
                lo = l * LANES
                ln_s[pl.ds(rr, CONV_ROWS), lo:lo + LANES] = _bf16(outs[l])
        return carry

    _unrolled(n_seg, seg_body, 0)

    yc = (_dot(ln_s[...], wpw_bf[...]) + b_pw_ref[...]) * _silu_of_twice(
        z_s[:, OFF_CONV_GATE:OFF_CONV_GATE + D_CONV])
    mix_s[:, D_LRU:D_LRU + D_CONV] = _bf16(yc)

    def attn_body(c, carry):
        r0 = _aligned(c * ATTN_ROWS, ATTN_ROWS)
        p_s[pl.ds(r0, ATTN_ROWS), :] = _softmax_heads(sc_s[pl.ds(r0, ATTN_ROWS), :])
        return carry

    _unrolled(tile // ATTN_ROWS, attn_body, 0)
    o = _dot_nt(p_s[...], vst_s[...])
    mix_s[:, D_LRU + D_CONV:D_MIX] = _bf16(
        o * _silu_of_twice(z_s[:, OFF_ATTN_GATE:OFF_ATTN_GATE + D_XATTN]))

    y = x_ref[0] + _dot(mix_s[...], wout_bf[...])
    y_ref[0] = _rmsnorm(y, g_final_ref[...])

    for l in range(N_LRU_SLABS):
        xc_s[l, LRU_HALO - (LRU_CONV - 1):LRU_HALO, :] = (
            xc_s[l, LRU_HALO + tile - (LRU_CONV - 1):LRU_HALO + tile, :])
    for l in range(N_CONV_SLABS):
        uc_s[l, CONV_HALO - (CONV_W - 1):CONV_HALO, :] = (
            uc_s[l, CONV_HALO + tile - (CONV_W - 1):CONV_HALO + tile, :])

    def write_state(bb):
        for l in range(N_LRU_SLABS):
            lo = l * LANES
            h_out_ref[bb:bb + 1, lo:lo + LANES] = hin_s[l, 0:1, :]
            for k in range(LRU_CONV - 1):
                r = LRU_HALO - (LRU_CONV - 1) + k
                lb_out_ref[k, bb:bb + 1, lo:lo + LANES] = xc_s[l, r:r + 1, :]
        for l in range(N_CONV_SLABS):
            lo = l * LANES
            for k in range(CONV_W - 1):
                r = CONV_HALO - (CONV_W - 1) + k
                cb_out_ref[k, bb:bb + 1, lo:lo + LANES] = uc_s[l, r:r + 1, :]

    for bb in range(h_out_ref.shape[0]):
        pl.when(jnp.logical_and(t == n_t - 1, b == bb))(functools.partial(write_state, bb))


def _whole(shape):
    nd = len(shape)
    return pl.BlockSpec(shape, lambda *_: (0,) * nd, pipeline_mode=pl.Buffered(1))


def _prompt(x, mem, w):
    bsz, seq, _ = x.shape
    tile = PROMPT_TILE
    n_t = seq // tile
    n_seg = tile // PROMPT_SEG
    weights = [w["g_mem"], w["w_mem_k"], w["w_mem_v"],
               w["g_norm"], w["w_in"], w["w_lc"], w["b_lc"], w["w_ga"], w["w_gx"], w["b_ga"], w["b_gx"],
               w["lam"], w["w_dw"], w["b_dw"], w["ln_g"], w["ln_b"], w["w_pw"], w["b_pw"],
               w["w_out"], w["g_final"]]
    weight_specs = [
        _whole((1, D_MODEL)), _whole((None, D_MODEL, D_XATTN)), _whole((None, D_MODEL, D_XATTN)),
        _whole((1, D_MODEL)), _whole((None, D_MODEL, D_IN)), _whole((None, LRU_CONV, D_LRU)),
        _whole((1, D_LRU)), _whole((None, LRU_BLOCKS, LRU_BLOCK, LRU_BLOCK)),
        _whole((None, LRU_BLOCKS, LRU_BLOCK, LRU_BLOCK)), _whole((1, D_LRU)), _whole((1, D_LRU)),
        _whole((1, D_LRU)), _whole((CONV_W, 1, D_CONV)), _whole((1, D_CONV)), _whole((1, D_CONV)),
        _whole((1, D_CONV)), _whole((None, D_CONV, D_CONV)), _whole((1, D_CONV)),
        _whole((None, D_MIX, D_MODEL)), _whole((1, D_MODEL)),
    ]
    in_specs = [
        pl.BlockSpec((1, tile, D_MODEL), lambda b, t: (b, t, 0)),
        pl.BlockSpec((1, N_MEM, D_MODEL), lambda b, t: (b, 0, 0)),
    ] + weight_specs
    out_specs = [
        pl.BlockSpec((1, tile, D_MODEL), lambda b, t: (b, t, 0)),
        pl.BlockSpec((1, D_XATTN, N_MEM), lambda b, t: (b, 0, 0)),
        pl.BlockSpec((1, D_XATTN, N_MEM), lambda b, t: (b, 0, 0)),
        _whole((bsz, D_LRU)),
        _whole((LRU_CONV - 1, bsz, D_LRU)),
        _whole((CONV_W - 1, bsz, D_CONV)),
        _whole((D_MODEL, D_IN)), _whole((D_MIX, D_MODEL)), _whole((D_CONV, D_CONV)),
        _whole((2, GATE_HALF, D_LRU)),
    ]
    out_shape = [
        jax.ShapeDtypeStruct((bsz, seq, D_MODEL), jnp.float32),
        jax.ShapeDtypeStruct((bsz, D_XATTN, N_MEM), jnp.float32),
        jax.ShapeDtypeStruct((bsz, D_XATTN, N_MEM), jnp.float32),
        jax.ShapeDtypeStruct((bsz, D_LRU), jnp.float32),
        jax.ShapeDtypeStruct((LRU_CONV - 1, bsz, D_LRU), jnp.float32),
        jax.ShapeDtypeStruct((CONV_W - 1, bsz, D_CONV), jnp.float32),
        jax.ShapeDtypeStruct((D_MODEL, D_IN), jnp.bfloat16),
        jax.ShapeDtypeStruct((D_MIX, D_MODEL), jnp.bfloat16),
        jax.ShapeDtypeStruct((D_CONV, D_CONV), jnp.bfloat16),
        jax.ShapeDtypeStruct((2, GATE_HALF, D_LRU), jnp.bfloat16),
    ]
    scratch = [
        pltpu.VMEM((tile, D_IN), jnp.float32),
        pltpu.VMEM((N_LRU_SLABS, LRU_HALO + tile, LANES), jnp.float32),
        pltpu.VMEM((tile, D_LRU), jnp.float32),
        pltpu.VMEM((tile, 2 * D_LRU), jnp.float32),
        pltpu.VMEM((N_LRU_SLABS, n_seg * PROMPT_PITCH, LANES), jnp.float32),
        pltpu.VMEM((N_LRU_SLABS, n_seg * PROMPT_PITCH, LANES), jnp.float32),
        pltpu.VMEM((N_CONV_SLABS, CONV_HALO + tile, LANES), jnp.float32),
        pltpu.VMEM((tile, D_CONV), jnp.bfloat16),
        pltpu.VMEM((tile, D_MIX), jnp.bfloat16),
        pltpu.VMEM((N_LRU_SLABS, 2 * SUBLANES, LANES), jnp.float32),
        pltpu.VMEM((D_XATTN, N_XHEADS * N_MEM), jnp.bfloat16),
        pltpu.VMEM((D_XATTN, N_XHEADS * N_MEM), jnp.bfloat16),
        pltpu.VMEM((tile, N_XHEADS * N_MEM), jnp.float32),
        pltpu.VMEM((tile, N_XHEADS * N_MEM), jnp.bfloat16),
    ]
    return pl.pallas_call(
        _prompt_kernel,
        grid=(bsz, n_t),
        in_specs=in_specs,
        out_specs=out_specs,
        out_shape=out_shape,
        scratch_shapes=scratch,
        compiler_params=pltpu.CompilerParams(
            dimension_semantics=("arbitrary", "arbitrary"),
            vmem_limit_bytes=VMEM_LIMIT_BYTES),
        name="prompt_layer",
    )(x, mem, *weights)


def _sample_kernel(nb, steps,
                   x_ref, h0_ref, lb_ref, cb_ref, kt_ref, vt_ref,
                   win_bf, w_lc_ref, b_lc_ref, wg_bf,
                   b_ga_ref, b_gx_ref, lam_ref, w_dw_ref, b_dw_ref, ln_g_ref, ln_b_ref,
                   wpw_bf, b_pw_ref, wout_bf, g_final_ref,
                   y_ref, h_out_ref, lb_out_ref, cb_out_ref,
                   z_s, xc_s, xl_s, pre_s, a_s, u_s, uc_s, ln_s, mix_s, sc_s, p_s):
    rows = nb * steps
    pitch = SAMPLE_PITCH
    cpitch = CONV_HALO + steps
    prow = nb * pitch

    x = x_ref[...].reshape(rows, D_MODEL)
    z_s[...] = _dot(_bf16(_rms_scaled(x)), win_bf[...])

    for b in range(nb):
        for l in range(N_LRU_SLABS):
            lo = l * LANES
            zcol = OFF_LRU_IN + lo
            xc_s[l, b * pitch:b * pitch + LRU_HALO, :] = jnp.zeros((LRU_HALO, LANES), jnp.float32)
            for k in range(LRU_CONV - 1):
                r = b * pitch + LRU_HALO - (LRU_CONV - 1) + k
                xc_s[l, r:r + 1, :] = lb_ref[k, b:b + 1, lo:lo + LANES]
                rn = (b + 1) * steps - (LRU_CONV - 1) + k
                lb_out_ref[k, b:b + 1, lo:lo + LANES] = z_s[rn:rn + 1, zcol:zcol + LANES]
            xc_s[l, b * pitch + LRU_HALO:(b + 1) * pitch, :] = (
                z_s[b * steps:(b + 1) * steps, zcol:zcol + LANES])
    n_conv = prow - LRU_HALO
    for l in range(N_LRU_SLABS):
        lo = l * LANES
        w_row = lambda k, lo=lo: 0.5 * w_lc_ref[k:k + 1, lo:lo + LANES]
        xl = (_conv_taps(xc_s, l, w_row, 0, n_conv, LRU_CONV, LRU_HALO)
              + 0.5 * b_lc_ref[:, lo:lo + LANES])
        xl_s[0:n_conv, lo:lo + LANES] = xl
    xl_s[n_conv:prow, :] = jnp.zeros((LRU_HALO, D_LRU), jnp.float32)
    _lru_gates(_bf16(xl_s[...]), wg_bf, pre_s)

    hc_ls = (0.5 * LRU_C) * _log_sigmoid(lam_ref[...])
    hb_ga = 0.5 * b_ga_ref[...]
    hb_gx = 0.5 * b_gx_ref[...]
    for l in range(N_LRU_SLABS):
        lo = l * LANES
        ca, cx = _pre_cols(l)
        a, u = _lru_coeffs(pre_s[:, ca:ca + LANES], pre_s[:, cx:cx + LANES], xl_s[:, lo:lo + LANES],
                           hb_ga[:, lo:lo + LANES], hb_gx[:, lo:lo + LANES],
                           hc_ls[:, lo:lo + LANES])
        a_s[l, :, :] = a
        u_s[l, :, :] = u

    n_grp = nb // SUBLANES

    def scan_body(j, carry):
        new = []
        for g in range(n_grp):
            for l in range(N_LRU_SLABS):
                base = g * SUBLANES * pitch
                a_j = a_s[l, pl.ds(base + j, SUBLANES, stride=pitch), :]
                u_j = u_s[l, pl.ds(base + j, SUBLANES, stride=pitch), :]
                h = a_j * carry[g * N_LRU_SLABS + l] + u_j
                u_s[l, pl.ds(base + j, SUBLANES, stride=pitch), :] = h
                new.append(h)
        return tuple(new)

    h0 = tuple(h0_ref[g * SUBLANES:(g + 1) * SUBLANES, l * LANES:(l + 1) * LANES]
               for g in range(n_grp) for l in range(N_LRU_SLABS))
    h_fin = _unrolled(steps, scan_body, h0)
    for g in range(n_grp):
        for l in range(N_LRU_SLABS):
            h_out_ref[g * SUBLANES:(g + 1) * SUBLANES, l * LANES:(l + 1) * LANES] = (
                h_fin[g * N_LRU_SLABS + l])

    for b in range(nb):
        for l in range(N_LRU_SLABS):
            lo = l * LANES
            gate = z_s[b * steps:(b + 1) * steps, OFF_LRU_GATE + lo:OFF_LRU_GATE + lo + LANES]
            mix_s[b * steps:(b + 1) * steps, lo:lo + LANES] = _bf16(
                u_s[l, b * pitch:b * pitch + steps, :] * _silu_of_twice(gate))

    for b in range(nb):
        for l in range(N_CONV_SLABS):
            lo = l * LANES
            uc = _gated_of_twice(
                z_s[b * steps:(b + 1) * steps, OFF_GLU_A + lo:OFF_GLU_A + lo + LANES],
                z_s[b * steps:(b + 1) * steps, OFF_GLU_B + lo:OFF_GLU_B + lo + LANES])
            uc_s[l, b * cpitch + CONV_HALO:(b + 1) * cpitch, :] = uc
            for k in range(CONV_W - 1):
                r = b * cpitch + CONV_HALO - (CONV_W - 1) + k
                uc_s[l, r:r + 1, :] = cb_ref[k, b:b + 1, lo:lo + LANES]
                rn = (b + 1) * cpitch - (CONV_W - 1) + k
                cb_out_ref[k, b:b + 1, lo:lo + LANES] = uc_s[l, rn:rn + 1, :]

    for b in range(nb):
        r0 = b * cpitch
        o0 = b * steps
        ys = []
        for l in range(N_CONV_SLABS):
            lo = l * LANES
            w_row = lambda k, lo=lo: w_dw_ref[k, :, lo:lo + LANES]
            ys.append(_conv_taps(uc_s, l, w_row, r0, steps, CONV_W, CONV_HALO)
                      + b_dw_ref[:, lo:lo + LANES])
        outs = _layernorm_silu_slabs(ys, ln_g_ref, ln_b_ref)
        for l in range(N_CONV_SLABS):
            lo = l * LANES
            ln_s[o0:o0 + steps, lo:lo + LANES] = _bf16(outs[l])

    yc = (_dot(ln_s[...], wpw_bf[...]) + b_pw_ref[...]) * _silu_of_twice(
        z_s[:, OFF_CONV_GATE:OFF_CONV_GATE + D_CONV])
    mix_s[:, D_LRU:D_LRU + D_CONV] = _bf16(yc)

    masks = _lane_head_masks()
    scale = SCORE_SCALE
    qrows = N_XHEADS * steps
    for b in range(nb):
        q = z_s[b * steps:(b + 1) * steps, OFF_Q:OFF_Q + D_XATTN]
        qm = jnp.concatenate([_bf16(q * (m * scale)) for m in masks], axis=0)
        sc_s[b * qrows:(b + 1) * qrows, :] = _dot(qm, _bf16(kt_ref[b]))
    s = sc_s[...]
    e = jnp.exp2(s - jnp.max(s, axis=-1, keepdims=True))
    p_s[...] = _bf16(e * (1.0 / jnp.sum(e, axis=-1, keepdims=True)))
    for b in range(nb):
        o_all = _dot_nt(p_s[b * qrows:(b + 1) * qrows, :], _bf16(vt_ref[b]))
        o = o_all[0:steps] * masks[0]
        for h in range(1, N_XHEADS):
            o = o + o_all[h * steps:(h + 1) * steps] * masks[h]
        gate = z_s[b * steps:(b + 1) * steps, OFF_ATTN_GATE:OFF_ATTN_GATE + D_XATTN]
        mix_s[b * steps:(b + 1) * steps, D_LRU + D_CONV:D_MIX] = _bf16(o * _silu_of_twice(gate))

    y = x + _dot(mix_s[...], wout_bf[...])
    y_ref[...] = _rmsnorm(y, g_final_ref[...]).reshape(nb, steps, D_MODEL)


def _sample(x, h0, lb, cb, kt, vt, w, win_bf, wout_bf, wpw_bf, wg_bf):
    bsz, steps, _ = x.shape
    nb = SAMPLE_GROUP
    assert bsz % nb == 0 and nb % SUBLANES == 0
    assert steps + LRU_HALO == SAMPLE_PITCH and steps >= CONV_W - 1
    rows = nb * steps
    prow = nb * SAMPLE_PITCH
    weights = [win_bf, w["w_lc"], w["b_lc"], wg_bf, w["b_ga"], w["b_gx"],
               w["lam"], w["w_dw"], w["b_dw"], w["ln_g"], w["ln_b"], wpw_bf, w["b_pw"],
               wout_bf, w["g_final"]]
    weight_specs = [
        _whole((D_MODEL, D_IN)), _whole((None, LRU_CONV, D_LRU)),
        _whole((1, D_LRU)), _whole((2, GATE_HALF, D_LRU)), _whole((1, D_LRU)), _whole((1, D_LRU)),
        _whole((1, D_LRU)), _whole((CONV_W, 1, D_CONV)), _whole((1, D_CONV)), _whole((1, D_CONV)),
        _whole((1, D_CONV)), _whole((D_CONV, D_CONV)), _whole((1, D_CONV)),
        _whole((D_MIX, D_MODEL)), _whole((1, D_MODEL)),
    ]
    in_specs = [
        pl.BlockSpec((nb, steps, D_MODEL), lambda g: (g, 0, 0)),
        pl.BlockSpec((nb, D_LRU), lambda g: (g, 0)),
        pl.BlockSpec((LRU_CONV - 1, nb, D_LRU), lambda g: (0, g, 0)),
        pl.BlockSpec((CONV_W - 1, nb, D_CONV), lambda g: (0, g, 0)),
        pl.BlockSpec((nb, D_XATTN, N_MEM), lambda g: (g, 0, 0)),
        pl.BlockSpec((nb, D_XATTN, N_MEM), lambda g: (g, 0, 0)),
    ] + weight_specs
    out_specs = [
        pl.BlockSpec((nb, steps, D_MODEL), lambda g: (g, 0, 0)),
        pl.BlockSpec((nb, D_LRU), lambda g: (g, 0)),
        pl.BlockSpec((LRU_CONV - 1, nb, D_LRU), lambda g: (0, g, 0)),
        pl.BlockSpec((CONV_W - 1, nb, D_CONV), lambda g: (0, g, 0)),
    ]
    out_shape = [
        jax.ShapeDtypeStruct((bsz, steps, D_MODEL), jnp.float32),
        jax.ShapeDtypeStruct((bsz, D_LRU), jnp.float32),
        jax.ShapeDtypeStruct((LRU_CONV - 1, bsz, D_LRU), jnp.float32),
        jax.ShapeDtypeStruct((CONV_W - 1, bsz, D_CONV), jnp.float32),
    ]
    scratch = [
        pltpu.VMEM((rows, D_IN), jnp.float32),
        pltpu.VMEM((N_LRU_SLABS, prow, LANES), jnp.float32),
        pltpu.VMEM((prow, D_LRU), jnp.float32),
        pltpu.VMEM((prow, 2 * D_LRU), jnp.float32),
        pltpu.VMEM((N_LRU_SLABS, prow, LANES), jnp.float32),
        pltpu.VMEM((N_LRU_SLABS, prow, LANES), jnp.float32),
        pltpu.VMEM((N_CONV_SLABS, nb * (CONV_HALO + steps), LANES), jnp.float32),
        pltpu.VMEM((rows, D_CONV), jnp.bfloat16),
        pltpu.VMEM((rows, D_MIX), jnp.bfloat16),
        pltpu.VMEM((N_XHEADS * rows, N_MEM), jnp.float32),
        pltpu.VMEM((N_XHEADS * rows, N_MEM), jnp.bfloat16),
    ]
    return pl.pallas_call(
        functools.partial(_sample_kernel, nb, steps),
        grid=(bsz // nb,),
        in_specs=in_specs,
        out_specs=out_specs,
        out_shape=out_shape,
        scratch_shapes=scratch,
        compiler_params=pltpu.CompilerParams(
            dimension_semantics=("arbitrary",),
            vmem_limit_bytes=VMEM_LIMIT_BYTES),
        name="sample_layer",
    )(x, h0, lb, cb, kt, vt, *weights)


def _time_major(state):
    return jnp.transpose(state, (1, 0, 2))


def _memory_t(mem):
    bsz = mem.shape[0]
    return jnp.transpose(mem, (0, 2, 3, 1)).reshape(bsz, D_XATTN, N_MEM)


def _memory_from_t(mem_t):
    bsz = mem_t.shape[0]
    return jnp.transpose(mem_t.reshape(bsz, N_XHEADS, XHEAD_DIM, N_MEM), (0, 3, 1, 2))


def kernel(x_prompt, x_sample, state_lru_h, state_lru_conv, state_conv, cache_mem_k, cache_mem_v,
           mem_prompt, g_norm, w_in, w_lru_conv, b_lru_conv, w_gate_a, b_gate_a, w_gate_x, b_gate_x,
           lru_lambda, w_dw, b_dw, ln_g, ln_b, w_pw, b_pw, g_mem, w_mem_k, w_mem_v, w_out, g_final):
    depth = g_norm.shape[0]
    assert depth == 1
    w = {
        "g_norm": g_norm, "w_in": w_in, "w_lc": w_lru_conv, "b_lc": b_lru_conv,
        "w_ga": w_gate_a, "w_gx": w_gate_x, "b_ga": b_gate_a, "b_gx": b_gate_x, "lam": lru_lambda,
        "w_dw": jnp.transpose(w_dw, (1, 0, 2)), "b_dw": b_dw, "ln_g": ln_g, "ln_b": ln_b,
        "w_pw": w_pw, "b_pw": b_pw, "w_out": w_out, "g_final": g_final.reshape(1, D_MODEL),
        "g_mem": g_mem, "w_mem_k": w_mem_k, "w_mem_v": w_mem_v,
    }
    y_prompt, mkt, mvt, ph, plb, pcb, win_bf, wout_bf, wpw_bf, wg_bf = _prompt(x_prompt, mem_prompt, w)

    y_sample, sh, slb, scb = _sample(
        x_sample, state_lru_h[0], _time_major(state_lru_conv[0]), _time_major(state_conv[0]),
        _memory_t(cache_mem_k[0]), _memory_t(cache_mem_v[0]), w, win_bf, wout_bf, wpw_bf, wg_bf)

    return (y_prompt, y_sample,
            ph[None], _time_major(plb)[None], _time_major(pcb)[None],
            _memory_from_t(mkt)[None], _memory_from_t(mvt)[None],
            sh[None], _time_major(slb)[None], _time_major(scb)[None])
```

```python
import functools

import jax
import jax.numpy as jnp
from jax import lax
from jax.experimental import pallas as pl
from jax.experimental.pallas import tpu as pltpu

D_MODEL = 1024
N_MEM = 256
D_LRU = 512
LRU_BLOCKS = 8
LRU_BLOCK = D_LRU // LRU_BLOCKS
LRU_CONV = 4
LRU_C = 8.0
D_CONV = 256
CONV_W = 31
N_XHEADS = 4
XHEAD_DIM = 64
D_XATTN = N_XHEADS * XHEAD_DIM
D_MIX = D_LRU + D_CONV + D_XATTN
D_IN = 2 * D_LRU + 3 * D_CONV + 2 * D_XATTN
EPS = 1e-6

LANES = 128
SUBLANES = 8
N_LRU_SLABS = D_LRU // LANES
N_CONV_SLABS = D_CONV // LANES
GATE_HALF = D_LRU // 2

OFF_LRU_IN = 0
OFF_LRU_GATE = D_LRU
OFF_GLU_A = 2 * D_LRU
OFF_GLU_B = OFF_GLU_A + D_CONV
OFF_CONV_GATE = OFF_GLU_B + D_CONV
OFF_Q = OFF_CONV_GATE + D_CONV
OFF_ATTN_GATE = OFF_Q + D_XATTN

LRU_HALO = 8
CONV_HALO = 32

PROMPT_TILE = 512
PROMPT_SEG = 64
PROMPT_PITCH = 68
SAMPLE_PITCH = 40
SAMPLE_GROUP = 16
ATTN_ROWS = 128
CONV_ROWS = 32
PROJ_ROWS = 256

VMEM_LIMIT_BYTES = 56 * 1024 * 1024
HALF_LOG2_E = 0.7213475204444817
SCORE_SCALE = XHEAD_DIM ** -0.5 * 2.0 * HALF_LOG2_E


def _silu_of_twice(hx):
    return hx + hx * jnp.tanh(hx)


def _gated_of_twice(ha, hb):
    return ha + ha * jnp.tanh(hb)


def _w_in_col_scale():
    col = lax.broadcasted_iota(jnp.int32, (1, D_IN), 1)
    plain = (col < OFF_LRU_GATE) | ((col >= OFF_Q) & (col < OFF_ATTN_GATE))
    return jnp.where(plain, 1.0, 0.5)


def _sqrt_nonneg(x):
    return jnp.exp2(jnp.log(x) * HALF_LOG2_E)


def _rms_scaled(x):
    ms = jnp.mean(x * x, axis=-1, keepdims=True)
    return x * lax.rsqrt(ms + EPS)


def _rmsnorm(x, g):
    return _rms_scaled(x) * g


def _as_column(row):
    n = row.shape[1] // LANES
    on_diag = (lax.broadcasted_iota(jnp.int32, (LANES, LANES), 0)
               == lax.broadcasted_iota(jnp.int32, (LANES, LANES), 1))
    cols = [jnp.sum(jnp.where(on_diag, row[:, j * LANES:(j + 1) * LANES], 0.0), axis=1, keepdims=True)
            for j in range(n)]
    return jnp.concatenate(cols, axis=0)


def _log_sigmoid(x):
    y = -x
    return -(jnp.maximum(y, 0.0) + jnp.log1p(jnp.exp(-jnp.abs(y))))


def _bf16(x):
    return x.astype(jnp.bfloat16)


def _dot(a, b):
    return jnp.dot(a, b, preferred_element_type=jnp.float32)


def _dot_nt(a, b):
    return lax.dot_general(a, b, (((1,), (1,)), ((), ())), preferred_element_type=jnp.float32)


def _unrolled(n, body, init):
    carry = init
    for i in range(n):
        carry = body(i, carry)
    return carry


def _aligned(x, m):
    return x if isinstance(x, int) else pl.multiple_of(x, m)


def _head_of(index):
    return lax.shift_right_logical(index, XHEAD_DIM.bit_length() - 1)


def _lane_head_masks():
    head = _head_of(lax.broadcasted_iota(jnp.int32, (1, D_XATTN), 1))
    return [(head == h).astype(jnp.float32) for h in range(N_XHEADS)]


def _row_head_masks():
    head = _head_of(lax.broadcasted_iota(jnp.int32, (D_XATTN, 1), 0))
    return [(head == h).astype(jnp.float32) for h in range(N_XHEADS)]


def _head_stack_t(kt, vt, kst_ref, vst_ref):
    scale = SCORE_SCALE
    for h, m in enumerate(_row_head_masks()):
        kst_ref[:, h * N_MEM:(h + 1) * N_MEM] = _bf16(kt * (m * scale))
        vst_ref[:, h * N_MEM:(h + 1) * N_MEM] = _bf16(vt * m)


def _zero_row_after(x):
    bits = lax.bitcast_convert_type(x, jnp.uint32)
    zero = lax.shift_right_logical(lax.shift_right_logical(bits, jnp.uint32(16)), jnp.uint32(16))
    return lax.bitcast_convert_type(zero, jnp.float32)[0:1, :]


def _softmax_heads(s):
    ps = []
    for h in range(N_XHEADS):
        sh = s[:, h * N_MEM:(h + 1) * N_MEM]
        e = jnp.exp2(sh - jnp.max(sh, axis=-1, keepdims=True))
        inv = 1.0 / jnp.sum(e, axis=-1, keepdims=True)
        ps.append(_bf16(e * inv))
    return jnp.concatenate(ps, axis=1)


def _lru_gates(xl_bf, wg_ref, pre_s):
    pre_s[:, 0:D_LRU] = _dot(xl_bf[:, 0:GATE_HALF], wg_ref[0])
    pre_s[:, D_LRU:2 * D_LRU] = _dot(xl_bf[:, GATE_HALF:D_LRU], wg_ref[1])


def _pre_cols(l):
    half_slabs = N_LRU_SLABS // 2
    base = (l // half_slabs) * D_LRU + (l % half_slabs) * LANES
    return base, base + GATE_HALF


def _lru_coeffs(hpre_a, hpre_x, hx, hb_a, hb_x, hc_ls):
    t_r = jnp.tanh(hpre_a + hb_a)
    t_i = jnp.tanh(hpre_x + hb_x)
    log_a = hc_ls + hc_ls * t_r
    a = jnp.exp(log_a)
    one_minus_a2 = -jnp.tanh(log_a) * (a * a + 1.0)
    u = _sqrt_nonneg(one_minus_a2) * (hx + hx * t_i)
    return a, u


def _conv_taps(src_s, l, w_row, row0, nrows, width, halo):
    acc = None
    for k in range(width):
        x_k = src_s[l, pl.ds(row0 + (halo - (width - 1) + k), nrows), :]
        acc = w_row(k) * x_k if acc is None else acc + w_row(k) * x_k
    return acc


def _layernorm_silu_slabs(ys, g_ref, b_ref):
    tot = ys[0]
    for y in ys[1:]:
        tot = tot + y
    mu = jnp.sum(tot, axis=-1, keepdims=True) * (1.0 / D_CONV)
    cs = [y - mu for y in ys]
    sq = cs[0] * cs[0]
    for c in cs[1:]:
        sq = sq + c * c
    var = jnp.sum(sq, axis=-1, keepdims=True) * (1.0 / D_CONV)
    inv = lax.rsqrt(var + EPS)
    outs = []
    for l, c in enumerate(cs):
        lo = l * LANES
        half_g = 0.5 * g_ref[:, lo:lo + LANES]
        half_b = 0.5 * b_ref[:, lo:lo + LANES]
        outs.append(_silu_of_twice(c * inv * half_g + half_b))
    return outs


def _gate_block_diag(w_blocks):
    n = (LRU_BLOCKS // 2) * LRU_BLOCK
    stacked = _bf16(w_blocks.reshape(n, LRU_BLOCK))
    src = lax.broadcasted_iota(jnp.int32, (LRU_BLOCK, n), 0)
    dst = lax.broadcasted_iota(jnp.int32, (LRU_BLOCK, n), 1)
    spread = _bf16(((dst & (LRU_BLOCK - 1)) == src).astype(jnp.float32))
    rows = _head_of(lax.broadcasted_iota(jnp.int32, (n, n), 0))
    cols = _head_of(lax.broadcasted_iota(jnp.int32, (n, n), 1))
    return _bf16(_dot(stacked, spread) * (rows == cols).astype(jnp.float32))


def _prompt_kernel(x_ref, mem_ref, g_mem_ref, wk_ref, wv_ref,
                   g_norm_ref, w_in_ref, w_lc_ref, b_lc_ref, wga_ref, wgx_ref,
                   b_ga_ref, b_gx_ref, lam_ref, w_dw_ref, b_dw_ref, ln_g_ref, ln_b_ref,
                   w_pw_ref, b_pw_ref, w_out_ref, g_final_ref,
                   y_ref, kt_ref, vt_ref, h_out_ref, lb_out_ref, cb_out_ref,
                   win_bf, wout_bf, wpw_bf, wg_bf,
                   z_s, xc_s, xl_s, pre_s, a_s, u_s, uc_s, ln_s, mix_s, hin_s, kst_s, vst_s, sc_s, p_s):
    b = pl.program_id(0)
    t = pl.program_id(1)
    n_t = pl.num_programs(1)
    tile = PROMPT_TILE
    seg = PROMPT_SEG
    n_seg = tile // seg
    pitch = PROMPT_PITCH

    @pl.when(jnp.logical_and(b == 0, t == 0))
    def _():
        win_bf[...] = _bf16(w_in_ref[...] * _as_column(g_norm_ref[...]) * _w_in_col_scale())
        wout_bf[...] = _bf16(w_out_ref[...])
        wpw_bf[...] = _bf16(w_pw_ref[...])
        half = LRU_BLOCKS // 2
        for h in range(2):
            wg_bf[h, :, 0:GATE_HALF] = _gate_block_diag(wga_ref[h * half:(h + 1) * half])
            wg_bf[h, :, GATE_HALF:D_LRU] = _gate_block_diag(wgx_ref[h * half:(h + 1) * half])

    @pl.when(t == 0)
    def _():
        xc_s[:, 0:LRU_HALO, :] = jnp.zeros((N_LRU_SLABS, LRU_HALO, LANES), jnp.float32)
        uc_s[:, 0:CONV_HALO, :] = jnp.zeros((N_CONV_SLABS, CONV_HALO, LANES), jnp.float32)
        hin_s[:, 0:1, :] = jnp.zeros((N_LRU_SLABS, 1, LANES), jnp.float32)
        m = _bf16(_rmsnorm(mem_ref[0], g_mem_ref[...]))
        kt = _dot(m, _bf16(wk_ref[...])).T
        vt = _dot(m, _bf16(wv_ref[...])).T
        kt_ref[0] = kt
        vt_ref[0] = vt
        _head_stack_t(kt, vt, kst_s, vst_s)

    for r0 in range(0, tile, PROJ_ROWS):
        z_s[r0:r0 + PROJ_ROWS, :] = _dot(_bf16(_rms_scaled(x_ref[0, r0:r0 + PROJ_ROWS, :])), win_bf[...])

    for l in range(N_LRU_SLABS):
        lo = l * LANES
        xc_s[l, LRU_HALO:LRU_HALO + tile, :] = z_s[:, OFF_LRU_IN + lo:OFF_LRU_IN + lo + LANES]
    for l in range(N_LRU_SLABS):
        lo = l * LANES
        w_row = lambda k, lo=lo: 0.5 * w_lc_ref[k:k + 1, lo:lo + LANES]
        xl = (_conv_taps(xc_s, l, w_row, 0, tile, LRU_CONV, LRU_HALO)
              + 0.5 * b_lc_ref[:, lo:lo + LANES])
        xl_s[:, lo:lo + LANES] = xl
    _lru_gates(_bf16(xl_s[...]), wg_bf, pre_s)

    hc_ls = (0.5 * LRU_C) * _log_sigmoid(lam_ref[...])
    hb_ga = 0.5 * b_ga_ref[...]
    hb_gx = 0.5 * b_gx_ref[...]

    def coeff_body(c, carry):
        r0 = _aligned(c * seg, seg)
        p0 = _aligned(c * pitch, SUBLANES)
        for l in range(N_LRU_SLABS):
            lo = l * LANES
            ca, cx = _pre_cols(l)
            a, u = _lru_coeffs(pre_s[pl.ds(r0, seg), ca:ca + LANES],
                               pre_s[pl.ds(r0, seg), cx:cx + LANES],
                               xl_s[pl.ds(r0, seg), lo:lo + LANES],
                               hb_ga[:, lo:lo + LANES], hb_gx[:, lo:lo + LANES],
                               hc_ls[:, lo:lo + LANES])
            a_s[l, pl.ds(p0, seg), :] = a
            u_s[l, pl.ds(p0, seg), :] = u
        return carry

    _unrolled(n_seg, coeff_body, 0)

    def tot_body(j, carry):
        new = []
        for l in range(N_LRU_SLABS):
            a_tot, u_tot = carry[l]
            a_j = a_s[l, pl.ds(j, n_seg, stride=pitch), :]
            u_j = u_s[l, pl.ds(j, n_seg, stride=pitch), :]
            new.append((a_j * a_tot, a_j * u_tot + u_j))
        return tuple(new)

    init = tuple((jnp.ones((n_seg, LANES), jnp.float32), jnp.zeros((n_seg, LANES), jnp.float32))
                 for _ in range(N_LRU_SLABS))
    totals = _unrolled(seg, tot_body, init)

    h_in = []
    for l in range(N_LRU_SLABS):
        a_tot, u_tot = totals[l]
        h = hin_s[l, 0:1, :]
        for c in range(n_seg):
            h = a_tot[c:c + 1, :] * h + u_tot[c:c + 1, :]
            hin_s[l, c + 1:c + 2, :] = h
        h_in.append(hin_s[l, 0:n_seg, :])
        hin_s[l, 0:1, :] = h

    def scan_body(j, carry):
        new = []
        for l in range(N_LRU_SLABS):
            a_j = a_s[l, pl.ds(j, n_seg, stride=pitch), :]
            u_j = u_s[l, pl.ds(j, n_seg, stride=pitch), :]
            h = a_j * carry[l] + u_j
            u_s[l, pl.ds(j, n_seg, stride=pitch), :] = h
            new.append(h)
        return tuple(new)

    scores = _dot(_bf16(z_s[:, OFF_Q:OFF_Q + D_XATTN]), kst_s[...])
    sc_s[...] = scores
    after_scores = _zero_row_after(scores[tile - SUBLANES:tile, N_XHEADS * N_MEM - LANES:])
    _unrolled(seg, scan_body, tuple(h + after_scores for h in h_in))

    for l in range(N_CONV_SLABS):
        lo = l * LANES
        uc_s[l, CONV_HALO:CONV_HALO + tile, :] = _gated_of_twice(
            z_s[:, OFF_GLU_A + lo:OFF_GLU_A + lo + LANES],
            z_s[:, OFF_GLU_B + lo:OFF_GLU_B + lo + LANES])

    def seg_body(c, carry):
        r0 = _aligned(c * seg, seg)
        p0 = _aligned(c * pitch, SUBLANES)
        for l in range(N_LRU_SLABS):
            lo = l * LANES
            gate = z_s[pl.ds(r0, seg), OFF_LRU_GATE + lo:OFF_LRU_GATE + lo + LANES]
            mix_s[pl.ds(r0, seg), lo:lo + LANES] = _bf16(u_s[l, pl.ds(p0, seg), :] * _silu_of_twice(gate))
        for piece in range(seg // CONV_ROWS):
            rr = r0 + piece * CONV_ROWS
            ys = []
            for l in range(N_CONV_SLABS):
                lo = l * LANES
                w_row = lambda k, lo=lo: w_dw_ref[k, :, lo:lo + LANES]
                ys.append(_conv_taps(uc_s, l, w_row, rr, CONV_ROWS, CONV_W, CONV_HALO)
                          + b_dw_ref[:, lo:lo + LANES])
            outs = _layernorm_silu_slabs(ys, ln_g_ref, ln_b_ref)
            for l in range(N_CONV_SLABS):
                lo = l * LANES
                ln_s[pl.ds(rr, CONV_ROWS), lo:lo + LANES] = _bf16(outs[l])
        return carry

    _unrolled(n_seg, seg_body, 0)

    yc = (_dot(ln_s[...], wpw_bf[...]) + b_pw_ref[...]) * _silu_of_twice(
        z_s[:, OFF_CONV_GATE:OFF_CONV_GATE + D_CONV])
    mix_s[:, D_LRU:D_LRU + D_CONV] = _bf16(yc)

    def attn_body(c, carry):
        r0 = _aligned(c * ATTN_ROWS, ATTN_ROWS)
        p_s[pl.ds(r0, ATTN_ROWS), :] = _softmax_heads(sc_s[pl.ds(r0, ATTN_ROWS), :])
        return carry

    _unrolled(tile // ATTN_ROWS, attn_body, 0)
    o = _dot_nt(p_s[...], vst_s[...])
    mix_s[:, D_LRU + D_CONV:D_MIX] = _bf16(
        o * _silu_of_twice(z_s[:, OFF_ATTN_GATE:OFF_ATTN_GATE + D_XATTN]))

    y = x_ref[0] + _dot(mix_s[...], wout_bf[...])
    y_ref[0] = _rmsnorm(y, g_final_ref[...])

    for l in range(N_LRU_SLABS):
        xc_s[l, LRU_HALO - (LRU_CONV - 1):LRU_HALO, :] = (
            xc_s[l, LRU_HALO + tile - (LRU_CONV - 1):LRU_HALO + tile, :])
    for l in range(N_CONV_SLABS):
        uc_s[l, CONV_HALO - (CONV_W - 1):CONV_HALO, :] = (
            uc_s[l, CONV_HALO + tile - (CONV_W - 1):CONV_HALO + tile, :])

    def write_state(bb):
        for l in range(N_LRU_SLABS):
            lo = l * LANES
            h_out_ref[bb:bb + 1, lo:lo + LANES] = hin_s[l, 0:1, :]
            for k in range(LRU_CONV - 1):
                r = LRU_HALO - (LRU_CONV - 1) + k
                lb_out_ref[k, bb:bb + 1, lo:lo + LANES] = xc_s[l, r:r + 1, :]
        for l in range(N_CONV_SLABS):
            lo = l * LANES
            for k in range(CONV_W - 1):
                r = CONV_HALO - (CONV_W - 1) + k
                cb_out_ref[k, bb:bb + 1, lo:lo + LANES] = uc_s[l, r:r + 1, :]

    for bb in range(h_out_ref.shape[0]):
        pl.when(jnp.logical_and(t == n_t - 1, b == bb))(functools.partial(write_state, bb))


def _whole(shape):
    nd = len(shape)
    return pl.BlockSpec(shape, lambda *_: (0,) * nd, pipeline_mode=pl.Buffered(1))


def _prompt(x, mem, w):
    bsz, seq, _ = x.shape
    tile = PROMPT_TILE
    n_t = seq // tile
    n_seg = tile // PROMPT_SEG
    weights = [w["g_mem"], w["w_mem_k"], w["w_mem_v"],
               w["g_norm"], w["w_in"], w["w_lc"], w["b_lc"], w["w_ga"], w["w_gx"], w["b_ga"], w["b_gx"],
               w["lam"], w["w_dw"], w["b_dw"], w["ln_g"], w["ln_b"], w["w_pw"], w["b_pw"],
               w["w_out"], w["g_final"]]
    weight_specs = [
        _whole((1, D_MODEL)), _whole((None, D_MODEL, D_XATTN)), _whole((None, D_MODEL, D_XATTN)),
        _whole((1, D_MODEL)), _whole((None, D_MODEL, D_IN)), _whole((None, LRU_CONV, D_LRU)),
        _whole((1, D_LRU)), _whole((None, LRU_BLOCKS, LRU_BLOCK, LRU_BLOCK)),
        _whole((None, LRU_BLOCKS, LRU_BLOCK, LRU_BLOCK)), _whole((1, D_LRU)), _whole((1, D_LRU)),
        _whole((1, D_LRU)), _whole((CONV_W, 1, D_CONV)), _whole((1, D_CONV)), _whole((1, D_CONV)),
        _whole((1, D_CONV)), _whole((None, D_CONV, D_CONV)), _whole((1, D_CONV)),
        _whole((None, D_MIX, D_MODEL)), _whole((1, D_MODEL)),
    ]
    in_specs = [
        pl.BlockSpec((1, tile, D_MODEL), lambda b, t: (b, t, 0)),
        pl.BlockSpec((1, N_MEM, D_MODEL), lambda b, t: (b, 0, 0)),
    ] + weight_specs
    out_specs = [
        pl.BlockSpec((1, tile, D_MODEL), lambda b, t: (b, t, 0)),
        pl.BlockSpec((1, D_XATTN, N_MEM), lambda b, t: (b, 0, 0)),
        pl.BlockSpec((1, D_XATTN, N_MEM), lambda b, t: (b, 0, 0)),
        _whole((bsz, D_LRU)),
        _whole((LRU_CONV - 1, bsz, D_LRU)),
        _whole((CONV_W - 1, bsz, D_CONV)),
        _whole((D_MODEL, D_IN)), _whole((D_MIX, D_MODEL)), _whole((D_CONV, D_CONV)),
        _whole((2, GATE_HALF, D_LRU)),
    ]
    out_shape = [
        jax.ShapeDtypeStruct((bsz, seq, D_MODEL), jnp.float32),
        jax.ShapeDtypeStruct((bsz, D_XATTN, N_MEM), jnp.float32),
        jax.ShapeDtypeStruct((bsz, D_XATTN, N_MEM), jnp.float32),
        jax.ShapeDtypeStruct((bsz, D_LRU), jnp.float32),
        jax.ShapeDtypeStruct((LRU_CONV - 1, bsz, D_LRU), jnp.float32),
        jax.ShapeDtypeStruct((CONV_W - 1, bsz, D_CONV), jnp.float32),
        jax.ShapeDtypeStruct((D_MODEL, D_IN), jnp.bfloat16),
        jax.ShapeDtypeStruct((D_MIX, D_MODEL), jnp.bfloat16),
        jax.ShapeDtypeStruct((D_CONV, D_CONV), jnp.bfloat16),
        jax.ShapeDtypeStruct((2, GATE_HALF, D_LRU), jnp.bfloat16),
    ]
    scratch = [
        pltpu.VMEM((tile, D_IN), jnp.float32),
        pltpu.VMEM((N_LRU_SLABS, LRU_HALO + tile, LANES), jnp.float32),
        pltpu.VMEM((tile, D_LRU), jnp.float32),
        pltpu.VMEM((tile, 2 * D_LRU), jnp.float32),
        pltpu.VMEM((N_LRU_SLABS, n_seg * PROMPT_PITCH, LANES), jnp.float32),
        pltpu.VMEM((N_LRU_SLABS, n_seg * PROMPT_PITCH, LANES), jnp.float32),
        pltpu.VMEM((N_CONV_SLABS, CONV_HALO + tile, LANES), jnp.float32),
        pltpu.VMEM((tile, D_CONV), jnp.bfloat16),
        pltpu.VMEM((tile, D_MIX), jnp.bfloat16),
        pltpu.VMEM((N_LRU_SLABS, 2 * SUBLANES, LANES), jnp.float32),
        pltpu.VMEM((D_XATTN, N_XHEADS * N_MEM), jnp.bfloat16),
        pltpu.VMEM((D_XATTN, N_XHEADS * N_MEM), jnp.bfloat16),
        pltpu.VMEM((tile, N_XHEADS * N_MEM), jnp.float32),
        pltpu.VMEM((tile, N_XHEADS * N_MEM), jnp.bfloat16),
    ]
    return pl.pallas_call(
        _prompt_kernel,
        grid=(bsz, n_t),
        in_specs=in_specs,
        out_specs=out_specs,
        out_shape=out_shape,
        scratch_shapes=scratch,
        compiler_params=pltpu.CompilerParams(
            dimension_semantics=("arbitrary", "arbitrary"),
            vmem_limit_bytes=VMEM_LIMIT_BYTES),
        name="prompt_layer",
    )(x, mem, *weights)


def _sample_kernel(nb, steps,
                   x_ref, h0_ref, lb_ref, cb_ref, kt_hbm, vt_hbm,
                   win_bf, w_lc_ref, b_lc_ref, wg_bf,
                   b_ga_ref, b_gx_ref, lam_ref, w_dw_ref, b_dw_ref, ln_g_ref, ln_b_ref,
                   wpw_bf, b_pw_ref, wout_bf, g_final_ref,
                   y_ref, h_out_ref, lb_out_ref, cb_out_ref,
                   z_s, xc_s, xl_s, pre_s, a_s, u_s, uc_s, ln_s, mix_s, sc_s, p_s,
                   kt_ref, vt_ref, kv_sem):
    rows = nb * steps
    pitch = SAMPLE_PITCH
    cpitch = CONV_HALO + steps
    prow = nb * pitch

    group = pl.program_id(0)
    first = group * nb
    n_groups = kt_ref.shape[0] // nb

    def kv_copies(grp):
        rows_of = pl.ds(grp * nb, nb)
        return (pltpu.make_async_copy(kt_hbm.at[rows_of], kt_ref.at[rows_of], kv_sem.at[2 * grp]),
                pltpu.make_async_copy(vt_hbm.at[rows_of], vt_ref.at[rows_of], kv_sem.at[2 * grp + 1]))

    @pl.when(group == 0)
    def _():
        for grp in range(n_groups):
            for copy in kv_copies(grp):
                copy.start()
        xl_s[prow - LRU_HALO:prow, :] = jnp.zeros((LRU_HALO, D_LRU), jnp.float32)

    x = x_ref[...].reshape(rows, D_MODEL)
    z_s[...] = _dot(_bf16(_rms_scaled(x)), win_bf[...])

    for b in range(nb):
        for l in range(N_LRU_SLABS):
            lo = l * LANES
            zcol = OFF_LRU_IN + lo
            xc_s[l, b * pitch:b * pitch + LRU_HALO, :] = jnp.zeros((LRU_HALO, LANES), jnp.float32)
            for k in range(LRU_CONV - 1):
                r = b * pitch + LRU_HALO - (LRU_CONV - 1) + k
                xc_s[l, r:r + 1, :] = lb_ref[k, b:b + 1, lo:lo + LANES]
                rn = (b + 1) * steps - (LRU_CONV - 1) + k
                lb_out_ref[k, b:b + 1, lo:lo + LANES] = z_s[rn:rn + 1, zcol:zcol + LANES]
            xc_s[l, b * pitch + LRU_HALO:(b + 1) * pitch, :] = (
                z_s[b * steps:(b + 1) * steps, zcol:zcol + LANES])
    n_conv = prow - LRU_HALO
    for l in range(N_LRU_SLABS):
        lo = l * LANES
        w_row = lambda k, lo=lo: 0.5 * w_lc_ref[k:k + 1, lo:lo + LANES]
        xl = (_conv_taps(xc_s, l, w_row, 0, n_conv, LRU_CONV, LRU_HALO)
              + 0.5 * b_lc_ref[:, lo:lo + LANES])
        xl_s[0:n_conv, lo:lo + LANES] = xl
    _lru_gates(_bf16(xl_s[...]), wg_bf, pre_s)

    hc_ls = (0.5 * LRU_C) * _log_sigmoid(lam_ref[...])
    hb_ga = 0.5 * b_ga_ref[...]
    hb_gx = 0.5 * b_gx_ref[...]
    for l in range(N_LRU_SLABS):
        lo = l * LANES
        ca, cx = _pre_cols(l)
        a, u = _lru_coeffs(pre_s[:, ca:ca + LANES], pre_s[:, cx:cx + LANES], xl_s[:, lo:lo + LANES],
                           hb_ga[:, lo:lo + LANES], hb_gx[:, lo:lo + LANES],
                           hc_ls[:, lo:lo + LANES])
        a_s[l, :, :] = a
        u_s[l, :, :] = u

    n_grp = nb // SUBLANES

    def scan_body(j, carry):
        new = []
        for g in range(n_grp):
            for l in range(N_LRU_SLABS):
                base = g * SUBLANES * pitch
                a_j = a_s[l, pl.ds(base + j, SUBLANES, stride=pitch), :]
                u_j = u_s[l, pl.ds(base + j, SUBLANES, stride=pitch), :]
                h = a_j * carry[g * N_LRU_SLABS + l] + u_j
                u_s[l, pl.ds(base + j, SUBLANES, stride=pitch), :] = h
                new.append(h)
        return tuple(new)

    h0 = tuple(h0_ref[g * SUBLANES:(g + 1) * SUBLANES, l * LANES:(l + 1) * LANES]
               for g in range(n_grp) for l in range(N_LRU_SLABS))
    h_fin = _unrolled(steps, scan_body, h0)
    for g in range(n_grp):
        for l in range(N_LRU_SLABS):
            h_out_ref[g * SUBLANES:(g + 1) * SUBLANES, l * LANES:(l + 1) * LANES] = (
                h_fin[g * N_LRU_SLABS + l])

    for b in range(nb):
        for l in range(N_CONV_SLABS):
            lo = l * LANES
            uc = _gated_of_twice(
                z_s[b * steps:(b + 1) * steps, OFF_GLU_A + lo:OFF_GLU_A + lo + LANES],
                z_s[b * steps:(b + 1) * steps, OFF_GLU_B + lo:OFF_GLU_B + lo + LANES])
            uc_s[l, b * cpitch + CONV_HALO:(b + 1) * cpitch, :] = uc
            for k in range(CONV_W - 1):
                r = b * cpitch + CONV_HALO - (CONV_W - 1) + k
                uc_s[l, r:r + 1, :] = cb_ref[k, b:b + 1, lo:lo + LANES]
                rn = (b + 1) * cpitch - (CONV_W - 1) + k
                cb_out_ref[k, b:b + 1, lo:lo + LANES] = uc_s[l, rn:rn + 1, :]

    for b in range(nb):
        r0 = b * cpitch
        o0 = b * steps
        ys = []
        for l in range(N_CONV_SLABS):
            lo = l * LANES
            w_row = lambda k, lo=lo: w_dw_ref[k, :, lo:lo + LANES]
            ys.append(_conv_taps(uc_s, l, w_row, r0, steps, CONV_W, CONV_HALO)
                      + b_dw_ref[:, lo:lo + LANES])
        outs = _layernorm_silu_slabs(ys, ln_g_ref, ln_b_ref)
        for l in range(N_CONV_SLABS):
            lo = l * LANES
            ln_s[o0:o0 + steps, lo:lo + LANES] = _bf16(outs[l])

    yc = (_dot(ln_s[...], wpw_bf[...]) + b_pw_ref[...]) * _silu_of_twice(
        z_s[:, OFF_CONV_GATE:OFF_CONV_GATE + D_CONV])
    mix_s[:, D_LRU:D_LRU + D_CONV] = _bf16(yc)

    masks = _lane_head_masks()
    scale = SCORE_SCALE
    qrows = N_XHEADS * steps
    for copy in kv_copies(group):
        copy.wait()
    for b in range(nb):
        for l in range(N_LRU_SLABS):
            lo = l * LANES
            gate = z_s[b * steps:(b + 1) * steps, OFF_LRU_GATE + lo:OFF_LRU_GATE + lo + LANES]
            mix_s[b * steps:(b + 1) * steps, lo:lo + LANES] = _bf16(
                u_s[l, b * pitch:b * pitch + steps, :] * _silu_of_twice(gate))
    for b in range(nb):
        q = z_s[b * steps:(b + 1) * steps, OFF_Q:OFF_Q + D_XATTN]
        qm = jnp.concatenate([_bf16(q * (m * scale)) for m in masks], axis=0)
        sc_s[b * qrows:(b + 1) * qrows, :] = _dot(qm, _bf16(kt_ref[first + b]))
    s = sc_s[...]
    e = jnp.exp2(s - jnp.max(s, axis=-1, keepdims=True))
    p_s[...] = _bf16(e * (1.0 / jnp.sum(e, axis=-1, keepdims=True)))
    for b in range(nb):
        o_all = _dot_nt(p_s[b * qrows:(b + 1) * qrows, :], _bf16(vt_ref[first + b]))
        o = o_all[0:steps] * masks[0]
        for h in range(1, N_XHEADS):
            o = o + o_all[h * steps:(h + 1) * steps] * masks[h]
        gate = z_s[b * steps:(b + 1) * steps, OFF_ATTN_GATE:OFF_ATTN_GATE + D_XATTN]
        mix_s[b * steps:(b + 1) * steps, D_LRU + D_CONV:D_MIX] = _bf16(o * _silu_of_twice(gate))

    y = x + _dot(mix_s[...], wout_bf[...])
    y_ref[...] = _rmsnorm(y, g_final_ref[...]).reshape(nb, steps, D_MODEL)


def _sample(x, h0, lb, cb, kt, vt, w, win_bf, wout_bf, wpw_bf, wg_bf):
    bsz, steps, _ = x.shape
    nb = SAMPLE_GROUP
    assert bsz % nb == 0 and nb % SUBLANES == 0
    assert steps + LRU_HALO == SAMPLE_PITCH and steps >= CONV_W - 1
    rows = nb * steps
    prow = nb * SAMPLE_PITCH
    weights = [win_bf, w["w_lc"], w["b_lc"], wg_bf, w["b_ga"], w["b_gx"],
               w["lam"], w["w_dw"], w["b_dw"], w["ln_g"], w["ln_b"], wpw_bf, w["b_pw"],
               wout_bf, w["g_final"]]
    weight_specs = [
        _whole((D_MODEL, D_IN)), _whole((None, LRU_CONV, D_LRU)),
        _whole((1, D_LRU)), _whole((2, GATE_HALF, D_LRU)), _whole((1, D_LRU)), _whole((1, D_LRU)),
        _whole((1, D_LRU)), _whole((CONV_W, 1, D_CONV)), _whole((1, D_CONV)), _whole((1, D_CONV)),
        _whole((1, D_CONV)), _whole((D_CONV, D_CONV)), _whole((1, D_CONV)),
        _whole((D_MIX, D_MODEL)), _whole((1, D_MODEL)),
    ]
    in_specs = [
        pl.BlockSpec((nb, steps, D_MODEL), lambda g: (g, 0, 0)),
        pl.BlockSpec((nb, D_LRU), lambda g: (g, 0)),
        pl.BlockSpec((LRU_CONV - 1, nb, D_LRU), lambda g: (0, g, 0)),
        pl.BlockSpec((CONV_W - 1, nb, D_CONV), lambda g: (0, g, 0)),
        pl.BlockSpec(memory_space=pl.ANY),
        pl.BlockSpec(memory_space=pl.ANY),
    ] + weight_specs
    out_specs = [
        pl.BlockSpec((nb, steps, D_MODEL), lambda g: (g, 0, 0)),
        pl.BlockSpec((nb, D_LRU), lambda g: (g, 0)),
        pl.BlockSpec((LRU_CONV - 1, nb, D_LRU), lambda g: (0, g, 0)),
        pl.BlockSpec((CONV_W - 1, nb, D_CONV), lambda g: (0, g, 0)),
    ]
    out_shape = [
        jax.ShapeDtypeStruct((bsz, steps, D_MODEL), jnp.float32),
        jax.ShapeDtypeStruct((bsz, D_LRU), jnp.float32),
        jax.ShapeDtypeStruct((LRU_CONV - 1, bsz, D_LRU), jnp.float32),
        jax.ShapeDtypeStruct((CONV_W - 1, bsz, D_CONV), jnp.float32),
    ]
    scratch = [
        pltpu.VMEM((rows, D_IN), jnp.float32),
        pltpu.VMEM((N_LRU_SLABS, prow, LANES), jnp.float32),
        pltpu.VMEM((prow, D_LRU), jnp.float32),
        pltpu.VMEM((prow, 2 * D_LRU), jnp.float32),
        pltpu.VMEM((N_LRU_SLABS, prow, LANES), jnp.float32),
        pltpu.VMEM((N_LRU_SLABS, prow, LANES), jnp.float32),
        pltpu.VMEM((N_CONV_SLABS, nb * (CONV_HALO + steps), LANES), jnp.float32),
        pltpu.VMEM((rows, D_CONV), jnp.bfloat16),
        pltpu.VMEM((rows, D_MIX), jnp.bfloat16),
        pltpu.VMEM((N_XHEADS * rows, N_MEM), jnp.float32),
        pltpu.VMEM((N_XHEADS * rows, N_MEM), jnp.bfloat16),
        pltpu.VMEM((bsz, D_XATTN, N_MEM), jnp.float32),
        pltpu.VMEM((bsz, D_XATTN, N_MEM), jnp.float32),
        pltpu.SemaphoreType.DMA((2 * (bsz // nb),)),
    ]
    return pl.pallas_call(
        functools.partial(_sample_kernel, nb, steps),
        grid=(bsz // nb,),
        in_specs=in_specs,
        out_specs=out_specs,
        out_shape=out_shape,
        scratch_shapes=scratch,
        compiler_params=pltpu.CompilerParams(
            dimension_semantics=("arbitrary",),
            vmem_limit_bytes=VMEM_LIMIT_BYTES),
        name="sample_layer",
    )(x, h0, lb, cb, kt, vt, *weights)


def _time_major(state):
    return jnp.transpose(state, (1, 0, 2))


def _memory_t(mem):
    bsz = mem.shape[0]
    return jnp.transpose(mem, (0, 2, 3, 1)).reshape(bsz, D_XATTN, N_MEM)


def _memory_from_t(mem_t):
    bsz = mem_t.shape[0]
    return jnp.transpose(mem_t.reshape(bsz, N_XHEADS, XHEAD_DIM, N_MEM), (0, 3, 1, 2))


def kernel(x_prompt, x_sample, state_lru_h, state_lru_conv, state_conv, cache_mem_k, cache_mem_v,
           mem_prompt, g_norm, w_in, w_lru_conv, b_lru_conv, w_gate_a, b_gate_a, w_gate_x, b_gate_x,
           lru_lambda, w_dw, b_dw, ln_g, ln_b, w_pw, b_pw, g_mem, w_mem_k, w_mem_v, w_out, g_final):
    depth = g_norm.shape[0]
    assert depth == 1
    w = {
        "g_norm": g_norm, "w_in": w_in, "w_lc": w_lru_conv, "b_lc": b_lru_conv,
        "w_ga": w_gate_a, "w_gx": w_gate_x, "b_ga": b_gate_a, "b_gx": b_gate_x, "lam": lru_lambda,
        "w_dw": jnp.transpose(w_dw, (1, 0, 2)), "b_dw": b_dw, "ln_g": ln_g, "ln_b": ln_b,
        "w_pw": w_pw, "b_pw": b_pw, "w_out": w_out, "g_final": g_final.reshape(1, D_MODEL),
        "g_mem": g_mem, "w_mem_k": w_mem_k, "w_mem_v": w_mem_v,
    }
    y_prompt, mkt, mvt, ph, plb, pcb, win_bf, wout_bf, wpw_bf, wg_bf = _prompt(x_prompt, mem_prompt, w)

    y_sample, sh, slb, scb = _sample(
        x_sample, state_lru_h[0], _time_major(state_lru_conv[0]), _time_major(state_conv[0]),
        _memory_t(cache_mem_k[0]), _memory_t(cache_mem_v[0]), w, win_bf, wout_bf, wpw_bf, wg_bf)

    return (y_prompt, y_sample,
            ph[None], _time_major(plb)[None], _time_major(pcb)[None],
            _memory_from_t(mkt)[None], _memory_from_t(mvt)[None],
            sh[None], _time_major(slb)[None], _time_major(scb)[None])
```

```python
import functools

import jax
import jax.numpy as jnp
from jax import lax
from jax.experimental import pallas as pl
from jax.experimental.pallas import tpu as pltpu

D_MODEL = 1024
N_MEM = 256
D_LRU = 512
LRU_BLOCKS = 8
LRU_BLOCK = D_LRU // LRU_BLOCKS
LRU_CONV = 4
LRU_C = 8.0
D_CONV = 256
CONV_W = 31
N_XHEADS = 4
XHEAD_DIM = 64
D_XATTN = N_XHEADS * XHEAD_DIM
D_MIX = D_LRU + D_CONV + D_XATTN
D_IN = 2 * D_LRU + 3 * D_CONV + 2 * D_XATTN
EPS = 1e-6

LANES = 128
SUBLANES = 8
N_LRU_SLABS = D_LRU // LANES
N_CONV_SLABS = D_CONV // LANES
GATE_HALF = D_LRU // 2

OFF_LRU_IN = 0
OFF_LRU_GATE = D_LRU
OFF_GLU_A = 2 * D_LRU
OFF_GLU_B = OFF_GLU_A + D_CONV
OFF_CONV_GATE = OFF_GLU_B + D_CONV
OFF_Q = OFF_CONV_GATE + D_CONV
OFF_ATTN_GATE = OFF_Q + D_XATTN

LRU_HALO = 8
CONV_HALO = 32

PROMPT_TILE = 512
PROMPT_SEG = 64
PROMPT_PITCH = 68
SAMPLE_PITCH = 40
SAMPLE_GROUP = 16
ATTN_ROWS = 128
CONV_ROWS = 32
PROJ_ROWS = 256
W_IN_CHUNK = 128
W_COPIES_IN_FLIGHT = 3

VMEM_LIMIT_BYTES = 56 * 1024 * 1024
HALF_LOG2_E = 0.7213475204444817
SCORE_SCALE = XHEAD_DIM ** -0.5 * 2.0 * HALF_LOG2_E


def _silu_of_twice(hx):
    return hx + hx * jnp.tanh(hx)


def _gated_of_twice(ha, hb):
    return ha + ha * jnp.tanh(hb)


def _w_in_col_scale():
    col = lax.broadcasted_iota(jnp.int32, (1, D_IN), 1)
    plain = (col < OFF_LRU_GATE) | ((col >= OFF_Q) & (col < OFF_ATTN_GATE))
    return jnp.where(plain, 1.0, 0.5)


def _sqrt_nonneg(x):
    return jnp.exp2(jnp.log(x) * HALF_LOG2_E)


def _rms_scaled(x):
    ms = jnp.mean(x * x, axis=-1, keepdims=True)
    return x * lax.rsqrt(ms + EPS)


def _rmsnorm(x, g):
    return _rms_scaled(x) * g


def _as_column(row):
    n = row.shape[1] // LANES
    on_diag = (lax.broadcasted_iota(jnp.int32, (LANES, LANES), 0)
               == lax.broadcasted_iota(jnp.int32, (LANES, LANES), 1))
    cols = [jnp.sum(jnp.where(on_diag, row[:, j * LANES:(j + 1) * LANES], 0.0), axis=1, keepdims=True)
            for j in range(n)]
    return jnp.concatenate(cols, axis=0)


def _log_sigmoid(x):
    y = -x
    return -(jnp.maximum(y, 0.0) + jnp.log1p(jnp.exp(-jnp.abs(y))))


def _bf16(x):
    return x.astype(jnp.bfloat16)


def _dot(a, b):
    return jnp.dot(a, b, preferred_element_type=jnp.float32)


def _dot_nt(a, b):
    return lax.dot_general(a, b, (((1,), (1,)), ((), ())), preferred_element_type=jnp.float32)


def _unrolled(n, body, init):
    carry = init
    for i in range(n):
        carry = body(i, carry)
    return carry


def _aligned(x, m):
    return x if isinstance(x, int) else pl.multiple_of(x, m)


def _head_of(index):
    return lax.shift_right_logical(index, XHEAD_DIM.bit_length() - 1)


def _lane_head_masks():
    head = _head_of(lax.broadcasted_iota(jnp.int32, (1, D_XATTN), 1))
    return [(head == h).astype(jnp.float32) for h in range(N_XHEADS)]


def _row_head_masks():
    head = _head_of(lax.broadcasted_iota(jnp.int32, (D_XATTN, 1), 0))
    return [(head == h).astype(jnp.float32) for h in range(N_XHEADS)]


def _head_stack_t(kt, vt, kst_ref, vst_ref):
    scale = SCORE_SCALE
    for h, m in enumerate(_row_head_masks()):
        kst_ref[:, h * N_MEM:(h + 1) * N_MEM] = _bf16(kt * (m * scale))
        vst_ref[:, h * N_MEM:(h + 1) * N_MEM] = _bf16(vt * m)


def _zero_row_after(x):
    bits = lax.bitcast_convert_type(x, jnp.uint32)
    zero = lax.shift_right_logical(lax.shift_right_logical(bits, jnp.uint32(16)), jnp.uint32(16))
    return lax.bitcast_convert_type(zero, jnp.float32)[0:1, :]


def _softmax_heads(s):
    ps = []
    for h in range(N_XHEADS):
        sh = s[:, h * N_MEM:(h + 1) * N_MEM]
        e = jnp.exp2(sh - jnp.max(sh, axis=-1, keepdims=True))
        inv = 1.0 / jnp.sum(e, axis=-1, keepdims=True)
        ps.append(_bf16(e * inv))
    return jnp.concatenate(ps, axis=1)


def _lru_gates(xl_bf, wg_ref, pre_s):
    pre_s[:, 0:D_LRU] = _dot(xl_bf[:, 0:GATE_HALF], wg_ref[0])
    pre_s[:, D_LRU:2 * D_LRU] = _dot(xl_bf[:, GATE_HALF:D_LRU], wg_ref[1])


def _pre_cols(l):
    half_slabs = N_LRU_SLABS // 2
    base = (l // half_slabs) * D_LRU + (l % half_slabs) * LANES
    return base, base + GATE_HALF


def _lru_coeffs(hpre_a, hpre_x, hx, hb_a, hb_x, hc_ls):
    t_r = jnp.tanh(hpre_a + hb_a)
    t_i = jnp.tanh(hpre_x + hb_x)
    log_a = hc_ls + hc_ls * t_r
    a = jnp.exp(log_a)
    one_minus_a2 = -jnp.tanh(log_a) * (a * a + 1.0)
    u = _sqrt_nonneg(one_minus_a2) * (hx + hx * t_i)
    return a, u


def _conv_taps(src_s, l, w_row, row0, nrows, width, halo):
    acc = None
    for k in range(width):
        x_k = src_s[l, pl.ds(row0 + (halo - (width - 1) + k), nrows), :]
        acc = w_row(k) * x_k if acc is None else acc + w_row(k) * x_k
    return acc


def _layernorm_silu_slabs(ys, g_ref, b_ref):
    tot = ys[0]
    for y in ys[1:]:
        tot = tot + y
    mu = jnp.sum(tot, axis=-1, keepdims=True) * (1.0 / D_CONV)
    cs = [y - mu for y in ys]
    sq = cs[0] * cs[0]
    for c in cs[1:]:
        sq = sq + c * c
    var = jnp.sum(sq, axis=-1, keepdims=True) * (1.0 / D_CONV)
    inv = lax.rsqrt(var + EPS)
    outs = []
    for l, c in enumerate(cs):
        lo = l * LANES
        half_g = 0.5 * g_ref[:, lo:lo + LANES]
        half_b = 0.5 * b_ref[:, lo:lo + LANES]
        outs.append(_silu_of_twice(c * inv * half_g + half_b))
    return outs


def _gate_block_diag(w_blocks):
    n = (LRU_BLOCKS // 2) * LRU_BLOCK
    stacked = _bf16(w_blocks.reshape(n, LRU_BLOCK))
    src = lax.broadcasted_iota(jnp.int32, (LRU_BLOCK, n), 0)
    dst = lax.broadcasted_iota(jnp.int32, (LRU_BLOCK, n), 1)
    spread = _bf16(((dst & (LRU_BLOCK - 1)) == src).astype(jnp.float32))
    rows = _head_of(lax.broadcasted_iota(jnp.int32, (n, n), 0))
    cols = _head_of(lax.broadcasted_iota(jnp.int32, (n, n), 1))
    return _bf16(_dot(stacked, spread) * (rows == cols).astype(jnp.float32))


def _prompt_kernel(x_ref, mem_ref, g_mem_ref, wk_ref, wv_ref,
                   g_norm_ref, w_in_hbm, w_lc_ref, b_lc_ref, wga_hbm, wgx_hbm,
                   b_ga_ref, b_gx_ref, lam_ref, w_dw_ref, b_dw_ref, ln_g_ref, ln_b_ref,
                   w_pw_hbm, b_pw_ref, w_out_hbm, g_final_ref,
                   y_ref, kt_ref, vt_ref, h_out_ref, lb_out_ref, cb_out_ref,
                   win_bf, wout_bf, wpw_bf, wg_bf,
                   z_s, xc_s, xl_s, pre_s, a_s, u_s, uc_s, ln_s, mix_s, hin_s, kst_s, vst_s, sc_s, p_s,
                   w_in_ref, wga_ref, wgx_ref, w_pw_ref, w_out_ref, w_sem):
    b = pl.program_id(0)
    t = pl.program_id(1)
    n_t = pl.num_programs(1)
    tile = PROMPT_TILE
    seg = PROMPT_SEG
    n_seg = tile // seg
    pitch = PROMPT_PITCH
    first_step = jnp.logical_and(b == 0, t == 0)

    n_chunks = D_MODEL // W_IN_CHUNK
    w_in_copies = [
        pltpu.make_async_copy(w_in_hbm.at[0, pl.ds(c * W_IN_CHUNK, W_IN_CHUNK)],
                              w_in_ref.at[pl.ds(c * W_IN_CHUNK, W_IN_CHUNK)], w_sem.at[c])
        for c in range(n_chunks)]
    late_copies = [
        pltpu.make_async_copy(src.at[0], dst, w_sem.at[n_chunks + i])
        for i, (src, dst) in enumerate([(wga_hbm, wga_ref), (wgx_hbm, wgx_ref),
                                        (w_pw_hbm, w_pw_ref), (w_out_hbm, w_out_ref)])]

    copies = w_in_copies + late_copies

    @pl.when(first_step)
    def _():
        for copy in copies[:W_COPIES_IN_FLIGHT]:
            copy.start()

    @pl.when(t == 0)
    def _():
        xc_s[:, 0:LRU_HALO, :] = jnp.zeros((N_LRU_SLABS, LRU_HALO, LANES), jnp.float32)
        uc_s[:, 0:CONV_HALO, :] = jnp.zeros((N_CONV_SLABS, CONV_HALO, LANES), jnp.float32)
        hin_s[:, 0:1, :] = jnp.zeros((N_LRU_SLABS, 1, LANES), jnp.float32)
        m = _bf16(_rmsnorm(mem_ref[0], g_mem_ref[...]))
        kt = _dot(m, _bf16(wk_ref[...])).T
        vt = _dot(m, _bf16(wv_ref[...])).T
        kt_ref[0] = kt
        vt_ref[0] = vt
        _head_stack_t(kt, vt, kst_s, vst_s)

    @pl.when(first_step)
    def _():
        row_gain = _as_column(g_norm_ref[...])
        col_scale = _w_in_col_scale()
        half = LRU_BLOCKS // 2

        def cast_w_in(c):
            rows_c = slice(c * W_IN_CHUNK, (c + 1) * W_IN_CHUNK)
            win_bf[rows_c, :] = _bf16(w_in_ref[rows_c, :] * row_gain[rows_c] * col_scale)

        def cast_gates(src_ref, lane0):
            for h in range(2):
                wg_bf[h, :, lane0:lane0 + GATE_HALF] = _gate_block_diag(src_ref[h * half:(h + 1) * half])

        def cast_w_pw():
            wpw_bf[...] = _bf16(w_pw_ref[...])

        def cast_w_out():
            wout_bf[...] = _bf16(w_out_ref[...])

        casts = [functools.partial(cast_w_in, c) for c in range(n_chunks)] + [
            functools.partial(cast_gates, wga_ref, 0), functools.partial(cast_gates, wgx_ref, GATE_HALF),
            cast_w_pw, cast_w_out]
        for i, (copy, cast) in enumerate(zip(copies, casts)):
            copy.wait()
            if i + W_COPIES_IN_FLIGHT < len(copies):
                copies[i + W_COPIES_IN_FLIGHT].start()
            cast()

    for r0 in range(0, tile, PROJ_ROWS):
        z_s[r0:r0 + PROJ_ROWS, :] = _dot(_bf16(_rms_scaled(x_ref[0, r0:r0 + PROJ_ROWS, :])), win_bf[...])

    for l in range(N_LRU_SLABS):
        lo = l * LANES
        xc_s[l, LRU_HALO:LRU_HALO + tile, :] = z_s[:, OFF_LRU_IN + lo:OFF_LRU_IN + lo + LANES]
    for l in range(N_LRU_SLABS):
        lo = l * LANES
        w_row = lambda k, lo=lo: 0.5 * w_lc_ref[k:k + 1, lo:lo + LANES]
        xl = (_conv_taps(xc_s, l, w_row, 0, tile, LRU_CONV, LRU_HALO)
              + 0.5 * b_lc_ref[:, lo:lo + LANES])
        xl_s[:, lo:lo + LANES] = xl
    _lru_gates(_bf16(xl_s[...]), wg_bf, pre_s)

    hc_ls = (0.5 * LRU_C) * _log_sigmoid(lam_ref[...])
    hb_ga = 0.5 * b_ga_ref[...]
    hb_gx = 0.5 * b_gx_ref[...]

    def coeff_body(c, carry):
        r0 = _aligned(c * seg, seg)
        p0 = _aligned(c * pitch, SUBLANES)
        for l in range(N_LRU_SLABS):
            lo = l * LANES
            ca, cx = _pre_cols(l)
            a, u = _lru_coeffs(pre_s[pl.ds(r0, seg), ca:ca + LANES],
                               pre_s[pl.ds(r0, seg), cx:cx + LANES],
                               xl_s[pl.ds(r0, seg), lo:lo + LANES],
                               hb_ga[:, lo:lo + LANES], hb_gx[:, lo:lo + LANES],
                               hc_ls[:, lo:lo + LANES])
            a_s[l, pl.ds(p0, seg), :] = a
            u_s[l, pl.ds(p0, seg), :] = u
        return carry

    _unrolled(n_seg, coeff_body, 0)

    def tot_body(j, carry):
        new = []
        for l in range(N_LRU_SLABS):
            a_tot, u_tot = carry[l]
            a_j = a_s[l, pl.ds(j, n_seg, stride=pitch), :]
            u_j = u_s[l, pl.ds(j, n_seg, stride=pitch), :]
            new.append((a_j * a_tot, a_j * u_tot + u_j))
        return tuple(new)

    init = tuple((jnp.ones((n_seg, LANES), jnp.float32), jnp.zeros((n_seg, LANES), jnp.float32))
                 for _ in range(N_LRU_SLABS))
    totals = _unrolled(seg, tot_body, init)

    h_in = []
    for l in range(N_LRU_SLABS):
        a_tot, u_tot = totals[l]
        h = hin_s[l, 0:1, :]
        for c in range(n_seg):
            h = a_tot[c:c + 1, :] * h + u_tot[c:c + 1, :]
            hin_s[l, c + 1:c + 2, :] = h
        h_in.append(hin_s[l, 0:n_seg, :])
        hin_s[l, 0:1, :] = h

    def scan_body(j, carry):
        new = []
        for l in range(N_LRU_SLABS):
            a_j = a_s[l, pl.ds(j, n_seg, stride=pitch), :]
            u_j = u_s[l, pl.ds(j, n_seg, stride=pitch), :]
            h = a_j * carry[l] + u_j
            u_s[l, pl.ds(j, n_seg, stride=pitch), :] = h
            new.append(h)
        return tuple(new)

    scores = _dot(_bf16(z_s[:, OFF_Q:OFF_Q + D_XATTN]), kst_s[...])
    sc_s[...] = scores
    after_scores = _zero_row_after(scores[tile - SUBLANES:tile, N_XHEADS * N_MEM - LANES:])
    _unrolled(seg, scan_body, tuple(h + after_scores for h in h_in))

    for l in range(N_CONV_SLABS):
        lo = l * LANES
        uc_s[l, CONV_HALO:CONV_HALO + tile, :] = _gated_of_twice(
            z_s[:, OFF_GLU_A + lo:OFF_GLU_A + lo + LANES],
            z_s[:, OFF_GLU_B + lo:OFF_GLU_B + lo + LANES])

    def seg_body(c, carry):
        r0 = _aligned(c * seg, seg)
        p0 = _aligned(c * pitch, SUBLANES)
        for l in range(N_LRU_SLABS):
            lo = l * LANES
            gate = z_s[pl.ds(r0, seg), OFF_LRU_GATE + lo:OFF_LRU_GATE + lo + LANES]
            mix_s[pl.ds(r0, seg), lo:lo + LANES] = _bf16(u_s[l, pl.ds(p0, seg), :] * _silu_of_twice(gate))
        for piece in range(seg // CONV_ROWS):
            rr = r0 + piece * CONV_ROWS
            ys = []
            for l in range(N_CONV_SLABS):
                lo = l * LANES
                w_row = lambda k, lo=lo: w_dw_ref[k, :, lo:lo + LANES]
                ys.append(_conv_taps(uc_s, l, w_row, rr, CONV_ROWS, CONV_W, CONV_HALO)
                          + b_dw_ref[:, lo:lo + LANES])
            outs = _layernorm_silu_slabs(ys, ln_g_ref, ln_b_ref)
            for l in range(N_CONV_SLABS):
                lo = l * LANES
                ln_s[pl.ds(rr, CONV_ROWS), lo:lo + LANES] = _bf16(outs[l])
        return carry

    _unrolled(n_seg, seg_body, 0)

    yc = (_dot(ln_s[...], wpw_bf[...]) + b_pw_ref[...]) * _silu_of_twice(
        z_s[:, OFF_CONV_GATE:OFF_CONV_GATE + D_CONV])
    mix_s[:, D_LRU:D_LRU + D_CONV] = _bf16(yc)

    def attn_body(c, carry):
        r0 = _aligned(c * ATTN_ROWS, ATTN_ROWS)
        p_s[pl.ds(r0, ATTN_ROWS), :] = _softmax_heads(sc_s[pl.ds(r0, ATTN_ROWS), :])
        return carry

    _unrolled(tile // ATTN_ROWS, attn_body, 0)
    o = _dot_nt(p_s[...], vst_s[...])
    mix_s[:, D_LRU + D_CONV:D_MIX] = _bf16(
        o * _silu_of_twice(z_s[:, OFF_ATTN_GATE:OFF_ATTN_GATE + D_XATTN]))

    y = x_ref[0] + _dot(mix_s[...], wout_bf[...])
    y_ref[0] = _rmsnorm(y, g_final_ref[...])

    for l in range(N_LRU_SLABS):
        xc_s[l, LRU_HALO - (LRU_CONV - 1):LRU_HALO, :] = (
            xc_s[l, LRU_HALO + tile - (LRU_CONV - 1):LRU_HALO + tile, :])
    for l in range(N_CONV_SLABS):
        uc_s[l, CONV_HALO - (CONV_W - 1):CONV_HALO, :] = (
            uc_s[l, CONV_HALO + tile - (CONV_W - 1):CONV_HALO + tile, :])

    def write_state(bb):
        for l in range(N_LRU_SLABS):
            lo = l * LANES
            h_out_ref[bb:bb + 1, lo:lo + LANES] = hin_s[l, 0:1, :]
            for k in range(LRU_CONV - 1):
                r = LRU_HALO - (LRU_CONV - 1) + k
                lb_out_ref[k, bb:bb + 1, lo:lo + LANES] = xc_s[l, r:r + 1, :]
        for l in range(N_CONV_SLABS):
            lo = l * LANES
            for k in range(CONV_W - 1):
                r = CONV_HALO - (CONV_W - 1) + k
                cb_out_ref[k, bb:bb + 1, lo:lo + LANES] = uc_s[l, r:r + 1, :]

    for bb in range(h_out_ref.shape[0]):
        pl.when(jnp.logical_and(t == n_t - 1, b == bb))(functools.partial(write_state, bb))


def _whole(shape):
    nd = len(shape)
    return pl.BlockSpec(shape, lambda *_: (0,) * nd, pipeline_mode=pl.Buffered(1))


def _prompt(x, mem, w):
    bsz, seq, _ = x.shape
    tile = PROMPT_TILE
    n_t = seq // tile
    n_seg = tile // PROMPT_SEG
    weights = [w["g_mem"], w["w_mem_k"], w["w_mem_v"],
               w["g_norm"], w["w_in"], w["w_lc"], w["b_lc"], w["w_ga"], w["w_gx"], w["b_ga"], w["b_gx"],
               w["lam"], w["w_dw"], w["b_dw"], w["ln_g"], w["ln_b"], w["w_pw"], w["b_pw"],
               w["w_out"], w["g_final"]]
    in_hbm = pl.BlockSpec(memory_space=pl.ANY)
    weight_specs = [
        _whole((1, D_MODEL)), _whole((None, D_MODEL, D_XATTN)), _whole((None, D_MODEL, D_XATTN)),
        _whole((1, D_MODEL)), in_hbm, _whole((None, LRU_CONV, D_LRU)),
        _whole((1, D_LRU)), in_hbm,
        in_hbm, _whole((1, D_LRU)), _whole((1, D_LRU)),
        _whole((1, D_LRU)), _whole((CONV_W, 1, D_CONV)), _whole((1, D_CONV)), _whole((1, D_CONV)),
        _whole((1, D_CONV)), in_hbm, _whole((1, D_CONV)),
        in_hbm, _whole((1, D_MODEL)),
    ]
    in_specs = [
        pl.BlockSpec((1, tile, D_MODEL), lambda b, t: (b, t, 0)),
        pl.BlockSpec((1, N_MEM, D_MODEL), lambda b, t: (b, 0, 0)),
    ] + weight_specs
    out_specs = [
        pl.BlockSpec((1, tile, D_MODEL), lambda b, t: (b, t, 0)),
        pl.BlockSpec((1, D_XATTN, N_MEM), lambda b, t: (b, 0, 0)),
        pl.BlockSpec((1, D_XATTN, N_MEM), lambda b, t: (b, 0, 0)),
        _whole((bsz, D_LRU)),
        _whole((LRU_CONV - 1, bsz, D_LRU)),
        _whole((CONV_W - 1, bsz, D_CONV)),
        _whole((D_MODEL, D_IN)), _whole((D_MIX, D_MODEL)), _whole((D_CONV, D_CONV)),
        _whole((2, GATE_HALF, D_LRU)),
    ]
    out_shape = [
        jax.ShapeDtypeStruct((bsz, seq, D_MODEL), jnp.float32),
        jax.ShapeDtypeStruct((bsz, D_XATTN, N_MEM), jnp.float32),
        jax.ShapeDtypeStruct((bsz, D_XATTN, N_MEM), jnp.float32),
        jax.ShapeDtypeStruct((bsz, D_LRU), jnp.float32),
        jax.ShapeDtypeStruct((LRU_CONV - 1, bsz, D_LRU), jnp.float32),
        jax.ShapeDtypeStruct((CONV_W - 1, bsz, D_CONV), jnp.float32),
        jax.ShapeDtypeStruct((D_MODEL, D_IN), jnp.bfloat16),
        jax.ShapeDtypeStruct((D_MIX, D_MODEL), jnp.bfloat16),
        jax.ShapeDtypeStruct((D_CONV, D_CONV), jnp.bfloat16),
        jax.ShapeDtypeStruct((2, GATE_HALF, D_LRU), jnp.bfloat16),
    ]
    scratch = [
        pltpu.VMEM((tile, D_IN), jnp.float32),
        pltpu.VMEM((N_LRU_SLABS, LRU_HALO + tile, LANES), jnp.float32),
        pltpu.VMEM((tile, D_LRU), jnp.float32),
        pltpu.VMEM((tile, 2 * D_LRU), jnp.float32),
        pltpu.VMEM((N_LRU_SLABS, n_seg * PROMPT_PITCH, LANES), jnp.float32),
        pltpu.VMEM((N_LRU_SLABS, n_seg * PROMPT_PITCH, LANES), jnp.float32),
        pltpu.VMEM((N_CONV_SLABS, CONV_HALO + tile, LANES), jnp.float32),
        pltpu.VMEM((tile, D_CONV), jnp.bfloat16),
        pltpu.VMEM((tile, D_MIX), jnp.bfloat16),
        pltpu.VMEM((N_LRU_SLABS, 2 * SUBLANES, LANES), jnp.float32),
        pltpu.VMEM((D_XATTN, N_XHEADS * N_MEM), jnp.bfloat16),
        pltpu.VMEM((D_XATTN, N_XHEADS * N_MEM), jnp.bfloat16),
        pltpu.VMEM((tile, N_XHEADS * N_MEM), jnp.float32),
        pltpu.VMEM((tile, N_XHEADS * N_MEM), jnp.bfloat16),
        pltpu.VMEM((D_MODEL, D_IN), jnp.float32),
        pltpu.VMEM((LRU_BLOCKS, LRU_BLOCK, LRU_BLOCK), jnp.float32),
        pltpu.VMEM((LRU_BLOCKS, LRU_BLOCK, LRU_BLOCK), jnp.float32),
        pltpu.VMEM((D_CONV, D_CONV), jnp.float32),
        pltpu.VMEM((D_MIX, D_MODEL), jnp.float32),
        pltpu.SemaphoreType.DMA((D_MODEL // W_IN_CHUNK + 4,)),
    ]
    return pl.pallas_call(
        _prompt_kernel,
        grid=(bsz, n_t),
        in_specs=in_specs,
        out_specs=out_specs,
        out_shape=out_shape,
        scratch_shapes=scratch,
        compiler_params=pltpu.CompilerParams(
            dimension_semantics=("arbitrary", "arbitrary"),
            vmem_limit_bytes=VMEM_LIMIT_BYTES),
        name="prompt_layer",
    )(x, mem, *weights)


def _sample_kernel(nb, steps,
                   x_ref, h0_ref, lb_ref, cb_ref, kt_hbm, vt_hbm,
                   win_bf, w_lc_ref, b_lc_ref, wg_bf,
                   b_ga_ref, b_gx_ref, lam_ref, w_dw_ref, b_dw_ref, ln_g_ref, ln_b_ref,
                   wpw_bf, b_pw_ref, wout_bf, g_final_ref,
                   y_ref, h_out_ref, lb_out_ref, cb_out_ref,
                   z_s, xc_s, xl_s, pre_s, a_s, u_s, uc_s, ln_s, mix_s, sc_s, p_s,
                   kt_ref, vt_ref, kv_sem):
    rows = nb * steps
    pitch = SAMPLE_PITCH
    cpitch = CONV_HALO + steps
    prow = nb * pitch

    group = pl.program_id(0)
    first = group * nb
    n_groups = kt_ref.shape[0] // nb

    def kv_copies(grp):
        rows_of = pl.ds(grp * nb, nb)
        return (pltpu.make_async_copy(kt_hbm.at[rows_of], kt_ref.at[rows_of], kv_sem.at[2 * grp]),
                pltpu.make_async_copy(vt_hbm.at[rows_of], vt_ref.at[rows_of], kv_sem.at[2 * grp + 1]))

    @pl.when(group == 0)
    def _():
        for grp in range(n_groups):
            for copy in kv_copies(grp):
                copy.start()
        xl_s[prow - LRU_HALO:prow, :] = jnp.zeros((LRU_HALO, D_LRU), jnp.float32)

    x = x_ref[...].reshape(rows, D_MODEL)
    z_s[...] = _dot(_bf16(_rms_scaled(x)), win_bf[...])

    for b in range(nb):
        for l in range(N_LRU_SLABS):
            lo = l * LANES
            zcol = OFF_LRU_IN + lo
            xc_s[l, b * pitch:b * pitch + LRU_HALO, :] = jnp.zeros((LRU_HALO, LANES), jnp.float32)
            for k in range(LRU_CONV - 1):
                r = b * pitch + LRU_HALO - (LRU_CONV - 1) + k
                xc_s[l, r:r + 1, :] = lb_ref[k, b:b + 1, lo:lo + LANES]
                rn = (b + 1) * steps - (LRU_CONV - 1) + k
                lb_out_ref[k, b:b + 1, lo:lo + LANES] = z_s[rn:rn + 1, zcol:zcol + LANES]
            xc_s[l, b * pitch + LRU_HALO:(b + 1) * pitch, :] = (
                z_s[b * steps:(b + 1) * steps, zcol:zcol + LANES])
    n_conv = prow - LRU_HALO
    for l in range(N_LRU_SLABS):
        lo = l * LANES
        w_row = lambda k, lo=lo: 0.5 * w_lc_ref[k:k + 1, lo:lo + LANES]
        xl = (_conv_taps(xc_s, l, w_row, 0, n_conv, LRU_CONV, LRU_HALO)
              + 0.5 * b_lc_ref[:, lo:lo + LANES])
        xl_s[0:n_conv, lo:lo + LANES] = xl
    _lru_gates(_bf16(xl_s[...]), wg_bf, pre_s)

    hc_ls = (0.5 * LRU_C) * _log_sigmoid(lam_ref[...])
    hb_ga = 0.5 * b_ga_ref[...]
    hb_gx = 0.5 * b_gx_ref[...]
    for l in range(N_LRU_SLABS):
        lo = l * LANES
        ca, cx = _pre_cols(l)
        a, u = _lru_coeffs(pre_s[:, ca:ca + LANES], pre_s[:, cx:cx + LANES], xl_s[:, lo:lo + LANES],
                           hb_ga[:, lo:lo + LANES], hb_gx[:, lo:lo + LANES],
                           hc_ls[:, lo:lo + LANES])
        a_s[l, :, :] = a
        u_s[l, :, :] = u

    n_grp = nb // SUBLANES

    def scan_body(j, carry):
        new = []
        for g in range(n_grp):
            for l in range(N_LRU_SLABS):
                base = g * SUBLANES * pitch
                a_j = a_s[l, pl.ds(base + j, SUBLANES, stride=pitch), :]
                u_j = u_s[l, pl.ds(base + j, SUBLANES, stride=pitch), :]
                h = a_j * carry[g * N_LRU_SLABS + l] + u_j
                u_s[l, pl.ds(base + j, SUBLANES, stride=pitch), :] = h
                new.append(h)
        return tuple(new)

    h0 = tuple(h0_ref[g * SUBLANES:(g + 1) * SUBLANES, l * LANES:(l + 1) * LANES]
               for g in range(n_grp) for l in range(N_LRU_SLABS))
    h_fin = _unrolled(steps, scan_body, h0)
    for g in range(n_grp):
        for l in range(N_LRU_SLABS):
            h_out_ref[g * SUBLANES:(g + 1) * SUBLANES, l * LANES:(l + 1) * LANES] = (
                h_fin[g * N_LRU_SLABS + l])

    for b in range(nb):
        for l in range(N_CONV_SLABS):
            lo = l * LANES
            uc = _gated_of_twice(
                z_s[b * steps:(b + 1) * steps, OFF_GLU_A + lo:OFF_GLU_A + lo + LANES],
                z_s[b * steps:(b + 1) * steps, OFF_GLU_B + lo:OFF_GLU_B + lo + LANES])
            uc_s[l, b * cpitch + CONV_HALO:(b + 1) * cpitch, :] = uc
            for k in range(CONV_W - 1):
                r = b * cpitch + CONV_HALO - (CONV_W - 1) + k
                uc_s[l, r:r + 1, :] = cb_ref[k, b:b + 1, lo:lo + LANES]
                rn = (b + 1) * cpitch - (CONV_W - 1) + k
                cb_out_ref[k, b:b + 1, lo:lo + LANES] = uc_s[l, rn:rn + 1, :]

    for b in range(nb):
        r0 = b * cpitch
        o0 = b * steps
        ys = []
        for l in range(N_CONV_SLABS):
            lo = l * LANES
            w_row = lambda k, lo=lo: w_dw_ref[k, :, lo:lo + LANES]
            ys.append(_conv_taps(uc_s, l, w_row, r0, steps, CONV_W, CONV_HALO)
                      + b_dw_ref[:, lo:lo + LANES])
        outs = _layernorm_silu_slabs(ys, ln_g_ref, ln_b_ref)
        for l in range(N_CONV_SLABS):
            lo = l * LANES
            ln_s[o0:o0 + steps, lo:lo + LANES] = _bf16(outs[l])

    yc = (_dot(ln_s[...], wpw_bf[...]) + b_pw_ref[...]) * _silu_of_twice(
        z_s[:, OFF_CONV_GATE:OFF_CONV_GATE + D_CONV])
    mix_s[:, D_LRU:D_LRU + D_CONV] = _bf16(yc)

    masks = _lane_head_masks()
    scale = SCORE_SCALE
    qrows = N_XHEADS * steps
    for copy in kv_copies(group):
        copy.wait()
    for b in range(nb):
        for l in range(N_LRU_SLABS):
            lo = l * LANES
            gate = z_s[b * steps:(b + 1) * steps, OFF_LRU_GATE + lo:OFF_LRU_GATE + lo + LANES]
            mix_s[b * steps:(b + 1) * steps, lo:lo + LANES] = _bf16(
                u_s[l, b * pitch:b * pitch + steps, :] * _silu_of_twice(gate))
    for b in range(nb):
        q = z_s[b * steps:(b + 1) * steps, OFF_Q:OFF_Q + D_XATTN]
        qm = jnp.concatenate([_bf16(q * (m * scale)) for m in masks], axis=0)
        sc_s[b * qrows:(b + 1) * qrows, :] = _dot(qm, _bf16(kt_ref[first + b]))
    s = sc_s[...]
    e = jnp.exp2(s - jnp.max(s, axis=-1, keepdims=True))
    p_s[...] = _bf16(e * (1.0 / jnp.sum(e, axis=-1, keepdims=True)))
    for b in range(nb):
        o_all = _dot_nt(p_s[b * qrows:(b + 1) * qrows, :], _bf16(vt_ref[first + b]))
        o = o_all[0:steps] * masks[0]
        for h in range(1, N_XHEADS):
            o = o + o_all[h * steps:(h + 1) * steps] * masks[h]
        gate = z_s[b * steps:(b + 1) * steps, OFF_ATTN_GATE:OFF_ATTN_GATE + D_XATTN]
        mix_s[b * steps:(b + 1) * steps, D_LRU + D_CONV:D_MIX] = _bf16(o * _silu_of_twice(gate))

    y = x + _dot(mix_s[...], wout_bf[...])
    y_ref[...] = _rmsnorm(y, g_final_ref[...]).reshape(nb, steps, D_MODEL)


def _sample(x, h0, lb, cb, kt, vt, w, win_bf, wout_bf, wpw_bf, wg_bf):
    bsz, steps, _ = x.shape
    nb = SAMPLE_GROUP
    assert bsz % nb == 0 and nb % SUBLANES == 0
    assert steps + LRU_HALO == SAMPLE_PITCH and steps >= CONV_W - 1
    rows = nb * steps
    prow = nb * SAMPLE_PITCH
    weights = [win_bf, w["w_lc"], w["b_lc"], wg_bf, w["b_ga"], w["b_gx"],
               w["lam"], w["w_dw"], w["b_dw"], w["ln_g"], w["ln_b"], wpw_bf, w["b_pw"],
               wout_bf, w["g_final"]]
    weight_specs = [
        _whole((D_MODEL, D_IN)), _whole((None, LRU_CONV, D_LRU)),
        _whole((1, D_LRU)), _whole((2, GATE_HALF, D_LRU)), _whole((1, D_LRU)), _whole((1, D_LRU)),
        _whole((1, D_LRU)), _whole((CONV_W, 1, D_CONV)), _whole((1, D_CONV)), _whole((1, D_CONV)),
        _whole((1, D_CONV)), _whole((D_CONV, D_CONV)), _whole((1, D_CONV)),
        _whole((D_MIX, D_MODEL)), _whole((1, D_MODEL)),
    ]
    in_specs = [
        pl.BlockSpec((nb, steps, D_MODEL), lambda g: (g, 0, 0)),
        pl.BlockSpec((nb, D_LRU), lambda g: (g, 0)),
        pl.BlockSpec((LRU_CONV - 1, nb, D_LRU), lambda g: (0, g, 0)),
        pl.BlockSpec((CONV_W - 1, nb, D_CONV), lambda g: (0, g, 0)),
        pl.BlockSpec(memory_space=pl.ANY),
        pl.BlockSpec(memory_space=pl.ANY),
    ] + weight_specs
    out_specs = [
        pl.BlockSpec((nb, steps, D_MODEL), lambda g: (g, 0, 0)),
        pl.BlockSpec((nb, D_LRU), lambda g: (g, 0)),
        pl.BlockSpec((LRU_CONV - 1, nb, D_LRU), lambda g: (0, g, 0)),
        pl.BlockSpec((CONV_W - 1, nb, D_CONV), lambda g: (0, g, 0)),
    ]
    out_shape = [
        jax.ShapeDtypeStruct((bsz, steps, D_MODEL), jnp.float32),
        jax.ShapeDtypeStruct((bsz, D_LRU), jnp.float32),
        jax.ShapeDtypeStruct((LRU_CONV - 1, bsz, D_LRU), jnp.float32),
        jax.ShapeDtypeStruct((CONV_W - 1, bsz, D_CONV), jnp.float32),
    ]
    scratch = [
        pltpu.VMEM((rows, D_IN), jnp.float32),
        pltpu.VMEM((N_LRU_SLABS, prow, LANES), jnp.float32),
        pltpu.VMEM((prow, D_LRU), jnp.float32),
        pltpu.VMEM((prow, 2 * D_LRU), jnp.float32),
        pltpu.VMEM((N_LRU_SLABS, prow, LANES), jnp.float32),
        pltpu.VMEM((N_LRU_SLABS, prow, LANES), jnp.float32),
        pltpu.VMEM((N_CONV_SLABS, nb * (CONV_HALO + steps), LANES), jnp.float32),
        pltpu.VMEM((rows, D_CONV), jnp.bfloat16),
        pltpu.VMEM((rows, D_MIX), jnp.bfloat16),
        pltpu.VMEM((N_XHEADS * rows, N_MEM), jnp.float32),
        pltpu.VMEM((N_XHEADS * rows, N_MEM), jnp.bfloat16),
        pltpu.VMEM((bsz, D_XATTN, N_MEM), jnp.float32),
        pltpu.VMEM((bsz, D_XATTN, N_MEM), jnp.float32),
        pltpu.SemaphoreType.DMA((2 * (bsz // nb),)),
    ]
    return pl.pallas_call(
        functools.partial(_sample_kernel, nb, steps),
        grid=(bsz // nb,),
        in_specs=in_specs,
        out_specs=out_specs,
        out_shape=out_shape,
        scratch_shapes=scratch,
        compiler_params=pltpu.CompilerParams(
            dimension_semantics=("arbitrary",),
            vmem_limit_bytes=VMEM_LIMIT_BYTES),
        name="sample_layer",
    )(x, h0, lb, cb, kt, vt, *weights)


def _time_major(state):
    return jnp.transpose(state, (1, 0, 2))


def _memory_t(mem):
    bsz = mem.shape[0]
    return jnp.transpose(mem, (0, 2, 3, 1)).reshape(bsz, D_XATTN, N_MEM)


def _memory_from_t(mem_t):
    bsz = mem_t.shape[0]
    return jnp.transpose(mem_t.reshape(bsz, N_XHEADS, XHEAD_DIM, N_MEM), (0, 3, 1, 2))


def kernel(x_prompt, x_sample, state_lru_h, state_lru_conv, state_conv, cache_mem_k, cache_mem_v,
           mem_prompt, g_norm, w_in, w_lru_conv, b_lru_conv, w_gate_a, b_gate_a, w_gate_x, b_gate_x,
           lru_lambda, w_dw, b_dw, ln_g, ln_b, w_pw, b_pw, g_mem, w_mem_k, w_mem_v, w_out, g_final):
    depth = g_norm.shape[0]
    assert depth == 1
    w = {
        "g_norm": g_norm, "w_in": w_in, "w_lc": w_lru_conv, "b_lc": b_lru_conv,
        "w_ga": w_gate_a, "w_gx": w_gate_x, "b_ga": b_gate_a, "b_gx": b_gate_x, "lam": lru_lambda,
        "w_dw": jnp.transpose(w_dw, (1, 0, 2)), "b_dw": b_dw, "ln_g": ln_g, "ln_b": ln_b,
        "w_pw": w_pw, "b_pw": b_pw, "w_out": w_out, "g_final": g_final.reshape(1, D_MODEL),
        "g_mem": g_mem, "w_mem_k": w_mem_k, "w_mem_v": w_mem_v,
    }
    y_prompt, mkt, mvt, ph, plb, pcb, win_bf, wout_bf, wpw_bf, wg_bf = _prompt(x_prompt, mem_prompt, w)

    y_sample, sh, slb, scb = _sample(
        x_sample, state_lru_h[0], _time_major(state_lru_conv[0]), _time_major(state_conv[0]),
        _memory_t(cache_mem_k[0]), _memory_t(cache_mem_v[0]), w, win_bf, wout_bf, wpw_bf, wg_bf)

    return (y_prompt, y_sample,
            ph[None], _time_major(plb)[None], _time_major(pcb)[None],
            _memory_from_t(mkt)[None], _memory_from_t(mvt)[None],
            sh[None], _time_major(slb)[None], _time_major(scb)[None])
```

```python
import functools

import jax
import jax.numpy as jnp
from jax import lax
from jax.experimental import pallas as pl
from jax.experimental.pallas import tpu as pltpu

D_MODEL = 1024
N_MEM = 256
D_LRU = 512
LRU_BLOCKS = 8
LRU_BLOCK = D_LRU // LRU_BLOCKS
LRU_CONV = 4
LRU_C = 8.0
D_CONV = 256
CONV_W = 31
N_XHEADS = 4
XHEAD_DIM = 64
D_XATTN = N_XHEADS * XHEAD_DIM
D_MIX = D_LRU + D_CONV + D_XATTN
D_IN = 2 * D_LRU + 3 * D_CONV + 2 * D_XATTN
EPS = 1e-6

LANES = 128
SUBLANES = 8
N_LRU_SLABS = D_LRU // LANES
N_CONV_SLABS = D_CONV // LANES
GATE_HALF = D_LRU // 2

OFF_LRU_IN = 0
OFF_LRU_GATE = D_LRU
OFF_GLU_A = 2 * D_LRU
OFF_GLU_B = OFF_GLU_A + D_CONV
OFF_CONV_GATE = OFF_GLU_B + D_CONV
OFF_Q = OFF_CONV_GATE + D_CONV
OFF_ATTN_GATE = OFF_Q + D_XATTN

LRU_HALO = 8
CONV_HALO = 32

PROMPT_TILE = 512
PROMPT_SEG = 64
PROMPT_PITCH = 68
SAMPLE_PITCH = 40
SAMPLE_GROUP = 16
ATTN_ROWS = 128
CONV_ROWS = 32
PROJ_ROWS = 256
W_IN_CHUNK = 256
VMEM_LIMIT_BYTES = 56 * 1024 * 1024
HALF_LOG2_E = 0.7213475204444817
SCORE_SCALE = XHEAD_DIM ** -0.5 * 2.0 * HALF_LOG2_E


def _silu_of_twice(hx):
    return hx + hx * jnp.tanh(hx)


def _gated_of_twice(ha, hb):
    return ha + ha * jnp.tanh(hb)


def _w_in_col_scale():
    col = lax.broadcasted_iota(jnp.int32, (1, D_IN), 1)
    plain = (col < OFF_LRU_GATE) | ((col >= OFF_Q) & (col < OFF_ATTN_GATE))
    return jnp.where(plain, 1.0, 0.5)


def _sqrt_nonneg(x):
    return jnp.exp2(jnp.log(x) * HALF_LOG2_E)


def _rms_scaled(x):
    ms = jnp.mean(x * x, axis=-1, keepdims=True)
    return x * lax.rsqrt(ms + EPS)


def _rmsnorm(x, g):
    return _rms_scaled(x) * g


def _as_column(row):
    n = row.shape[1] // LANES
    on_diag = (lax.broadcasted_iota(jnp.int32, (LANES, LANES), 0)
               == lax.broadcasted_iota(jnp.int32, (LANES, LANES), 1))
    cols = [jnp.sum(jnp.where(on_diag, row[:, j * LANES:(j + 1) * LANES], 0.0), axis=1, keepdims=True)
            for j in range(n)]
    return jnp.concatenate(cols, axis=0)


def _log_sigmoid(x):
    y = -x
    return -(jnp.maximum(y, 0.0) + jnp.log1p(jnp.exp(-jnp.abs(y))))


def _bf16(x):
    return x.astype(jnp.bfloat16)


def _dot(a, b):
    return jnp.dot(a, b, preferred_element_type=jnp.float32)


def _dot_nt(a, b):
    return lax.dot_general(a, b, (((1,), (1,)), ((), ())), preferred_element_type=jnp.float32)


def _unrolled(n, body, init):
    carry = init
    for i in range(n):
        carry = body(i, carry)
    return carry


def _aligned(x, m):
    return x if isinstance(x, int) else pl.multiple_of(x, m)


def _head_of(index):
    return lax.shift_right_logical(index, XHEAD_DIM.bit_length() - 1)


def _lane_head_masks():
    head = _head_of(lax.broadcasted_iota(jnp.int32, (1, D_XATTN), 1))
    return [(head == h).astype(jnp.float32) for h in range(N_XHEADS)]


def _row_head_masks():
    head = _head_of(lax.broadcasted_iota(jnp.int32, (D_XATTN, 1), 0))
    return [(head == h).astype(jnp.float32) for h in range(N_XHEADS)]


def _head_stack_t(kt, vt, kst_ref, vst_ref):
    scale = SCORE_SCALE
    for h, m in enumerate(_row_head_masks()):
        kst_ref[:, h * N_MEM:(h + 1) * N_MEM] = _bf16(kt * (m * scale))
        vst_ref[:, h * N_MEM:(h + 1) * N_MEM] = _bf16(vt * m)


def _zero_row_after(x):
    bits = lax.bitcast_convert_type(x, jnp.uint32)
    zero = lax.shift_right_logical(lax.shift_right_logical(bits, jnp.uint32(16)), jnp.uint32(16))
    return lax.bitcast_convert_type(zero, jnp.float32)[0:1, :]


def _softmax_heads(s):
    ps = []
    for h in range(N_XHEADS):
        sh = s[:, h * N_MEM:(h + 1) * N_MEM]
        e = jnp.exp2(sh - jnp.max(sh, axis=-1, keepdims=True))
        inv = 1.0 / jnp.sum(e, axis=-1, keepdims=True)
        ps.append(_bf16(e * inv))
    return jnp.concatenate(ps, axis=1)


def _lru_gates(xl_bf, wg_ref, pre_s):
    pre_s[:, 0:D_LRU] = _dot(xl_bf[:, 0:GATE_HALF], wg_ref[0])
    pre_s[:, D_LRU:2 * D_LRU] = _dot(xl_bf[:, GATE_HALF:D_LRU], wg_ref[1])


def _pre_cols(l):
    half_slabs = N_LRU_SLABS // 2
    base = (l // half_slabs) * D_LRU + (l % half_slabs) * LANES
    return base, base + GATE_HALF


def _lru_coeffs(hpre_a, hpre_x, hx, hb_a, hb_x, hc_ls):
    t_r = jnp.tanh(hpre_a + hb_a)
    t_i = jnp.tanh(hpre_x + hb_x)
    log_a = hc_ls + hc_ls * t_r
    a = jnp.exp(log_a)
    one_minus_a2 = -jnp.tanh(log_a) * (a * a + 1.0)
    u = _sqrt_nonneg(one_minus_a2) * (hx + hx * t_i)
    return a, u


def _conv_taps(src_s, l, w_row, row0, nrows, width, halo):
    acc = None
    for k in range(width):
        x_k = src_s[l, pl.ds(row0 + (halo - (width - 1) + k), nrows), :]
        acc = w_row(k) * x_k if acc is None else acc + w_row(k) * x_k
    return acc


def _layernorm_silu_slabs(ys, g_ref, b_ref):
    tot = ys[0]
    for y in ys[1:]:
        tot = tot + y
    mu = jnp.sum(tot, axis=-1, keepdims=True) * (1.0 / D_CONV)
    cs = [y - mu for y in ys]
    sq = cs[0] * cs[0]
    for c in cs[1:]:
        sq = sq + c * c
    var = jnp.sum(sq, axis=-1, keepdims=True) * (1.0 / D_CONV)
    inv = lax.rsqrt(var + EPS)
    outs = []
    for l, c in enumerate(cs):
        lo = l * LANES
        half_g = 0.5 * g_ref[:, lo:lo + LANES]
        half_b = 0.5 * b_ref[:, lo:lo + LANES]
        outs.append(_silu_of_twice(c * inv * half_g + half_b))
    return outs


def _gate_block_diag(w_blocks):
    n = (LRU_BLOCKS // 2) * LRU_BLOCK
    stacked = _bf16(w_blocks.reshape(n, LRU_BLOCK))
    src = lax.broadcasted_iota(jnp.int32, (LRU_BLOCK, n), 0)
    dst = lax.broadcasted_iota(jnp.int32, (LRU_BLOCK, n), 1)
    spread = _bf16(((dst & (LRU_BLOCK - 1)) == src).astype(jnp.float32))
    rows = _head_of(lax.broadcasted_iota(jnp.int32, (n, n), 0))
    cols = _head_of(lax.broadcasted_iota(jnp.int32, (n, n), 1))
    return _bf16(_dot(stacked, spread) * (rows == cols).astype(jnp.float32))


def _prompt_kernel(x_ref, mem_ref, g_mem_ref, wk_ref, wv_ref,
                   g_norm_ref, w_in_hbm, w_lc_ref, b_lc_ref, wga_hbm, wgx_hbm,
                   b_ga_ref, b_gx_ref, lam_ref, w_dw_ref, b_dw_ref, ln_g_ref, ln_b_ref,
                   w_pw_hbm, b_pw_ref, w_out_hbm, g_final_ref,
                   y_ref, kt_ref, vt_ref, h_out_ref, lb_out_ref, cb_out_ref,
                   win_bf, wout_bf, wpw_bf, wg_bf,
                   z_s, xc_s, xl_s, pre_s, a_s, u_s, uc_s, ln_s, mix_s, hin_s, kst_s, vst_s, sc_s, p_s,
                   w_in_ref, wga_ref, wgx_ref, w_pw_ref, w_out_ref, w_sem):
    b = pl.program_id(0)
    t = pl.program_id(1)
    n_t = pl.num_programs(1)
    tile = PROMPT_TILE
    seg = PROMPT_SEG
    n_seg = tile // seg
    pitch = PROMPT_PITCH
    first_step = jnp.logical_and(b == 0, t == 0)

    n_chunks = D_MODEL // W_IN_CHUNK
    w_in_copies = [
        pltpu.make_async_copy(w_in_hbm.at[0, pl.ds(c * W_IN_CHUNK, W_IN_CHUNK)],
                              w_in_ref.at[pl.ds(c * W_IN_CHUNK, W_IN_CHUNK)], w_sem.at[c])
        for c in range(n_chunks)]
    late_copies = [
        pltpu.make_async_copy(src.at[0], dst, w_sem.at[n_chunks + i])
        for i, (src, dst) in enumerate([(wga_hbm, wga_ref), (wgx_hbm, wgx_ref),
                                        (w_pw_hbm, w_pw_ref), (w_out_hbm, w_out_ref)])]

    copies = w_in_copies + late_copies

    @pl.when(first_step)
    def _():
        for copy in copies:
            copy.start()

    @pl.when(t == 0)
    def _():
        xc_s[:, 0:LRU_HALO, :] = jnp.zeros((N_LRU_SLABS, LRU_HALO, LANES), jnp.float32)
        uc_s[:, 0:CONV_HALO, :] = jnp.zeros((N_CONV_SLABS, CONV_HALO, LANES), jnp.float32)
        hin_s[:, 0:1, :] = jnp.zeros((N_LRU_SLABS, 1, LANES), jnp.float32)
        m = _bf16(_rmsnorm(mem_ref[0], g_mem_ref[...]))
        kt = _dot(m, _bf16(wk_ref[...])).T
        vt = _dot(m, _bf16(wv_ref[...])).T
        kt_ref[0] = kt
        vt_ref[0] = vt
        _head_stack_t(kt, vt, kst_s, vst_s)

    @pl.when(first_step)
    def _():
        row_gain = _as_column(g_norm_ref[...])
        col_scale = _w_in_col_scale()
        half = LRU_BLOCKS // 2

        def cast_w_in(c):
            rows_c = slice(c * W_IN_CHUNK, (c + 1) * W_IN_CHUNK)
            win_bf[rows_c, :] = _bf16(w_in_ref[rows_c, :] * row_gain[rows_c] * col_scale)

        def cast_gates(src_ref, lane0):
            for h in range(2):
                wg_bf[h, :, lane0:lane0 + GATE_HALF] = _gate_block_diag(src_ref[h * half:(h + 1) * half])

        def cast_w_pw():
            wpw_bf[...] = _bf16(w_pw_ref[...])

        def cast_w_out():
            wout_bf[...] = _bf16(w_out_ref[...])

        casts = [functools.partial(cast_w_in, c) for c in range(n_chunks)] + [
            functools.partial(cast_gates, wga_ref, 0), functools.partial(cast_gates, wgx_ref, GATE_HALF),
            cast_w_pw, cast_w_out]
        for copy, cast in zip(copies, casts):
            copy.wait()
            cast()

    for r0 in range(0, tile, PROJ_ROWS):
        z_s[r0:r0 + PROJ_ROWS, :] = _dot(_bf16(_rms_scaled(x_ref[0, r0:r0 + PROJ_ROWS, :])), win_bf[...])

    for l in range(N_LRU_SLABS):
        lo = l * LANES
        xc_s[l, LRU_HALO:LRU_HALO + tile, :] = z_s[:, OFF_LRU_IN + lo:OFF_LRU_IN + lo + LANES]
    for l in range(N_LRU_SLABS):
        lo = l * LANES
        w_row = lambda k, lo=lo: 0.5 * w_lc_ref[k:k + 1, lo:lo + LANES]
        xl = (_conv_taps(xc_s, l, w_row, 0, tile, LRU_CONV, LRU_HALO)
              + 0.5 * b_lc_ref[:, lo:lo + LANES])
        xl_s[:, lo:lo + LANES] = xl
    _lru_gates(_bf16(xl_s[...]), wg_bf, pre_s)

    hc_ls = (0.5 * LRU_C) * _log_sigmoid(lam_ref[...])
    hb_ga = 0.5 * b_ga_ref[...]
    hb_gx = 0.5 * b_gx_ref[...]

    def coeff_body(c, carry):
        r0 = _aligned(c * seg, seg)
        p0 = _aligned(c * pitch, SUBLANES)
        for l in range(N_LRU_SLABS):
            lo = l * LANES
            ca, cx = _pre_cols(l)
            a, u = _lru_coeffs(pre_s[pl.ds(r0, seg), ca:ca + LANES],
                               pre_s[pl.ds(r0, seg), cx:cx + LANES],
                               xl_s[pl.ds(r0, seg), lo:lo + LANES],
                               hb_ga[:, lo:lo + LANES], hb_gx[:, lo:lo + LANES],
                               hc_ls[:, lo:lo + LANES])
            a_s[l, pl.ds(p0, seg), :] = a
            u_s[l, pl.ds(p0, seg), :] = u
        return carry

    _unrolled(n_seg, coeff_body, 0)

    def tot_body(j, carry):
        new = []
        for l in range(N_LRU_SLABS):
            a_tot, u_tot = carry[l]
            a_j = a_s[l, pl.ds(j, n_seg, stride=pitch), :]
            u_j = u_s[l, pl.ds(j, n_seg, stride=pitch), :]
            new.append((a_j * a_tot, a_j * u_tot + u_j))
        return tuple(new)

    init = tuple((jnp.ones((n_seg, LANES), jnp.float32), jnp.zeros((n_seg, LANES), jnp.float32))
                 for _ in range(N_LRU_SLABS))
    totals = _unrolled(seg, tot_body, init)

    h_in = []
    for l in range(N_LRU_SLABS):
        a_tot, u_tot = totals[l]
        h = hin_s[l, 0:1, :]
        for c in range(n_seg):
            h = a_tot[c:c + 1, :] * h + u_tot[c:c + 1, :]
            hin_s[l, c + 1:c + 2, :] = h
        h_in.append(hin_s[l, 0:n_seg, :])
        hin_s[l, 0:1, :] = h

    def scan_body(j, carry):
        new = []
        for l in range(N_LRU_SLABS):
            a_j = a_s[l, pl.ds(j, n_seg, stride=pitch), :]
            u_j = u_s[l, pl.ds(j, n_seg, stride=pitch), :]
            h = a_j * carry[l] + u_j
            u_s[l, pl.ds(j, n_seg, stride=pitch), :] = h
            new.append(h)
        return tuple(new)

    scores = _dot(_bf16(z_s[:, OFF_Q:OFF_Q + D_XATTN]), kst_s[...])
    sc_s[...] = scores
    after_scores = _zero_row_after(scores[tile - SUBLANES:tile, N_XHEADS * N_MEM - LANES:])
    _unrolled(seg, scan_body, tuple(h + after_scores for h in h_in))

    for l in range(N_CONV_SLABS):
        lo = l * LANES
        uc_s[l, CONV_HALO:CONV_HALO + tile, :] = _gated_of_twice(
            z_s[:, OFF_GLU_A + lo:OFF_GLU_A + lo + LANES],
            z_s[:, OFF_GLU_B + lo:OFF_GLU_B + lo + LANES])

    def seg_body(c, carry):
        r0 = _aligned(c * seg, seg)
        p0 = _aligned(c * pitch, SUBLANES)
        for l in range(N_LRU_SLABS):
            lo = l * LANES
            gate = z_s[pl.ds(r0, seg), OFF_LRU_GATE + lo:OFF_LRU_GATE + lo + LANES]
            mix_s[pl.ds(r0, seg), lo:lo + LANES] = _bf16(u_s[l, pl.ds(p0, seg), :] * _silu_of_twice(gate))
        for piece in range(seg // CONV_ROWS):
            rr = r0 + piece * CONV_ROWS
            ys = []
            for l in range(N_CONV_SLABS):
                lo = l * LANES
                w_row = lambda k, lo=lo: w_dw_ref[k, :, lo:lo + LANES]
                ys.append(_conv_taps(uc_s, l, w_row, rr, CONV_ROWS, CONV_W, CONV_HALO)
                          + b_dw_ref[:, lo:lo + LANES])
            outs = _layernorm_silu_slabs(ys, ln_g_ref, ln_b_ref)
            for l in range(N_CONV_SLABS):
                lo = l * LANES
                ln_s[pl.ds(rr, CONV_ROWS), lo:lo + LANES] = _bf16(outs[l])
        return carry

    _unrolled(n_seg, seg_body, 0)

    yc = (_dot(ln_s[...], wpw_bf[...]) + b_pw_ref[...]) * _silu_of_twice(
        z_s[:, OFF_CONV_GATE:OFF_CONV_GATE + D_CONV])
    mix_s[:, D_LRU:D_LRU + D_CONV] = _bf16(yc)

    def attn_body(c, carry):
        r0 = _aligned(c * ATTN_ROWS, ATTN_ROWS)
        p_s[pl.ds(r0, ATTN_ROWS), :] = _softmax_heads(sc_s[pl.ds(r0, ATTN_ROWS), :])
        return carry

    _unrolled(tile // ATTN_ROWS, attn_body, 0)
    o = _dot_nt(p_s[...], vst_s[...])
    mix_s[:, D_LRU + D_CONV:D_MIX] = _bf16(
        o * _silu_of_twice(z_s[:, OFF_ATTN_GATE:OFF_ATTN_GATE + D_XATTN]))

    y = x_ref[0] + _dot(mix_s[...], wout_bf[...])
    y_ref[0] = _rmsnorm(y, g_final_ref[...])

    for l in range(N_LRU_SLABS):
        xc_s[l, LRU_HALO - (LRU_CONV - 1):LRU_HALO, :] = (
            xc_s[l, LRU_HALO + tile - (LRU_CONV - 1):LRU_HALO + tile, :])
    for l in range(N_CONV_SLABS):
        uc_s[l, CONV_HALO - (CONV_W - 1):CONV_HALO, :] = (
            uc_s[l, CONV_HALO + tile - (CONV_W - 1):CONV_HALO + tile, :])

    def write_state(bb):
        for l in range(N_LRU_SLABS):
            lo = l * LANES
            h_out_ref[bb:bb + 1, lo:lo + LANES] = hin_s[l, 0:1, :]
            for k in range(LRU_CONV - 1):
                r = LRU_HALO - (LRU_CONV - 1) + k
                lb_out_ref[k, bb:bb + 1, lo:lo + LANES] = xc_s[l, r:r + 1, :]
        for l in range(N_CONV_SLABS):
            lo = l * LANES
            for k in range(CONV_W - 1):
                r = CONV_HALO - (CONV_W - 1) + k
                cb_out_ref[k, bb:bb + 1, lo:lo + LANES] = uc_s[l, r:r + 1, :]

    for bb in range(h_out_ref.shape[0]):
        pl.when(jnp.logical_and(t == n_t - 1, b == bb))(functools.partial(write_state, bb))


def _whole(shape):
    nd = len(shape)
    return pl.BlockSpec(shape, lambda *_: (0,) * nd, pipeline_mode=pl.Buffered(1))


def _prompt(x, mem, w):
    bsz, seq, _ = x.shape
    tile = PROMPT_TILE
    n_t = seq // tile
    n_seg = tile // PROMPT_SEG
    weights = [w["g_mem"], w["w_mem_k"], w["w_mem_v"],
               w["g_norm"], w["w_in"], w["w_lc"], w["b_lc"], w["w_ga"], w["w_gx"], w["b_ga"], w["b_gx"],
               w["lam"], w["w_dw"], w["b_dw"], w["ln_g"], w["ln_b"], w["w_pw"], w["b_pw"],
               w["w_out"], w["g_final"]]
    in_hbm = pl.BlockSpec(memory_space=pl.ANY)
    weight_specs = [
        _whole((1, D_MODEL)), _whole((None, D_MODEL, D_XATTN)), _whole((None, D_MODEL, D_XATTN)),
        _whole((1, D_MODEL)), in_hbm, _whole((None, LRU_CONV, D_LRU)),
        _whole((1, D_LRU)), in_hbm,
        in_hbm, _whole((1, D_LRU)), _whole((1, D_LRU)),
        _whole((1, D_LRU)), _whole((CONV_W, 1, D_CONV)), _whole((1, D_CONV)), _whole((1, D_CONV)),
        _whole((1, D_CONV)), in_hbm, _whole((1, D_CONV)),
        in_hbm, _whole((1, D_MODEL)),
    ]
    in_specs = [
        pl.BlockSpec((1, tile, D_MODEL), lambda b, t: (b, t, 0)),
        pl.BlockSpec((1, N_MEM, D_MODEL), lambda b, t: (b, 0, 0)),
    ] + weight_specs
    out_specs = [
        pl.BlockSpec((1, tile, D_MODEL), lambda b, t: (b, t, 0)),
        pl.BlockSpec((1, D_XATTN, N_MEM), lambda b, t: (b, 0, 0)),
        pl.BlockSpec((1, D_XATTN, N_MEM), lambda b, t: (b, 0, 0)),
        _whole((bsz, D_LRU)),
        _whole((LRU_CONV - 1, bsz, D_LRU)),
        _whole((CONV_W - 1, bsz, D_CONV)),
        _whole((D_MODEL, D_IN)), _whole((D_MIX, D_MODEL)), _whole((D_CONV, D_CONV)),
        _whole((2, GATE_HALF, D_LRU)),
    ]
    out_shape = [
        jax.ShapeDtypeStruct((bsz, seq, D_MODEL), jnp.float32),
        jax.ShapeDtypeStruct((bsz, D_XATTN, N_MEM), jnp.float32),
        jax.ShapeDtypeStruct((bsz, D_XATTN, N_MEM), jnp.float32),
        jax.ShapeDtypeStruct((bsz, D_LRU), jnp.float32),
        jax.ShapeDtypeStruct((LRU_CONV - 1, bsz, D_LRU), jnp.float32),
        jax.ShapeDtypeStruct((CONV_W - 1, bsz, D_CONV), jnp.float32),
        jax.ShapeDtypeStruct((D_MODEL, D_IN), jnp.bfloat16),
        jax.ShapeDtypeStruct((D_MIX, D_MODEL), jnp.bfloat16),
        jax.ShapeDtypeStruct((D_CONV, D_CONV), jnp.bfloat16),
        jax.ShapeDtypeStruct((2, GATE_HALF, D_LRU), jnp.bfloat16),
    ]
    scratch = [
        pltpu.VMEM((tile, D_IN), jnp.float32),
        pltpu.VMEM((N_LRU_SLABS, LRU_HALO + tile, LANES), jnp.float32),
        pltpu.VMEM((tile, D_LRU), jnp.float32),
        pltpu.VMEM((tile, 2 * D_LRU), jnp.float32),
        pltpu.VMEM((N_LRU_SLABS, n_seg * PROMPT_PITCH, LANES), jnp.float32),
        pltpu.VMEM((N_LRU_SLABS, n_seg * PROMPT_PITCH, LANES), jnp.float32),
        pltpu.VMEM((N_CONV_SLABS, CONV_HALO + tile, LANES), jnp.float32),
        pltpu.VMEM((tile, D_CONV), jnp.bfloat16),
        pltpu.VMEM((tile, D_MIX), jnp.bfloat16),
        pltpu.VMEM((N_LRU_SLABS, 2 * SUBLANES, LANES), jnp.float32),
        pltpu.VMEM((D_XATTN, N_XHEADS * N_MEM), jnp.bfloat16),
        pltpu.VMEM((D_XATTN, N_XHEADS * N_MEM), jnp.bfloat16),
        pltpu.VMEM((tile, N_XHEADS * N_MEM), jnp.float32),
        pltpu.VMEM((tile, N_XHEADS * N_MEM), jnp.bfloat16),
        pltpu.VMEM((D_MODEL, D_IN), jnp.float32),
        pltpu.VMEM((LRU_BLOCKS, LRU_BLOCK, LRU_BLOCK), jnp.float32),
        pltpu.VMEM((LRU_BLOCKS, LRU_BLOCK, LRU_BLOCK), jnp.float32),
        pltpu.VMEM((D_CONV, D_CONV), jnp.float32),
        pltpu.VMEM((D_MIX, D_MODEL), jnp.float32),
        pltpu.SemaphoreType.DMA((D_MODEL // W_IN_CHUNK + 4,)),
    ]
    return pl.pallas_call(
        _prompt_kernel,
        grid=(bsz, n_t),
        in_specs=in_specs,
        out_specs=out_specs,
        out_shape=out_shape,
        scratch_shapes=scratch,
        compiler_params=pltpu.CompilerParams(
            dimension_semantics=("arbitrary", "arbitrary"),
            vmem_limit_bytes=VMEM_LIMIT_BYTES),
        name="prompt_layer",
    )(x, mem, *weights)


def _sample_kernel(nb, steps,
                   x_ref, h0_ref, lb_ref, cb_ref, kt_hbm, vt_hbm,
                   win_bf, w_lc_ref, b_lc_ref, wg_bf,
                   b_ga_ref, b_gx_ref, lam_ref, w_dw_ref, b_dw_ref, ln_g_ref, ln_b_ref,
                   wpw_bf, b_pw_ref, wout_bf, g_final_ref,
                   y_ref, h_out_ref, lb_out_ref, cb_out_ref,
                   z_s, xc_s, xl_s, pre_s, a_s, u_s, uc_s, ln_s, mix_s, sc_s, p_s,
                   kt_ref, vt_ref, kv_sem):
    rows = nb * steps
    pitch = SAMPLE_PITCH
    cpitch = CONV_HALO + steps
    prow = nb * pitch

    group = pl.program_id(0)
    first = group * nb
    n_groups = kt_ref.shape[0] // nb

    def kv_copies(grp):
        rows_of = pl.ds(grp * nb, nb)
        return (pltpu.make_async_copy(kt_hbm.at[rows_of], kt_ref.at[rows_of], kv_sem.at[2 * grp]),
                pltpu.make_async_copy(vt_hbm.at[rows_of], vt_ref.at[rows_of], kv_sem.at[2 * grp + 1]))

    @pl.when(group == 0)
    def _():
        for grp in range(n_groups):
            for copy in kv_copies(grp):
                copy.start()
        xl_s[prow - LRU_HALO:prow, :] = jnp.zeros((LRU_HALO, D_LRU), jnp.float32)

    x = x_ref[...].reshape(rows, D_MODEL)
    z_s[...] = _dot(_bf16(_rms_scaled(x)), win_bf[...])

    for b in range(nb):
        for l in range(N_LRU_SLABS):
            lo = l * LANES
            zcol = OFF_LRU_IN + lo
            xc_s[l, b * pitch:b * pitch + LRU_HALO, :] = jnp.zeros((LRU_HALO, LANES), jnp.float32)
            for k in range(LRU_CONV - 1):
                r = b * pitch + LRU_HALO - (LRU_CONV - 1) + k
                xc_s[l, r:r + 1, :] = lb_ref[k, b:b + 1, lo:lo + LANES]
                rn = (b + 1) * steps - (LRU_CONV - 1) + k
                lb_out_ref[k, b:b + 1, lo:lo + LANES] = z_s[rn:rn + 1, zcol:zcol + LANES]
            xc_s[l, b * pitch + LRU_HALO:(b + 1) * pitch, :] = (
                z_s[b * steps:(b + 1) * steps, zcol:zcol + LANES])
    n_conv = prow - LRU_HALO
    for l in range(N_LRU_SLABS):
        lo = l * LANES
        w_row = lambda k, lo=lo: 0.5 * w_lc_ref[k:k + 1, lo:lo + LANES]
        xl = (_conv_taps(xc_s, l, w_row, 0, n_conv, LRU_CONV, LRU_HALO)
              + 0.5 * b_lc_ref[:, lo:lo + LANES])
        xl_s[0:n_conv, lo:lo + LANES] = xl
    _lru_gates(_bf16(xl_s[...]), wg_bf, pre_s)

    hc_ls = (0.5 * LRU_C) * _log_sigmoid(lam_ref[...])
    hb_ga = 0.5 * b_ga_ref[...]
    hb_gx = 0.5 * b_gx_ref[...]
    for l in range(N_LRU_SLABS):
        lo = l * LANES
        ca, cx = _pre_cols(l)
        a, u = _lru_coeffs(pre_s[:, ca:ca + LANES], pre_s[:, cx:cx + LANES], xl_s[:, lo:lo + LANES],
                           hb_ga[:, lo:lo + LANES], hb_gx[:, lo:lo + LANES],
                           hc_ls[:, lo:lo + LANES])
        a_s[l, :, :] = a
        u_s[l, :, :] = u

    n_grp = nb // SUBLANES

    def scan_body(j, carry):
        new = []
        for g in range(n_grp):
            for l in range(N_LRU_SLABS):
                base = g * SUBLANES * pitch
                a_j = a_s[l, pl.ds(base + j, SUBLANES, stride=pitch), :]
                u_j = u_s[l, pl.ds(base + j, SUBLANES, stride=pitch), :]
                h = a_j * carry[g * N_LRU_SLABS + l] + u_j
                u_s[l, pl.ds(base + j, SUBLANES, stride=pitch), :] = h
                new.append(h)
        return tuple(new)

    h0 = tuple(h0_ref[g * SUBLANES:(g + 1) * SUBLANES, l * LANES:(l + 1) * LANES]
               for g in range(n_grp) for l in range(N_LRU_SLABS))
    h_fin = _unrolled(steps, scan_body, h0)
    for g in range(n_grp):
        for l in range(N_LRU_SLABS):
            h_out_ref[g * SUBLANES:(g + 1) * SUBLANES, l * LANES:(l + 1) * LANES] = (
                h_fin[g * N_LRU_SLABS + l])

    for b in range(nb):
        for l in range(N_CONV_SLABS):
            lo = l * LANES
            uc = _gated_of_twice(
                z_s[b * steps:(b + 1) * steps, OFF_GLU_A + lo:OFF_GLU_A + lo + LANES],
                z_s[b * steps:(b + 1) * steps, OFF_GLU_B + lo:OFF_GLU_B + lo + LANES])
            uc_s[l, b * cpitch + CONV_HALO:(b + 1) * cpitch, :] = uc
            for k in range(CONV_W - 1):
                r = b * cpitch + CONV_HALO - (CONV_W - 1) + k
                uc_s[l, r:r + 1, :] = cb_ref[k, b:b + 1, lo:lo + LANES]
                rn = (b + 1) * cpitch - (CONV_W - 1) + k
                cb_out_ref[k, b:b + 1, lo:lo + LANES] = uc_s[l, rn:rn + 1, :]

    for b in range(nb):
        r0 = b * cpitch
        o0 = b * steps
        ys = []
        for l in range(N_CONV_SLABS):
            lo = l * LANES
            w_row = lambda k, lo=lo: w_dw_ref[k, :, lo:lo + LANES]
            ys.append(_conv_taps(uc_s, l, w_row, r0, steps, CONV_W, CONV_HALO)
                      + b_dw_ref[:, lo:lo + LANES])
        outs = _layernorm_silu_slabs(ys, ln_g_ref, ln_b_ref)
        for l in range(N_CONV_SLABS):
            lo = l * LANES
            ln_s[o0:o0 + steps, lo:lo + LANES] = _bf16(outs[l])

    yc = (_dot(ln_s[...], wpw_bf[...]) + b_pw_ref[...]) * _silu_of_twice(
        z_s[:, OFF_CONV_GATE:OFF_CONV_GATE + D_CONV])
    mix_s[:, D_LRU:D_LRU + D_CONV] = _bf16(yc)

    masks = _lane_head_masks()
    scale = SCORE_SCALE
    qrows = N_XHEADS * steps
    for copy in kv_copies(group):
        copy.wait()
    for b in range(nb):
        for l in range(N_LRU_SLABS):
            lo = l * LANES
            gate = z_s[b * steps:(b + 1) * steps, OFF_LRU_GATE + lo:OFF_LRU_GATE + lo + LANES]
            mix_s[b * steps:(b + 1) * steps, lo:lo + LANES] = _bf16(
                u_s[l, b * pitch:b * pitch + steps, :] * _silu_of_twice(gate))
    for b in range(nb):
        q = z_s[b * steps:(b + 1) * steps, OFF_Q:OFF_Q + D_XATTN]
        qm = jnp.concatenate([_bf16(q * (m * scale)) for m in masks], axis=0)
        sc_s[b * qrows:(b + 1) * qrows, :] = _dot(qm, _bf16(kt_ref[first + b]))
    s = sc_s[...]
    e = jnp.exp2(s - jnp.max(s, axis=-1, keepdims=True))
    p_s[...] = _bf16(e * (1.0 / jnp.sum(e, axis=-1, keepdims=True)))
    for b in range(nb):
        o_all = _dot_nt(p_s[b * qrows:(b + 1) * qrows, :], _bf16(vt_ref[first + b]))
        o = o_all[0:steps] * masks[0]
        for h in range(1, N_XHEADS):
            o = o + o_all[h * steps:(h + 1) * steps] * masks[h]
        gate = z_s[b * steps:(b + 1) * steps, OFF_ATTN_GATE:OFF_ATTN_GATE + D_XATTN]
        mix_s[b * steps:(b + 1) * steps, D_LRU + D_CONV:D_MIX] = _bf16(o * _silu_of_twice(gate))

    y = x + _dot(mix_s[...], wout_bf[...])
    y_ref[...] = _rmsnorm(y, g_final_ref[...]).reshape(nb, steps, D_MODEL)


def _sample(x, h0, lb, cb, kt, vt, w, win_bf, wout_bf, wpw_bf, wg_bf):
    bsz, steps, _ = x.shape
    nb = SAMPLE_GROUP
    assert bsz % nb == 0 and nb % SUBLANES == 0
    assert steps + LRU_HALO == SAMPLE_PITCH and steps >= CONV_W - 1
    rows = nb * steps
    prow = nb * SAMPLE_PITCH
    weights = [win_bf, w["w_lc"], w["b_lc"], wg_bf, w["b_ga"], w["b_gx"],
               w["lam"], w["w_dw"], w["b_dw"], w["ln_g"], w["ln_b"], wpw_bf, w["b_pw"],
               wout_bf, w["g_final"]]
    weight_specs = [
        _whole((D_MODEL, D_IN)), _whole((None, LRU_CONV, D_LRU)),
        _whole((1, D_LRU)), _whole((2, GATE_HALF, D_LRU)), _whole((1, D_LRU)), _whole((1, D_LRU)),
        _whole((1, D_LRU)), _whole((CONV_W, 1, D_CONV)), _whole((1, D_CONV)), _whole((1, D_CONV)),
        _whole((1, D_CONV)), _whole((D_CONV, D_CONV)), _whole((1, D_CONV)),
        _whole((D_MIX, D_MODEL)), _whole((1, D_MODEL)),
    ]
    in_specs = [
        pl.BlockSpec((nb, steps, D_MODEL), lambda g: (g, 0, 0)),
        pl.BlockSpec((nb, D_LRU), lambda g: (g, 0)),
        pl.BlockSpec((LRU_CONV - 1, nb, D_LRU), lambda g: (0, g, 0)),
        pl.BlockSpec((CONV_W - 1, nb, D_CONV), lambda g: (0, g, 0)),
        pl.BlockSpec(memory_space=pl.ANY),
        pl.BlockSpec(memory_space=pl.ANY),
    ] + weight_specs
    out_specs = [
        pl.BlockSpec((nb, steps, D_MODEL), lambda g: (g, 0, 0)),
        pl.BlockSpec((nb, D_LRU), lambda g: (g, 0)),
        pl.BlockSpec((LRU_CONV - 1, nb, D_LRU), lambda g: (0, g, 0)),
        pl.BlockSpec((CONV_W - 1, nb, D_CONV), lambda g: (0, g, 0)),
    ]
    out_shape = [
        jax.ShapeDtypeStruct((bsz, steps, D_MODEL), jnp.float32),
        jax.ShapeDtypeStruct((bsz, D_LRU), jnp.float32),
        jax.ShapeDtypeStruct((LRU_CONV - 1, bsz, D_LRU), jnp.float32),
        jax.ShapeDtypeStruct((CONV_W - 1, bsz, D_CONV), jnp.float32),
    ]
    scratch = [
        pltpu.VMEM((rows, D_IN), jnp.float32),
        pltpu.VMEM((N_LRU_SLABS, prow, LANES), jnp.float32),
        pltpu.VMEM((prow, D_LRU), jnp.float32),
        pltpu.VMEM((prow, 2 * D_LRU), jnp.float32),
        pltpu.VMEM((N_LRU_SLABS, prow, LANES), jnp.float32),
        pltpu.VMEM((N_LRU_SLABS, prow, LANES), jnp.float32),
        pltpu.VMEM((N_CONV_SLABS, nb * (CONV_HALO + steps), LANES), jnp.float32),
        pltpu.VMEM((rows, D_CONV), jnp.bfloat16),
        pltpu.VMEM((rows, D_MIX), jnp.bfloat16),
        pltpu.VMEM((N_XHEADS * rows, N_MEM), jnp.float32),
        pltpu.VMEM((N_XHEADS * rows, N_MEM), jnp.bfloat16),
        pltpu.VMEM((bsz, D_XATTN, N_MEM), jnp.float32),
        pltpu.VMEM((bsz, D_XATTN, N_MEM), jnp.float32),
        pltpu.SemaphoreType.DMA((2 * (bsz // nb),)),
    ]
    return pl.pallas_call(
        functools.partial(_sample_kernel, nb, steps),
        grid=(bsz // nb,),
        in_specs=in_specs,
        out_specs=out_specs,
        out_shape=out_shape,
        scratch_shapes=scratch,
        compiler_params=pltpu.CompilerParams(
            dimension_semantics=("arbitrary",),
            vmem_limit_bytes=VMEM_LIMIT_BYTES),
        name="sample_layer",
    )(x, h0, lb, cb, kt, vt, *weights)


def _time_major(state):
    return jnp.transpose(state, (1, 0, 2))


def _memory_t(mem):
    bsz = mem.shape[0]
    return jnp.transpose(mem, (0, 2, 3, 1)).reshape(bsz, D_XATTN, N_MEM)


def _memory_from_t(mem_t):
    bsz = mem_t.shape[0]
    return jnp.transpose(mem_t.reshape(bsz, N_XHEADS, XHEAD_DIM, N_MEM), (0, 3, 1, 2))


def kernel(x_prompt, x_sample, state_lru_h, state_lru_conv, state_conv, cache_mem_k, cache_mem_v,
           mem_prompt, g_norm, w_in, w_lru_conv, b_lru_conv, w_gate_a, b_gate_a, w_gate_x, b_gate_x,
           lru_lambda, w_dw, b_dw, ln_g, ln_b, w_pw, b_pw, g_mem, w_mem_k, w_mem_v, w_out, g_final):
    depth = g_norm.shape[0]
    assert depth == 1
    w = {
        "g_norm": g_norm, "w_in": w_in, "w_lc": w_lru_conv, "b_lc": b_lru_conv,
        "w_ga": w_gate_a, "w_gx": w_gate_x, "b_ga": b_gate_a, "b_gx": b_gate_x, "lam": lru_lambda,
        "w_dw": jnp.transpose(w_dw, (1, 0, 2)), "b_dw": b_dw, "ln_g": ln_g, "ln_b": ln_b,
        "w_pw": w_pw, "b_pw": b_pw, "w_out": w_out, "g_final": g_final.reshape(1, D_MODEL),
        "g_mem": g_mem, "w_mem_k": w_mem_k, "w_mem_v": w_mem_v,
    }
    y_prompt, mkt, mvt, ph, plb, pcb, win_bf, wout_bf, wpw_bf, wg_bf = _prompt(x_prompt, mem_prompt, w)

    y_sample, sh, slb, scb = _sample(
        x_sample, state_lru_h[0], _time_major(state_lru_conv[0]), _time_major(state_conv[0]),
        _memory_t(cache_mem_k[0]), _memory_t(cache_mem_v[0]), w, win_bf, wout_bf, wpw_bf, wg_bf)

    return (y_prompt, y_sample,
            ph[None], _time_major(plb)[None], _time_major(pcb)[None],
            _memory_from_t(mkt)[None], _memory_from_t(mvt)[None],
            sh[None], _time_major(slb)[None], _time_major(scb)[None])
```

```python
import functools

import jax
import jax.numpy as jnp
from jax import lax
from jax.experimental import pallas as pl
from jax.experimental.pallas import tpu as pltpu

D_MODEL = 1024
N_MEM = 256
D_LRU = 512
LRU_BLOCKS = 8
LRU_BLOCK = D_LRU // LRU_BLOCKS
LRU_CONV = 4
LRU_C = 8.0
D_CONV = 256
CONV_W = 31
N_XHEADS = 4
XHEAD_DIM = 64
D_XATTN = N_XHEADS * XHEAD_DIM
D_MIX = D_LRU + D_CONV + D_XATTN
D_IN = 2 * D_LRU + 3 * D_CONV + 2 * D_XATTN
EPS = 1e-6

LANES = 128
SUBLANES = 8
N_LRU_SLABS = D_LRU // LANES
N_CONV_SLABS = D_CONV // LANES
GATE_HALF = D_LRU // 2

OFF_LRU_IN = 0
OFF_LRU_GATE = D_LRU
OFF_GLU_A = 2 * D_LRU
OFF_GLU_B = OFF_GLU_A + D_CONV
OFF_CONV_GATE = OFF_GLU_B + D_CONV
OFF_Q = OFF_CONV_GATE + D_CONV
OFF_ATTN_GATE = OFF_Q + D_XATTN

LRU_HALO = 8
CONV_HALO = 32

PROMPT_TILE = 512
PROMPT_SEG = 64
PROMPT_PITCH = 68
SAMPLE_PITCH = 40
SAMPLE_GROUP = 16
ATTN_ROWS = 128
CONV_ROWS = 32
PROJ_ROWS = 256

VMEM_LIMIT_BYTES = 56 * 1024 * 1024
HALF_LOG2_E = 0.7213475204444817
SCORE_SCALE = XHEAD_DIM ** -0.5 * 2.0 * HALF_LOG2_E


def _silu_of_twice(hx):
    return hx + hx * jnp.tanh(hx)


def _gated_of_twice(ha, hb):
    return ha + ha * jnp.tanh(hb)


def _w_in_col_scale():
    col = lax.broadcasted_iota(jnp.int32, (1, D_IN), 1)
    plain = (col < OFF_LRU_GATE) | ((col >= OFF_Q) & (col < OFF_ATTN_GATE))
    return jnp.where(plain, 1.0, 0.5)


def _sqrt_nonneg(x):
    return jnp.exp2(jnp.log(x) * HALF_LOG2_E)


def _rms_scaled(x):
    ms = jnp.mean(x * x, axis=-1, keepdims=True)
    return x * lax.rsqrt(ms + EPS)


def _rmsnorm(x, g):
    return _rms_scaled(x) * g


def _as_column(row):
    n = row.shape[1] // LANES
    on_diag = (lax.broadcasted_iota(jnp.int32, (LANES, LANES), 0)
               == lax.broadcasted_iota(jnp.int32, (LANES, LANES), 1))
    cols = [jnp.sum(jnp.where(on_diag, row[:, j * LANES:(j + 1) * LANES], 0.0), axis=1, keepdims=True)
            for j in range(n)]
    return jnp.concatenate(cols, axis=0)


def _log_sigmoid(x):
    y = -x
    return -(jnp.maximum(y, 0.0) + jnp.log1p(jnp.exp(-jnp.abs(y))))


def _bf16(x):
    return x.astype(jnp.bfloat16)


def _dot(a, b):
    return jnp.dot(a, b, preferred_element_type=jnp.float32)


def _dot_nt(a, b):
    return lax.dot_general(a, b, (((1,), (1,)), ((), ())), preferred_element_type=jnp.float32)


def _unrolled(n, body, init):
    carry = init
    for i in range(n):
        carry = body(i, carry)
    return carry


def _aligned(x, m):
    return x if isinstance(x, int) else pl.multiple_of(x, m)


def _head_of(index):
    return lax.shift_right_logical(index, XHEAD_DIM.bit_length() - 1)


def _lane_head_masks():
    head = _head_of(lax.broadcasted_iota(jnp.int32, (1, D_XATTN), 1))
    return [(head == h).astype(jnp.float32) for h in range(N_XHEADS)]


def _row_head_masks():
    head = _head_of(lax.broadcasted_iota(jnp.int32, (D_XATTN, 1), 0))
    return [(head == h).astype(jnp.float32) for h in range(N_XHEADS)]


def _head_stack_t(kt, vt, kst_ref, vst_ref):
    scale = SCORE_SCALE
    for h, m in enumerate(_row_head_masks()):
        kst_ref[:, h * N_MEM:(h + 1) * N_MEM] = _bf16(kt * (m * scale))
        vst_ref[:, h * N_MEM:(h + 1) * N_MEM] = _bf16(vt * m)


def _zero_row_after(x):
    bits = lax.bitcast_convert_type(x, jnp.uint32)
    zero = lax.shift_right_logical(lax.shift_right_logical(bits, jnp.uint32(16)), jnp.uint32(16))
    return lax.bitcast_convert_type(zero, jnp.float32)[0:1, :]


def _softmax_heads(s):
    ps = []
    for h in range(N_XHEADS):
        sh = s[:, h * N_MEM:(h + 1) * N_MEM]
        e = jnp.exp2(sh - jnp.max(sh, axis=-1, keepdims=True))
        inv = 1.0 / jnp.sum(e, axis=-1, keepdims=True)
        ps.append(_bf16(e * inv))
    return jnp.concatenate(ps, axis=1)


def _lru_gates(xl_bf, wg_ref, pre_s):
    pre_s[:, 0:D_LRU] = _dot(xl_bf[:, 0:GATE_HALF], wg_ref[0])
    pre_s[:, D_LRU:2 * D_LRU] = _dot(xl_bf[:, GATE_HALF:D_LRU], wg_ref[1])


def _pre_cols(l):
    half_slabs = N_LRU_SLABS // 2
    base = (l // half_slabs) * D_LRU + (l % half_slabs) * LANES
    return base, base + GATE_HALF


def _lru_coeffs(hpre_a, hpre_x, hx, hb_a, hb_x, hc_ls):
    t_r = jnp.tanh(hpre_a + hb_a)
    t_i = jnp.tanh(hpre_x + hb_x)
    log_a = hc_ls + hc_ls * t_r
    a = jnp.exp(log_a)
    one_minus_a2 = -jnp.tanh(log_a) * (a * a + 1.0)
    u = _sqrt_nonneg(one_minus_a2) * (hx + hx * t_i)
    return a, u


def _conv_taps(src_s, l, w_row, row0, nrows, width, halo):
    acc = None
    for k in range(width):
        x_k = src_s[l, pl.ds(row0 + (halo - (width - 1) + k), nrows), :]
        acc = w_row(k) * x_k if acc is None else acc + w_row(k) * x_k
    return acc


def _layernorm_silu_slabs(ys, g_ref, b_ref):
    tot = ys[0]
    for y in ys[1:]:
        tot = tot + y
    mu = jnp.sum(tot, axis=-1, keepdims=True) * (1.0 / D_CONV)
    cs = [y - mu for y in ys]
    sq = cs[0] * cs[0]
    for c in cs[1:]:
        sq = sq + c * c
    var = jnp.sum(sq, axis=-1, keepdims=True) * (1.0 / D_CONV)
    inv = lax.rsqrt(var + EPS)
    outs = []
    for l, c in enumerate(cs):
        lo = l * LANES
        half_g = 0.5 * g_ref[:, lo:lo + LANES]
        half_b = 0.5 * b_ref[:, lo:lo + LANES]
        outs.append(_silu_of_twice(c * inv * half_g + half_b))
    return outs


def _gate_block_diag(w_blocks):
    n = (LRU_BLOCKS // 2) * LRU_BLOCK
    stacked = _bf16(w_blocks.reshape(n, LRU_BLOCK))
    src = lax.broadcasted_iota(jnp.int32, (LRU_BLOCK, n), 0)
    dst = lax.broadcasted_iota(jnp.int32, (LRU_BLOCK, n), 1)
    spread = _bf16(((dst & (LRU_BLOCK - 1)) == src).astype(jnp.float32))
    rows = _head_of(lax.broadcasted_iota(jnp.int32, (n, n), 0))
    cols = _head_of(lax.broadcasted_iota(jnp.int32, (n, n), 1))
    return _bf16(_dot(stacked, spread) * (rows == cols).astype(jnp.float32))


def _prompt_kernel(x_ref, mem_ref, g_mem_ref, wk_ref, wv_ref,
                   g_norm_ref, w_in_ref, w_lc_ref, b_lc_ref, wga_ref, wgx_ref,
                   b_ga_ref, b_gx_ref, lam_ref, w_dw_ref, b_dw_ref, ln_g_ref, ln_b_ref,
                   w_pw_ref, b_pw_ref, w_out_ref, g_final_ref,
                   y_ref, kt_ref, vt_ref, h_out_ref, lb_out_ref, cb_out_ref,
                   win_bf, wout_bf, wpw_bf, wg_bf,
                   z_s, xc_s, xl_s, pre_s, a_s, u_s, uc_s, ln_s, mix_s, hin_s, kst_s, vst_s, sc_s, p_s):
    b = pl.program_id(0)
    t = pl.program_id(1)
    n_t = pl.num_programs(1)
    tile = PROMPT_TILE
    seg = PROMPT_SEG
    n_seg = tile // seg
    pitch = PROMPT_PITCH

    @pl.when(jnp.logical_and(b == 0, t == 0))
    def _():
        win_bf[...] = _bf16(w_in_ref[...] * _as_column(g_norm_ref[...]) * _w_in_col_scale())
        wout_bf[...] = _bf16(w_out_ref[...])
        wpw_bf[...] = _bf16(w_pw_ref[...])
        half = LRU_BLOCKS // 2
        for h in range(2):
            wg_bf[h, :, 0:GATE_HALF] = _gate_block_diag(wga_ref[h * half:(h + 1) * half])
            wg_bf[h, :, GATE_HALF:D_LRU] = _gate_block_diag(wgx_ref[h * half:(h + 1) * half])

    @pl.when(t == 0)
    def _():
        xc_s[:, 0:LRU_HALO, :] = jnp.zeros((N_LRU_SLABS, LRU_HALO, LANES), jnp.float32)
        uc_s[:, 0:CONV_HALO, :] = jnp.zeros((N_CONV_SLABS, CONV_HALO, LANES), jnp.float32)
        hin_s[:, 0:1, :] = jnp.zeros((N_LRU_SLABS, 1, LANES), jnp.float32)
        m = _bf16(_rmsnorm(mem_ref[0], g_mem_ref[...]))
        kt = _dot(m, _bf16(wk_ref[...])).T
        vt = _dot(m, _bf16(wv_ref[...])).T
        kt_ref[0] = kt
        vt_ref[0] = vt
        _head_stack_t(kt, vt, kst_s, vst_s)

    for r0 in range(0, tile, PROJ_ROWS):
        z_s[r0:r0 + PROJ_ROWS, :] = _dot(_bf16(_rms_scaled(x_ref[0, r0:r0 + PROJ_ROWS, :])), win_bf[...])

    for l in range(N_LRU_SLABS):
        lo = l * LANES
        xc_s[l, LRU_HALO:LRU_HALO + tile, :] = z_s[:, OFF_LRU_IN + lo:OFF_LRU_IN + lo + LANES]
    for l in range(N_LRU_SLABS):
        lo = l * LANES
        w_row = lambda k, lo=lo: 0.5 * w_lc_ref[k:k + 1, lo:lo + LANES]
        xl = (_conv_taps(xc_s, l, w_row, 0, tile, LRU_CONV, LRU_HALO)
              + 0.5 * b_lc_ref[:, lo:lo + LANES])
        xl_s[:, lo:lo + LANES] = xl
    _lru_gates(_bf16(xl_s[...]), wg_bf, pre_s)

    hc_ls = (0.5 * LRU_C) * _log_sigmoid(lam_ref[...])
    hb_ga = 0.5 * b_ga_ref[...]
    hb_gx = 0.5 * b_gx_ref[...]

    def coeff_body(c, carry):
        r0 = _aligned(c * seg, seg)
        p0 = _aligned(c * pitch, SUBLANES)
        for l in range(N_LRU_SLABS):
            lo = l * LANES
            ca, cx = _pre_cols(l)
            a, u = _lru_coeffs(pre_s[pl.ds(r0, seg), ca:ca + LANES],
                               pre_s[pl.ds(r0, seg), cx:cx + LANES],
                               xl_s[pl.ds(r0, seg), lo:lo + LANES],
                               hb_ga[:, lo:lo + LANES], hb_gx[:, lo:lo + LANES],
                               hc_ls[:, lo:lo + LANES])
            a_s[l, pl.ds(p0, seg), :] = a
            u_s[l, pl.ds(p0, seg), :] = u
        return carry

    _unrolled(n_seg, coeff_body, 0)

    def tot_body(j, carry):
        new = []
        for l in range(N_LRU_SLABS):
            a_tot, u_tot = carry[l]
            a_j = a_s[l, pl.ds(j, n_seg, stride=pitch), :]
            u_j = u_s[l, pl.ds(j, n_seg, stride=pitch), :]
            new.append((a_j * a_tot, a_j * u_tot + u_j))
        return tuple(new)

    init = tuple((jnp.ones((n_seg, LANES), jnp.float32), jnp.zeros((n_seg, LANES), jnp.float32))
                 for _ in range(N_LRU_SLABS))
    totals = _unrolled(seg, tot_body, init)

    h_in = []
    for l in range(N_LRU_SLABS):
        a_tot, u_tot = totals[l]
        h = hin_s[l, 0:1, :]
        for c in range(n_seg):
            h = a_tot[c:c + 1, :] * h + u_tot[c:c + 1, :]
            hin_s[l, c + 1:c + 2, :] = h
        h_in.append(hin_s[l, 0:n_seg, :])
        hin_s[l, 0:1, :] = h

    def scan_body(j, carry):
        new = []
        for l in range(N_LRU_SLABS):
            a_j = a_s[l, pl.ds(j, n_seg, stride=pitch), :]
            u_j = u_s[l, pl.ds(j, n_seg, stride=pitch), :]
            h = a_j * carry[l] + u_j
            u_s[l, pl.ds(j, n_seg, stride=pitch), :] = h
            new.append(h)
        return tuple(new)

    scores = _dot(_bf16(z_s[:, OFF_Q:OFF_Q + D_XATTN]), kst_s[...])
    sc_s[...] = scores
    after_scores = _zero_row_after(scores[tile - SUBLANES:tile, N_XHEADS * N_MEM - LANES:])
    _unrolled(seg, scan_body, tuple(h + after_scores for h in h_in))

    for l in range(N_CONV_SLABS):
        lo = l * LANES
        uc_s[l, CONV_HALO:CONV_HALO + tile, :] = _gated_of_twice(
            z_s[:, OFF_GLU_A + lo:OFF_GLU_A + lo + LANES],
            z_s[:, OFF_GLU_B + lo:OFF_GLU_B + lo + LANES])

    def seg_body(c, carry):
        r0 = _aligned(c * seg, seg)
        p0 = _aligned(c * pitch, SUBLANES)
        for l in range(N_LRU_SLABS):
            lo = l * LANES
            gate = z_s[pl.ds(r0, seg), OFF_LRU_GATE + lo:OFF_LRU_GATE + lo + LANES]
            mix_s[pl.ds(r0, seg), lo:lo + LANES] = _bf16(u_s[l, pl.ds(p0, seg), :] * _silu_of_twice(gate))
        for piece in range(seg // CONV_ROWS):
            rr = r0 + piece * CONV_ROWS
            ys = []
            for l in range(N_CONV_SLABS):
                lo = l * LANES
                w_row = lambda k, lo=lo: w_dw_ref[k, :, lo:lo + LANES]
                ys.append(_conv_taps(uc_s, l, w_row, rr, CONV_ROWS, CONV_W, CONV_HALO)
                          + b_dw_ref[:, lo:lo + LANES])
            outs = _layernorm_silu_slabs(ys, ln_g_ref, ln_b_ref)
            for l in range(N_CONV_SLABS):
                lo = l * LANES
                ln_s[pl.ds(rr, CONV_ROWS), lo:lo + LANES] = _bf16(outs[l])
        return carry

    _unrolled(n_seg, seg_body, 0)

    yc = (_dot(ln_s[...], wpw_bf[...]) + b_pw_ref[...]) * _silu_of_twice(
        z_s[:, OFF_CONV_GATE:OFF_CONV_GATE + D_CONV])
    mix_s[:, D_LRU:D_LRU + D_CONV] = _bf16(yc)

    def attn_body(c, carry):
        r0 = _aligned(c * ATTN_ROWS, ATTN_ROWS)
        p_s[pl.ds(r0, ATTN_ROWS), :] = _softmax_heads(sc_s[pl.ds(r0, ATTN_ROWS), :])
        return carry

    _unrolled(tile // ATTN_ROWS, attn_body, 0)
    o = _dot_nt(p_s[...], vst_s[...])
    mix_s[:, D_LRU + D_CONV:D_MIX] = _bf16(
        o * _silu_of_twice(z_s[:, OFF_ATTN_GATE:OFF_ATTN_GATE + D_XATTN]))

    y = x_ref[0] + _dot(mix_s[...], wout_bf[...])
    y_ref[0] = _rmsnorm(y, g_final_ref[...])

    for l in range(N_LRU_SLABS):
        xc_s[l, LRU_HALO - (LRU_CONV - 1):LRU_HALO, :] = (
            xc_s[l, LRU_HALO + tile - (LRU_CONV - 1):LRU_HALO + tile, :])
    for l in range(N_CONV_SLABS):
        uc_s[l, CONV_HALO - (CONV_W - 1):CONV_HALO, :] = (
            uc_s[l, CONV_HALO + tile - (CONV_W - 1):CONV_HALO + tile, :])

    def write_state(bb):
        for l in range(N_LRU_SLABS):
            lo = l * LANES
            h_out_ref[bb:bb + 1, lo:lo + LANES] = hin_s[l, 0:1, :]
            for k in range(LRU_CONV - 1):
                r = LRU_HALO - (LRU_CONV - 1) + k
                lb_out_ref[k, bb:bb + 1, lo:lo + LANES] = xc_s[l, r:r + 1, :]
        for l in range(N_CONV_SLABS):
            lo = l * LANES
            for k in range(CONV_W - 1):
                r = CONV_HALO - (CONV_W - 1) + k
                cb_out_ref[k, bb:bb + 1, lo:lo + LANES] = uc_s[l, r:r + 1, :]

    for bb in range(h_out_ref.shape[0]):
        pl.when(jnp.logical_and(t == n_t - 1, b == bb))(functools.partial(write_state, bb))


def _whole(shape):
    nd = len(shape)
    return pl.BlockSpec(shape, lambda *_: (0,) * nd, pipeline_mode=pl.Buffered(1))


def _prompt(x, mem, w):
    bsz, seq, _ = x.shape
    tile = PROMPT_TILE
    n_t = seq // tile
    n_seg = tile // PROMPT_SEG
    weights = [w["g_mem"], w["w_mem_k"], w["w_mem_v"],
               w["g_norm"], w["w_in"], w["w_lc"], w["b_lc"], w["w_ga"], w["w_gx"], w["b_ga"], w["b_gx"],
               w["lam"], w["w_dw"], w["b_dw"], w["ln_g"], w["ln_b"], w["w_pw"], w["b_pw"],
               w["w_out"], w["g_final"]]
    weight_specs = [
        _whole((1, D_MODEL)), _whole((None, D_MODEL, D_XATTN)), _whole((None, D_MODEL, D_XATTN)),
        _whole((1, D_MODEL)), _whole((None, D_MODEL, D_IN)), _whole((None, LRU_CONV, D_LRU)),
        _whole((1, D_LRU)), _whole((None, LRU_BLOCKS, LRU_BLOCK, LRU_BLOCK)),
        _whole((None, LRU_BLOCKS, LRU_BLOCK, LRU_BLOCK)), _whole((1, D_LRU)), _whole((1, D_LRU)),
        _whole((1, D_LRU)), _whole((CONV_W, 1, D_CONV)), _whole((1, D_CONV)), _whole((1, D_CONV)),
        _whole((1, D_CONV)), _whole((None, D_CONV, D_CONV)), _whole((1, D_CONV)),
        _whole((None, D_MIX, D_MODEL)), _whole((1, D_MODEL)),
    ]
    in_specs = [
        pl.BlockSpec((1, tile, D_MODEL), lambda b, t: (b, t, 0)),
        pl.BlockSpec((1, N_MEM, D_MODEL), lambda b, t: (b, 0, 0)),
    ] + weight_specs
    out_specs = [
        pl.BlockSpec((1, tile, D_MODEL), lambda b, t: (b, t, 0)),
        pl.BlockSpec((1, D_XATTN, N_MEM), lambda b, t: (b, 0, 0)),
        pl.BlockSpec((1, D_XATTN, N_MEM), lambda b, t: (b, 0, 0)),
        _whole((bsz, D_LRU)),
        _whole((LRU_CONV - 1, bsz, D_LRU)),
        _whole((CONV_W - 1, bsz, D_CONV)),
        _whole((D_MODEL, D_IN)), _whole((D_MIX, D_MODEL)), _whole((D_CONV, D_CONV)),
        _whole((2, GATE_HALF, D_LRU)),
    ]
    out_shape = [
        jax.ShapeDtypeStruct((bsz, seq, D_MODEL), jnp.float32),
        jax.ShapeDtypeStruct((bsz, D_XATTN, N_MEM), jnp.float32),
        jax.ShapeDtypeStruct((bsz, D_XATTN, N_MEM), jnp.float32),
        jax.ShapeDtypeStruct((bsz, D_LRU), jnp.float32),
        jax.ShapeDtypeStruct((LRU_CONV - 1, bsz, D_LRU), jnp.float32),
        jax.ShapeDtypeStruct((CONV_W - 1, bsz, D_CONV), jnp.float32),
        jax.ShapeDtypeStruct((D_MODEL, D_IN), jnp.bfloat16),
        jax.ShapeDtypeStruct((D_MIX, D_MODEL), jnp.bfloat16),
        jax.ShapeDtypeStruct((D_CONV, D_CONV), jnp.bfloat16),
        jax.ShapeDtypeStruct((2, GATE_HALF, D_LRU), jnp.bfloat16),
    ]
    scratch = [
        pltpu.VMEM((tile, D_IN), jnp.float32),
        pltpu.VMEM((N_LRU_SLABS, LRU_HALO + tile, LANES), jnp.float32),
        pltpu.VMEM((tile, D_LRU), jnp.float32),
        pltpu.VMEM((tile, 2 * D_LRU), jnp.float32),
        pltpu.VMEM((N_LRU_SLABS, n_seg * PROMPT_PITCH, LANES), jnp.float32),
        pltpu.VMEM((N_LRU_SLABS, n_seg * PROMPT_PITCH, LANES), jnp.float32),
        pltpu.VMEM((N_CONV_SLABS, CONV_HALO + tile, LANES), jnp.float32),
        pltpu.VMEM((tile, D_CONV), jnp.bfloat16),
        pltpu.VMEM((tile, D_MIX), jnp.bfloat16),
        pltpu.VMEM((N_LRU_SLABS, 2 * SUBLANES, LANES), jnp.float32),
        pltpu.VMEM((D_XATTN, N_XHEADS * N_MEM), jnp.bfloat16),
        pltpu.VMEM((D_XATTN, N_XHEADS * N_MEM), jnp.bfloat16),
        pltpu.VMEM((tile, N_XHEADS * N_MEM), jnp.float32),
        pltpu.VMEM((tile, N_XHEADS * N_MEM), jnp.bfloat16),
    ]
    return pl.pallas_call(
        _prompt_kernel,
        grid=(bsz, n_t),
        in_specs=in_specs,
        out_specs=out_specs,
        out_shape=out_shape,
        scratch_shapes=scratch,
        compiler_params=pltpu.CompilerParams(
            dimension_semantics=("arbitrary", "arbitrary"),
            vmem_limit_bytes=VMEM_LIMIT_BYTES),
        name="prompt_layer",
    )(x, mem, *weights)


def _sample_kernel(nb, steps,
                   x_ref, h0_ref, lb_ref, cb_ref, kt_hbm, vt_hbm,
                   win_bf, w_lc_ref, b_lc_ref, wg_bf,
                   b_ga_ref, b_gx_ref, lam_ref, w_dw_ref, b_dw_ref, ln_g_ref, ln_b_ref,
                   wpw_bf, b_pw_ref, wout_bf, g_final_ref,
                   y_ref, h_out_ref, lb_out_ref, cb_out_ref,
                   z_s, xc_s, xl_s, pre_s, a_s, u_s, uc_s, ln_s, mix_s, sc_s, p_s,
                   kt_ref, vt_ref, kv_sem):
    rows = nb * steps
    pitch = SAMPLE_PITCH
    cpitch = CONV_HALO + steps
    prow = nb * pitch

    group = pl.program_id(0)
    first = group * nb
    n_groups = kt_ref.shape[0] // nb

    def kv_copies(grp):
        rows_of = pl.ds(grp * nb, nb)
        return (pltpu.make_async_copy(kt_hbm.at[rows_of], kt_ref.at[rows_of], kv_sem.at[2 * grp]),
                pltpu.make_async_copy(vt_hbm.at[rows_of], vt_ref.at[rows_of], kv_sem.at[2 * grp + 1]))

    @pl.when(group == 0)
    def _():
        for grp in range(n_groups):
            for copy in kv_copies(grp):
                copy.start()
        xl_s[prow - LRU_HALO:prow, :] = jnp.zeros((LRU_HALO, D_LRU), jnp.float32)

    x = x_ref[...].reshape(rows, D_MODEL)
    z_s[...] = _dot(_bf16(_rms_scaled(x)), win_bf[...])

    for b in range(nb):
        for l in range(N_LRU_SLABS):
            lo = l * LANES
            zcol = OFF_LRU_IN + lo
            xc_s[l, b * pitch:b * pitch + LRU_HALO, :] = jnp.zeros((LRU_HALO, LANES), jnp.float32)
            for k in range(LRU_CONV - 1):
                r = b * pitch + LRU_HALO - (LRU_CONV - 1) + k
                xc_s[l, r:r + 1, :] = lb_ref[k, b:b + 1, lo:lo + LANES]
                rn = (b + 1) * steps - (LRU_CONV - 1) + k
                lb_out_ref[k, b:b + 1, lo:lo + LANES] = z_s[rn:rn + 1, zcol:zcol + LANES]
            xc_s[l, b * pitch + LRU_HALO:(b + 1) * pitch, :] = (
                z_s[b * steps:(b + 1) * steps, zcol:zcol + LANES])
    n_conv = prow - LRU_HALO
    for l in range(N_LRU_SLABS):
        lo = l * LANES
        w_row = lambda k, lo=lo: 0.5 * w_lc_ref[k:k + 1, lo:lo + LANES]
        xl = (_conv_taps(xc_s, l, w_row, 0, n_conv, LRU_CONV, LRU_HALO)
              + 0.5 * b_lc_ref[:, lo:lo + LANES])
        xl_s[0:n_conv, lo:lo + LANES] = xl
    _lru_gates(_bf16(xl_s[...]), wg_bf, pre_s)

    hc_ls = (0.5 * LRU_C) * _log_sigmoid(lam_ref[...])
    hb_ga = 0.5 * b_ga_ref[...]
    hb_gx = 0.5 * b_gx_ref[...]
    for l in range(N_LRU_SLABS):
        lo = l * LANES
        ca, cx = _pre_cols(l)
        a, u = _lru_coeffs(pre_s[:, ca:ca + LANES], pre_s[:, cx:cx + LANES], xl_s[:, lo:lo + LANES],
                           hb_ga[:, lo:lo + LANES], hb_gx[:, lo:lo + LANES],
                           hc_ls[:, lo:lo + LANES])
        a_s[l, :, :] = a
        u_s[l, :, :] = u

    n_grp = nb // SUBLANES

    def scan_body(j, carry):
        new = []
        for g in range(n_grp):
            for l in range(N_LRU_SLABS):
                base = g * SUBLANES * pitch
                a_j = a_s[l, pl.ds(base + j, SUBLANES, stride=pitch), :]
                u_j = u_s[l, pl.ds(base + j, SUBLANES, stride=pitch), :]
                h = a_j * carry[g * N_LRU_SLABS + l] + u_j
                u_s[l, pl.ds(base + j, SUBLANES, stride=pitch), :] = h
                new.append(h)
        return tuple(new)

    h0 = tuple(h0_ref[g * SUBLANES:(g + 1) * SUBLANES, l * LANES:(l + 1) * LANES]
               for g in range(n_grp) for l in range(N_LRU_SLABS))
    h_fin = _unrolled(steps, scan_body, h0)
    for g in range(n_grp):
        for l in range(N_LRU_SLABS):
            h_out_ref[g * SUBLANES:(g + 1) * SUBLANES, l * LANES:(l + 1) * LANES] = (
                h_fin[g * N_LRU_SLABS + l])

    for b in range(nb):
        for l in range(N_CONV_SLABS):
            lo = l * LANES
            uc = _gated_of_twice(
                z_s[b * steps:(b + 1) * steps, OFF_GLU_A + lo:OFF_GLU_A + lo + LANES],
                z_s[b * steps:(b + 1) * steps, OFF_GLU_B + lo:OFF_GLU_B + lo + LANES])
            uc_s[l, b * cpitch + CONV_HALO:(b + 1) * cpitch, :] = uc
            for k in range(CONV_W - 1):
                r = b * cpitch + CONV_HALO - (CONV_W - 1) + k
                uc_s[l, r:r + 1, :] = cb_ref[k, b:b + 1, lo:lo + LANES]
                rn = (b + 1) * cpitch - (CONV_W - 1) + k
                cb_out_ref[k, b:b + 1, lo:lo + LANES] = uc_s[l, rn:rn + 1, :]

    for b in range(nb):
        r0 = b * cpitch
        o0 = b * steps
        ys = []
        for l in range(N_CONV_SLABS):
            lo = l * LANES
            w_row = lambda k, lo=lo: w_dw_ref[k, :, lo:lo + LANES]
            ys.append(_conv_taps(uc_s, l, w_row, r0, steps, CONV_W, CONV_HALO)
                      + b_dw_ref[:, lo:lo + LANES])
        outs = _layernorm_silu_slabs(ys, ln_g_ref, ln_b_ref)
        for l in range(N_CONV_SLABS):
            lo = l * LANES
            ln_s[o0:o0 + steps, lo:lo + LANES] = _bf16(outs[l])

    yc = (_dot(ln_s[...], wpw_bf[...]) + b_pw_ref[...]) * _silu_of_twice(
        z_s[:, OFF_CONV_GATE:OFF_CONV_GATE + D_CONV])
    mix_s[:, D_LRU:D_LRU + D_CONV] = _bf16(yc)

    masks = _lane_head_masks()
    scale = SCORE_SCALE
    qrows = N_XHEADS * steps
    for copy in kv_copies(group):
        copy.wait()
    for b in range(nb):
        for l in range(N_LRU_SLABS):
            lo = l * LANES
            gate = z_s[b * steps:(b + 1) * steps, OFF_LRU_GATE + lo:OFF_LRU_GATE + lo + LANES]
            mix_s[b * steps:(b + 1) * steps, lo:lo + LANES] = _bf16(
                u_s[l, b * pitch:b * pitch + steps, :] * _silu_of_twice(gate))
    for b in range(nb):
        q = z_s[b * steps:(b + 1) * steps, OFF_Q:OFF_Q + D_XATTN]
        qm = jnp.concatenate([_bf16(q * (m * scale)) for m in masks], axis=0)
        sc_s[b * qrows:(b + 1) * qrows, :] = _dot(qm, _bf16(kt_ref[first + b]))
    s = sc_s[...]
    e = jnp.exp2(s - jnp.max(s, axis=-1, keepdims=True))
    p_s[...] = _bf16(e * (1.0 / jnp.sum(e, axis=-1, keepdims=True)))
    for b in range(nb):
        o_all = _dot_nt(p_s[b * qrows:(b + 1) * qrows, :], _bf16(vt_ref[first + b]))
        o = o_all[0:steps] * masks[0]
        for h in range(1, N_XHEADS):
            o = o + o_all[h * steps:(h + 1) * steps] * masks[h]
        gate = z_s[b * steps:(b + 1) * steps, OFF_ATTN_GATE:OFF_ATTN_GATE + D_XATTN]
        mix_s[b * steps:(b + 1) * steps, D_LRU + D_CONV:D_MIX] = _bf16(o * _silu_of_twice(gate))

    y = x + _dot(mix_s[...], wout_bf[...])
    y_ref[...] = _rmsnorm(y, g_final_ref[...]).reshape(nb, steps, D_MODEL)


def _sample(x, h0, lb, cb, kt, vt, w, win_bf, wout_bf, wpw_bf, wg_bf):
    bsz, steps, _ = x.shape
    nb = SAMPLE_GROUP
    assert bsz % nb == 0 and nb % SUBLANES == 0
    assert steps + LRU_HALO == SAMPLE_PITCH and steps >= CONV_W - 1
    rows = nb * steps
    prow = nb * SAMPLE_PITCH
    weights = [win_bf, w["w_lc"], w["b_lc"], wg_bf, w["b_ga"], w["b_gx"],
               w["lam"], w["w_dw"], w["b_dw"], w["ln_g"], w["ln_b"], wpw_bf, w["b_pw"],
               wout_bf, w["g_final"]]
    weight_specs = [
        _whole((D_MODEL, D_IN)), _whole((None, LRU_CONV, D_LRU)),
        _whole((1, D_LRU)), _whole((2, GATE_HALF, D_LRU)), _whole((1, D_LRU)), _whole((1, D_LRU)),
        _whole((1, D_LRU)), _whole((CONV_W, 1, D_CONV)), _whole((1, D_CONV)), _whole((1, D_CONV)),
        _whole((1, D_CONV)), _whole((D_CONV, D_CONV)), _whole((1, D_CONV)),
        _whole((D_MIX, D_MODEL)), _whole((1, D_MODEL)),
    ]
    in_specs = [
        pl.BlockSpec((nb, steps, D_MODEL), lambda g: (g, 0, 0)),
        pl.BlockSpec((nb, D_LRU), lambda g: (g, 0)),
        pl.BlockSpec((LRU_CONV - 1, nb, D_LRU), lambda g: (0, g, 0)),
        pl.BlockSpec((CONV_W - 1, nb, D_CONV), lambda g: (0, g, 0)),
        pl.BlockSpec(memory_space=pl.ANY),
        pl.BlockSpec(memory_space=pl.ANY),
    ] + weight_specs
    out_specs = [
        pl.BlockSpec((nb, steps, D_MODEL), lambda g: (g, 0, 0)),
        pl.BlockSpec((nb, D_LRU), lambda g: (g, 0)),
        pl.BlockSpec((LRU_CONV - 1, nb, D_LRU), lambda g: (0, g, 0)),
        pl.BlockSpec((CONV_W - 1, nb, D_CONV), lambda g: (0, g, 0)),
    ]
    out_shape = [
        jax.ShapeDtypeStruct((bsz, steps, D_MODEL), jnp.float32),
        jax.ShapeDtypeStruct((bsz, D_LRU), jnp.float32),
        jax.ShapeDtypeStruct((LRU_CONV - 1, bsz, D_LRU), jnp.float32),
        jax.ShapeDtypeStruct((CONV_W - 1, bsz, D_CONV), jnp.float32),
    ]
    scratch = [
        pltpu.VMEM((rows, D_IN), jnp.float32),
        pltpu.VMEM((N_LRU_SLABS, prow, LANES), jnp.float32),
        pltpu.VMEM((prow, D_LRU), jnp.float32),
        pltpu.VMEM((prow, 2 * D_LRU), jnp.float32),
        pltpu.VMEM((N_LRU_SLABS, prow, LANES), jnp.float32),
        pltpu.VMEM((N_LRU_SLABS, prow, LANES), jnp.float32),
        pltpu.VMEM((N_CONV_SLABS, nb * (CONV_HALO + steps), LANES), jnp.float32),
        pltpu.VMEM((rows, D_CONV), jnp.bfloat16),
        pltpu.VMEM((rows, D_MIX), jnp.bfloat16),
        pltpu.VMEM((N_XHEADS * rows, N_MEM), jnp.float32),
        pltpu.VMEM((N_XHEADS * rows, N_MEM), jnp.bfloat16),
        pltpu.VMEM((bsz, D_XATTN, N_MEM), jnp.float32),
        pltpu.VMEM((bsz, D_XATTN, N_MEM), jnp.float32),
        pltpu.SemaphoreType.DMA((2 * (bsz // nb),)),
    ]
    return pl.pallas_call(
        functools.partial(_sample_kernel, nb, steps),
        grid=(bsz // nb,),
        in_specs=in_specs,
        out_specs=out_specs,
        out_shape=out_shape,
        scratch_shapes=scratch,
        compiler_params=pltpu.CompilerParams(
            dimension_semantics=("arbitrary",),
            vmem_limit_bytes=VMEM_LIMIT_BYTES),
        name="sample_layer",
    )(x, h0, lb, cb, kt, vt, *weights)


def _time_major(state):
    return jnp.transpose(state, (1, 0, 2))


def _memory_t(mem):
    bsz = mem.shape[0]
    return jnp.transpose(mem, (0, 2, 3, 1)).reshape(bsz, D_XATTN, N_MEM)


def _memory_from_t(mem_t):
    bsz = mem_t.shape[0]
    return jnp.transpose(mem_t.reshape(bsz, N_XHEADS, XHEAD_DIM, N_MEM), (0, 3, 1, 2))


def kernel(x_prompt, x_sample, state_lru_h, state_lru_conv, state_conv, cache_mem_k, cache_mem_v,
           mem_prompt, g_norm, w_in, w_lru_conv, b_lru_conv, w_gate_a, b_gate_a, w_gate_x, b_gate_x,
           lru_lambda, w_dw, b_dw, ln_g, ln_b, w_pw, b_pw, g_mem, w_mem_k, w_mem_v, w_out, g_final):
    depth = g_norm.shape[0]
    assert depth == 1
    w = {
        "g_norm": g_norm, "w_in": w_in, "w_lc": w_lru_conv, "b_lc": b_lru_conv,
        "w_ga": w_gate_a, "w_gx": w_gate_x, "b_ga": b_gate_a, "b_gx": b_gate_x, "lam": lru_lambda,
        "w_dw": jnp.transpose(w_dw, (1, 0, 2)), "b_dw": b_dw, "ln_g": ln_g, "ln_b": ln_b,
        "w_pw": w_pw, "b_pw": b_pw, "w_out": w_out, "g_final": g_final.reshape(1, D_MODEL),
        "g_mem": g_mem, "w_mem_k": w_mem_k, "w_mem_v": w_mem_v,
    }
    y_prompt, mkt, mvt, ph, plb, pcb, win_bf, wout_bf, wpw_bf, wg_bf = _prompt(x_prompt, mem_prompt, w)

    y_sample, sh, slb, scb = _sample(
        x_sample, state_lru_h[0], _time_major(state_lru_conv[0]), _time_major(state_conv[0]),
        _memory_t(cache_mem_k[0]), _memory_t(cache_mem_v[0]), w, win_bf, wout_bf, wpw_bf, wg_bf)

    return (y_prompt, y_sample,
            ph[None], _time_major(plb)[None], _time_major(pcb)[None],
            _memory_from_t(mkt)[None], _memory_from_t(mvt)[None],
            sh[None], _time_major(slb)[None], _time_major(scb)[None])
```

```python
import functools

import jax
import jax.numpy as jnp
from jax import lax
from jax.experimental import pallas as pl
from jax.experimental.pallas import tpu as pltpu

D_MODEL = 1024
N_MEM = 256
D_LRU = 512
LRU_BLOCKS = 8
LRU_BLOCK = D_LRU // LRU_BLOCKS
LRU_CONV = 4
LRU_C = 8.0
D_CONV = 256
CONV_W = 31
N_XHEADS = 4
XHEAD_DIM = 64
D_XATTN = N_XHEADS * XHEAD_DIM
D_MIX = D_LRU + D_CONV + D_XATTN
D_IN = 2 * D_LRU + 3 * D_CONV + 2 * D_XATTN
EPS = 1e-6

LANES = 128
SUBLANES = 8
N_LRU_SLABS = D_LRU // LANES
N_CONV_SLABS = D_CONV // LANES
GATE_HALF = D_LRU // 2

OFF_LRU_IN = 0
OFF_LRU_GATE = D_LRU
OFF_GLU_A = 2 * D_LRU
OFF_GLU_B = OFF_GLU_A + D_CONV
OFF_CONV_GATE = OFF_GLU_B + D_CONV
OFF_Q = OFF_CONV_GATE + D_CONV
OFF_ATTN_GATE = OFF_Q + D_XATTN

LRU_HALO = 8
CONV_HALO = 32

PROMPT_TILE = 512
PROMPT_SEG = 64
PROMPT_PITCH = 68
SAMPLE_PITCH = 40
SAMPLE_GROUP = 16
ATTN_ROWS = 128
CONV_ROWS = 32
PROJ_ROWS = 256

VMEM_LIMIT_BYTES = 56 * 1024 * 1024
HALF_LOG2_E = 0.7213475204444817
SCORE_SCALE = XHEAD_DIM ** -0.5 * 2.0 * HALF_LOG2_E


def _silu_of_twice(hx):
    return hx + hx * jnp.tanh(hx)


def _gated_of_twice(ha, hb):
    return ha + ha * jnp.tanh(hb)


def _w_in_col_scale():
    col = lax.broadcasted_iota(jnp.int32, (1, D_IN), 1)
    plain = (col < OFF_LRU_GATE) | ((col >= OFF_Q) & (col < OFF_ATTN_GATE))
    return jnp.where(plain, 1.0, 0.5)


def _sqrt_nonneg(x):
    return jnp.exp2(jnp.log(x) * HALF_LOG2_E)


def _rms_scaled(x):
    ms = jnp.mean(x * x, axis=-1, keepdims=True)
    return x * lax.rsqrt(ms + EPS)


def _rmsnorm(x, g):
    return _rms_scaled(x) * g


def _as_column(row):
    n = row.shape[1] // LANES
    on_diag = (lax.broadcasted_iota(jnp.int32, (LANES, LANES), 0)
               == lax.broadcasted_iota(jnp.int32, (LANES, LANES), 1))
    cols = [jnp.sum(jnp.where(on_diag, row[:, j * LANES:(j + 1) * LANES], 0.0), axis=1, keepdims=True)
            for j in range(n)]
    return jnp.concatenate(cols, axis=0)


def _log_sigmoid(x):
    y = -x
    return -(jnp.maximum(y, 0.0) + jnp.log1p(jnp.exp(-jnp.abs(y))))


def _bf16(x):
    return x.astype(jnp.bfloat16)


def _dot(a, b):
    return jnp.dot(a, b, preferred_element_type=jnp.float32)


def _dot_nt(a, b):
    return lax.dot_general(a, b, (((1,), (1,)), ((), ())), preferred_element_type=jnp.float32)


def _unrolled(n, body, init):
    carry = init
    for i in range(n):
        carry = body(i, carry)
    return carry


def _aligned(x, m):
    return x if isinstance(x, int) else pl.multiple_of(x, m)


def _head_of(index):
    return lax.shift_right_logical(index, XHEAD_DIM.bit_length() - 1)


def _lane_head_masks():
    head = _head_of(lax.broadcasted_iota(jnp.int32, (1, D_XATTN), 1))
    return [(head == h).astype(jnp.float32) for h in range(N_XHEADS)]


def _row_head_masks():
    head = _head_of(lax.broadcasted_iota(jnp.int32, (D_XATTN, 1), 0))
    return [(head == h).astype(jnp.float32) for h in range(N_XHEADS)]


def _head_stack_t(kt, vt, kst_ref, vst_ref):
    scale = SCORE_SCALE
    for h, m in enumerate(_row_head_masks()):
        kst_ref[:, h * N_MEM:(h + 1) * N_MEM] = _bf16(kt * (m * scale))
        vst_ref[:, h * N_MEM:(h + 1) * N_MEM] = _bf16(vt * m)


def _zero_row_after(x):
    bits = lax.bitcast_convert_type(x, jnp.uint32)
    zero = lax.shift_right_logical(lax.shift_right_logical(bits, jnp.uint32(16)), jnp.uint32(16))
    return lax.bitcast_convert_type(zero, jnp.float32)[0:1, :]


def _softmax_heads(s):
    ps = []
    for h in range(N_XHEADS):
        sh = s[:, h * N_MEM:(h + 1) * N_MEM]
        e = jnp.exp2(sh - jnp.max(sh, axis=-1, keepdims=True))
        inv = 1.0 / jnp.sum(e, axis=-1, keepdims=True)
        ps.append(_bf16(e * inv))
    return jnp.concatenate(ps, axis=1)


def _lru_gates(xl_bf, wg_ref, pre_s):
    pre_s[:, 0:D_LRU] = _dot(xl_bf[:, 0:GATE_HALF], wg_ref[0])
    pre_s[:, D_LRU:2 * D_LRU] = _dot(xl_bf[:, GATE_HALF:D_LRU], wg_ref[1])


def _pre_cols(l):
    half_slabs = N_LRU_SLABS // 2
    base = (l // half_slabs) * D_LRU + (l % half_slabs) * LANES
    return base, base + GATE_HALF


def _lru_coeffs(hpre_a, hpre_x, hx, hb_a, hb_x, hc_ls):
    t_r = jnp.tanh(hpre_a + hb_a)
    t_i = jnp.tanh(hpre_x + hb_x)
    log_a = hc_ls + hc_ls * t_r
    a = jnp.exp(log_a)
    one_minus_a2 = -jnp.tanh(log_a) * (a * a + 1.0)
    u = _sqrt_nonneg(one_minus_a2) * (hx + hx * t_i)
    return a, u


def _conv_taps(src_s, l, w_row, row0, nrows, width, halo):
    acc = None
    for k in range(width):
        x_k = src_s[l, pl.ds(row0 + (halo - (width - 1) + k), nrows), :]
        acc = w_row(k) * x_k if acc is None else acc + w_row(k) * x_k
    return acc


def _layernorm_silu_slabs(ys, g_ref, b_ref):
    tot = ys[0]
    for y in ys[1:]:
        tot = tot + y
    mu = jnp.sum(tot, axis=-1, keepdims=True) * (1.0 / D_CONV)
    cs = [y - mu for y in ys]
    sq = cs[0] * cs[0]
    for c in cs[1:]:
        sq = sq + c * c
    var = jnp.sum(sq, axis=-1, keepdims=True) * (1.0 / D_CONV)
    inv = lax.rsqrt(var + EPS)
    outs = []
    for l, c in enumerate(cs):
        lo = l * LANES
        half_g = 0.5 * g_ref[:, lo:lo + LANES]
        half_b = 0.5 * b_ref[:, lo:lo + LANES]
        outs.append(_silu_of_twice(c * inv * half_g + half_b))
    return outs


def _gate_block_diag(w_blocks):
    n = (LRU_BLOCKS // 2) * LRU_BLOCK
    stacked = _bf16(w_blocks.reshape(n, LRU_BLOCK))
    src = lax.broadcasted_iota(jnp.int32, (LRU_BLOCK, n), 0)
    dst = lax.broadcasted_iota(jnp.int32, (LRU_BLOCK, n), 1)
    spread = _bf16(((dst & (LRU_BLOCK - 1)) == src).astype(jnp.float32))
    rows = _head_of(lax.broadcasted_iota(jnp.int32, (n, n), 0))
    cols = _head_of(lax.broadcasted_iota(jnp.int32, (n, n), 1))
    return _bf16(_dot(stacked, spread) * (rows == cols).astype(jnp.float32))


def _prompt_kernel(x_ref, mem_ref, g_mem_ref, wk_ref, wv_ref,
                   g_norm_ref, w_in_ref, w_lc_ref, b_lc_ref, wga_ref, wgx_ref,
                   b_ga_ref, b_gx_ref, lam_ref, w_dw_ref, b_dw_ref, ln_g_ref, ln_b_ref,
                   w_pw_ref, b_pw_ref, w_out_ref, g_final_ref,
                   y_ref, kt_ref, vt_ref, h_out_ref, lb_out_ref, cb_out_ref,
                   win_hbm, wout_hbm, wpw_hbm, wg_hbm,
                   z_s, xc_s, xl_s, pre_s, a_s, u_s, uc_s, ln_s, mix_s, hin_s, kst_s, vst_s, sc_s, p_s,
                   win_bf, wout_bf, wpw_bf, wg_bf, w_sem):
    b = pl.program_id(0)
    t = pl.program_id(1)
    n_t = pl.num_programs(1)
    tile = PROMPT_TILE
    seg = PROMPT_SEG
    n_seg = tile // seg
    pitch = PROMPT_PITCH

    weight_copies = [
        pltpu.make_async_copy(src, dst, w_sem.at[i])
        for i, (src, dst) in enumerate([(win_bf, win_hbm), (wout_bf, wout_hbm),
                                        (wpw_bf, wpw_hbm), (wg_bf, wg_hbm)])]

    @pl.when(jnp.logical_and(b == 0, t == 0))
    def _():
        win_bf[...] = _bf16(w_in_ref[...] * _as_column(g_norm_ref[...]) * _w_in_col_scale())
        wout_bf[...] = _bf16(w_out_ref[...])
        wpw_bf[...] = _bf16(w_pw_ref[...])
        half = LRU_BLOCKS // 2
        for h in range(2):
            wg_bf[h, :, 0:GATE_HALF] = _gate_block_diag(wga_ref[h * half:(h + 1) * half])
            wg_bf[h, :, GATE_HALF:D_LRU] = _gate_block_diag(wgx_ref[h * half:(h + 1) * half])
        for copy in weight_copies:
            copy.start()

    @pl.when(t == 0)
    def _():
        xc_s[:, 0:LRU_HALO, :] = jnp.zeros((N_LRU_SLABS, LRU_HALO, LANES), jnp.float32)
        uc_s[:, 0:CONV_HALO, :] = jnp.zeros((N_CONV_SLABS, CONV_HALO, LANES), jnp.float32)
        hin_s[:, 0:1, :] = jnp.zeros((N_LRU_SLABS, 1, LANES), jnp.float32)
        m = _bf16(_rmsnorm(mem_ref[0], g_mem_ref[...]))
        kt = _dot(m, _bf16(wk_ref[...])).T
        vt = _dot(m, _bf16(wv_ref[...])).T
        kt_ref[0] = kt
        vt_ref[0] = vt
        _head_stack_t(kt, vt, kst_s, vst_s)

    for r0 in range(0, tile, PROJ_ROWS):
        z_s[r0:r0 + PROJ_ROWS, :] = _dot(_bf16(_rms_scaled(x_ref[0, r0:r0 + PROJ_ROWS, :])), win_bf[...])

    for l in range(N_LRU_SLABS):
        lo = l * LANES
        xc_s[l, LRU_HALO:LRU_HALO + tile, :] = z_s[:, OFF_LRU_IN + lo:OFF_LRU_IN + lo + LANES]
    for l in range(N_LRU_SLABS):
        lo = l * LANES
        w_row = lambda k, lo=lo: 0.5 * w_lc_ref[k:k + 1, lo:lo + LANES]
        xl = (_conv_taps(xc_s, l, w_row, 0, tile, LRU_CONV, LRU_HALO)
              + 0.5 * b_lc_ref[:, lo:lo + LANES])
        xl_s[:, lo:lo + LANES] = xl
    _lru_gates(_bf16(xl_s[...]), wg_bf, pre_s)

    hc_ls = (0.5 * LRU_C) * _log_sigmoid(lam_ref[...])
    hb_ga = 0.5 * b_ga_ref[...]
    hb_gx = 0.5 * b_gx_ref[...]

    def coeff_body(c, carry):
        r0 = _aligned(c * seg, seg)
        p0 = _aligned(c * pitch, SUBLANES)
        for l in range(N_LRU_SLABS):
            lo = l * LANES
            ca, cx = _pre_cols(l)
            a, u = _lru_coeffs(pre_s[pl.ds(r0, seg), ca:ca + LANES],
                               pre_s[pl.ds(r0, seg), cx:cx + LANES],
                               xl_s[pl.ds(r0, seg), lo:lo + LANES],
                               hb_ga[:, lo:lo + LANES], hb_gx[:, lo:lo + LANES],
                               hc_ls[:, lo:lo + LANES])
            a_s[l, pl.ds(p0, seg), :] = a
            u_s[l, pl.ds(p0, seg), :] = u
        return carry

    _unrolled(n_seg, coeff_body, 0)

    def tot_body(j, carry):
        new = []
        for l in range(N_LRU_SLABS):
            a_tot, u_tot = carry[l]
            a_j = a_s[l, pl.ds(j, n_seg, stride=pitch), :]
            u_j = u_s[l, pl.ds(j, n_seg, stride=pitch), :]
            new.append((a_j * a_tot, a_j * u_tot + u_j))
        return tuple(new)

    init = tuple((jnp.ones((n_seg, LANES), jnp.float32), jnp.zeros((n_seg, LANES), jnp.float32))
                 for _ in range(N_LRU_SLABS))
    totals = _unrolled(seg, tot_body, init)

    h_in = []
    for l in range(N_LRU_SLABS):
        a_tot, u_tot = totals[l]
        h = hin_s[l, 0:1, :]
        for c in range(n_seg):
            h = a_tot[c:c + 1, :] * h + u_tot[c:c + 1, :]
            hin_s[l, c + 1:c + 2, :] = h
        h_in.append(hin_s[l, 0:n_seg, :])
        hin_s[l, 0:1, :] = h

    def scan_body(j, carry):
        new = []
        for l in range(N_LRU_SLABS):
            a_j = a_s[l, pl.ds(j, n_seg, stride=pitch), :]
            u_j = u_s[l, pl.ds(j, n_seg, stride=pitch), :]
            h = a_j * carry[l] + u_j
            u_s[l, pl.ds(j, n_seg, stride=pitch), :] = h
            new.append(h)
        return tuple(new)

    scores = _dot(_bf16(z_s[:, OFF_Q:OFF_Q + D_XATTN]), kst_s[...])
    sc_s[...] = scores
    after_scores = _zero_row_after(scores[tile - SUBLANES:tile, N_XHEADS * N_MEM - LANES:])
    _unrolled(seg, scan_body, tuple(h + after_scores for h in h_in))

    for l in range(N_CONV_SLABS):
        lo = l * LANES
        uc_s[l, CONV_HALO:CONV_HALO + tile, :] = _gated_of_twice(
            z_s[:, OFF_GLU_A + lo:OFF_GLU_A + lo + LANES],
            z_s[:, OFF_GLU_B + lo:OFF_GLU_B + lo + LANES])

    def seg_body(c, carry):
        r0 = _aligned(c * seg, seg)
        p0 = _aligned(c * pitch, SUBLANES)
        for l in range(N_LRU_SLABS):
            lo = l * LANES
            gate = z_s[pl.ds(r0, seg), OFF_LRU_GATE + lo:OFF_LRU_GATE + lo + LANES]
            mix_s[pl.ds(r0, seg), lo:lo + LANES] = _bf16(u_s[l, pl.ds(p0, seg), :] * _silu_of_twice(gate))
        for piece in range(seg // CONV_ROWS):
            rr = r0 + piece * CONV_ROWS
            ys = []
            for l in range(N_CONV_SLABS):
                lo = l * LANES
                w_row = lambda k, lo=lo: w_dw_ref[k, :, lo:lo + LANES]
                ys.append(_conv_taps(uc_s, l, w_row, rr, CONV_ROWS, CONV_W, CONV_HALO)
                          + b_dw_ref[:, lo:lo + LANES])
            outs = _layernorm_silu_slabs(ys, ln_g_ref, ln_b_ref)
            for l in range(N_CONV_SLABS):
                lo = l * LANES
                ln_s[pl.ds(rr, CONV_ROWS), lo:lo + LANES] = _bf16(outs[l])
        return carry

    _unrolled(n_seg, seg_body, 0)

    yc = (_dot(ln_s[...], wpw_bf[...]) + b_pw_ref[...]) * _silu_of_twice(
        z_s[:, OFF_CONV_GATE:OFF_CONV_GATE + D_CONV])
    mix_s[:, D_LRU:D_LRU + D_CONV] = _bf16(yc)

    def attn_body(c, carry):
        r0 = _aligned(c * ATTN_ROWS, ATTN_ROWS)
        p_s[pl.ds(r0, ATTN_ROWS), :] = _softmax_heads(sc_s[pl.ds(r0, ATTN_ROWS), :])
        return carry

    _unrolled(tile // ATTN_ROWS, attn_body, 0)
    o = _dot_nt(p_s[...], vst_s[...])
    mix_s[:, D_LRU + D_CONV:D_MIX] = _bf16(
        o * _silu_of_twice(z_s[:, OFF_ATTN_GATE:OFF_ATTN_GATE + D_XATTN]))

    y = x_ref[0] + _dot(mix_s[...], wout_bf[...])
    y_ref[0] = _rmsnorm(y, g_final_ref[...])

    for l in range(N_LRU_SLABS):
        xc_s[l, LRU_HALO - (LRU_CONV - 1):LRU_HALO, :] = (
            xc_s[l, LRU_HALO + tile - (LRU_CONV - 1):LRU_HALO + tile, :])
    for l in range(N_CONV_SLABS):
        uc_s[l, CONV_HALO - (CONV_W - 1):CONV_HALO, :] = (
            uc_s[l, CONV_HALO + tile - (CONV_W - 1):CONV_HALO + tile, :])

    def write_state(bb):
        for l in range(N_LRU_SLABS):
            lo = l * LANES
            h_out_ref[bb:bb + 1, lo:lo + LANES] = hin_s[l, 0:1, :]
            for k in range(LRU_CONV - 1):
                r = LRU_HALO - (LRU_CONV - 1) + k
                lb_out_ref[k, bb:bb + 1, lo:lo + LANES] = xc_s[l, r:r + 1, :]
        for l in range(N_CONV_SLABS):
            lo = l * LANES
            for k in range(CONV_W - 1):
                r = CONV_HALO - (CONV_W - 1) + k
                cb_out_ref[k, bb:bb + 1, lo:lo + LANES] = uc_s[l, r:r + 1, :]

    for bb in range(h_out_ref.shape[0]):
        pl.when(jnp.logical_and(t == n_t - 1, b == bb))(functools.partial(write_state, bb))

    @pl.when(jnp.logical_and(t == n_t - 1, b == h_out_ref.shape[0] - 1))
    def _():
        for copy in weight_copies:
            copy.wait()


def _whole(shape):
    nd = len(shape)
    return pl.BlockSpec(shape, lambda *_: (0,) * nd, pipeline_mode=pl.Buffered(1))


def _prompt(x, mem, w):
    bsz, seq, _ = x.shape
    tile = PROMPT_TILE
    n_t = seq // tile
    n_seg = tile // PROMPT_SEG
    weights = [w["g_mem"], w["w_mem_k"], w["w_mem_v"],
               w["g_norm"], w["w_in"], w["w_lc"], w["b_lc"], w["w_ga"], w["w_gx"], w["b_ga"], w["b_gx"],
               w["lam"], w["w_dw"], w["b_dw"], w["ln_g"], w["ln_b"], w["w_pw"], w["b_pw"],
               w["w_out"], w["g_final"]]
    weight_specs = [
        _whole((1, D_MODEL)), _whole((None, D_MODEL, D_XATTN)), _whole((None, D_MODEL, D_XATTN)),
        _whole((1, D_MODEL)), _whole((None, D_MODEL, D_IN)), _whole((None, LRU_CONV, D_LRU)),
        _whole((1, D_LRU)), _whole((None, LRU_BLOCKS, LRU_BLOCK, LRU_BLOCK)),
        _whole((None, LRU_BLOCKS, LRU_BLOCK, LRU_BLOCK)), _whole((1, D_LRU)), _whole((1, D_LRU)),
        _whole((1, D_LRU)), _whole((CONV_W, 1, D_CONV)), _whole((1, D_CONV)), _whole((1, D_CONV)),
        _whole((1, D_CONV)), _whole((None, D_CONV, D_CONV)), _whole((1, D_CONV)),
        _whole((None, D_MIX, D_MODEL)), _whole((1, D_MODEL)),
    ]
    in_specs = [
        pl.BlockSpec((1, tile, D_MODEL), lambda b, t: (b, t, 0)),
        pl.BlockSpec((1, N_MEM, D_MODEL), lambda b, t: (b, 0, 0)),
    ] + weight_specs
    out_specs = [
        pl.BlockSpec((1, tile, D_MODEL), lambda b, t: (b, t, 0)),
        pl.BlockSpec((1, D_XATTN, N_MEM), lambda b, t: (b, 0, 0)),
        pl.BlockSpec((1, D_XATTN, N_MEM), lambda b, t: (b, 0, 0)),
        _whole((bsz, D_LRU)),
        _whole((LRU_CONV - 1, bsz, D_LRU)),
        _whole((CONV_W - 1, bsz, D_CONV)),
    ] + [pl.BlockSpec(memory_space=pl.ANY)] * 4
    out_shape = [
        jax.ShapeDtypeStruct((bsz, seq, D_MODEL), jnp.float32),
        jax.ShapeDtypeStruct((bsz, D_XATTN, N_MEM), jnp.float32),
        jax.ShapeDtypeStruct((bsz, D_XATTN, N_MEM), jnp.float32),
        jax.ShapeDtypeStruct((bsz, D_LRU), jnp.float32),
        jax.ShapeDtypeStruct((LRU_CONV - 1, bsz, D_LRU), jnp.float32),
        jax.ShapeDtypeStruct((CONV_W - 1, bsz, D_CONV), jnp.float32),
        jax.ShapeDtypeStruct((D_MODEL, D_IN), jnp.bfloat16),
        jax.ShapeDtypeStruct((D_MIX, D_MODEL), jnp.bfloat16),
        jax.ShapeDtypeStruct((D_CONV, D_CONV), jnp.bfloat16),
        jax.ShapeDtypeStruct((2, GATE_HALF, D_LRU), jnp.bfloat16),
    ]
    scratch = [
        pltpu.VMEM((tile, D_IN), jnp.float32),
        pltpu.VMEM((N_LRU_SLABS, LRU_HALO + tile, LANES), jnp.float32),
        pltpu.VMEM((tile, D_LRU), jnp.float32),
        pltpu.VMEM((tile, 2 * D_LRU), jnp.float32),
        pltpu.VMEM((N_LRU_SLABS, n_seg * PROMPT_PITCH, LANES), jnp.float32),
        pltpu.VMEM((N_LRU_SLABS, n_seg * PROMPT_PITCH, LANES), jnp.float32),
        pltpu.VMEM((N_CONV_SLABS, CONV_HALO + tile, LANES), jnp.float32),
        pltpu.VMEM((tile, D_CONV), jnp.bfloat16),
        pltpu.VMEM((tile, D_MIX), jnp.bfloat16),
        pltpu.VMEM((N_LRU_SLABS, 2 * SUBLANES, LANES), jnp.float32),
        pltpu.VMEM((D_XATTN, N_XHEADS * N_MEM), jnp.bfloat16),
        pltpu.VMEM((D_XATTN, N_XHEADS * N_MEM), jnp.bfloat16),
        pltpu.VMEM((tile, N_XHEADS * N_MEM), jnp.float32),
        pltpu.VMEM((tile, N_XHEADS * N_MEM), jnp.bfloat16),
        pltpu.VMEM((D_MODEL, D_IN), jnp.bfloat16),
        pltpu.VMEM((D_MIX, D_MODEL), jnp.bfloat16),
        pltpu.VMEM((D_CONV, D_CONV), jnp.bfloat16),
        pltpu.VMEM((2, GATE_HALF, D_LRU), jnp.bfloat16),
        pltpu.SemaphoreType.DMA((4,)),
    ]
    return pl.pallas_call(
        _prompt_kernel,
        grid=(bsz, n_t),
        in_specs=in_specs,
        out_specs=out_specs,
        out_shape=out_shape,
        scratch_shapes=scratch,
        compiler_params=pltpu.CompilerParams(
            dimension_semantics=("arbitrary", "arbitrary"),
            vmem_limit_bytes=VMEM_LIMIT_BYTES),
        name="prompt_layer",
    )(x, mem, *weights)


def _sample_kernel(nb, steps,
                   x_ref, h0_ref, lb_ref, cb_ref, kt_hbm, vt_hbm,
                   win_bf, w_lc_ref, b_lc_ref, wg_bf,
                   b_ga_ref, b_gx_ref, lam_ref, w_dw_ref, b_dw_ref, ln_g_ref, ln_b_ref,
                   wpw_bf, b_pw_ref, wout_bf, g_final_ref,
                   y_ref, h_out_ref, lb_out_ref, cb_out_ref,
                   z_s, xc_s, xl_s, pre_s, a_s, u_s, uc_s, ln_s, mix_s, sc_s, p_s,
                   kt_ref, vt_ref, kv_sem):
    rows = nb * steps
    pitch = SAMPLE_PITCH
    cpitch = CONV_HALO + steps
    prow = nb * pitch

    group = pl.program_id(0)
    first = group * nb
    n_groups = kt_ref.shape[0] // nb

    def kv_copies(grp):
        rows_of = pl.ds(grp * nb, nb)
        return (pltpu.make_async_copy(kt_hbm.at[rows_of], kt_ref.at[rows_of], kv_sem.at[2 * grp]),
                pltpu.make_async_copy(vt_hbm.at[rows_of], vt_ref.at[rows_of], kv_sem.at[2 * grp + 1]))

    @pl.when(group == 0)
    def _():
        for grp in range(n_groups):
            for copy in kv_copies(grp):
                copy.start()
        xl_s[prow - LRU_HALO:prow, :] = jnp.zeros((LRU_HALO, D_LRU), jnp.float32)

    x = x_ref[...].reshape(rows, D_MODEL)
    z_s[...] = _dot(_bf16(_rms_scaled(x)), win_bf[...])

    for b in range(nb):
        for l in range(N_LRU_SLABS):
            lo = l * LANES
            zcol = OFF_LRU_IN + lo
            xc_s[l, b * pitch:b * pitch + LRU_HALO, :] = jnp.zeros((LRU_HALO, LANES), jnp.float32)
            for k in range(LRU_CONV - 1):
                r = b * pitch + LRU_HALO - (LRU_CONV - 1) + k
                xc_s[l, r:r + 1, :] = lb_ref[k, b:b + 1, lo:lo + LANES]
                rn = (b + 1) * steps - (LRU_CONV - 1) + k
                lb_out_ref[k, b:b + 1, lo:lo + LANES] = z_s[rn:rn + 1, zcol:zcol + LANES]
            xc_s[l, b * pitch + LRU_HALO:(b + 1) * pitch, :] = (
                z_s[b * steps:(b + 1) * steps, zcol:zcol + LANES])
    n_conv = prow - LRU_HALO
    for l in range(N_LRU_SLABS):
        lo = l * LANES
        w_row = lambda k, lo=lo: 0.5 * w_lc_ref[k:k + 1, lo:lo + LANES]
        xl = (_conv_taps(xc_s, l, w_row, 0, n_conv, LRU_CONV, LRU_HALO)
              + 0.5 * b_lc_ref[:, lo:lo + LANES])
        xl_s[0:n_conv, lo:lo + LANES] = xl
    _lru_gates(_bf16(xl_s[...]), wg_bf, pre_s)

    hc_ls = (0.5 * LRU_C) * _log_sigmoid(lam_ref[...])
    hb_ga = 0.5 * b_ga_ref[...]
    hb_gx = 0.5 * b_gx_ref[...]
    for l in range(N_LRU_SLABS):
        lo = l * LANES
        ca, cx = _pre_cols(l)
        a, u = _lru_coeffs(pre_s[:, ca:ca + LANES], pre_s[:, cx:cx + LANES], xl_s[:, lo:lo + LANES],
                           hb_ga[:, lo:lo + LANES], hb_gx[:, lo:lo + LANES],
                           hc_ls[:, lo:lo + LANES])
        a_s[l, :, :] = a
        u_s[l, :, :] = u

    n_grp = nb // SUBLANES

    def scan_body(j, carry):
        new = []
        for g in range(n_grp):
            for l in range(N_LRU_SLABS):
                base = g * SUBLANES * pitch
                a_j = a_s[l, pl.ds(base + j, SUBLANES, stride=pitch), :]
                u_j = u_s[l, pl.ds(base + j, SUBLANES, stride=pitch), :]
                h = a_j * carry[g * N_LRU_SLABS + l] + u_j
                u_s[l, pl.ds(base + j, SUBLANES, stride=pitch), :] = h
                new.append(h)
        return tuple(new)

    h0 = tuple(h0_ref[g * SUBLANES:(g + 1) * SUBLANES, l * LANES:(l + 1) * LANES]
               for g in range(n_grp) for l in range(N_LRU_SLABS))
    h_fin = _unrolled(steps, scan_body, h0)
    for g in range(n_grp):
        for l in range(N_LRU_SLABS):
            h_out_ref[g * SUBLANES:(g + 1) * SUBLANES, l * LANES:(l + 1) * LANES] = (
                h_fin[g * N_LRU_SLABS + l])

    for b in range(nb):
        for l in range(N_CONV_SLABS):
            lo = l * LANES
            uc = _gated_of_twice(
                z_s[b * steps:(b + 1) * steps, OFF_GLU_A + lo:OFF_GLU_A + lo + LANES],
                z_s[b * steps:(b + 1) * steps, OFF_GLU_B + lo:OFF_GLU_B + lo + LANES])
            uc_s[l, b * cpitch + CONV_HALO:(b + 1) * cpitch, :] = uc
            for k in range(CONV_W - 1):
                r = b * cpitch + CONV_HALO - (CONV_W - 1) + k
                uc_s[l, r:r + 1, :] = cb_ref[k, b:b + 1, lo:lo + LANES]
                rn = (b + 1) * cpitch - (CONV_W - 1) + k
                cb_out_ref[k, b:b + 1, lo:lo + LANES] = uc_s[l, rn:rn + 1, :]

    for b in range(nb):
        r0 = b * cpitch
        o0 = b * steps
        ys = []
        for l in range(N_CONV_SLABS):
            lo = l * LANES
            w_row = lambda k, lo=lo: w_dw_ref[k, :, lo:lo + LANES]
            ys.append(_conv_taps(uc_s, l, w_row, r0, steps, CONV_W, CONV_HALO)
                      + b_dw_ref[:, lo:lo + LANES])
        outs = _layernorm_silu_slabs(ys, ln_g_ref, ln_b_ref)
        for l in range(N_CONV_SLABS):
            lo = l * LANES
            ln_s[o0:o0 + steps, lo:lo + LANES] = _bf16(outs[l])

    yc = (_dot(ln_s[...], wpw_bf[...]) + b_pw_ref[...]) * _silu_of_twice(
        z_s[:, OFF_CONV_GATE:OFF_CONV_GATE + D_CONV])
    mix_s[:, D_LRU:D_LRU + D_CONV] = _bf16(yc)

    masks = _lane_head_masks()
    scale = SCORE_SCALE
    qrows = N_XHEADS * steps
    for copy in kv_copies(group):
        copy.wait()
    for b in range(nb):
        for l in range(N_LRU_SLABS):
            lo = l * LANES
            gate = z_s[b * steps:(b + 1) * steps, OFF_LRU_GATE + lo:OFF_LRU_GATE + lo + LANES]
            mix_s[b * steps:(b + 1) * steps, lo:lo + LANES] = _bf16(
                u_s[l, b * pitch:b * pitch + steps, :] * _silu_of_twice(gate))
    for b in range(nb):
        q = z_s[b * steps:(b + 1) * steps, OFF_Q:OFF_Q + D_XATTN]
        qm = jnp.concatenate([_bf16(q * (m * scale)) for m in masks], axis=0)
        sc_s[b * qrows:(b + 1) * qrows, :] = _dot(qm, _bf16(kt_ref[first + b]))
    s = sc_s[...]
    e = jnp.exp2(s - jnp.max(s, axis=-1, keepdims=True))
    p_s[...] = _bf16(e * (1.0 / jnp.sum(e, axis=-1, keepdims=True)))
    for b in range(nb):
        o_all = _dot_nt(p_s[b * qrows:(b + 1) * qrows, :], _bf16(vt_ref[first + b]))
        o = o_all[0:steps] * masks[0]
        for h in range(1, N_XHEADS):
            o = o + o_all[h * steps:(h + 1) * steps] * masks[h]
        gate = z_s[b * steps:(b + 1) * steps, OFF_ATTN_GATE:OFF_ATTN_GATE + D_XATTN]
        mix_s[b * steps:(b + 1) * steps, D_LRU + D_CONV:D_MIX] = _bf16(o * _silu_of_twice(gate))

    y = x + _dot(mix_s[...], wout_bf[...])
    y_ref[...] = _rmsnorm(y, g_final_ref[...]).reshape(nb, steps, D_MODEL)


def _sample(x, h0, lb, cb, kt, vt, w, win_bf, wout_bf, wpw_bf, wg_bf):
    bsz, steps, _ = x.shape
    nb = SAMPLE_GROUP
    assert bsz % nb == 0 and nb % SUBLANES == 0
    assert steps + LRU_HALO == SAMPLE_PITCH and steps >= CONV_W - 1
    rows = nb * steps
    prow = nb * SAMPLE_PITCH
    weights = [win_bf, w["w_lc"], w["b_lc"], wg_bf, w["b_ga"], w["b_gx"],
               w["lam"], w["w_dw"], w["b_dw"], w["ln_g"], w["ln_b"], wpw_bf, w["b_pw"],
               wout_bf, w["g_final"]]
    weight_specs = [
        _whole((D_MODEL, D_IN)), _whole((None, LRU_CONV, D_LRU)),
        _whole((1, D_LRU)), _whole((2, GATE_HALF, D_LRU)), _whole((1, D_LRU)), _whole((1, D_LRU)),
        _whole((1, D_LRU)), _whole((CONV_W, 1, D_CONV)), _whole((1, D_CONV)), _whole((1, D_CONV)),
        _whole((1, D_CONV)), _whole((D_CONV, D_CONV)), _whole((1, D_CONV)),
        _whole((D_MIX, D_MODEL)), _whole((1, D_MODEL)),
    ]
    in_specs = [
        pl.BlockSpec((nb, steps, D_MODEL), lambda g: (g, 0, 0)),
        pl.BlockSpec((nb, D_LRU), lambda g: (g, 0)),
        pl.BlockSpec((LRU_CONV - 1, nb, D_LRU), lambda g: (0, g, 0)),
        pl.BlockSpec((CONV_W - 1, nb, D_CONV), lambda g: (0, g, 0)),
        pl.BlockSpec(memory_space=pl.ANY),
        pl.BlockSpec(memory_space=pl.ANY),
    ] + weight_specs
    out_specs = [
        pl.BlockSpec((nb, steps, D_MODEL), lambda g: (g, 0, 0)),
        pl.BlockSpec((nb, D_LRU), lambda g: (g, 0)),
        pl.BlockSpec((LRU_CONV - 1, nb, D_LRU), lambda g: (0, g, 0)),
        pl.BlockSpec((CONV_W - 1, nb, D_CONV), lambda g: (0, g, 0)),
    ]
    out_shape = [
        jax.ShapeDtypeStruct((bsz, steps, D_MODEL), jnp.float32),
        jax.ShapeDtypeStruct((bsz, D_LRU), jnp.float32),
        jax.ShapeDtypeStruct((LRU_CONV - 1, bsz, D_LRU), jnp.float32),
        jax.ShapeDtypeStruct((CONV_W - 1, bsz, D_CONV), jnp.float32),
    ]
    scratch = [
        pltpu.VMEM((rows, D_IN), jnp.float32),
        pltpu.VMEM((N_LRU_SLABS, prow, LANES), jnp.float32),
        pltpu.VMEM((prow, D_LRU), jnp.float32),
        pltpu.VMEM((prow, 2 * D_LRU), jnp.float32),
        pltpu.VMEM((N_LRU_SLABS, prow, LANES), jnp.float32),
        pltpu.VMEM((N_LRU_SLABS, prow, LANES), jnp.float32),
        pltpu.VMEM((N_CONV_SLABS, nb * (CONV_HALO + steps), LANES), jnp.float32),
        pltpu.VMEM((rows, D_CONV), jnp.bfloat16),
        pltpu.VMEM((rows, D_MIX), jnp.bfloat16),
        pltpu.VMEM((N_XHEADS * rows, N_MEM), jnp.float32),
        pltpu.VMEM((N_XHEADS * rows, N_MEM), jnp.bfloat16),
        pltpu.VMEM((bsz, D_XATTN, N_MEM), jnp.float32),
        pltpu.VMEM((bsz, D_XATTN, N_MEM), jnp.float32),
        pltpu.SemaphoreType.DMA((2 * (bsz // nb),)),
    ]
    return pl.pallas_call(
        functools.partial(_sample_kernel, nb, steps),
        grid=(bsz // nb,),
        in_specs=in_specs,
        out_specs=out_specs,
        out_shape=out_shape,
        scratch_shapes=scratch,
        compiler_params=pltpu.CompilerParams(
            dimension_semantics=("arbitrary",),
            vmem_limit_bytes=VMEM_LIMIT_BYTES),
        name="sample_layer",
    )(x, h0, lb, cb, kt, vt, *weights)


def _time_major(state):
    return jnp.transpose(state, (1, 0, 2))


def _memory_t(mem):
    bsz = mem.shape[0]
    return jnp.transpose(mem, (0, 2, 3, 1)).reshape(bsz, D_XATTN, N_MEM)


def _memory_from_t(mem_t):
    bsz = mem_t.shape[0]
    return jnp.transpose(mem_t.reshape(bsz, N_XHEADS, XHEAD_DIM, N_MEM), (0, 3, 1, 2))


def kernel(x_prompt, x_sample, state_lru_h, state_lru_conv, state_conv, cache_mem_k, cache_mem_v,
           mem_prompt, g_norm, w_in, w_lru_conv, b_lru_conv, w_gate_a, b_gate_a, w_gate_x, b_gate_x,
           lru_lambda, w_dw, b_dw, ln_g, ln_b, w_pw, b_pw, g_mem, w_mem_k, w_mem_v, w_out, g_final):
    depth = g_norm.shape[0]
    assert depth == 1
    w = {
        "g_norm": g_norm, "w_in": w_in, "w_lc": w_lru_conv, "b_lc": b_lru_conv,
        "w_ga": w_gate_a, "w_gx": w_gate_x, "b_ga": b_gate_a, "b_gx": b_gate_x, "lam": lru_lambda,
        "w_dw": jnp.transpose(w_dw, (1, 0, 2)), "b_dw": b_dw, "ln_g": ln_g, "ln_b": ln_b,
        "w_pw": w_pw, "b_pw": b_pw, "w_out": w_out, "g_final": g_final.reshape(1, D_MODEL),
        "g_mem": g_mem, "w_mem_k": w_mem_k, "w_mem_v": w_mem_v,
    }
    y_prompt, mkt, mvt, ph, plb, pcb, win_bf, wout_bf, wpw_bf, wg_bf = _prompt(x_prompt, mem_prompt, w)

    y_sample, sh, slb, scb = _sample(
        x_sample, state_lru_h[0], _time_major(state_lru_conv[0]), _time_major(state_conv[0]),
        _memory_t(cache_mem_k[0]), _memory_t(cache_mem_v[0]), w, win_bf, wout_bf, wpw_bf, wg_bf)

    return (y_prompt, y_sample,
            ph[None], _time_major(plb)[None], _time_major(pcb)[None],
            _memory_from_t(mkt)[None], _memory_from_t(mvt)[None],
            sh[None], _time_major(slb)[None], _time_major(scb)[None])
```

```python
import functools

import jax
import jax.numpy as jnp
from jax import lax
from jax.experimental import pallas as pl
from jax.experimental.pallas import tpu as pltpu

D_MODEL = 1024
N_MEM = 256
D_LRU = 512
LRU_BLOCKS = 8
LRU_BLOCK = D_LRU // LRU_BLOCKS
LRU_CONV = 4
LRU_C = 8.0
D_CONV = 256
CONV_W = 31
N_XHEADS = 4
XHEAD_DIM = 64
D_XATTN = N_XHEADS * XHEAD_DIM
D_MIX = D_LRU + D_CONV + D_XATTN
D_IN = 2 * D_LRU + 3 * D_CONV + 2 * D_XATTN
EPS = 1e-6

LANES = 128
SUBLANES = 8
N_LRU_SLABS = D_LRU // LANES
N_CONV_SLABS = D_CONV // LANES
GATE_HALF = D_LRU // 2

OFF_LRU_IN = 0
OFF_LRU_GATE = D_LRU
OFF_GLU_A = 2 * D_LRU
OFF_GLU_B = OFF_GLU_A + D_CONV
OFF_CONV_GATE = OFF_GLU_B + D_CONV
OFF_Q = OFF_CONV_GATE + D_CONV
OFF_ATTN_GATE = OFF_Q + D_XATTN

LRU_HALO = 8
CONV_HALO = 32

PROMPT_TILE = 512
PROMPT_SEG = 64
PROMPT_PITCH = 68
SAMPLE_PITCH = 40
SAMPLE_GROUP = 16
ATTN_ROWS = 128
CONV_ROWS = 32
PROJ_ROWS = 256

VMEM_LIMIT_BYTES = 56 * 1024 * 1024
HALF_LOG2_E = 0.7213475204444817
SCORE_SCALE = XHEAD_DIM ** -0.5 * 2.0 * HALF_LOG2_E


def _silu_of_twice(hx):
    return hx + hx * jnp.tanh(hx)


def _times_silu_of_twice_bf16(v, hx):
    return _bf16(v) * _silu_of_twice(_bf16(hx))


def _gated_of_twice(ha, hb):
    return ha + ha * jnp.tanh(hb)


def _w_in_col_scale():
    col = lax.broadcasted_iota(jnp.int32, (1, D_IN), 1)
    plain = (col < OFF_LRU_GATE) | ((col >= OFF_Q) & (col < OFF_ATTN_GATE))
    return jnp.where(plain, 1.0, 0.5)


def _sqrt_nonneg(x):
    return jnp.exp2(jnp.log(x) * HALF_LOG2_E)


def _rms_scaled(x):
    ms = jnp.mean(x * x, axis=-1, keepdims=True)
    return x * lax.rsqrt(ms + EPS)


def _rmsnorm(x, g):
    return _rms_scaled(x) * g


def _as_column(row):
    n = row.shape[1] // LANES
    on_diag = (lax.broadcasted_iota(jnp.int32, (LANES, LANES), 0)
               == lax.broadcasted_iota(jnp.int32, (LANES, LANES), 1))
    cols = [jnp.sum(jnp.where(on_diag, row[:, j * LANES:(j + 1) * LANES], 0.0), axis=1, keepdims=True)
            for j in range(n)]
    return jnp.concatenate(cols, axis=0)


def _log_sigmoid(x):
    y = -x
    return -(jnp.maximum(y, 0.0) + jnp.log1p(jnp.exp(-jnp.abs(y))))


def _bf16(x):
    return x.astype(jnp.bfloat16)


def _dot(a, b):
    return jnp.dot(a, b, preferred_element_type=jnp.float32)


def _dot_nt(a, b):
    return lax.dot_general(a, b, (((1,), (1,)), ((), ())), preferred_element_type=jnp.float32)


def _unrolled(n, body, init):
    carry = init
    for i in range(n):
        carry = body(i, carry)
    return carry


def _aligned(x, m):
    return x if isinstance(x, int) else pl.multiple_of(x, m)


def _head_of(index):
    return lax.shift_right_logical(index, XHEAD_DIM.bit_length() - 1)


def _lane_head_masks():
    head = _head_of(lax.broadcasted_iota(jnp.int32, (1, D_XATTN), 1))
    return [(head == h).astype(jnp.float32) for h in range(N_XHEADS)]


def _row_head_masks():
    head = _head_of(lax.broadcasted_iota(jnp.int32, (D_XATTN, 1), 0))
    return [(head == h).astype(jnp.float32) for h in range(N_XHEADS)]


def _head_stack_t(kt, vt, kst_ref, vst_ref):
    scale = SCORE_SCALE
    for h, m in enumerate(_row_head_masks()):
        kst_ref[:, h * N_MEM:(h + 1) * N_MEM] = _bf16(kt * (m * scale))
        vst_ref[:, h * N_MEM:(h + 1) * N_MEM] = _bf16(vt * m)


def _zero_row_after(x):
    bits = lax.bitcast_convert_type(x, jnp.uint32)
    zero = lax.shift_right_logical(lax.shift_right_logical(bits, jnp.uint32(16)), jnp.uint32(16))
    return lax.bitcast_convert_type(zero, jnp.float32)[0:1, :]


def _softmax_heads(s):
    ps = []
    for h in range(N_XHEADS):
        sh = s[:, h * N_MEM:(h + 1) * N_MEM]
        e = jnp.exp2(sh - jnp.max(sh, axis=-1, keepdims=True))
        inv = 1.0 / jnp.sum(e, axis=-1, keepdims=True)
        ps.append(_bf16(e * inv))
    return jnp.concatenate(ps, axis=1)


def _lru_gates(xl_bf, wg_ref, pre_s):
    pre_s[:, 0:D_LRU] = _dot(xl_bf[:, 0:GATE_HALF], wg_ref[0])
    pre_s[:, D_LRU:2 * D_LRU] = _dot(xl_bf[:, GATE_HALF:D_LRU], wg_ref[1])


def _pre_cols(l):
    half_slabs = N_LRU_SLABS // 2
    base = (l // half_slabs) * D_LRU + (l % half_slabs) * LANES
    return base, base + GATE_HALF


def _lru_coeffs(hpre_a, hpre_x, hx, hb_a, hb_x, hc_ls):
    t_r = jnp.tanh(hpre_a + hb_a)
    t_i = jnp.tanh(hpre_x + hb_x)
    log_a = hc_ls + hc_ls * t_r
    a = jnp.exp(log_a)
    one_minus_a2 = -jnp.tanh(log_a) * (a * a + 1.0)
    u = _sqrt_nonneg(one_minus_a2) * (hx + hx * t_i)
    return a, u


def _conv_taps(src_s, l, w_row, row0, nrows, width, halo):
    acc = None
    for k in range(width):
        x_k = src_s[l, pl.ds(row0 + (halo - (width - 1) + k), nrows), :]
        acc = w_row(k) * x_k if acc is None else acc + w_row(k) * x_k
    return acc


def _layernorm_silu_slabs(ys, g_ref, b_ref):
    tot = ys[0]
    for y in ys[1:]:
        tot = tot + y
    mu = jnp.sum(tot, axis=-1, keepdims=True) * (1.0 / D_CONV)
    cs = [y - mu for y in ys]
    sq = cs[0] * cs[0]
    for c in cs[1:]:
        sq = sq + c * c
    var = jnp.sum(sq, axis=-1, keepdims=True) * (1.0 / D_CONV)
    inv = lax.rsqrt(var + EPS)
    outs = []
    for l, c in enumerate(cs):
        lo = l * LANES
        half_g = 0.5 * g_ref[:, lo:lo + LANES]
        half_b = 0.5 * b_ref[:, lo:lo + LANES]
        outs.append(_silu_of_twice(_bf16(c * inv * half_g + half_b)))
    return outs


def _gate_block_diag(w_blocks):
    n = (LRU_BLOCKS // 2) * LRU_BLOCK
    stacked = _bf16(w_blocks.reshape(n, LRU_BLOCK))
    src = lax.broadcasted_iota(jnp.int32, (LRU_BLOCK, n), 0)
    dst = lax.broadcasted_iota(jnp.int32, (LRU_BLOCK, n), 1)
    spread = _bf16(((dst & (LRU_BLOCK - 1)) == src).astype(jnp.float32))
    rows = _head_of(lax.broadcasted_iota(jnp.int32, (n, n), 0))
    cols = _head_of(lax.broadcasted_iota(jnp.int32, (n, n), 1))
    return _bf16(_dot(stacked, spread) * (rows == cols).astype(jnp.float32))


def _prompt_kernel(x_ref, mem_ref, g_mem_ref, wk_ref, wv_ref,
                   g_norm_ref, w_in_ref, w_lc_ref, b_lc_ref, wga_ref, wgx_ref,
                   b_ga_ref, b_gx_ref, lam_ref, w_dw_ref, b_dw_ref, ln_g_ref, ln_b_ref,
                   w_pw_ref, b_pw_ref, w_out_ref, g_final_ref,
                   y_ref, kt_ref, vt_ref, h_out_ref, lb_out_ref, cb_out_ref,
                   win_hbm, wout_hbm, wpw_hbm, wg_hbm,
                   z_s, xc_s, xl_s, pre_s, a_s, u_s, uc_s, ln_s, mix_s, hin_s, kst_s, vst_s, sc_s, p_s,
                   win_bf, wout_bf, wpw_bf, wg_bf, w_sem):
    b = pl.program_id(0)
    t = pl.program_id(1)
    n_t = pl.num_programs(1)
    tile = PROMPT_TILE
    seg = PROMPT_SEG
    n_seg = tile // seg
    pitch = PROMPT_PITCH

    weight_copies = [
        pltpu.make_async_copy(src, dst, w_sem.at[i])
        for i, (src, dst) in enumerate([(win_bf, win_hbm), (wout_bf, wout_hbm),
                                        (wpw_bf, wpw_hbm), (wg_bf, wg_hbm)])]

    @pl.when(jnp.logical_and(b == 0, t == 0))
    def _():
        win_bf[...] = _bf16(w_in_ref[...] * _as_column(g_norm_ref[...]) * _w_in_col_scale())
        wout_bf[...] = _bf16(w_out_ref[...])
        wpw_bf[...] = _bf16(w_pw_ref[...])
        half = LRU_BLOCKS // 2
        for h in range(2):
            wg_bf[h, :, 0:GATE_HALF] = _gate_block_diag(wga_ref[h * half:(h + 1) * half])
            wg_bf[h, :, GATE_HALF:D_LRU] = _gate_block_diag(wgx_ref[h * half:(h + 1) * half])
        for copy in weight_copies:
            copy.start()

    @pl.when(t == 0)
    def _():
        xc_s[:, 0:LRU_HALO, :] = jnp.zeros((N_LRU_SLABS, LRU_HALO, LANES), jnp.float32)
        uc_s[:, 0:CONV_HALO, :] = jnp.zeros((N_CONV_SLABS, CONV_HALO, LANES), jnp.float32)
        hin_s[:, 0:1, :] = jnp.zeros((N_LRU_SLABS, 1, LANES), jnp.float32)
        m = _bf16(_rmsnorm(mem_ref[0], g_mem_ref[...]))
        kt = _dot(m, _bf16(wk_ref[...])).T
        vt = _dot(m, _bf16(wv_ref[...])).T
        kt_ref[0] = kt
        vt_ref[0] = vt
        _head_stack_t(kt, vt, kst_s, vst_s)

    for r0 in range(0, tile, PROJ_ROWS):
        z_s[r0:r0 + PROJ_ROWS, :] = _dot(_bf16(_rms_scaled(x_ref[0, r0:r0 + PROJ_ROWS, :])), win_bf[...])

    for l in range(N_LRU_SLABS):
        lo = l * LANES
        xc_s[l, LRU_HALO:LRU_HALO + tile, :] = z_s[:, OFF_LRU_IN + lo:OFF_LRU_IN + lo + LANES]
    for l in range(N_LRU_SLABS):
        lo = l * LANES
        w_row = lambda k, lo=lo: 0.5 * w_lc_ref[k:k + 1, lo:lo + LANES]
        xl = (_conv_taps(xc_s, l, w_row, 0, tile, LRU_CONV, LRU_HALO)
              + 0.5 * b_lc_ref[:, lo:lo + LANES])
        xl_s[:, lo:lo + LANES] = xl
    _lru_gates(_bf16(xl_s[...]), wg_bf, pre_s)

    hc_ls = (0.5 * LRU_C) * _log_sigmoid(lam_ref[...])
    hb_ga = 0.5 * b_ga_ref[...]
    hb_gx = 0.5 * b_gx_ref[...]

    def coeff_body(c, carry):
        r0 = _aligned(c * seg, seg)
        p0 = _aligned(c * pitch, SUBLANES)
        for l in range(N_LRU_SLABS):
            lo = l * LANES
            ca, cx = _pre_cols(l)
            a, u = _lru_coeffs(pre_s[pl.ds(r0, seg), ca:ca + LANES],
                               pre_s[pl.ds(r0, seg), cx:cx + LANES],
                               xl_s[pl.ds(r0, seg), lo:lo + LANES],
                               hb_ga[:, lo:lo + LANES], hb_gx[:, lo:lo + LANES],
                               hc_ls[:, lo:lo + LANES])
            a_s[l, pl.ds(p0, seg), :] = a
            u_s[l, pl.ds(p0, seg), :] = u
        return carry

    _unrolled(n_seg, coeff_body, 0)

    def tot_body(j, carry):
        new = []
        for l in range(N_LRU_SLABS):
            a_tot, u_tot = carry[l]
            a_j = a_s[l, pl.ds(j, n_seg, stride=pitch), :]
            u_j = u_s[l, pl.ds(j, n_seg, stride=pitch), :]
            new.append((a_j * a_tot, a_j * u_tot + u_j))
        return tuple(new)

    init = tuple((jnp.ones((n_seg, LANES), jnp.float32), jnp.zeros((n_seg, LANES), jnp.float32))
                 for _ in range(N_LRU_SLABS))
    totals = _unrolled(seg, tot_body, init)

    h_in = []
    for l in range(N_LRU_SLABS):
        a_tot, u_tot = totals[l]
        h = hin_s[l, 0:1, :]
        for c in range(n_seg):
            h = a_tot[c:c + 1, :] * h + u_tot[c:c + 1, :]
            hin_s[l, c + 1:c + 2, :] = h
        h_in.append(hin_s[l, 0:n_seg, :])
        hin_s[l, 0:1, :] = h

    def scan_body(j, carry):
        new = []
        for l in range(N_LRU_SLABS):
            a_j = a_s[l, pl.ds(j, n_seg, stride=pitch), :]
            u_j = u_s[l, pl.ds(j, n_seg, stride=pitch), :]
            h = a_j * carry[l] + u_j
            u_s[l, pl.ds(j, n_seg, stride=pitch), :] = h
            new.append(h)
        return tuple(new)

    scores = _dot(_bf16(z_s[:, OFF_Q:OFF_Q + D_XATTN]), kst_s[...])
    sc_s[...] = scores
    after_scores = _zero_row_after(scores[tile - SUBLANES:tile, N_XHEADS * N_MEM - LANES:])
    _unrolled(seg, scan_body, tuple(h + after_scores for h in h_in))

    for l in range(N_CONV_SLABS):
        lo = l * LANES
        uc_s[l, CONV_HALO:CONV_HALO + tile, :] = _gated_of_twice(
            z_s[:, OFF_GLU_A + lo:OFF_GLU_A + lo + LANES],
            z_s[:, OFF_GLU_B + lo:OFF_GLU_B + lo + LANES])

    def seg_body(c, carry):
        r0 = _aligned(c * seg, seg)
        p0 = _aligned(c * pitch, SUBLANES)
        for l in range(N_LRU_SLABS):
            lo = l * LANES
            gate = z_s[pl.ds(r0, seg), OFF_LRU_GATE + lo:OFF_LRU_GATE + lo + LANES]
            mix_s[pl.ds(r0, seg), lo:lo + LANES] = _times_silu_of_twice_bf16(u_s[l, pl.ds(p0, seg), :], gate)
        for piece in range(seg // CONV_ROWS):
            rr = r0 + piece * CONV_ROWS
            ys = []
            for l in range(N_CONV_SLABS):
                lo = l * LANES
                w_row = lambda k, lo=lo: w_dw_ref[k, :, lo:lo + LANES]
                ys.append(_conv_taps(uc_s, l, w_row, rr, CONV_ROWS, CONV_W, CONV_HALO)
                          + b_dw_ref[:, lo:lo + LANES])
            outs = _layernorm_silu_slabs(ys, ln_g_ref, ln_b_ref)
            for l in range(N_CONV_SLABS):
                lo = l * LANES
                ln_s[pl.ds(rr, CONV_ROWS), lo:lo + LANES] = outs[l]
        return carry

    _unrolled(n_seg, seg_body, 0)

    mix_s[:, D_LRU:D_LRU + D_CONV] = _times_silu_of_twice_bf16(
        _dot(ln_s[...], wpw_bf[...]) + b_pw_ref[...], z_s[:, OFF_CONV_GATE:OFF_CONV_GATE + D_CONV])

    def attn_body(c, carry):
        r0 = _aligned(c * ATTN_ROWS, ATTN_ROWS)
        p_s[pl.ds(r0, ATTN_ROWS), :] = _softmax_heads(sc_s[pl.ds(r0, ATTN_ROWS), :])
        return carry

    _unrolled(tile // ATTN_ROWS, attn_body, 0)
    o = _dot_nt(p_s[...], vst_s[...])
    mix_s[:, D_LRU + D_CONV:D_MIX] = _times_silu_of_twice_bf16(
        o, z_s[:, OFF_ATTN_GATE:OFF_ATTN_GATE + D_XATTN])

    y = x_ref[0] + _dot(mix_s[...], wout_bf[...])
    y_ref[0] = _rmsnorm(y, g_final_ref[...])

    for l in range(N_LRU_SLABS):
        xc_s[l, LRU_HALO - (LRU_CONV - 1):LRU_HALO, :] = (
            xc_s[l, LRU_HALO + tile - (LRU_CONV - 1):LRU_HALO + tile, :])
    for l in range(N_CONV_SLABS):
        uc_s[l, CONV_HALO - (CONV_W - 1):CONV_HALO, :] = (
            uc_s[l, CONV_HALO + tile - (CONV_W - 1):CONV_HALO + tile, :])

    def write_state(bb):
        for l in range(N_LRU_SLABS):
            lo = l * LANES
            h_out_ref[bb:bb + 1, lo:lo + LANES] = hin_s[l, 0:1, :]
            for k in range(LRU_CONV - 1):
                r = LRU_HALO - (LRU_CONV - 1) + k
                lb_out_ref[k, bb:bb + 1, lo:lo + LANES] = xc_s[l, r:r + 1, :]
        for l in range(N_CONV_SLABS):
            lo = l * LANES
            for k in range(CONV_W - 1):
                r = CONV_HALO - (CONV_W - 1) + k
                cb_out_ref[k, bb:bb + 1, lo:lo + LANES] = uc_s[l, r:r + 1, :]

    for bb in range(h_out_ref.shape[0]):
        pl.when(jnp.logical_and(t == n_t - 1, b == bb))(functools.partial(write_state, bb))

    @pl.when(jnp.logical_and(t == n_t - 1, b == h_out_ref.shape[0] - 1))
    def _():
        for copy in weight_copies:
            copy.wait()


def _whole(shape):
    nd = len(shape)
    return pl.BlockSpec(shape, lambda *_: (0,) * nd, pipeline_mode=pl.Buffered(1))


def _prompt(x, mem, w):
    bsz, seq, _ = x.shape
    tile = PROMPT_TILE
    n_t = seq // tile
    n_seg = tile // PROMPT_SEG
    weights = [w["g_mem"], w["w_mem_k"], w["w_mem_v"],
               w["g_norm"], w["w_in"], w["w_lc"], w["b_lc"], w["w_ga"], w["w_gx"], w["b_ga"], w["b_gx"],
               w["lam"], w["w_dw"], w["b_dw"], w["ln_g"], w["ln_b"], w["w_pw"], w["b_pw"],
               w["w_out"], w["g_final"]]
    weight_specs = [
        _whole((1, D_MODEL)), _whole((None, D_MODEL, D_XATTN)), _whole((None, D_MODEL, D_XATTN)),
        _whole((1, D_MODEL)), _whole((None, D_MODEL, D_IN)), _whole((None, LRU_CONV, D_LRU)),
        _whole((1, D_LRU)), _whole((None, LRU_BLOCKS, LRU_BLOCK, LRU_BLOCK)),
        _whole((None, LRU_BLOCKS, LRU_BLOCK, LRU_BLOCK)), _whole((1, D_LRU)), _whole((1, D_LRU)),
        _whole((1, D_LRU)), _whole((CONV_W, 1, D_CONV)), _whole((1, D_CONV)), _whole((1, D_CONV)),
        _whole((1, D_CONV)), _whole((None, D_CONV, D_CONV)), _whole((1, D_CONV)),
        _whole((None, D_MIX, D_MODEL)), _whole((1, D_MODEL)),
    ]
    in_specs = [
        pl.BlockSpec((1, tile, D_MODEL), lambda b, t: (b, t, 0)),
        pl.BlockSpec((1, N_MEM, D_MODEL), lambda b, t: (b, 0, 0)),
    ] + weight_specs
    out_specs = [
        pl.BlockSpec((1, tile, D_MODEL), lambda b, t: (b, t, 0)),
        pl.BlockSpec((1, D_XATTN, N_MEM), lambda b, t: (b, 0, 0)),
        pl.BlockSpec((1, D_XATTN, N_MEM), lambda b, t: (b, 0, 0)),
        _whole((bsz, D_LRU)),
        _whole((LRU_CONV - 1, bsz, D_LRU)),
        _whole((CONV_W - 1, bsz, D_CONV)),
    ] + [pl.BlockSpec(memory_space=pl.ANY)] * 4
    out_shape = [
        jax.ShapeDtypeStruct((bsz, seq, D_MODEL), jnp.float32),
        jax.ShapeDtypeStruct((bsz, D_XATTN, N_MEM), jnp.float32),
        jax.ShapeDtypeStruct((bsz, D_XATTN, N_MEM), jnp.float32),
        jax.ShapeDtypeStruct((bsz, D_LRU), jnp.float32),
        jax.ShapeDtypeStruct((LRU_CONV - 1, bsz, D_LRU), jnp.float32),
        jax.ShapeDtypeStruct((CONV_W - 1, bsz, D_CONV), jnp.float32),
        jax.ShapeDtypeStruct((D_MODEL, D_IN), jnp.bfloat16),
        jax.ShapeDtypeStruct((D_MIX, D_MODEL), jnp.bfloat16),
        jax.ShapeDtypeStruct((D_CONV, D_CONV), jnp.bfloat16),
        jax.ShapeDtypeStruct((2, GATE_HALF, D_LRU), jnp.bfloat16),
    ]
    scratch = [
        pltpu.VMEM((tile, D_IN), jnp.float32),
        pltpu.VMEM((N_LRU_SLABS, LRU_HALO + tile, LANES), jnp.float32),
        pltpu.VMEM((tile, D_LRU), jnp.float32),
        pltpu.VMEM((tile, 2 * D_LRU), jnp.float32),
        pltpu.VMEM((N_LRU_SLABS, n_seg * PROMPT_PITCH, LANES), jnp.float32),
        pltpu.VMEM((N_LRU_SLABS, n_seg * PROMPT_PITCH, LANES), jnp.float32),
        pltpu.VMEM((N_CONV_SLABS, CONV_HALO + tile, LANES), jnp.float32),
        pltpu.VMEM((tile, D_CONV), jnp.bfloat16),
        pltpu.VMEM((tile, D_MIX), jnp.bfloat16),
        pltpu.VMEM((N_LRU_SLABS, 2 * SUBLANES, LANES), jnp.float32),
        pltpu.VMEM((D_XATTN, N_XHEADS * N_MEM), jnp.bfloat16),
        pltpu.VMEM((D_XATTN, N_XHEADS * N_MEM), jnp.bfloat16),
        pltpu.VMEM((tile, N_XHEADS * N_MEM), jnp.float32),
        pltpu.VMEM((tile, N_XHEADS * N_MEM), jnp.bfloat16),
        pltpu.VMEM((D_MODEL, D_IN), jnp.bfloat16),
        pltpu.VMEM((D_MIX, D_MODEL), jnp.bfloat16),
        pltpu.VMEM((D_CONV, D_CONV), jnp.bfloat16),
        pltpu.VMEM((2, GATE_HALF, D_LRU), jnp.bfloat16),
        pltpu.SemaphoreType.DMA((4,)),
    ]
    return pl.pallas_call(
        _prompt_kernel,
        grid=(bsz, n_t),
        in_specs=in_specs,
        out_specs=out_specs,
        out_shape=out_shape,
        scratch_shapes=scratch,
        compiler_params=pltpu.CompilerParams(
            dimension_semantics=("arbitrary", "arbitrary"),
            vmem_limit_bytes=VMEM_LIMIT_BYTES),
        name="prompt_layer",
    )(x, mem, *weights)


def _sample_kernel(nb, steps,
                   x_ref, h0_ref, lb_ref, cb_ref, kt_hbm, vt_hbm,
                   win_bf, w_lc_ref, b_lc_ref, wg_bf,
                   b_ga_ref, b_gx_ref, lam_ref, w_dw_ref, b_dw_ref, ln_g_ref, ln_b_ref,
                   wpw_bf, b_pw_ref, wout_bf, g_final_ref,
                   y_ref, h_out_ref, lb_out_ref, cb_out_ref,
                   z_s, xc_s, xl_s, pre_s, a_s, u_s, uc_s, ln_s, mix_s, sc_s, p_s,
                   kt_ref, vt_ref, kv_sem):
    rows = nb * steps
    pitch = SAMPLE_PITCH
    cpitch = CONV_HALO + steps
    prow = nb * pitch

    group = pl.program_id(0)
    first = group * nb
    n_groups = kt_ref.shape[0] // nb

    def kv_copies(grp):
        rows_of = pl.ds(grp * nb, nb)
        return (pltpu.make_async_copy(kt_hbm.at[rows_of], kt_ref.at[rows_of], kv_sem.at[2 * grp]),
                pltpu.make_async_copy(vt_hbm.at[rows_of], vt_ref.at[rows_of], kv_sem.at[2 * grp + 1]))

    @pl.when(group == 0)
    def _():
        for grp in range(n_groups):
            for copy in kv_copies(grp):
                copy.start()
        xl_s[prow - LRU_HALO:prow, :] = jnp.zeros((LRU_HALO, D_LRU), jnp.float32)

    x = x_ref[...].reshape(rows, D_MODEL)
    z_s[...] = _dot(_bf16(_rms_scaled(x)), win_bf[...])

    for b in range(nb):
        for l in range(N_LRU_SLABS):
            lo = l * LANES
            zcol = OFF_LRU_IN + lo
            xc_s[l, b * pitch:b * pitch + LRU_HALO, :] = jnp.zeros((LRU_HALO, LANES), jnp.float32)
            for k in range(LRU_CONV - 1):
                r = b * pitch + LRU_HALO - (LRU_CONV - 1) + k
                xc_s[l, r:r + 1, :] = lb_ref[k, b:b + 1, lo:lo + LANES]
                rn = (b + 1) * steps - (LRU_CONV - 1) + k
                lb_out_ref[k, b:b + 1, lo:lo + LANES] = z_s[rn:rn + 1, zcol:zcol + LANES]
            xc_s[l, b * pitch + LRU_HALO:(b + 1) * pitch, :] = (
                z_s[b * steps:(b + 1) * steps, zcol:zcol + LANES])
    n_conv = prow - LRU_HALO
    for l in range(N_LRU_SLABS):
        lo = l * LANES
        w_row = lambda k, lo=lo: 0.5 * w_lc_ref[k:k + 1, lo:lo + LANES]
        xl = (_conv_taps(xc_s, l, w_row, 0, n_conv, LRU_CONV, LRU_HALO)
              + 0.5 * b_lc_ref[:, lo:lo + LANES])
        xl_s[0:n_conv, lo:lo + LANES] = xl
    _lru_gates(_bf16(xl_s[...]), wg_bf, pre_s)

    hc_ls = (0.5 * LRU_C) * _log_sigmoid(lam_ref[...])
    hb_ga = 0.5 * b_ga_ref[...]
    hb_gx = 0.5 * b_gx_ref[...]
    for l in range(N_LRU_SLABS):
        lo = l * LANES
        ca, cx = _pre_cols(l)
        a, u = _lru_coeffs(pre_s[:, ca:ca + LANES], pre_s[:, cx:cx + LANES], xl_s[:, lo:lo + LANES],
                           hb_ga[:, lo:lo + LANES], hb_gx[:, lo:lo + LANES],
                           hc_ls[:, lo:lo + LANES])
        a_s[l, :, :] = a
        u_s[l, :, :] = u

    n_grp = nb // SUBLANES

    def scan_body(j, carry):
        new = []
        for g in range(n_grp):
            for l in range(N_LRU_SLABS):
                base = g * SUBLANES * pitch
                a_j = a_s[l, pl.ds(base + j, SUBLANES, stride=pitch), :]
                u_j = u_s[l, pl.ds(base + j, SUBLANES, stride=pitch), :]
                h = a_j * carry[g * N_LRU_SLABS + l] + u_j
                u_s[l, pl.ds(base + j, SUBLANES, stride=pitch), :] = h
                new.append(h)
        return tuple(new)

    h0 = tuple(h0_ref[g * SUBLANES:(g + 1) * SUBLANES, l * LANES:(l + 1) * LANES]
               for g in range(n_grp) for l in range(N_LRU_SLABS))
    h_fin = _unrolled(steps, scan_body, h0)
    for g in range(n_grp):
        for l in range(N_LRU_SLABS):
            h_out_ref[g * SUBLANES:(g + 1) * SUBLANES, l * LANES:(l + 1) * LANES] = (
                h_fin[g * N_LRU_SLABS + l])

    for b in range(nb):
        for l in range(N_CONV_SLABS):
            lo = l * LANES
            uc = _gated_of_twice(
                z_s[b * steps:(b + 1) * steps, OFF_GLU_A + lo:OFF_GLU_A + lo + LANES],
                z_s[b * steps:(b + 1) * steps, OFF_GLU_B + lo:OFF_GLU_B + lo + LANES])
            uc_s[l, b * cpitch + CONV_HALO:(b + 1) * cpitch, :] = uc
            for k in range(CONV_W - 1):
                r = b * cpitch + CONV_HALO - (CONV_W - 1) + k
                uc_s[l, r:r + 1, :] = cb_ref[k, b:b + 1, lo:lo + LANES]
                rn = (b + 1) * cpitch - (CONV_W - 1) + k
                cb_out_ref[k, b:b + 1, lo:lo + LANES] = uc_s[l, rn:rn + 1, :]

    for b in range(nb):
        r0 = b * cpitch
        o0 = b * steps
        ys = []
        for l in range(N_CONV_SLABS):
            lo = l * LANES
            w_row = lambda k, lo=lo: w_dw_ref[k, :, lo:lo + LANES]
            ys.append(_conv_taps(uc_s, l, w_row, r0, steps, CONV_W, CONV_HALO)
                      + b_dw_ref[:, lo:lo + LANES])
        outs = _layernorm_silu_slabs(ys, ln_g_ref, ln_b_ref)
        for l in range(N_CONV_SLABS):
            lo = l * LANES
            ln_s[o0:o0 + steps, lo:lo + LANES] = outs[l]

    mix_s[:, D_LRU:D_LRU + D_CONV] = _times_silu_of_twice_bf16(
        _dot(ln_s[...], wpw_bf[...]) + b_pw_ref[...], z_s[:, OFF_CONV_GATE:OFF_CONV_GATE + D_CONV])

    masks = _lane_head_masks()
    scale = SCORE_SCALE
    qrows = N_XHEADS * steps
    for copy in kv_copies(group):
        copy.wait()
    for b in range(nb):
        for l in range(N_LRU_SLABS):
            lo = l * LANES
            gate = z_s[b * steps:(b + 1) * steps, OFF_LRU_GATE + lo:OFF_LRU_GATE + lo + LANES]
            mix_s[b * steps:(b + 1) * steps, lo:lo + LANES] = _times_silu_of_twice_bf16(
                u_s[l, b * pitch:b * pitch + steps, :], gate)
    for b in range(nb):
        q = z_s[b * steps:(b + 1) * steps, OFF_Q:OFF_Q + D_XATTN]
        qm = jnp.concatenate([_bf16(q * (m * scale)) for m in masks], axis=0)
        sc_s[b * qrows:(b + 1) * qrows, :] = _dot(qm, _bf16(kt_ref[first + b]))
    s = sc_s[...]
    e = jnp.exp2(s - jnp.max(s, axis=-1, keepdims=True))
    p_s[...] = _bf16(e * (1.0 / jnp.sum(e, axis=-1, keepdims=True)))
    for b in range(nb):
        o_all = _dot_nt(p_s[b * qrows:(b + 1) * qrows, :], _bf16(vt_ref[first + b]))
        o = o_all[0:steps] * masks[0]
        for h in range(1, N_XHEADS):
            o = o + o_all[h * steps:(h + 1) * steps] * masks[h]
        gate = z_s[b * steps:(b + 1) * steps, OFF_ATTN_GATE:OFF_ATTN_GATE + D_XATTN]
        mix_s[b * steps:(b + 1) * steps, D_LRU + D_CONV:D_MIX] = _times_silu_of_twice_bf16(o, gate)

    y = x + _dot(mix_s[...], wout_bf[...])
    y_ref[...] = _rmsnorm(y, g_final_ref[...]).reshape(nb, steps, D_MODEL)


def _sample(x, h0, lb, cb, kt, vt, w, win_bf, wout_bf, wpw_bf, wg_bf):
    bsz, steps, _ = x.shape
    nb = SAMPLE_GROUP
    assert bsz % nb == 0 and nb % SUBLANES == 0
    assert steps + LRU_HALO == SAMPLE_PITCH and steps >= CONV_W - 1
    rows = nb * steps
    prow = nb * SAMPLE_PITCH
    weights = [win_bf, w["w_lc"], w["b_lc"], wg_bf, w["b_ga"], w["b_gx"],
               w["lam"], w["w_dw"], w["b_dw"], w["ln_g"], w["ln_b"], wpw_bf, w["b_pw"],
               wout_bf, w["g_final"]]
    weight_specs = [
        _whole((D_MODEL, D_IN)), _whole((None, LRU_CONV, D_LRU)),
        _whole((1, D_LRU)), _whole((2, GATE_HALF, D_LRU)), _whole((1, D_LRU)), _whole((1, D_LRU)),
        _whole((1, D_LRU)), _whole((CONV_W, 1, D_CONV)), _whole((1, D_CONV)), _whole((1, D_CONV)),
        _whole((1, D_CONV)), _whole((D_CONV, D_CONV)), _whole((1, D_CONV)),
        _whole((D_MIX, D_MODEL)), _whole((1, D_MODEL)),
    ]
    in_specs = [
        pl.BlockSpec((nb, steps, D_MODEL), lambda g: (g, 0, 0)),
        pl.BlockSpec((nb, D_LRU), lambda g: (g, 0)),
        pl.BlockSpec((LRU_CONV - 1, nb, D_LRU), lambda g: (0, g, 0)),
        pl.BlockSpec((CONV_W - 1, nb, D_CONV), lambda g: (0, g, 0)),
        pl.BlockSpec(memory_space=pl.ANY),
        pl.BlockSpec(memory_space=pl.ANY),
    ] + weight_specs
    out_specs = [
        pl.BlockSpec((nb, steps, D_MODEL), lambda g: (g, 0, 0)),
        pl.BlockSpec((nb, D_LRU), lambda g: (g, 0)),
        pl.BlockSpec((LRU_CONV - 1, nb, D_LRU), lambda g: (0, g, 0)),
        pl.BlockSpec((CONV_W - 1, nb, D_CONV), lambda g: (0, g, 0)),
    ]
    out_shape = [
        jax.ShapeDtypeStruct((bsz, steps, D_MODEL), jnp.float32),
        jax.ShapeDtypeStruct((bsz, D_LRU), jnp.float32),
        jax.ShapeDtypeStruct((LRU_CONV - 1, bsz, D_LRU), jnp.float32),
        jax.ShapeDtypeStruct((CONV_W - 1, bsz, D_CONV), jnp.float32),
    ]
    scratch = [
        pltpu.VMEM((rows, D_IN), jnp.float32),
        pltpu.VMEM((N_LRU_SLABS, prow, LANES), jnp.float32),
        pltpu.VMEM((prow, D_LRU), jnp.float32),
        pltpu.VMEM((prow, 2 * D_LRU), jnp.float32),
        pltpu.VMEM((N_LRU_SLABS, prow, LANES), jnp.float32),
        pltpu.VMEM((N_LRU_SLABS, prow, LANES), jnp.float32),
        pltpu.VMEM((N_CONV_SLABS, nb * (CONV_HALO + steps), LANES), jnp.float32),
        pltpu.VMEM((rows, D_CONV), jnp.bfloat16),
        pltpu.VMEM((rows, D_MIX), jnp.bfloat16),
        pltpu.VMEM((N_XHEADS * rows, N_MEM), jnp.float32),
        pltpu.VMEM((N_XHEADS * rows, N_MEM), jnp.bfloat16),
        pltpu.VMEM((bsz, D_XATTN, N_MEM), jnp.float32),
        pltpu.VMEM((bsz, D_XATTN, N_MEM), jnp.float32),
        pltpu.SemaphoreType.DMA((2 * (bsz // nb),)),
    ]
    return pl.pallas_call(
        functools.partial(_sample_kernel, nb, steps),
        grid=(bsz // nb,),
        in_specs=in_specs,
        out_specs=out_specs,
        out_shape=out_shape,
        scratch_shapes=scratch,
        compiler_params=pltpu.CompilerParams(
            dimension_semantics=("arbitrary",),
            vmem_limit_bytes=VMEM_LIMIT_BYTES),
        name="sample_layer",
    )(x, h0, lb, cb, kt, vt, *weights)


def _time_major(state):
    return jnp.transpose(state, (1, 0, 2))


def _memory_t(mem):
    bsz = mem.shape[0]
    return jnp.transpose(mem, (0, 2, 3, 1)).reshape(bsz, D_XATTN, N_MEM)


def _memory_from_t(mem_t):
    bsz = mem_t.shape[0]
    return jnp.transpose(mem_t.reshape(bsz, N_XHEADS, XHEAD_DIM, N_MEM), (0, 3, 1, 2))


def kernel(x_prompt, x_sample, state_lru_h, state_lru_conv, state_conv, cache_mem_k, cache_mem_v,
           mem_prompt, g_norm, w_in, w_lru_conv, b_lru_conv, w_gate_a, b_gate_a, w_gate_x, b_gate_x,
           lru_lambda, w_dw, b_dw, ln_g, ln_b, w_pw, b_pw, g_mem, w_mem_k, w_mem_v, w_out, g_final):
    depth = g_norm.shape[0]
    assert depth == 1
    w = {
        "g_norm": g_norm, "w_in": w_in, "w_lc": w_lru_conv, "b_lc": b_lru_conv,
        "w_ga": w_gate_a, "w_gx": w_gate_x, "b_ga": b_gate_a, "b_gx": b_gate_x, "lam": lru_lambda,
        "w_dw": jnp.transpose(w_dw, (1, 0, 2)), "b_dw": b_dw, "ln_g": ln_g, "ln_b": ln_b,
        "w_pw": w_pw, "b_pw": b_pw, "w_out": w_out, "g_final": g_final.reshape(1, D_MODEL),
        "g_mem": g_mem, "w_mem_k": w_mem_k, "w_mem_v": w_mem_v,
    }
    y_prompt, mkt, mvt, ph, plb, pcb, win_bf, wout_bf, wpw_bf, wg_bf = _prompt(x_prompt, mem_prompt, w)

    y_sample, sh, slb, scb = _sample(
        x_sample, state_lru_h[0], _time_major(state_lru_conv[0]), _time_major(state_conv[0]),
        _memory_t(cache_mem_k[0]), _memory_t(cache_mem_v[0]), w, win_bf, wout_bf, wpw_bf, wg_bf)

    return (y_prompt, y_sample,
            ph[None], _time_major(plb)[None], _time_major(pcb)[None],
            _memory_from_t(mkt)[None], _memory_from_t(mvt)[None],
            sh[None], _time_major(slb)[None], _time_major(scb)[None])
```

```python
import functools

import jax
import jax.numpy as jnp
from jax import lax
from jax.experimental import pallas as pl
from jax.experimental.pallas import tpu as pltpu

D_MODEL = 1024
N_MEM = 256
D_LRU = 512
LRU_BLOCKS = 8
LRU_BLOCK = D_LRU // LRU_BLOCKS
LRU_CONV = 4
LRU_C = 8.0
D_CONV = 256
CONV_W = 31
N_XHEADS = 4
XHEAD_DIM = 64
D_XATTN = N_XHEADS * XHEAD_DIM
D_MIX = D_LRU + D_CONV + D_XATTN
D_IN = 2 * D_LRU + 3 * D_CONV + 2 * D_XATTN
EPS = 1e-6

LANES = 128
SUBLANES = 8
N_LRU_SLABS = D_LRU // LANES
N_CONV_SLABS = D_CONV // LANES
GATE_HALF = D_LRU // 2

OFF_LRU_IN = 0
OFF_LRU_GATE = D_LRU
OFF_GLU_A = 2 * D_LRU
OFF_GLU_B = OFF_GLU_A + D_CONV
OFF_CONV_GATE = OFF_GLU_B + D_CONV
OFF_Q = OFF_CONV_GATE + D_CONV
OFF_ATTN_GATE = OFF_Q + D_XATTN

LRU_HALO = 8
CONV_HALO = 32

PROMPT_TILE = 512
PROMPT_SEG = 64
PROMPT_PITCH = 68
SAMPLE_PITCH = 40
SAMPLE_GROUP = 16
ATTN_ROWS = 128
CONV_ROWS = 32
PROJ_ROWS = 256

VMEM_LIMIT_BYTES = 56 * 1024 * 1024
HALF_LOG2_E = 0.7213475204444817
SCORE_SCALE = XHEAD_DIM ** -0.5 * 2.0 * HALF_LOG2_E


def _silu_of_twice(hx):
    return hx + hx * jnp.tanh(hx)


def _times_silu_of_twice_bf16(v, hx):
    return _bf16(v) * _silu_of_twice(_bf16(hx))


def _gated_of_twice(ha, hb):
    return ha + ha * jnp.tanh(hb)


def _w_in_col_scale():
    col = lax.broadcasted_iota(jnp.int32, (1, D_IN), 1)
    plain = (col < OFF_LRU_GATE) | ((col >= OFF_Q) & (col < OFF_ATTN_GATE))
    return jnp.where(plain, 1.0, 0.5)


def _sqrt_nonneg(x):
    return jnp.exp2(jnp.log(x) * HALF_LOG2_E)


def _rms_scaled(x):
    ms = jnp.mean(x * x, axis=-1, keepdims=True)
    return x * lax.rsqrt(ms + EPS)


def _rmsnorm(x, g):
    return _rms_scaled(x) * g


def _as_column(row):
    n = row.shape[1] // LANES
    on_diag = (lax.broadcasted_iota(jnp.int32, (LANES, LANES), 0)
               == lax.broadcasted_iota(jnp.int32, (LANES, LANES), 1))
    cols = [jnp.sum(jnp.where(on_diag, row[:, j * LANES:(j + 1) * LANES], 0.0), axis=1, keepdims=True)
            for j in range(n)]
    return jnp.concatenate(cols, axis=0)


def _log_sigmoid(x):
    y = -x
    return -(jnp.maximum(y, 0.0) + jnp.log1p(jnp.exp(-jnp.abs(y))))


def _bf16(x):
    return x.astype(jnp.bfloat16)


def _dot(a, b):
    return jnp.dot(a, b, preferred_element_type=jnp.float32)


def _dot_nt(a, b):
    return lax.dot_general(a, b, (((1,), (1,)), ((), ())), preferred_element_type=jnp.float32)


def _unrolled(n, body, init):
    carry = init
    for i in range(n):
        carry = body(i, carry)
    return carry


def _aligned(x, m):
    return x if isinstance(x, int) else pl.multiple_of(x, m)


def _head_of(index):
    return lax.shift_right_logical(index, XHEAD_DIM.bit_length() - 1)


def _lane_head_masks():
    head = _head_of(lax.broadcasted_iota(jnp.int32, (1, D_XATTN), 1))
    return [(head == h).astype(jnp.float32) for h in range(N_XHEADS)]


def _row_head_masks():
    head = _head_of(lax.broadcasted_iota(jnp.int32, (D_XATTN, 1), 0))
    return [(head == h).astype(jnp.float32) for h in range(N_XHEADS)]


def _head_stack_t(kt, vt, kst_ref, vst_ref):
    scale = SCORE_SCALE
    for h, m in enumerate(_row_head_masks()):
        kst_ref[:, h * N_MEM:(h + 1) * N_MEM] = _bf16(kt * (m * scale))
        vst_ref[:, h * N_MEM:(h + 1) * N_MEM] = _bf16(vt * m)


def _zero_row_after(x):
    bits = lax.bitcast_convert_type(x, jnp.uint32)
    zero = lax.shift_right_logical(lax.shift_right_logical(bits, jnp.uint32(16)), jnp.uint32(16))
    return lax.bitcast_convert_type(zero, jnp.float32)[0:1, :]


def _softmax_heads(s):
    ps = []
    for h in range(N_XHEADS):
        sh = s[:, h * N_MEM:(h + 1) * N_MEM]
        e = jnp.exp2(sh - jnp.max(sh, axis=-1, keepdims=True))
        inv = 1.0 / jnp.sum(e, axis=-1, keepdims=True)
        ps.append(_bf16(e * inv))
    return jnp.concatenate(ps, axis=1)


def _lru_gates(xl_bf, wg_ref, pre_s):
    pre_s[:, 0:D_LRU] = _dot(xl_bf[:, 0:GATE_HALF], wg_ref[0])
    pre_s[:, D_LRU:2 * D_LRU] = _dot(xl_bf[:, GATE_HALF:D_LRU], wg_ref[1])


def _pre_cols(l):
    half_slabs = N_LRU_SLABS // 2
    base = (l // half_slabs) * D_LRU + (l % half_slabs) * LANES
    return base, base + GATE_HALF


def _lru_coeffs(hpre_a, hpre_x, hx, hb_a, hb_x, hc_ls):
    t_r = jnp.tanh(hpre_a + hb_a)
    t_i = jnp.tanh(hpre_x + hb_x)
    log_a = hc_ls + hc_ls * t_r
    a = jnp.exp(log_a)
    one_minus_a2 = -jnp.tanh(log_a) * (a * a + 1.0)
    u = _sqrt_nonneg(one_minus_a2) * (hx + hx * t_i)
    return a, u


def _conv_taps(src_s, l, w_row, row0, nrows, width, halo):
    acc = None
    for k in range(width):
        x_k = src_s[l, pl.ds(row0 + (halo - (width - 1) + k), nrows), :]
        acc = w_row(k) * x_k if acc is None else acc + w_row(k) * x_k
    return acc


def _layernorm_silu_slabs(ys, g_ref, b_ref):
    tot = ys[0]
    for y in ys[1:]:
        tot = tot + y
    mu = jnp.sum(tot, axis=-1, keepdims=True) * (1.0 / D_CONV)
    cs = [y - mu for y in ys]
    sq = cs[0] * cs[0]
    for c in cs[1:]:
        sq = sq + c * c
    var = jnp.sum(sq, axis=-1, keepdims=True) * (1.0 / D_CONV)
    inv = lax.rsqrt(var + EPS)
    outs = []
    for l, c in enumerate(cs):
        lo = l * LANES
        half_g = 0.5 * g_ref[:, lo:lo + LANES]
        half_b = 0.5 * b_ref[:, lo:lo + LANES]
        outs.append(_silu_of_twice(_bf16(c * inv * half_g + half_b)))
    return outs


def _gate_block_diag(w_blocks):
    n = (LRU_BLOCKS // 2) * LRU_BLOCK
    stacked = _bf16(w_blocks.reshape(n, LRU_BLOCK))
    src = lax.broadcasted_iota(jnp.int32, (LRU_BLOCK, n), 0)
    dst = lax.broadcasted_iota(jnp.int32, (LRU_BLOCK, n), 1)
    spread = _bf16(((dst & (LRU_BLOCK - 1)) == src).astype(jnp.float32))
    rows = _head_of(lax.broadcasted_iota(jnp.int32, (n, n), 0))
    cols = _head_of(lax.broadcasted_iota(jnp.int32, (n, n), 1))
    return _bf16(_dot(stacked, spread) * (rows == cols).astype(jnp.float32))


def _prompt_kernel(x_ref, mem_ref, g_mem_ref, wk_ref, wv_ref,
                   g_norm_ref, w_in_ref, w_lc_ref, b_lc_ref, wga_ref, wgx_ref,
                   b_ga_ref, b_gx_ref, lam_ref, w_dw_ref, b_dw_ref, ln_g_ref, ln_b_ref,
                   w_pw_ref, b_pw_ref, w_out_ref, g_final_ref,
                   y_ref, kt_ref, vt_ref, h_out_ref, lb_out_ref, cb_out_ref,
                   win_hbm, wout_hbm, wpw_hbm, wg_hbm,
                   z_s, xc_s, xl_s, pre_s, a_s, u_s, uc_s, ln_s, mix_s, hin_s, kst_s, vst_s, sc_s, p_s,
                   win_bf, wout_bf, wpw_bf, wg_bf, w_sem):
    b = pl.program_id(0)
    t = pl.program_id(1)
    n_t = pl.num_programs(1)
    tile = PROMPT_TILE
    seg = PROMPT_SEG
    n_seg = tile // seg
    pitch = PROMPT_PITCH

    weight_copies = [
        pltpu.make_async_copy(src, dst, w_sem.at[i])
        for i, (src, dst) in enumerate([(win_bf, win_hbm), (wout_bf, wout_hbm),
                                        (wpw_bf, wpw_hbm), (wg_bf, wg_hbm)])]

    @pl.when(jnp.logical_and(b == 0, t == 0))
    def _():
        win_bf[...] = _bf16(w_in_ref[...] * _as_column(g_norm_ref[...]) * _w_in_col_scale())
        wout_bf[...] = _bf16(w_out_ref[...])
        wpw_bf[...] = _bf16(w_pw_ref[...])
        half = LRU_BLOCKS // 2
        for h in range(2):
            wg_bf[h, :, 0:GATE_HALF] = _gate_block_diag(wga_ref[h * half:(h + 1) * half])
            wg_bf[h, :, GATE_HALF:D_LRU] = _gate_block_diag(wgx_ref[h * half:(h + 1) * half])
        for copy in weight_copies:
            copy.start()

    @pl.when(t == 0)
    def _():
        xc_s[:, 0:LRU_HALO, :] = jnp.zeros((N_LRU_SLABS, LRU_HALO, LANES), jnp.float32)
        uc_s[:, 0:CONV_HALO, :] = jnp.zeros((N_CONV_SLABS, CONV_HALO, LANES), jnp.float32)
        hin_s[:, 0:1, :] = jnp.zeros((N_LRU_SLABS, 1, LANES), jnp.float32)
        m = _bf16(_rmsnorm(mem_ref[0], g_mem_ref[...]))
        kt = _dot(m, _bf16(wk_ref[...])).T
        vt = _dot(m, _bf16(wv_ref[...])).T
        kt_ref[0] = kt
        vt_ref[0] = vt
        _head_stack_t(kt, vt, kst_s, vst_s)

    hc_ls = (0.5 * LRU_C) * _log_sigmoid(lam_ref[...])
    hb_ga = 0.5 * b_ga_ref[...]
    hb_gx = 0.5 * b_gx_ref[...]
    traced_zero = jnp.minimum(t, n_t - 1) - t

    def block_front(r0, held):
        rows = pl.ds(r0, PROJ_ROWS)
        row0 = pl.multiple_of(r0 + traced_zero, SUBLANES) if held else r0
        for l in range(N_LRU_SLABS):
            lo = l * LANES
            w_row = lambda k, lo=lo: 0.5 * w_lc_ref[k:k + 1, lo:lo + LANES]
            xl = (_conv_taps(xc_s, l, w_row, row0, PROJ_ROWS, LRU_CONV, LRU_HALO)
                  + 0.5 * b_lc_ref[:, lo:lo + LANES])
            xl_s[rows, lo:lo + LANES] = xl
        xl_bf = _bf16(xl_s[rows, :])
        pre_s[rows, 0:D_LRU] = _dot(xl_bf[:, 0:GATE_HALF], wg_bf[0])
        pre_s[rows, D_LRU:2 * D_LRU] = _dot(xl_bf[:, GATE_HALF:D_LRU], wg_bf[1])
        tails = []
        for c in range(r0 // seg, (r0 + PROJ_ROWS) // seg):
            s0 = c * seg
            p0 = c * pitch
            for l in range(N_LRU_SLABS):
                lo = l * LANES
                ca, cx = _pre_cols(l)
                a, u = _lru_coeffs(pre_s[pl.ds(s0, seg), ca:ca + LANES],
                                   pre_s[pl.ds(s0, seg), cx:cx + LANES],
                                   xl_s[pl.ds(s0, seg), lo:lo + LANES],
                                   hb_ga[:, lo:lo + LANES], hb_gx[:, lo:lo + LANES],
                                   hc_ls[:, lo:lo + LANES])
                a_s[l, pl.ds(p0, seg), :] = a
                u_s[l, pl.ds(p0, seg), :] = u
                if c == (r0 + PROJ_ROWS) // seg - 1:
                    tails.append(u[seg - SUBLANES:, :])
        zero = _zero_row_after(tails[0])
        for tail in tails[1:]:
            zero = zero + _zero_row_after(tail)
        return zero

    xn0 = _bf16(_rms_scaled(x_ref[0, 0:PROJ_ROWS, :]))
    z_s[0:PROJ_ROWS, :] = _dot(xn0, win_bf[...])
    for l in range(N_LRU_SLABS):
        lo = l * LANES
        xc_s[l, LRU_HALO:LRU_HALO + PROJ_ROWS, :] = z_s[0:PROJ_ROWS, OFF_LRU_IN + lo:OFF_LRU_IN + lo + LANES]
    xn1 = _bf16(_rms_scaled(x_ref[0, PROJ_ROWS:tile, :]))
    first = _dot(xn1, win_bf[:, 0:2 * LANES])
    z_s[PROJ_ROWS:tile, 0:2 * LANES] = first
    xc_s[0, LRU_HALO + tile:LRU_HALO + tile + SUBLANES, :] = first[PROJ_ROWS - SUBLANES:, 0:LANES]
    block_front(0, True)
    z_s[PROJ_ROWS:tile, 2 * LANES:] = _dot(xn1, win_bf[:, 2 * LANES:])
    for l in range(N_LRU_SLABS):
        lo = l * LANES
        xc_s[l, LRU_HALO + PROJ_ROWS:LRU_HALO + tile, :] = z_s[PROJ_ROWS:tile, OFF_LRU_IN + lo:OFF_LRU_IN + lo + LANES]
    block_front(PROJ_ROWS, False)

    def tot_body(j, carry):
        new = []
        for l in range(N_LRU_SLABS):
            a_tot, u_tot = carry[l]
            a_j = a_s[l, pl.ds(j, n_seg, stride=pitch), :]
            u_j = u_s[l, pl.ds(j, n_seg, stride=pitch), :]
            new.append((a_j * a_tot, a_j * u_tot + u_j))
        return tuple(new)

    init = tuple((jnp.ones((n_seg, LANES), jnp.float32), jnp.zeros((n_seg, LANES), jnp.float32))
                 for _ in range(N_LRU_SLABS))
    totals = _unrolled(seg, tot_body, init)

    h_in = []
    for l in range(N_LRU_SLABS):
        a_tot, u_tot = totals[l]
        h = hin_s[l, 0:1, :]
        for c in range(n_seg):
            h = a_tot[c:c + 1, :] * h + u_tot[c:c + 1, :]
            hin_s[l, c + 1:c + 2, :] = h
        h_in.append(hin_s[l, 0:n_seg, :])
        hin_s[l, 0:1, :] = h

    def scan_body(j, carry):
        new = []
        for l in range(N_LRU_SLABS):
            a_j = a_s[l, pl.ds(j, n_seg, stride=pitch), :]
            u_j = u_s[l, pl.ds(j, n_seg, stride=pitch), :]
            h = a_j * carry[l] + u_j
            u_s[l, pl.ds(j, n_seg, stride=pitch), :] = h
            new.append(h)
        return tuple(new)

    scores = _dot(_bf16(z_s[:, OFF_Q:OFF_Q + D_XATTN]), kst_s[...])
    sc_s[...] = scores
    after_scores = _zero_row_after(scores[tile - SUBLANES:tile, N_XHEADS * N_MEM - LANES:])
    _unrolled(seg, scan_body, tuple(h + after_scores for h in h_in))

    for l in range(N_CONV_SLABS):
        lo = l * LANES
        uc_s[l, CONV_HALO:CONV_HALO + tile, :] = _gated_of_twice(
            z_s[:, OFF_GLU_A + lo:OFF_GLU_A + lo + LANES],
            z_s[:, OFF_GLU_B + lo:OFF_GLU_B + lo + LANES])

    def seg_body(c, carry):
        r0 = _aligned(c * seg, seg)
        p0 = _aligned(c * pitch, SUBLANES)
        for l in range(N_LRU_SLABS):
            lo = l * LANES
            gate = z_s[pl.ds(r0, seg), OFF_LRU_GATE + lo:OFF_LRU_GATE + lo + LANES]
            mix_s[pl.ds(r0, seg), lo:lo + LANES] = _times_silu_of_twice_bf16(u_s[l, pl.ds(p0, seg), :], gate)
        for piece in range(seg // CONV_ROWS):
            rr = r0 + piece * CONV_ROWS
            ys = []
            for l in range(N_CONV_SLABS):
                lo = l * LANES
                w_row = lambda k, lo=lo: w_dw_ref[k, :, lo:lo + LANES]
                ys.append(_conv_taps(uc_s, l, w_row, rr, CONV_ROWS, CONV_W, CONV_HALO)
                          + b_dw_ref[:, lo:lo + LANES])
            outs = _layernorm_silu_slabs(ys, ln_g_ref, ln_b_ref)
            for l in range(N_CONV_SLABS):
                lo = l * LANES
                ln_s[pl.ds(rr, CONV_ROWS), lo:lo + LANES] = outs[l]
        return carry

    _unrolled(n_seg, seg_body, 0)

    mix_s[:, D_LRU:D_LRU + D_CONV] = _times_silu_of_twice_bf16(
        _dot(ln_s[...], wpw_bf[...]) + b_pw_ref[...], z_s[:, OFF_CONV_GATE:OFF_CONV_GATE + D_CONV])

    def attn_body(c, carry):
        r0 = _aligned(c * ATTN_ROWS, ATTN_ROWS)
        p_s[pl.ds(r0, ATTN_ROWS), :] = _softmax_heads(sc_s[pl.ds(r0, ATTN_ROWS), :])
        return carry

    _unrolled(tile // ATTN_ROWS, attn_body, 0)
    o = _dot_nt(p_s[...], vst_s[...])
    mix_s[:, D_LRU + D_CONV:D_MIX] = _times_silu_of_twice_bf16(
        o, z_s[:, OFF_ATTN_GATE:OFF_ATTN_GATE + D_XATTN])

    y = x_ref[0] + _dot(mix_s[...], wout_bf[...])
    y_ref[0] = _rmsnorm(y, g_final_ref[...])

    for l in range(N_LRU_SLABS):
        xc_s[l, LRU_HALO - (LRU_CONV - 1):LRU_HALO, :] = (
            xc_s[l, LRU_HALO + tile - (LRU_CONV - 1):LRU_HALO + tile, :])
    for l in range(N_CONV_SLABS):
        uc_s[l, CONV_HALO - (CONV_W - 1):CONV_HALO, :] = (
            uc_s[l, CONV_HALO + tile - (CONV_W - 1):CONV_HALO + tile, :])

    def write_state(bb):
        for l in range(N_LRU_SLABS):
            lo = l * LANES
            h_out_ref[bb:bb + 1, lo:lo + LANES] = hin_s[l, 0:1, :]
            for k in range(LRU_CONV - 1):
                r = LRU_HALO - (LRU_CONV - 1) + k
                lb_out_ref[k, bb:bb + 1, lo:lo + LANES] = xc_s[l, r:r + 1, :]
        for l in range(N_CONV_SLABS):
            lo = l * LANES
            for k in range(CONV_W - 1):
                r = CONV_HALO - (CONV_W - 1) + k
                cb_out_ref[k, bb:bb + 1, lo:lo + LANES] = uc_s[l, r:r + 1, :]

    for bb in range(h_out_ref.shape[0]):
        pl.when(jnp.logical_and(t == n_t - 1, b == bb))(functools.partial(write_state, bb))

    @pl.when(jnp.logical_and(t == n_t - 1, b == h_out_ref.shape[0] - 1))
    def _():
        for copy in weight_copies:
            copy.wait()


def _whole(shape):
    nd = len(shape)
    return pl.BlockSpec(shape, lambda *_: (0,) * nd, pipeline_mode=pl.Buffered(1))


def _prompt(x, mem, w):
    bsz, seq, _ = x.shape
    tile = PROMPT_TILE
    n_t = seq // tile
    n_seg = tile // PROMPT_SEG
    weights = [w["g_mem"], w["w_mem_k"], w["w_mem_v"],
               w["g_norm"], w["w_in"], w["w_lc"], w["b_lc"], w["w_ga"], w["w_gx"], w["b_ga"], w["b_gx"],
               w["lam"], w["w_dw"], w["b_dw"], w["ln_g"], w["ln_b"], w["w_pw"], w["b_pw"],
               w["w_out"], w["g_final"]]
    weight_specs = [
        _whole((1, D_MODEL)), _whole((None, D_MODEL, D_XATTN)), _whole((None, D_MODEL, D_XATTN)),
        _whole((1, D_MODEL)), _whole((None, D_MODEL, D_IN)), _whole((None, LRU_CONV, D_LRU)),
        _whole((1, D_LRU)), _whole((None, LRU_BLOCKS, LRU_BLOCK, LRU_BLOCK)),
        _whole((None, LRU_BLOCKS, LRU_BLOCK, LRU_BLOCK)), _whole((1, D_LRU)), _whole((1, D_LRU)),
        _whole((1, D_LRU)), _whole((CONV_W, 1, D_CONV)), _whole((1, D_CONV)), _whole((1, D_CONV)),
        _whole((1, D_CONV)), _whole((None, D_CONV, D_CONV)), _whole((1, D_CONV)),
        _whole((None, D_MIX, D_MODEL)), _whole((1, D_MODEL)),
    ]
    in_specs = [
        pl.BlockSpec((1, tile, D_MODEL), lambda b, t: (b, t, 0)),
        pl.BlockSpec((1, N_MEM, D_MODEL), lambda b, t: (b, 0, 0)),
    ] + weight_specs
    out_specs = [
        pl.BlockSpec((1, tile, D_MODEL), lambda b, t: (b, t, 0)),
        pl.BlockSpec((1, D_XATTN, N_MEM), lambda b, t: (b, 0, 0)),
        pl.BlockSpec((1, D_XATTN, N_MEM), lambda b, t: (b, 0, 0)),
        _whole((bsz, D_LRU)),
        _whole((LRU_CONV - 1, bsz, D_LRU)),
        _whole((CONV_W - 1, bsz, D_CONV)),
    ] + [pl.BlockSpec(memory_space=pl.ANY)] * 4
    out_shape = [
        jax.ShapeDtypeStruct((bsz, seq, D_MODEL), jnp.float32),
        jax.ShapeDtypeStruct((bsz, D_XATTN, N_MEM), jnp.float32),
        jax.ShapeDtypeStruct((bsz, D_XATTN, N_MEM), jnp.float32),
        jax.ShapeDtypeStruct((bsz, D_LRU), jnp.float32),
        jax.ShapeDtypeStruct((LRU_CONV - 1, bsz, D_LRU), jnp.float32),
        jax.ShapeDtypeStruct((CONV_W - 1, bsz, D_CONV), jnp.float32),
        jax.ShapeDtypeStruct((D_MODEL, D_IN), jnp.bfloat16),
        jax.ShapeDtypeStruct((D_MIX, D_MODEL), jnp.bfloat16),
        jax.ShapeDtypeStruct((D_CONV, D_CONV), jnp.bfloat16),
        jax.ShapeDtypeStruct((2, GATE_HALF, D_LRU), jnp.bfloat16),
    ]
    scratch = [
        pltpu.VMEM((tile, D_IN), jnp.float32),
        pltpu.VMEM((N_LRU_SLABS, LRU_HALO + tile + SUBLANES, LANES), jnp.float32),
        pltpu.VMEM((tile, D_LRU), jnp.float32),
        pltpu.VMEM((tile, 2 * D_LRU), jnp.float32),
        pltpu.VMEM((N_LRU_SLABS, n_seg * PROMPT_PITCH, LANES), jnp.float32),
        pltpu.VMEM((N_LRU_SLABS, n_seg * PROMPT_PITCH, LANES), jnp.float32),
        pltpu.VMEM((N_CONV_SLABS, CONV_HALO + tile, LANES), jnp.float32),
        pltpu.VMEM((tile, D_CONV), jnp.bfloat16),
        pltpu.VMEM((tile, D_MIX), jnp.bfloat16),
        pltpu.VMEM((N_LRU_SLABS, 2 * SUBLANES, LANES), jnp.float32),
        pltpu.VMEM((D_XATTN, N_XHEADS * N_MEM), jnp.bfloat16),
        pltpu.VMEM((D_XATTN, N_XHEADS * N_MEM), jnp.bfloat16),
        pltpu.VMEM((tile, N_XHEADS * N_MEM), jnp.float32),
        pltpu.VMEM((tile, N_XHEADS * N_MEM), jnp.bfloat16),
        pltpu.VMEM((D_MODEL, D_IN), jnp.bfloat16),
        pltpu.VMEM((D_MIX, D_MODEL), jnp.bfloat16),
        pltpu.VMEM((D_CONV, D_CONV), jnp.bfloat16),
        pltpu.VMEM((2, GATE_HALF, D_LRU), jnp.bfloat16),
        pltpu.SemaphoreType.DMA((4,)),
    ]
    return pl.pallas_call(
        _prompt_kernel,
        grid=(bsz, n_t),
        in_specs=in_specs,
        out_specs=out_specs,
        out_shape=out_shape,
        scratch_shapes=scratch,
        compiler_params=pltpu.CompilerParams(
            dimension_semantics=("arbitrary", "arbitrary"),
            vmem_limit_bytes=VMEM_LIMIT_BYTES),
        name="prompt_layer",
    )(x, mem, *weights)


def _sample_kernel(nb, steps,
                   x_ref, h0_ref, lb_ref, cb_ref, kt_hbm, vt_hbm,
                   win_bf, w_lc_ref, b_lc_ref, wg_bf,
                   b_ga_ref, b_gx_ref, lam_ref, w_dw_ref, b_dw_ref, ln_g_ref, ln_b_ref,
                   wpw_bf, b_pw_ref, wout_bf, g_final_ref,
                   y_ref, h_out_ref, lb_out_ref, cb_out_ref,
                   z_s, xc_s, xl_s, pre_s, a_s, u_s, uc_s, ln_s, mix_s, sc_s, p_s,
                   kt_ref, vt_ref, kv_sem):
    rows = nb * steps
    pitch = SAMPLE_PITCH
    cpitch = CONV_HALO + steps
    prow = nb * pitch

    group = pl.program_id(0)
    first = group * nb
    n_groups = kt_ref.shape[0] // nb

    def kv_copies(grp):
        rows_of = pl.ds(grp * nb, nb)
        return (pltpu.make_async_copy(kt_hbm.at[rows_of], kt_ref.at[rows_of], kv_sem.at[2 * grp]),
                pltpu.make_async_copy(vt_hbm.at[rows_of], vt_ref.at[rows_of], kv_sem.at[2 * grp + 1]))

    @pl.when(group == 0)
    def _():
        for grp in range(n_groups):
            for copy in kv_copies(grp):
                copy.start()
        xl_s[prow - LRU_HALO:prow, :] = jnp.zeros((LRU_HALO, D_LRU), jnp.float32)

    x = x_ref[...].reshape(rows, D_MODEL)
    z_s[...] = _dot(_bf16(_rms_scaled(x)), win_bf[...])

    for b in range(nb):
        for l in range(N_LRU_SLABS):
            lo = l * LANES
            zcol = OFF_LRU_IN + lo
            xc_s[l, b * pitch:b * pitch + LRU_HALO, :] = jnp.zeros((LRU_HALO, LANES), jnp.float32)
            for k in range(LRU_CONV - 1):
                r = b * pitch + LRU_HALO - (LRU_CONV - 1) + k
                xc_s[l, r:r + 1, :] = lb_ref[k, b:b + 1, lo:lo + LANES]
                rn = (b + 1) * steps - (LRU_CONV - 1) + k
                lb_out_ref[k, b:b + 1, lo:lo + LANES] = z_s[rn:rn + 1, zcol:zcol + LANES]
            xc_s[l, b * pitch + LRU_HALO:(b + 1) * pitch, :] = (
                z_s[b * steps:(b + 1) * steps, zcol:zcol + LANES])
    n_conv = prow - LRU_HALO
    for l in range(N_LRU_SLABS):
        lo = l * LANES
        w_row = lambda k, lo=lo: 0.5 * w_lc_ref[k:k + 1, lo:lo + LANES]
        xl = (_conv_taps(xc_s, l, w_row, 0, n_conv, LRU_CONV, LRU_HALO)
              + 0.5 * b_lc_ref[:, lo:lo + LANES])
        xl_s[0:n_conv, lo:lo + LANES] = xl
    _lru_gates(_bf16(xl_s[...]), wg_bf, pre_s)

    hc_ls = (0.5 * LRU_C) * _log_sigmoid(lam_ref[...])
    hb_ga = 0.5 * b_ga_ref[...]
    hb_gx = 0.5 * b_gx_ref[...]
    for l in range(N_LRU_SLABS):
        lo = l * LANES
        ca, cx = _pre_cols(l)
        a, u = _lru_coeffs(pre_s[:, ca:ca + LANES], pre_s[:, cx:cx + LANES], xl_s[:, lo:lo + LANES],
                           hb_ga[:, lo:lo + LANES], hb_gx[:, lo:lo + LANES],
                           hc_ls[:, lo:lo + LANES])
        a_s[l, :, :] = a
        u_s[l, :, :] = u

    n_grp = nb // SUBLANES

    def scan_body(j, carry):
        new = []
        for g in range(n_grp):
            for l in range(N_LRU_SLABS):
                base = g * SUBLANES * pitch
                a_j = a_s[l, pl.ds(base + j, SUBLANES, stride=pitch), :]
                u_j = u_s[l, pl.ds(base + j, SUBLANES, stride=pitch), :]
                h = a_j * carry[g * N_LRU_SLABS + l] + u_j
                u_s[l, pl.ds(base + j, SUBLANES, stride=pitch), :] = h
                new.append(h)
        return tuple(new)

    h0 = tuple(h0_ref[g * SUBLANES:(g + 1) * SUBLANES, l * LANES:(l + 1) * LANES]
               for g in range(n_grp) for l in range(N_LRU_SLABS))
    h_fin = _unrolled(steps, scan_body, h0)
    for g in range(n_grp):
        for l in range(N_LRU_SLABS):
            h_out_ref[g * SUBLANES:(g + 1) * SUBLANES, l * LANES:(l + 1) * LANES] = (
                h_fin[g * N_LRU_SLABS + l])

    for b in range(nb):
        for l in range(N_CONV_SLABS):
            lo = l * LANES
            uc = _gated_of_twice(
                z_s[b * steps:(b + 1) * steps, OFF_GLU_A + lo:OFF_GLU_A + lo + LANES],
                z_s[b * steps:(b + 1) * steps, OFF_GLU_B + lo:OFF_GLU_B + lo + LANES])
            uc_s[l, b * cpitch + CONV_HALO:(b + 1) * cpitch, :] = uc
            for k in range(CONV_W - 1):
                r = b * cpitch + CONV_HALO - (CONV_W - 1) + k
                uc_s[l, r:r + 1, :] = cb_ref[k, b:b + 1, lo:lo + LANES]
                rn = (b + 1) * cpitch - (CONV_W - 1) + k
                cb_out_ref[k, b:b + 1, lo:lo + LANES] = uc_s[l, rn:rn + 1, :]

    for b in range(nb):
        r0 = b * cpitch
        o0 = b * steps
        ys = []
        for l in range(N_CONV_SLABS):
            lo = l * LANES
            w_row = lambda k, lo=lo: w_dw_ref[k, :, lo:lo + LANES]
            ys.append(_conv_taps(uc_s, l, w_row, r0, steps, CONV_W, CONV_HALO)
                      + b_dw_ref[:, lo:lo + LANES])
        outs = _layernorm_silu_slabs(ys, ln_g_ref, ln_b_ref)
        for l in range(N_CONV_SLABS):
            lo = l * LANES
            ln_s[o0:o0 + steps, lo:lo + LANES] = outs[l]

    mix_s[:, D_LRU:D_LRU + D_CONV] = _times_silu_of_twice_bf16(
        _dot(ln_s[...], wpw_bf[...]) + b_pw_ref[...], z_s[:, OFF_CONV_GATE:OFF_CONV_GATE + D_CONV])

    masks = _lane_head_masks()
    scale = SCORE_SCALE
    qrows = N_XHEADS * steps
    for copy in kv_copies(group):
        copy.wait()
    for b in range(nb):
        for l in range(N_LRU_SLABS):
            lo = l * LANES
            gate = z_s[b * steps:(b + 1) * steps, OFF_LRU_GATE + lo:OFF_LRU_GATE + lo + LANES]
            mix_s[b * steps:(b + 1) * steps, lo:lo + LANES] = _times_silu_of_twice_bf16(
                u_s[l, b * pitch:b * pitch + steps, :], gate)
    for b in range(nb):
        q = z_s[b * steps:(b + 1) * steps, OFF_Q:OFF_Q + D_XATTN]
        qm = jnp.concatenate([_bf16(q * (m * scale)) for m in masks], axis=0)
        sc_s[b * qrows:(b + 1) * qrows, :] = _dot(qm, _bf16(kt_ref[first + b]))
    s = sc_s[...]
    e = jnp.exp2(s - jnp.max(s, axis=-1, keepdims=True))
    p_s[...] = _bf16(e * (1.0 / jnp.sum(e, axis=-1, keepdims=True)))
    for b in range(nb):
        o_all = _dot_nt(p_s[b * qrows:(b + 1) * qrows, :], _bf16(vt_ref[first + b]))
        o = o_all[0:steps] * masks[0]
        for h in range(1, N_XHEADS):
            o = o + o_all[h * steps:(h + 1) * steps] * masks[h]
        gate = z_s[b * steps:(b + 1) * steps, OFF_ATTN_GATE:OFF_ATTN_GATE + D_XATTN]
        mix_s[b * steps:(b + 1) * steps, D_LRU + D_CONV:D_MIX] = _times_silu_of_twice_bf16(o, gate)

    y = x + _dot(mix_s[...], wout_bf[...])
    y_ref[...] = _rmsnorm(y, g_final_ref[...]).reshape(nb, steps, D_MODEL)


def _sample(x, h0, lb, cb, kt, vt, w, win_bf, wout_bf, wpw_bf, wg_bf):
    bsz, steps, _ = x.shape
    nb = SAMPLE_GROUP
    assert bsz % nb == 0 and nb % SUBLANES == 0
    assert steps + LRU_HALO == SAMPLE_PITCH and steps >= CONV_W - 1
    rows = nb * steps
    prow = nb * SAMPLE_PITCH
    weights = [win_bf, w["w_lc"], w["b_lc"], wg_bf, w["b_ga"], w["b_gx"],
               w["lam"], w["w_dw"], w["b_dw"], w["ln_g"], w["ln_b"], wpw_bf, w["b_pw"],
               wout_bf, w["g_final"]]
    weight_specs = [
        _whole((D_MODEL, D_IN)), _whole((None, LRU_CONV, D_LRU)),
        _whole((1, D_LRU)), _whole((2, GATE_HALF, D_LRU)), _whole((1, D_LRU)), _whole((1, D_LRU)),
        _whole((1, D_LRU)), _whole((CONV_W, 1, D_CONV)), _whole((1, D_CONV)), _whole((1, D_CONV)),
        _whole((1, D_CONV)), _whole((D_CONV, D_CONV)), _whole((1, D_CONV)),
        _whole((D_MIX, D_MODEL)), _whole((1, D_MODEL)),
    ]
    in_specs = [
        pl.BlockSpec((nb, steps, D_MODEL), lambda g: (g, 0, 0)),
        pl.BlockSpec((nb, D_LRU), lambda g: (g, 0)),
        pl.BlockSpec((LRU_CONV - 1, nb, D_LRU), lambda g: (0, g, 0)),
        pl.BlockSpec((CONV_W - 1, nb, D_CONV), lambda g: (0, g, 0)),
        pl.BlockSpec(memory_space=pl.ANY),
        pl.BlockSpec(memory_space=pl.ANY),
    ] + weight_specs
    out_specs = [
        pl.BlockSpec((nb, steps, D_MODEL), lambda g: (g, 0, 0)),
        pl.BlockSpec((nb, D_LRU), lambda g: (g, 0)),
        pl.BlockSpec((LRU_CONV - 1, nb, D_LRU), lambda g: (0, g, 0)),
        pl.BlockSpec((CONV_W - 1, nb, D_CONV), lambda g: (0, g, 0)),
    ]
    out_shape = [
        jax.ShapeDtypeStruct((bsz, steps, D_MODEL), jnp.float32),
        jax.ShapeDtypeStruct((bsz, D_LRU), jnp.float32),
        jax.ShapeDtypeStruct((LRU_CONV - 1, bsz, D_LRU), jnp.float32),
        jax.ShapeDtypeStruct((CONV_W - 1, bsz, D_CONV), jnp.float32),
    ]
    scratch = [
        pltpu.VMEM((rows, D_IN), jnp.float32),
        pltpu.VMEM((N_LRU_SLABS, prow, LANES), jnp.float32),
        pltpu.VMEM((prow, D_LRU), jnp.float32),
        pltpu.VMEM((prow, 2 * D_LRU), jnp.float32),
        pltpu.VMEM((N_LRU_SLABS, prow, LANES), jnp.float32),
        pltpu.VMEM((N_LRU_SLABS, prow, LANES), jnp.float32),
        pltpu.VMEM((N_CONV_SLABS, nb * (CONV_HALO + steps), LANES), jnp.float32),
        pltpu.VMEM((rows, D_CONV), jnp.bfloat16),
        pltpu.VMEM((rows, D_MIX), jnp.bfloat16),
        pltpu.VMEM((N_XHEADS * rows, N_MEM), jnp.float32),
        pltpu.VMEM((N_XHEADS * rows, N_MEM), jnp.bfloat16),
        pltpu.VMEM((bsz, D_XATTN, N_MEM), jnp.float32),
        pltpu.VMEM((bsz, D_XATTN, N_MEM), jnp.float32),
        pltpu.SemaphoreType.DMA((2 * (bsz // nb),)),
    ]
    return pl.pallas_call(
        functools.partial(_sample_kernel, nb, steps),
        grid=(bsz // nb,),
        in_specs=in_specs,
        out_specs=out_specs,
        out_shape=out_shape,
        scratch_shapes=scratch,
        compiler_params=pltpu.CompilerParams(
            dimension_semantics=("arbitrary",),
            vmem_limit_bytes=VMEM_LIMIT_BYTES),
        name="sample_layer",
    )(x, h0, lb, cb, kt, vt, *weights)


def _time_major(state):
    return jnp.transpose(state, (1, 0, 2))


def _memory_t(mem):
    bsz = mem.shape[0]
    return jnp.transpose(mem, (0, 2, 3, 1)).reshape(bsz, D_XATTN, N_MEM)


def _memory_from_t(mem_t):
    bsz = mem_t.shape[0]
    return jnp.transpose(mem_t.reshape(bsz, N_XHEADS, XHEAD_DIM, N_MEM), (0, 3, 1, 2))


def kernel(x_prompt, x_sample, state_lru_h, state_lru_conv, state_conv, cache_mem_k, cache_mem_v,
           mem_prompt, g_norm, w_in, w_lru_conv, b_lru_conv, w_gate_a, b_gate_a, w_gate_x, b_gate_x,
           lru_lambda, w_dw, b_dw, ln_g, ln_b, w_pw, b_pw, g_mem, w_mem_k, w_mem_v, w_out, g_final):
    depth = g_norm.shape[0]
    assert depth == 1
    w = {
        "g_norm": g_norm, "w_in": w_in, "w_lc": w_lru_conv, "b_lc": b_lru_conv,
        "w_ga": w_gate_a, "w_gx": w_gate_x, "b_ga": b_gate_a, "b_gx": b_gate_x, "lam": lru_lambda,
        "w_dw": jnp.transpose(w_dw, (1, 0, 2)), "b_dw": b_dw, "ln_g": ln_g, "ln_b": ln_b,
        "w_pw": w_pw, "b_pw": b_pw, "w_out": w_out, "g_final": g_final.reshape(1, D_MODEL),
        "g_mem": g_mem, "w_mem_k": w_mem_k, "w_mem_v": w_mem_v,
    }
    y_prompt, mkt, mvt, ph, plb, pcb, win_bf, wout_bf, wpw_bf, wg_bf = _prompt(x_prompt, mem_prompt, w)

    y_sample, sh, slb, scb = _sample(
        x_sample, state_lru_h[0], _time_major(state_lru_conv[0]), _time_major(state_conv[0]),
        _memory_t(cache_mem_k[0]), _memory_t(cache_mem_v[0]), w, win_bf, wout_bf, wpw_bf, wg_bf)

    return (y_prompt, y_sample,
            ph[None], _time_major(plb)[None], _time_major(pcb)[None],
            _memory_from_t(mkt)[None], _memory_from_t(mvt)[None],
            sh[None], _time_major(slb)[None], _time_major(scb)[None])
```

```python
import functools

import jax
import jax.numpy as jnp
from jax import lax
from jax.experimental import pallas as pl
from jax.experimental.pallas import tpu as pltpu

D_MODEL = 1024
N_MEM = 256
D_LRU = 512
LRU_BLOCKS = 8
LRU_BLOCK = D_LRU // LRU_BLOCKS
LRU_CONV = 4
LRU_C = 8.0
D_CONV = 256
CONV_W = 31
N_XHEADS = 4
XHEAD_DIM = 64
D_XATTN = N_XHEADS * XHEAD_DIM
D_MIX = D_LRU + D_CONV + D_XATTN
D_IN = 2 * D_LRU + 3 * D_CONV + 2 * D_XATTN
EPS = 1e-6

LANES = 128
SUBLANES = 8
N_LRU_SLABS = D_LRU // LANES
N_CONV_SLABS = D_CONV // LANES
GATE_HALF = D_LRU // 2

OFF_LRU_IN = 0
OFF_LRU_GATE = D_LRU
OFF_GLU_A = 2 * D_LRU
OFF_GLU_B = OFF_GLU_A + D_CONV
OFF_CONV_GATE = OFF_GLU_B + D_CONV
OFF_Q = OFF_CONV_GATE + D_CONV
OFF_ATTN_GATE = OFF_Q + D_XATTN

LRU_HALO = 8
CONV_HALO = 32

PROMPT_TILE = 512
PROMPT_SEG = 64
PROMPT_PITCH = 68
SAMPLE_HALO = 4
SAMPLE_PITCH = 36
SAMPLE_GROUP = 16
ATTN_ROWS = 128
CONV_ROWS = 32
PROJ_ROWS = 256

VMEM_LIMIT_BYTES = 56 * 1024 * 1024
HALF_LOG2_E = 0.7213475204444817
SCORE_SCALE = XHEAD_DIM ** -0.5 * 2.0 * HALF_LOG2_E


def _silu_of_twice(hx):
    return hx + hx * jnp.tanh(hx)


def _times_silu_of_twice_bf16(v, hx):
    return _bf16(v) * _silu_of_twice(_bf16(hx))


def _gated_of_twice(ha, hb):
    return ha + ha * jnp.tanh(hb)


def _w_in_col_scale():
    col = lax.broadcasted_iota(jnp.int32, (1, D_IN), 1)
    plain = (col < OFF_LRU_GATE) | ((col >= OFF_Q) & (col < OFF_ATTN_GATE))
    return jnp.where(plain, 1.0, 0.5)


def _sqrt_nonneg(x):
    return jnp.exp2(jnp.log(x) * HALF_LOG2_E)


def _rms_scaled(x):
    ms = jnp.mean(x * x, axis=-1, keepdims=True)
    return x * lax.rsqrt(ms + EPS)


def _rmsnorm(x, g):
    return _rms_scaled(x) * g


def _as_column(row):
    n = row.shape[1] // LANES
    on_diag = (lax.broadcasted_iota(jnp.int32, (LANES, LANES), 0)
               == lax.broadcasted_iota(jnp.int32, (LANES, LANES), 1))
    cols = [jnp.sum(jnp.where(on_diag, row[:, j * LANES:(j + 1) * LANES], 0.0), axis=1, keepdims=True)
            for j in range(n)]
    return jnp.concatenate(cols, axis=0)


def _log_sigmoid(x):
    y = -x
    return -(jnp.maximum(y, 0.0) + jnp.log1p(jnp.exp(-jnp.abs(y))))


def _bf16(x):
    return x.astype(jnp.bfloat16)


def _dot(a, b):
    return jnp.dot(a, b, preferred_element_type=jnp.float32)


def _dot_nt(a, b):
    return lax.dot_general(a, b, (((1,), (1,)), ((), ())), preferred_element_type=jnp.float32)


def _unrolled(n, body, init):
    carry = init
    for i in range(n):
        carry = body(i, carry)
    return carry


def _aligned(x, m):
    return x if isinstance(x, int) else pl.multiple_of(x, m)


def _head_of(index):
    return lax.shift_right_logical(index, XHEAD_DIM.bit_length() - 1)


def _lane_head_masks():
    head = _head_of(lax.broadcasted_iota(jnp.int32, (1, D_XATTN), 1))
    return [(head == h).astype(jnp.float32) for h in range(N_XHEADS)]


def _row_head_masks():
    head = _head_of(lax.broadcasted_iota(jnp.int32, (D_XATTN, 1), 0))
    return [(head == h).astype(jnp.float32) for h in range(N_XHEADS)]


def _head_stack_t(kt, vt, kst_ref, vst_ref):
    scale = SCORE_SCALE
    for h, m in enumerate(_row_head_masks()):
        kst_ref[:, h * N_MEM:(h + 1) * N_MEM] = _bf16(kt * (m * scale))
        vst_ref[:, h * N_MEM:(h + 1) * N_MEM] = _bf16(vt * m)


def _zero_row_after(x):
    bits = lax.bitcast_convert_type(x, jnp.uint32)
    zero = lax.shift_right_logical(lax.shift_right_logical(bits, jnp.uint32(16)), jnp.uint32(16))
    return lax.bitcast_convert_type(zero, jnp.float32)[0:1, :]


def _softmax_heads(s):
    ps = []
    for h in range(N_XHEADS):
        sh = s[:, h * N_MEM:(h + 1) * N_MEM]
        e = jnp.exp2(sh - jnp.max(sh, axis=-1, keepdims=True))
        inv = 1.0 / jnp.sum(e, axis=-1, keepdims=True)
        ps.append(_bf16(e * inv))
    return jnp.concatenate(ps, axis=1)


def _lru_gates(xl_bf, wg_ref, pre_s):
    pre_s[:, 0:D_LRU] = _dot(xl_bf[:, 0:GATE_HALF], wg_ref[0])
    pre_s[:, D_LRU:2 * D_LRU] = _dot(xl_bf[:, GATE_HALF:D_LRU], wg_ref[1])


def _pre_cols(l):
    half_slabs = N_LRU_SLABS // 2
    base = (l // half_slabs) * D_LRU + (l % half_slabs) * LANES
    return base, base + GATE_HALF


def _lru_coeffs(hpre_a, hpre_x, hx, hb_a, hb_x, hc_ls):
    t_r = jnp.tanh(hpre_a + hb_a)
    t_i = jnp.tanh(hpre_x + hb_x)
    log_a = hc_ls + hc_ls * t_r
    a = jnp.exp(log_a)
    one_minus_a2 = -jnp.tanh(log_a) * (a * a + 1.0)
    u = _sqrt_nonneg(one_minus_a2) * (hx + hx * t_i)
    return a, u


def _conv_taps(src_s, l, w_row, row0, nrows, width, halo):
    acc = None
    for k in range(width):
        x_k = src_s[l, pl.ds(row0 + (halo - (width - 1) + k), nrows), :]
        acc = w_row(k) * x_k if acc is None else acc + w_row(k) * x_k
    return acc


def _layernorm_silu_slabs(ys, g_ref, b_ref):
    tot = ys[0]
    for y in ys[1:]:
        tot = tot + y
    mu = jnp.sum(tot, axis=-1, keepdims=True) * (1.0 / D_CONV)
    cs = [y - mu for y in ys]
    sq = cs[0] * cs[0]
    for c in cs[1:]:
        sq = sq + c * c
    var = jnp.sum(sq, axis=-1, keepdims=True) * (1.0 / D_CONV)
    inv = lax.rsqrt(var + EPS)
    outs = []
    for l, c in enumerate(cs):
        lo = l * LANES
        half_g = 0.5 * g_ref[:, lo:lo + LANES]
        half_b = 0.5 * b_ref[:, lo:lo + LANES]
        outs.append(_silu_of_twice(_bf16(c * inv * half_g + half_b)))
    return outs


def _gate_block_diag(w_blocks):
    n = (LRU_BLOCKS // 2) * LRU_BLOCK
    stacked = _bf16(w_blocks.reshape(n, LRU_BLOCK))
    src = lax.broadcasted_iota(jnp.int32, (LRU_BLOCK, n), 0)
    dst = lax.broadcasted_iota(jnp.int32, (LRU_BLOCK, n), 1)
    spread = _bf16(((dst & (LRU_BLOCK - 1)) == src).astype(jnp.float32))
    rows = _head_of(lax.broadcasted_iota(jnp.int32, (n, n), 0))
    cols = _head_of(lax.broadcasted_iota(jnp.int32, (n, n), 1))
    return _bf16(_dot(stacked, spread) * (rows == cols).astype(jnp.float32))


def _prompt_kernel(x_ref, mem_ref, g_mem_ref, wk_ref, wv_ref,
                   g_norm_ref, w_in_ref, w_lc_ref, b_lc_ref, wga_ref, wgx_ref,
                   b_ga_ref, b_gx_ref, lam_ref, w_dw_ref, b_dw_ref, ln_g_ref, ln_b_ref,
                   w_pw_ref, b_pw_ref, w_out_ref, g_final_ref,
                   y_ref, kt_ref, vt_ref, h_out_ref, lb_out_ref, cb_out_ref,
                   win_hbm, wout_hbm, wpw_hbm, wg_hbm,
                   z_s, xc_s, xl_s, pre_s, a_s, u_s, uc_s, ln_s, mix_s, hin_s, kst_s, vst_s, sc_s, p_s,
                   win_bf, wout_bf, wpw_bf, wg_bf, w_sem):
    b = pl.program_id(0)
    t = pl.program_id(1)
    n_t = pl.num_programs(1)
    tile = PROMPT_TILE
    seg = PROMPT_SEG
    n_seg = tile // seg
    pitch = PROMPT_PITCH

    weight_copies = [
        pltpu.make_async_copy(src, dst, w_sem.at[i])
        for i, (src, dst) in enumerate([(win_bf, win_hbm), (wout_bf, wout_hbm),
                                        (wpw_bf, wpw_hbm), (wg_bf, wg_hbm)])]

    @pl.when(jnp.logical_and(b == 0, t == 0))
    def _():
        win_bf[...] = _bf16(w_in_ref[...] * _as_column(g_norm_ref[...]) * _w_in_col_scale())
        wout_bf[...] = _bf16(w_out_ref[...])
        wpw_bf[...] = _bf16(w_pw_ref[...])
        half = LRU_BLOCKS // 2
        for h in range(2):
            wg_bf[h, :, 0:GATE_HALF] = _gate_block_diag(wga_ref[h * half:(h + 1) * half])
            wg_bf[h, :, GATE_HALF:D_LRU] = _gate_block_diag(wgx_ref[h * half:(h + 1) * half])
        for copy in weight_copies:
            copy.start()

    @pl.when(t == 0)
    def _():
        xc_s[:, 0:LRU_HALO, :] = jnp.zeros((N_LRU_SLABS, LRU_HALO, LANES), jnp.float32)
        uc_s[:, 0:CONV_HALO, :] = jnp.zeros((N_CONV_SLABS, CONV_HALO, LANES), jnp.float32)
        hin_s[:, 0:1, :] = jnp.zeros((N_LRU_SLABS, 1, LANES), jnp.float32)
        m = _bf16(_rmsnorm(mem_ref[0], g_mem_ref[...]))
        kt = _dot(m, _bf16(wk_ref[...])).T
        vt = _dot(m, _bf16(wv_ref[...])).T
        kt_ref[0] = kt
        vt_ref[0] = vt
        _head_stack_t(kt, vt, kst_s, vst_s)

    for r0 in range(0, tile, PROJ_ROWS):
        z_s[r0:r0 + PROJ_ROWS, :] = _dot(_bf16(_rms_scaled(x_ref[0, r0:r0 + PROJ_ROWS, :])), win_bf[...])

    for l in range(N_LRU_SLABS):
        lo = l * LANES
        xc_s[l, LRU_HALO:LRU_HALO + tile, :] = z_s[:, OFF_LRU_IN + lo:OFF_LRU_IN + lo + LANES]
    for l in range(N_LRU_SLABS):
        lo = l * LANES
        w_row = lambda k, lo=lo: 0.5 * w_lc_ref[k:k + 1, lo:lo + LANES]
        xl = (_conv_taps(xc_s, l, w_row, 0, tile, LRU_CONV, LRU_HALO)
              + 0.5 * b_lc_ref[:, lo:lo + LANES])
        xl_s[:, lo:lo + LANES] = xl
    _lru_gates(_bf16(xl_s[...]), wg_bf, pre_s)

    hc_ls = (0.5 * LRU_C) * _log_sigmoid(lam_ref[...])
    hb_ga = 0.5 * b_ga_ref[...]
    hb_gx = 0.5 * b_gx_ref[...]

    def coeff_body(c, carry):
        r0 = _aligned(c * seg, seg)
        p0 = _aligned(c * pitch, SUBLANES)
        for l in range(N_LRU_SLABS):
            lo = l * LANES
            ca, cx = _pre_cols(l)
            a, u = _lru_coeffs(pre_s[pl.ds(r0, seg), ca:ca + LANES],
                               pre_s[pl.ds(r0, seg), cx:cx + LANES],
                               xl_s[pl.ds(r0, seg), lo:lo + LANES],
                               hb_ga[:, lo:lo + LANES], hb_gx[:, lo:lo + LANES],
                               hc_ls[:, lo:lo + LANES])
            a_s[l, pl.ds(p0, seg), :] = a
            u_s[l, pl.ds(p0, seg), :] = u
        return carry

    _unrolled(n_seg, coeff_body, 0)

    def tot_body(j, carry):
        new = []
        for l in range(N_LRU_SLABS):
            a_tot, u_tot = carry[l]
            a_j = a_s[l, pl.ds(j, n_seg, stride=pitch), :]
            u_j = u_s[l, pl.ds(j, n_seg, stride=pitch), :]
            new.append((a_j * a_tot, a_j * u_tot + u_j))
        return tuple(new)

    init = tuple((jnp.ones((n_seg, LANES), jnp.float32), jnp.zeros((n_seg, LANES), jnp.float32))
                 for _ in range(N_LRU_SLABS))
    totals = _unrolled(seg, tot_body, init)

    h_in = []
    for l in range(N_LRU_SLABS):
        a_tot, u_tot = totals[l]
        h = hin_s[l, 0:1, :]
        for c in range(n_seg):
            h = a_tot[c:c + 1, :] * h + u_tot[c:c + 1, :]
            hin_s[l, c + 1:c + 2, :] = h
        h_in.append(hin_s[l, 0:n_seg, :])
        hin_s[l, 0:1, :] = h

    def scan_body(j, carry):
        new = []
        for l in range(N_LRU_SLABS):
            a_j = a_s[l, pl.ds(j, n_seg, stride=pitch), :]
            u_j = u_s[l, pl.ds(j, n_seg, stride=pitch), :]
            h = a_j * carry[l] + u_j
            u_s[l, pl.ds(j, n_seg, stride=pitch), :] = h
            new.append(h)
        return tuple(new)

    scores = _dot(_bf16(z_s[:, OFF_Q:OFF_Q + D_XATTN]), kst_s[...])
    sc_s[...] = scores
    after_scores = _zero_row_after(scores[tile - SUBLANES:tile, N_XHEADS * N_MEM - LANES:])
    _unrolled(seg, scan_body, tuple(h + after_scores for h in h_in))

    for l in range(N_CONV_SLABS):
        lo = l * LANES
        uc_s[l, CONV_HALO:CONV_HALO + tile, :] = _gated_of_twice(
            z_s[:, OFF_GLU_A + lo:OFF_GLU_A + lo + LANES],
            z_s[:, OFF_GLU_B + lo:OFF_GLU_B + lo + LANES])

    def seg_body(c, carry):
        r0 = _aligned(c * seg, seg)
        p0 = _aligned(c * pitch, SUBLANES)
        for l in range(N_LRU_SLABS):
            lo = l * LANES
            gate = z_s[pl.ds(r0, seg), OFF_LRU_GATE + lo:OFF_LRU_GATE + lo + LANES]
            mix_s[pl.ds(r0, seg), lo:lo + LANES] = _times_silu_of_twice_bf16(u_s[l, pl.ds(p0, seg), :], gate)
        for piece in range(seg // CONV_ROWS):
            rr = r0 + piece * CONV_ROWS
            ys = []
            for l in range(N_CONV_SLABS):
                lo = l * LANES
                w_row = lambda k, lo=lo: w_dw_ref[k, :, lo:lo + LANES]
                ys.append(_conv_taps(uc_s, l, w_row, rr, CONV_ROWS, CONV_W, CONV_HALO)
                          + b_dw_ref[:, lo:lo + LANES])
            outs = _layernorm_silu_slabs(ys, ln_g_ref, ln_b_ref)
            for l in range(N_CONV_SLABS):
                lo = l * LANES
                ln_s[pl.ds(rr, CONV_ROWS), lo:lo + LANES] = outs[l]
        return carry

    _unrolled(n_seg, seg_body, 0)

    mix_s[:, D_LRU:D_LRU + D_CONV] = _times_silu_of_twice_bf16(
        _dot(ln_s[...], wpw_bf[...]) + b_pw_ref[...], z_s[:, OFF_CONV_GATE:OFF_CONV_GATE + D_CONV])

    def attn_body(c, carry):
        r0 = _aligned(c * ATTN_ROWS, ATTN_ROWS)
        p_s[pl.ds(r0, ATTN_ROWS), :] = _softmax_heads(sc_s[pl.ds(r0, ATTN_ROWS), :])
        return carry

    _unrolled(tile // ATTN_ROWS, attn_body, 0)
    o = _dot_nt(p_s[...], vst_s[...])
    mix_s[:, D_LRU + D_CONV:D_MIX] = _times_silu_of_twice_bf16(
        o, z_s[:, OFF_ATTN_GATE:OFF_ATTN_GATE + D_XATTN])

    y = x_ref[0] + _dot(mix_s[...], wout_bf[...])
    y_ref[0] = _rmsnorm(y, g_final_ref[...])

    for l in range(N_LRU_SLABS):
        xc_s[l, LRU_HALO - (LRU_CONV - 1):LRU_HALO, :] = (
            xc_s[l, LRU_HALO + tile - (LRU_CONV - 1):LRU_HALO + tile, :])
    for l in range(N_CONV_SLABS):
        uc_s[l, CONV_HALO - (CONV_W - 1):CONV_HALO, :] = (
            uc_s[l, CONV_HALO + tile - (CONV_W - 1):CONV_HALO + tile, :])

    def write_state(bb):
        for l in range(N_LRU_SLABS):
            lo = l * LANES
            h_out_ref[bb:bb + 1, lo:lo + LANES] = hin_s[l, 0:1, :]
            for k in range(LRU_CONV - 1):
                r = LRU_HALO - (LRU_CONV - 1) + k
                lb_out_ref[k, bb:bb + 1, lo:lo + LANES] = xc_s[l, r:r + 1, :]
        for l in range(N_CONV_SLABS):
            lo = l * LANES
            for k in range(CONV_W - 1):
                r = CONV_HALO - (CONV_W - 1) + k
                cb_out_ref[k, bb:bb + 1, lo:lo + LANES] = uc_s[l, r:r + 1, :]

    for bb in range(h_out_ref.shape[0]):
        pl.when(jnp.logical_and(t == n_t - 1, b == bb))(functools.partial(write_state, bb))

    @pl.when(jnp.logical_and(t == n_t - 1, b == h_out_ref.shape[0] - 1))
    def _():
        for copy in weight_copies:
            copy.wait()


def _whole(shape):
    nd = len(shape)
    return pl.BlockSpec(shape, lambda *_: (0,) * nd, pipeline_mode=pl.Buffered(1))


def _prompt(x, mem, w):
    bsz, seq, _ = x.shape
    tile = PROMPT_TILE
    n_t = seq // tile
    n_seg = tile // PROMPT_SEG
    weights = [w["g_mem"], w["w_mem_k"], w["w_mem_v"],
               w["g_norm"], w["w_in"], w["w_lc"], w["b_lc"], w["w_ga"], w["w_gx"], w["b_ga"], w["b_gx"],
               w["lam"], w["w_dw"], w["b_dw"], w["ln_g"], w["ln_b"], w["w_pw"], w["b_pw"],
               w["w_out"], w["g_final"]]
    weight_specs = [
        _whole((1, D_MODEL)), _whole((None, D_MODEL, D_XATTN)), _whole((None, D_MODEL, D_XATTN)),
        _whole((1, D_MODEL)), _whole((None, D_MODEL, D_IN)), _whole((None, LRU_CONV, D_LRU)),
        _whole((1, D_LRU)), _whole((None, LRU_BLOCKS, LRU_BLOCK, LRU_BLOCK)),
        _whole((None, LRU_BLOCKS, LRU_BLOCK, LRU_BLOCK)), _whole((1, D_LRU)), _whole((1, D_LRU)),
        _whole((1, D_LRU)), _whole((CONV_W, 1, D_CONV)), _whole((1, D_CONV)), _whole((1, D_CONV)),
        _whole((1, D_CONV)), _whole((None, D_CONV, D_CONV)), _whole((1, D_CONV)),
        _whole((None, D_MIX, D_MODEL)), _whole((1, D_MODEL)),
    ]
    in_specs = [
        pl.BlockSpec((1, tile, D_MODEL), lambda b, t: (b, t, 0)),
        pl.BlockSpec((1, N_MEM, D_MODEL), lambda b, t: (b, 0, 0)),
    ] + weight_specs
    out_specs = [
        pl.BlockSpec((1, tile, D_MODEL), lambda b, t: (b, t, 0)),
        pl.BlockSpec((1, D_XATTN, N_MEM), lambda b, t: (b, 0, 0)),
        pl.BlockSpec((1, D_XATTN, N_MEM), lambda b, t: (b, 0, 0)),
        _whole((bsz, D_LRU)),
        _whole((LRU_CONV - 1, bsz, D_LRU)),
        _whole((CONV_W - 1, bsz, D_CONV)),
    ] + [pl.BlockSpec(memory_space=pl.ANY)] * 4
    out_shape = [
        jax.ShapeDtypeStruct((bsz, seq, D_MODEL), jnp.float32),
        jax.ShapeDtypeStruct((bsz, D_XATTN, N_MEM), jnp.float32),
        jax.ShapeDtypeStruct((bsz, D_XATTN, N_MEM), jnp.float32),
        jax.ShapeDtypeStruct((bsz, D_LRU), jnp.float32),
        jax.ShapeDtypeStruct((LRU_CONV - 1, bsz, D_LRU), jnp.float32),
        jax.ShapeDtypeStruct((CONV_W - 1, bsz, D_CONV), jnp.float32),
        jax.ShapeDtypeStruct((D_MODEL, D_IN), jnp.bfloat16),
        jax.ShapeDtypeStruct((D_MIX, D_MODEL), jnp.bfloat16),
        jax.ShapeDtypeStruct((D_CONV, D_CONV), jnp.bfloat16),
        jax.ShapeDtypeStruct((2, GATE_HALF, D_LRU), jnp.bfloat16),
    ]
    scratch = [
        pltpu.VMEM((tile, D_IN), jnp.float32),
        pltpu.VMEM((N_LRU_SLABS, LRU_HALO + tile, LANES), jnp.float32),
        pltpu.VMEM((tile, D_LRU), jnp.float32),
        pltpu.VMEM((tile, 2 * D_LRU), jnp.float32),
        pltpu.VMEM((N_LRU_SLABS, n_seg * PROMPT_PITCH, LANES), jnp.float32),
        pltpu.VMEM((N_LRU_SLABS, n_seg * PROMPT_PITCH, LANES), jnp.float32),
        pltpu.VMEM((N_CONV_SLABS, CONV_HALO + tile, LANES), jnp.float32),
        pltpu.VMEM((tile, D_CONV), jnp.bfloat16),
        pltpu.VMEM((tile, D_MIX), jnp.bfloat16),
        pltpu.VMEM((N_LRU_SLABS, 2 * SUBLANES, LANES), jnp.float32),
        pltpu.VMEM((D_XATTN, N_XHEADS * N_MEM), jnp.bfloat16),
        pltpu.VMEM((D_XATTN, N_XHEADS * N_MEM), jnp.bfloat16),
        pltpu.VMEM((tile, N_XHEADS * N_MEM), jnp.float32),
        pltpu.VMEM((tile, N_XHEADS * N_MEM), jnp.bfloat16),
        pltpu.VMEM((D_MODEL, D_IN), jnp.bfloat16),
        pltpu.VMEM((D_MIX, D_MODEL), jnp.bfloat16),
        pltpu.VMEM((D_CONV, D_CONV), jnp.bfloat16),
        pltpu.VMEM((2, GATE_HALF, D_LRU), jnp.bfloat16),
        pltpu.SemaphoreType.DMA((4,)),
    ]
    return pl.pallas_call(
        _prompt_kernel,
        grid=(bsz, n_t),
        in_specs=in_specs,
        out_specs=out_specs,
        out_shape=out_shape,
        scratch_shapes=scratch,
        compiler_params=pltpu.CompilerParams(
            dimension_semantics=("arbitrary", "arbitrary"),
            vmem_limit_bytes=VMEM_LIMIT_BYTES),
        name="prompt_layer",
    )(x, mem, *weights)


def _sample_kernel(nb, steps,
                   x_ref, h0_ref, lb_ref, cb_ref, kt_hbm, vt_hbm,
                   win_bf, w_lc_ref, b_lc_ref, wg_bf,
                   b_ga_ref, b_gx_ref, lam_ref, w_dw_ref, b_dw_ref, ln_g_ref, ln_b_ref,
                   wpw_bf, b_pw_ref, wout_bf, g_final_ref,
                   y_ref, h_out_ref, lb_out_ref, cb_out_ref,
                   z_s, xc_s, xl_s, pre_s, a_s, u_s, uc_s, ln_s, mix_s, sc_s, p_s,
                   kt_ref, vt_ref, kv_sem):
    rows = nb * steps
    pitch = SAMPLE_PITCH
    cpitch = CONV_HALO + steps
    prow = nb * pitch

    group = pl.program_id(0)
    first = group * nb
    n_groups = kt_ref.shape[0] // nb

    def kv_copies(grp):
        rows_of = pl.ds(grp * nb, nb)
        return (pltpu.make_async_copy(kt_hbm.at[rows_of], kt_ref.at[rows_of], kv_sem.at[2 * grp]),
                pltpu.make_async_copy(vt_hbm.at[rows_of], vt_ref.at[rows_of], kv_sem.at[2 * grp + 1]))

    @pl.when(group == 0)
    def _():
        for grp in range(n_groups):
            for copy in kv_copies(grp):
                copy.start()
        xl_s[prow - SAMPLE_HALO:prow, :] = jnp.zeros((SAMPLE_HALO, D_LRU), jnp.float32)

    x = x_ref[...].reshape(rows, D_MODEL)
    z_s[...] = _dot(_bf16(_rms_scaled(x)), win_bf[...])

    for b in range(nb):
        for l in range(N_LRU_SLABS):
            lo = l * LANES
            zcol = OFF_LRU_IN + lo
            xc_s[l, b * pitch:b * pitch + SAMPLE_HALO, :] = jnp.zeros((SAMPLE_HALO, LANES), jnp.float32)
            for k in range(LRU_CONV - 1):
                r = b * pitch + SAMPLE_HALO - (LRU_CONV - 1) + k
                xc_s[l, r:r + 1, :] = lb_ref[k, b:b + 1, lo:lo + LANES]
                rn = (b + 1) * steps - (LRU_CONV - 1) + k
                lb_out_ref[k, b:b + 1, lo:lo + LANES] = z_s[rn:rn + 1, zcol:zcol + LANES]
            xc_s[l, b * pitch + SAMPLE_HALO:(b + 1) * pitch, :] = (
                z_s[b * steps:(b + 1) * steps, zcol:zcol + LANES])
    n_conv = prow - SAMPLE_HALO
    for l in range(N_LRU_SLABS):
        lo = l * LANES
        w_row = lambda k, lo=lo: 0.5 * w_lc_ref[k:k + 1, lo:lo + LANES]
        xl = (_conv_taps(xc_s, l, w_row, 0, n_conv, LRU_CONV, SAMPLE_HALO)
              + 0.5 * b_lc_ref[:, lo:lo + LANES])
        xl_s[0:n_conv, lo:lo + LANES] = xl
    _lru_gates(_bf16(xl_s[...]), wg_bf, pre_s)

    hc_ls = (0.5 * LRU_C) * _log_sigmoid(lam_ref[...])
    hb_ga = 0.5 * b_ga_ref[...]
    hb_gx = 0.5 * b_gx_ref[...]
    for l in range(N_LRU_SLABS):
        lo = l * LANES
        ca, cx = _pre_cols(l)
        a, u = _lru_coeffs(pre_s[:, ca:ca + LANES], pre_s[:, cx:cx + LANES], xl_s[:, lo:lo + LANES],
                           hb_ga[:, lo:lo + LANES], hb_gx[:, lo:lo + LANES],
                           hc_ls[:, lo:lo + LANES])
        a_s[l, :, :] = a
        u_s[l, :, :] = u

    n_grp = nb // SUBLANES

    def scan_body(j, carry):
        new = []
        for g in range(n_grp):
            for l in range(N_LRU_SLABS):
                base = g * SUBLANES * pitch
                a_j = a_s[l, pl.ds(base + j, SUBLANES, stride=pitch), :]
                u_j = u_s[l, pl.ds(base + j, SUBLANES, stride=pitch), :]
                h = a_j * carry[g * N_LRU_SLABS + l] + u_j
                u_s[l, pl.ds(base + j, SUBLANES, stride=pitch), :] = h
                new.append(h)
        return tuple(new)

    h0 = tuple(h0_ref[g * SUBLANES:(g + 1) * SUBLANES, l * LANES:(l + 1) * LANES]
               for g in range(n_grp) for l in range(N_LRU_SLABS))
    h_fin = _unrolled(steps, scan_body, h0)
    for g in range(n_grp):
        for l in range(N_LRU_SLABS):
            h_out_ref[g * SUBLANES:(g + 1) * SUBLANES, l * LANES:(l + 1) * LANES] = (
                h_fin[g * N_LRU_SLABS + l])

    for b in range(nb):
        for l in range(N_CONV_SLABS):
            lo = l * LANES
            uc = _gated_of_twice(
                z_s[b * steps:(b + 1) * steps, OFF_GLU_A + lo:OFF_GLU_A + lo + LANES],
                z_s[b * steps:(b + 1) * steps, OFF_GLU_B + lo:OFF_GLU_B + lo + LANES])
            uc_s[l, b * cpitch + CONV_HALO:(b + 1) * cpitch, :] = uc
            for k in range(CONV_W - 1):
                r = b * cpitch + CONV_HALO - (CONV_W - 1) + k
                uc_s[l, r:r + 1, :] = cb_ref[k, b:b + 1, lo:lo + LANES]
                rn = (b + 1) * cpitch - (CONV_W - 1) + k
                cb_out_ref[k, b:b + 1, lo:lo + LANES] = uc_s[l, rn:rn + 1, :]

    for b in range(nb):
        r0 = b * cpitch
        o0 = b * steps
        ys = []
        for l in range(N_CONV_SLABS):
            lo = l * LANES
            w_row = lambda k, lo=lo: w_dw_ref[k, :, lo:lo + LANES]
            ys.append(_conv_taps(uc_s, l, w_row, r0, steps, CONV_W, CONV_HALO)
                      + b_dw_ref[:, lo:lo + LANES])
        outs = _layernorm_silu_slabs(ys, ln_g_ref, ln_b_ref)
        for l in range(N_CONV_SLABS):
            lo = l * LANES
            ln_s[o0:o0 + steps, lo:lo + LANES] = outs[l]

    mix_s[:, D_LRU:D_LRU + D_CONV] = _times_silu_of_twice_bf16(
        _dot(ln_s[...], wpw_bf[...]) + b_pw_ref[...], z_s[:, OFF_CONV_GATE:OFF_CONV_GATE + D_CONV])

    masks = _lane_head_masks()
    scale = SCORE_SCALE
    qrows = N_XHEADS * steps
    for copy in kv_copies(group):
        copy.wait()
    for b in range(nb):
        for l in range(N_LRU_SLABS):
            lo = l * LANES
            gate = z_s[b * steps:(b + 1) * steps, OFF_LRU_GATE + lo:OFF_LRU_GATE + lo + LANES]
            mix_s[b * steps:(b + 1) * steps, lo:lo + LANES] = _times_silu_of_twice_bf16(
                u_s[l, b * pitch:b * pitch + steps, :], gate)
    for b in range(nb):
        q = z_s[b * steps:(b + 1) * steps, OFF_Q:OFF_Q + D_XATTN]
        qm = jnp.concatenate([_bf16(q * (m * scale)) for m in masks], axis=0)
        sc_s[b * qrows:(b + 1) * qrows, :] = _dot(qm, _bf16(kt_ref[first + b]))
    s = sc_s[...]
    e = jnp.exp2(s - jnp.max(s, axis=-1, keepdims=True))
    p_s[...] = _bf16(e * (1.0 / jnp.sum(e, axis=-1, keepdims=True)))
    for b in range(nb):
        o_all = _dot_nt(p_s[b * qrows:(b + 1) * qrows, :], _bf16(vt_ref[first + b]))
        o = o_all[0:steps] * masks[0]
        for h in range(1, N_XHEADS):
            o = o + o_all[h * steps:(h + 1) * steps] * masks[h]
        gate = z_s[b * steps:(b + 1) * steps, OFF_ATTN_GATE:OFF_ATTN_GATE + D_XATTN]
        mix_s[b * steps:(b + 1) * steps, D_LRU + D_CONV:D_MIX] = _times_silu_of_twice_bf16(o, gate)

    y = x + _dot(mix_s[...], wout_bf[...])
    y_ref[...] = _rmsnorm(y, g_final_ref[...]).reshape(nb, steps, D_MODEL)


def _sample(x, h0, lb, cb, kt, vt, w, win_bf, wout_bf, wpw_bf, wg_bf):
    bsz, steps, _ = x.shape
    nb = SAMPLE_GROUP
    assert bsz % nb == 0 and nb % SUBLANES == 0
    assert steps + SAMPLE_HALO == SAMPLE_PITCH and steps >= CONV_W - 1
    rows = nb * steps
    prow = nb * SAMPLE_PITCH
    weights = [win_bf, w["w_lc"], w["b_lc"], wg_bf, w["b_ga"], w["b_gx"],
               w["lam"], w["w_dw"], w["b_dw"], w["ln_g"], w["ln_b"], wpw_bf, w["b_pw"],
               wout_bf, w["g_final"]]
    weight_specs = [
        _whole((D_MODEL, D_IN)), _whole((None, LRU_CONV, D_LRU)),
        _whole((1, D_LRU)), _whole((2, GATE_HALF, D_LRU)), _whole((1, D_LRU)), _whole((1, D_LRU)),
        _whole((1, D_LRU)), _whole((CONV_W, 1, D_CONV)), _whole((1, D_CONV)), _whole((1, D_CONV)),
        _whole((1, D_CONV)), _whole((D_CONV, D_CONV)), _whole((1, D_CONV)),
        _whole((D_MIX, D_MODEL)), _whole((1, D_MODEL)),
    ]
    in_specs = [
        pl.BlockSpec((nb, steps, D_MODEL), lambda g: (g, 0, 0)),
        pl.BlockSpec((nb, D_LRU), lambda g: (g, 0)),
        pl.BlockSpec((LRU_CONV - 1, nb, D_LRU), lambda g: (0, g, 0)),
        pl.BlockSpec((CONV_W - 1, nb, D_CONV), lambda g: (0, g, 0)),
        pl.BlockSpec(memory_space=pl.ANY),
        pl.BlockSpec(memory_space=pl.ANY),
    ] + weight_specs
    out_specs = [
        pl.BlockSpec((nb, steps, D_MODEL), lambda g: (g, 0, 0)),
        pl.BlockSpec((nb, D_LRU), lambda g: (g, 0)),
        pl.BlockSpec((LRU_CONV - 1, nb, D_LRU), lambda g: (0, g, 0)),
        pl.BlockSpec((CONV_W - 1, nb, D_CONV), lambda g: (0, g, 0)),
    ]
    out_shape = [
        jax.ShapeDtypeStruct((bsz, steps, D_MODEL), jnp.float32),
        jax.ShapeDtypeStruct((bsz, D_LRU), jnp.float32),
        jax.ShapeDtypeStruct((LRU_CONV - 1, bsz, D_LRU), jnp.float32),
        jax.ShapeDtypeStruct((CONV_W - 1, bsz, D_CONV), jnp.float32),
    ]
    scratch = [
        pltpu.VMEM((rows, D_IN), jnp.float32),
        pltpu.VMEM((N_LRU_SLABS, prow, LANES), jnp.float32),
        pltpu.VMEM((prow, D_LRU), jnp.float32),
        pltpu.VMEM((prow, 2 * D_LRU), jnp.float32),
        pltpu.VMEM((N_LRU_SLABS, prow, LANES), jnp.float32),
        pltpu.VMEM((N_LRU_SLABS, prow, LANES), jnp.float32),
        pltpu.VMEM((N_CONV_SLABS, nb * (CONV_HALO + steps), LANES), jnp.float32),
        pltpu.VMEM((rows, D_CONV), jnp.bfloat16),
        pltpu.VMEM((rows, D_MIX), jnp.bfloat16),
        pltpu.VMEM((N_XHEADS * rows, N_MEM), jnp.float32),
        pltpu.VMEM((N_XHEADS * rows, N_MEM), jnp.bfloat16),
        pltpu.VMEM((bsz, D_XATTN, N_MEM), jnp.float32),
        pltpu.VMEM((bsz, D_XATTN, N_MEM), jnp.float32),
        pltpu.SemaphoreType.DMA((2 * (bsz // nb),)),
    ]
    return pl.pallas_call(
        functools.partial(_sample_kernel, nb, steps),
        grid=(bsz // nb,),
        in_specs=in_specs,
        out_specs=out_specs,
        out_shape=out_shape,
        scratch_shapes=scratch,
        compiler_params=pltpu.CompilerParams(
            dimension_semantics=("arbitrary",),
            vmem_limit_bytes=VMEM_LIMIT_BYTES),
        name="sample_layer",
    )(x, h0, lb, cb, kt, vt, *weights)


def _time_major(state):
    return jnp.transpose(state, (1, 0, 2))


def _memory_t(mem):
    bsz = mem.shape[0]
    return jnp.transpose(mem, (0, 2, 3, 1)).reshape(bsz, D_XATTN, N_MEM)


def _memory_from_t(mem_t):
    bsz = mem_t.shape[0]
    return jnp.transpose(mem_t.reshape(bsz, N_XHEADS, XHEAD_DIM, N_MEM), (0, 3, 1, 2))


def kernel(x_prompt, x_sample, state_lru_h, state_lru_conv, state_conv, cache_mem_k, cache_mem_v,
           mem_prompt, g_norm, w_in, w_lru_conv, b_lru_conv, w_gate_a, b_gate_a, w_gate_x, b_gate_x,
           lru_lambda, w_dw, b_dw, ln_g, ln_b, w_pw, b_pw, g_mem, w_mem_k, w_mem_v, w_out, g_final):
    depth = g_norm.shape[0]
    assert depth == 1
    w = {
        "g_norm": g_norm, "w_in": w_in, "w_lc": w_lru_conv, "b_lc": b_lru_conv,
        "w_ga": w_gate_a, "w_gx": w_gate_x, "b_ga": b_gate_a, "b_gx": b_gate_x, "lam": lru_lambda,
        "w_dw": jnp.transpose(w_dw, (1, 0, 2)), "b_dw": b_dw, "ln_g": ln_g, "ln_b": ln_b,
        "w_pw": w_pw, "b_pw": b_pw, "w_out": w_out, "g_final": g_final.reshape(1, D_MODEL),
        "g_mem": g_mem, "w_mem_k": w_mem_k, "w_mem_v": w_mem_v,
    }
    y_prompt, mkt, mvt, ph, plb, pcb, win_bf, wout_bf, wpw_bf, wg_bf = _prompt(x_prompt, mem_prompt, w)

    y_sample, sh, slb, scb = _sample(
        x_sample, state_lru_h[0], _time_major(state_lru_conv[0]), _time_major(state_conv[0]),
        _memory_t(cache_mem_k[0]), _memory_t(cache_mem_v[0]), w, win_bf, wout_bf, wpw_bf, wg_bf)

    return (y_prompt, y_sample,
            ph[None], _time_major(plb)[None], _time_major(pcb)[None],
            _memory_from_t(mkt)[None], _memory_from_t(mvt)[None],
            sh[None], _time_major(slb)[None], _time_major(scb)[None])
```

```python
import functools

import jax
import jax.numpy as jnp
from jax import lax
from jax.experimental import pallas as pl
from jax.experimental.pallas import tpu as pltpu

D_MODEL = 1024
N_MEM = 256
D_LRU = 512
LRU_BLOCKS = 8
LRU_BLOCK = D_LRU // LRU_BLOCKS
LRU_CONV = 4
LRU_C = 8.0
D_CONV = 256
CONV_W = 31
N_XHEADS = 4
XHEAD_DIM = 64
D_XATTN = N_XHEADS * XHEAD_DIM
D_MIX = D_LRU + D_CONV + D_XATTN
D_IN = 2 * D_LRU + 3 * D_CONV + 2 * D_XATTN
EPS = 1e-6

LANES = 128
SUBLANES = 8
N_LRU_SLABS = D_LRU // LANES
N_CONV_SLABS = D_CONV // LANES
GATE_HALF = D_LRU // 2

OFF_LRU_IN = 0
OFF_LRU_GATE = D_LRU
OFF_GLU_A = 2 * D_LRU
OFF_GLU_B = OFF_GLU_A + D_CONV
OFF_CONV_GATE = OFF_GLU_B + D_CONV
OFF_Q = OFF_CONV_GATE + D_CONV
OFF_ATTN_GATE = OFF_Q + D_XATTN

LRU_HALO = 8
CONV_HALO = 32

PROMPT_TILE = 512
PROMPT_SEG = 64
PROMPT_PITCH = 68
SAMPLE_HALO = 4
SAMPLE_PITCH = 36
SAMPLE_GROUP = 16
ATTN_ROWS = 128
CONV_ROWS = 32
PROJ_ROWS = 256
NORM_ROWS = 64

VMEM_LIMIT_BYTES = 56 * 1024 * 1024
HALF_LOG2_E = 0.7213475204444817
SCORE_SCALE = XHEAD_DIM ** -0.5 * 2.0 * HALF_LOG2_E


def _silu_of_twice(hx):
    return hx + hx * jnp.tanh(hx)


def _times_silu_of_twice_bf16(v, hx):
    return _bf16(v) * _silu_of_twice(_bf16(hx))


def _gated_of_twice(ha, hb):
    return ha + ha * jnp.tanh(hb)


def _w_in_col_scale():
    col = lax.broadcasted_iota(jnp.int32, (1, D_IN), 1)
    plain = (col < OFF_LRU_GATE) | ((col >= OFF_Q) & (col < OFF_ATTN_GATE))
    return jnp.where(plain, 1.0, 0.5)


def _sqrt_nonneg(x):
    return jnp.exp2(jnp.log(x) * HALF_LOG2_E)


def _rms_scaled(x):
    ms = jnp.mean(x * x, axis=-1, keepdims=True)
    return x * lax.rsqrt(ms + EPS)


def _rmsnorm(x, g):
    return _rms_scaled(x) * g


def _as_column(row):
    n = row.shape[1] // LANES
    on_diag = (lax.broadcasted_iota(jnp.int32, (LANES, LANES), 0)
               == lax.broadcasted_iota(jnp.int32, (LANES, LANES), 1))
    cols = [jnp.sum(jnp.where(on_diag, row[:, j * LANES:(j + 1) * LANES], 0.0), axis=1, keepdims=True)
            for j in range(n)]
    return jnp.concatenate(cols, axis=0)


def _log_sigmoid(x):
    y = -x
    return -(jnp.maximum(y, 0.0) + jnp.log1p(jnp.exp(-jnp.abs(y))))


def _bf16(x):
    return x.astype(jnp.bfloat16)


def _dot(a, b):
    return jnp.dot(a, b, preferred_element_type=jnp.float32)


def _dot_nt(a, b):
    return lax.dot_general(a, b, (((1,), (1,)), ((), ())), preferred_element_type=jnp.float32)


def _unrolled(n, body, init):
    carry = init
    for i in range(n):
        carry = body(i, carry)
    return carry


def _aligned(x, m):
    return x if isinstance(x, int) else pl.multiple_of(x, m)


def _head_of(index):
    return lax.shift_right_logical(index, XHEAD_DIM.bit_length() - 1)


def _lane_head_masks():
    head = _head_of(lax.broadcasted_iota(jnp.int32, (1, D_XATTN), 1))
    return [(head == h).astype(jnp.float32) for h in range(N_XHEADS)]


def _row_head_masks():
    head = _head_of(lax.broadcasted_iota(jnp.int32, (D_XATTN, 1), 0))
    return [(head == h).astype(jnp.float32) for h in range(N_XHEADS)]


def _head_stack_t(kt, vt, kst_ref, vst_ref):
    scale = SCORE_SCALE
    for h, m in enumerate(_row_head_masks()):
        kst_ref[:, h * N_MEM:(h + 1) * N_MEM] = _bf16(kt * (m * scale))
        vst_ref[:, h * N_MEM:(h + 1) * N_MEM] = _bf16(vt * m)


def _zero_row_after(x):
    bits = lax.bitcast_convert_type(x, jnp.uint32)
    zero = lax.shift_right_logical(lax.shift_right_logical(bits, jnp.uint32(16)), jnp.uint32(16))
    return lax.bitcast_convert_type(zero, jnp.float32)[0:1, :]


def _softmax_heads(s):
    ps = []
    for h in range(N_XHEADS):
        sh = s[:, h * N_MEM:(h + 1) * N_MEM]
        e = jnp.exp2(sh - jnp.max(sh, axis=-1, keepdims=True))
        inv = 1.0 / jnp.sum(e, axis=-1, keepdims=True)
        ps.append(_bf16(e * inv))
    return jnp.concatenate(ps, axis=1)


def _lru_gates(xl_bf, wg_ref, pre_s):
    pre_s[:, 0:D_LRU] = _dot(xl_bf[:, 0:GATE_HALF], wg_ref[0])
    pre_s[:, D_LRU:2 * D_LRU] = _dot(xl_bf[:, GATE_HALF:D_LRU], wg_ref[1])


def _pre_cols(l):
    half_slabs = N_LRU_SLABS // 2
    base = (l // half_slabs) * D_LRU + (l % half_slabs) * LANES
    return base, base + GATE_HALF


def _lru_coeffs(hpre_a, hpre_x, hx, hb_a, hb_x, hc_ls):
    t_r = jnp.tanh(hpre_a + hb_a)
    t_i = jnp.tanh(hpre_x + hb_x)
    log_a = hc_ls + hc_ls * t_r
    a = jnp.exp(log_a)
    one_minus_a2 = -jnp.tanh(log_a) * (a * a + 1.0)
    u = _sqrt_nonneg(one_minus_a2) * (hx + hx * t_i)
    return a, u


def _conv_taps(src_s, l, w_row, row0, nrows, width, halo):
    acc = None
    for k in range(width):
        x_k = src_s[l, pl.ds(row0 + (halo - (width - 1) + k), nrows), :]
        acc = w_row(k) * x_k if acc is None else acc + w_row(k) * x_k
    return acc


def _layernorm_silu_slabs(ys, g_ref, b_ref):
    tot = ys[0]
    for y in ys[1:]:
        tot = tot + y
    mu = jnp.sum(tot, axis=-1, keepdims=True) * (1.0 / D_CONV)
    cs = [y - mu for y in ys]
    sq = cs[0] * cs[0]
    for c in cs[1:]:
        sq = sq + c * c
    var = jnp.sum(sq, axis=-1, keepdims=True) * (1.0 / D_CONV)
    inv = lax.rsqrt(var + EPS)
    outs = []
    for l, c in enumerate(cs):
        lo = l * LANES
        half_g = 0.5 * g_ref[:, lo:lo + LANES]
        half_b = 0.5 * b_ref[:, lo:lo + LANES]
        outs.append(_silu_of_twice(_bf16(c * inv * half_g + half_b)))
    return outs


def _gate_block_diag(w_blocks):
    n = (LRU_BLOCKS // 2) * LRU_BLOCK
    stacked = _bf16(w_blocks.reshape(n, LRU_BLOCK))
    src = lax.broadcasted_iota(jnp.int32, (LRU_BLOCK, n), 0)
    dst = lax.broadcasted_iota(jnp.int32, (LRU_BLOCK, n), 1)
    spread = _bf16(((dst & (LRU_BLOCK - 1)) == src).astype(jnp.float32))
    rows = _head_of(lax.broadcasted_iota(jnp.int32, (n, n), 0))
    cols = _head_of(lax.broadcasted_iota(jnp.int32, (n, n), 1))
    return _bf16(_dot(stacked, spread) * (rows == cols).astype(jnp.float32))


def _prompt_kernel(x_ref, mem_ref, g_mem_ref, wk_ref, wv_ref,
                   g_norm_ref, w_in_ref, w_lc_ref, b_lc_ref, wga_ref, wgx_ref,
                   b_ga_ref, b_gx_ref, lam_ref, w_dw_ref, b_dw_ref, ln_g_ref, ln_b_ref,
                   w_pw_ref, b_pw_ref, w_out_ref, g_final_ref,
                   y_ref, kt_ref, vt_ref, h_out_ref, lb_out_ref, cb_out_ref,
                   win_hbm, wout_hbm, wpw_hbm, wg_hbm,
                   z_s, xc_s, xl_s, pre_s, a_s, u_s, uc_s, ln_s, mix_s, hin_s, kst_s, vst_s, sc_s, p_s,
                   win_bf, wout_bf, wpw_bf, wg_bf, w_sem):
    b = pl.program_id(0)
    t = pl.program_id(1)
    n_t = pl.num_programs(1)
    tile = PROMPT_TILE
    seg = PROMPT_SEG
    n_seg = tile // seg
    pitch = PROMPT_PITCH

    weight_copies = [
        pltpu.make_async_copy(src, dst, w_sem.at[i])
        for i, (src, dst) in enumerate([(win_bf, win_hbm), (wout_bf, wout_hbm),
                                        (wpw_bf, wpw_hbm), (wg_bf, wg_hbm)])]

    @pl.when(jnp.logical_and(b == 0, t == 0))
    def _():
        win_bf[...] = _bf16(w_in_ref[...] * _as_column(g_norm_ref[...]) * _w_in_col_scale())
        wout_bf[...] = _bf16(w_out_ref[...])
        wpw_bf[...] = _bf16(w_pw_ref[...])
        half = LRU_BLOCKS // 2
        for h in range(2):
            wg_bf[h, :, 0:GATE_HALF] = _gate_block_diag(wga_ref[h * half:(h + 1) * half])
            wg_bf[h, :, GATE_HALF:D_LRU] = _gate_block_diag(wgx_ref[h * half:(h + 1) * half])
        for copy in weight_copies:
            copy.start()

    @pl.when(t == 0)
    def _():
        xc_s[:, 0:LRU_HALO, :] = jnp.zeros((N_LRU_SLABS, LRU_HALO, LANES), jnp.float32)
        uc_s[:, 0:CONV_HALO, :] = jnp.zeros((N_CONV_SLABS, CONV_HALO, LANES), jnp.float32)
        hin_s[:, 0:1, :] = jnp.zeros((N_LRU_SLABS, 1, LANES), jnp.float32)
        m = _bf16(_rmsnorm(mem_ref[0], g_mem_ref[...]))
        kt = _dot(m, _bf16(wk_ref[...])).T
        vt = _dot(m, _bf16(wv_ref[...])).T
        kt_ref[0] = kt
        vt_ref[0] = vt
        _head_stack_t(kt, vt, kst_s, vst_s)

    for r0 in range(0, tile, PROJ_ROWS):
        z_s[r0:r0 + PROJ_ROWS, :] = _dot(_bf16(_rms_scaled(x_ref[0, r0:r0 + PROJ_ROWS, :])), win_bf[...])

    for l in range(N_LRU_SLABS):
        lo = l * LANES
        xc_s[l, LRU_HALO:LRU_HALO + tile, :] = z_s[:, OFF_LRU_IN + lo:OFF_LRU_IN + lo + LANES]
    for l in range(N_LRU_SLABS):
        lo = l * LANES
        w_row = lambda k, lo=lo: 0.5 * w_lc_ref[k:k + 1, lo:lo + LANES]
        xl = (_conv_taps(xc_s, l, w_row, 0, tile, LRU_CONV, LRU_HALO)
              + 0.5 * b_lc_ref[:, lo:lo + LANES])
        xl_s[:, lo:lo + LANES] = xl
    _lru_gates(_bf16(xl_s[...]), wg_bf, pre_s)

    hc_ls = (0.5 * LRU_C) * _log_sigmoid(lam_ref[...])
    hb_ga = 0.5 * b_ga_ref[...]
    hb_gx = 0.5 * b_gx_ref[...]

    def coeff_body(c, carry):
        r0 = _aligned(c * seg, seg)
        p0 = _aligned(c * pitch, SUBLANES)
        for l in range(N_LRU_SLABS):
            lo = l * LANES
            ca, cx = _pre_cols(l)
            a, u = _lru_coeffs(pre_s[pl.ds(r0, seg), ca:ca + LANES],
                               pre_s[pl.ds(r0, seg), cx:cx + LANES],
                               xl_s[pl.ds(r0, seg), lo:lo + LANES],
                               hb_ga[:, lo:lo + LANES], hb_gx[:, lo:lo + LANES],
                               hc_ls[:, lo:lo + LANES])
            a_s[l, pl.ds(p0, seg), :] = a
            u_s[l, pl.ds(p0, seg), :] = u
        return carry

    _unrolled(n_seg, coeff_body, 0)

    def tot_body(j, carry):
        new = []
        for l in range(N_LRU_SLABS):
            a_tot, u_tot = carry[l]
            a_j = a_s[l, pl.ds(j, n_seg, stride=pitch), :]
            u_j = u_s[l, pl.ds(j, n_seg, stride=pitch), :]
            new.append((a_j * a_tot, a_j * u_tot + u_j))
        return tuple(new)

    init = tuple((jnp.ones((n_seg, LANES), jnp.float32), jnp.zeros((n_seg, LANES), jnp.float32))
                 for _ in range(N_LRU_SLABS))
    totals = _unrolled(seg, tot_body, init)

    h_in = []
    for l in range(N_LRU_SLABS):
        a_tot, u_tot = totals[l]
        h = hin_s[l, 0:1, :]
        for c in range(n_seg):
            h = a_tot[c:c + 1, :] * h + u_tot[c:c + 1, :]
            hin_s[l, c + 1:c + 2, :] = h
        h_in.append(hin_s[l, 0:n_seg, :])
        hin_s[l, 0:1, :] = h

    def scan_body(j, carry):
        new = []
        for l in range(N_LRU_SLABS):
            a_j = a_s[l, pl.ds(j, n_seg, stride=pitch), :]
            u_j = u_s[l, pl.ds(j, n_seg, stride=pitch), :]
            h = a_j * carry[l] + u_j
            u_s[l, pl.ds(j, n_seg, stride=pitch), :] = h
            new.append(h)
        return tuple(new)

    scores = _dot(_bf16(z_s[:, OFF_Q:OFF_Q + D_XATTN]), kst_s[...])
    sc_s[...] = scores
    after_scores = _zero_row_after(scores[tile - SUBLANES:tile, N_XHEADS * N_MEM - LANES:])
    _unrolled(seg, scan_body, tuple(h + after_scores for h in h_in))

    for l in range(N_CONV_SLABS):
        lo = l * LANES
        uc_s[l, CONV_HALO:CONV_HALO + tile, :] = _gated_of_twice(
            z_s[:, OFF_GLU_A + lo:OFF_GLU_A + lo + LANES],
            z_s[:, OFF_GLU_B + lo:OFF_GLU_B + lo + LANES])

    def seg_body(c, carry):
        r0 = _aligned(c * seg, seg)
        p0 = _aligned(c * pitch, SUBLANES)
        for l in range(N_LRU_SLABS):
            lo = l * LANES
            gate = z_s[pl.ds(r0, seg), OFF_LRU_GATE + lo:OFF_LRU_GATE + lo + LANES]
            mix_s[pl.ds(r0, seg), lo:lo + LANES] = _times_silu_of_twice_bf16(u_s[l, pl.ds(p0, seg), :], gate)
        for piece in range(seg // CONV_ROWS):
            rr = r0 + piece * CONV_ROWS
            ys = []
            for l in range(N_CONV_SLABS):
                lo = l * LANES
                w_row = lambda k, lo=lo: w_dw_ref[k, :, lo:lo + LANES]
                ys.append(_conv_taps(uc_s, l, w_row, rr, CONV_ROWS, CONV_W, CONV_HALO)
                          + b_dw_ref[:, lo:lo + LANES])
            outs = _layernorm_silu_slabs(ys, ln_g_ref, ln_b_ref)
            for l in range(N_CONV_SLABS):
                lo = l * LANES
                ln_s[pl.ds(rr, CONV_ROWS), lo:lo + LANES] = outs[l]
        return carry

    _unrolled(n_seg, seg_body, 0)

    mix_s[:, D_LRU:D_LRU + D_CONV] = _times_silu_of_twice_bf16(
        _dot(ln_s[...], wpw_bf[...]) + b_pw_ref[...], z_s[:, OFF_CONV_GATE:OFF_CONV_GATE + D_CONV])

    def attn_body(c, carry):
        r0 = _aligned(c * ATTN_ROWS, ATTN_ROWS)
        p_s[pl.ds(r0, ATTN_ROWS), :] = _softmax_heads(sc_s[pl.ds(r0, ATTN_ROWS), :])
        return carry

    _unrolled(tile // ATTN_ROWS, attn_body, 0)
    o = _dot_nt(p_s[...], vst_s[...])
    mix_s[:, D_LRU + D_CONV:D_MIX] = _times_silu_of_twice_bf16(
        o, z_s[:, OFF_ATTN_GATE:OFF_ATTN_GATE + D_XATTN])

    y_ref[0] = x_ref[0] + _dot(mix_s[...], wout_bf[...])
    for r0 in range(0, tile, NORM_ROWS):
        y_ref[0, r0:r0 + NORM_ROWS, :] = _rmsnorm(y_ref[0, r0:r0 + NORM_ROWS, :], g_final_ref[...])

    for l in range(N_LRU_SLABS):
        xc_s[l, LRU_HALO - (LRU_CONV - 1):LRU_HALO, :] = (
            xc_s[l, LRU_HALO + tile - (LRU_CONV - 1):LRU_HALO + tile, :])
    for l in range(N_CONV_SLABS):
        uc_s[l, CONV_HALO - (CONV_W - 1):CONV_HALO, :] = (
            uc_s[l, CONV_HALO + tile - (CONV_W - 1):CONV_HALO + tile, :])

    def write_state(bb):
        for l in range(N_LRU_SLABS):
            lo = l * LANES
            h_out_ref[bb:bb + 1, lo:lo + LANES] = hin_s[l, 0:1, :]
            for k in range(LRU_CONV - 1):
                r = LRU_HALO - (LRU_CONV - 1) + k
                lb_out_ref[k, bb:bb + 1, lo:lo + LANES] = xc_s[l, r:r + 1, :]
        for l in range(N_CONV_SLABS):
            lo = l * LANES
            for k in range(CONV_W - 1):
                r = CONV_HALO - (CONV_W - 1) + k
                cb_out_ref[k, bb:bb + 1, lo:lo + LANES] = uc_s[l, r:r + 1, :]

    for bb in range(h_out_ref.shape[0]):
        pl.when(jnp.logical_and(t == n_t - 1, b == bb))(functools.partial(write_state, bb))

    @pl.when(jnp.logical_and(t == n_t - 1, b == h_out_ref.shape[0] - 1))
    def _():
        for copy in weight_copies:
            copy.wait()


def _whole(shape):
    nd = len(shape)
    return pl.BlockSpec(shape, lambda *_: (0,) * nd, pipeline_mode=pl.Buffered(1))


def _prompt(x, mem, w):
    bsz, seq, _ = x.shape
    tile = PROMPT_TILE
    n_t = seq // tile
    n_seg = tile // PROMPT_SEG
    weights = [w["g_mem"], w["w_mem_k"], w["w_mem_v"],
               w["g_norm"], w["w_in"], w["w_lc"], w["b_lc"], w["w_ga"], w["w_gx"], w["b_ga"], w["b_gx"],
               w["lam"], w["w_dw"], w["b_dw"], w["ln_g"], w["ln_b"], w["w_pw"], w["b_pw"],
               w["w_out"], w["g_final"]]
    weight_specs = [
        _whole((1, D_MODEL)), _whole((None, D_MODEL, D_XATTN)), _whole((None, D_MODEL, D_XATTN)),
        _whole((1, D_MODEL)), _whole((None, D_MODEL, D_IN)), _whole((None, LRU_CONV, D_LRU)),
        _whole((1, D_LRU)), _whole((None, LRU_BLOCKS, LRU_BLOCK, LRU_BLOCK)),
        _whole((None, LRU_BLOCKS, LRU_BLOCK, LRU_BLOCK)), _whole((1, D_LRU)), _whole((1, D_LRU)),
        _whole((1, D_LRU)), _whole((CONV_W, 1, D_CONV)), _whole((1, D_CONV)), _whole((1, D_CONV)),
        _whole((1, D_CONV)), _whole((None, D_CONV, D_CONV)), _whole((1, D_CONV)),
        _whole((None, D_MIX, D_MODEL)), _whole((1, D_MODEL)),
    ]
    in_specs = [
        pl.BlockSpec((1, tile, D_MODEL), lambda b, t: (b, t, 0)),
        pl.BlockSpec((1, N_MEM, D_MODEL), lambda b, t: (b, 0, 0)),
    ] + weight_specs
    out_specs = [
        pl.BlockSpec((1, tile, D_MODEL), lambda b, t: (b, t, 0)),
        pl.BlockSpec((1, D_XATTN, N_MEM), lambda b, t: (b, 0, 0)),
        pl.BlockSpec((1, D_XATTN, N_MEM), lambda b, t: (b, 0, 0)),
        _whole((bsz, D_LRU)),
        _whole((LRU_CONV - 1, bsz, D_LRU)),
        _whole((CONV_W - 1, bsz, D_CONV)),
    ] + [pl.BlockSpec(memory_space=pl.ANY)] * 4
    out_shape = [
        jax.ShapeDtypeStruct((bsz, seq, D_MODEL), jnp.float32),
        jax.ShapeDtypeStruct((bsz, D_XATTN, N_MEM), jnp.float32),
        jax.ShapeDtypeStruct((bsz, D_XATTN, N_MEM), jnp.float32),
        jax.ShapeDtypeStruct((bsz, D_LRU), jnp.float32),
        jax.ShapeDtypeStruct((LRU_CONV - 1, bsz, D_LRU), jnp.float32),
        jax.ShapeDtypeStruct((CONV_W - 1, bsz, D_CONV), jnp.float32),
        jax.ShapeDtypeStruct((D_MODEL, D_IN), jnp.bfloat16),
        jax.ShapeDtypeStruct((D_MIX, D_MODEL), jnp.bfloat16),
        jax.ShapeDtypeStruct((D_CONV, D_CONV), jnp.bfloat16),
        jax.ShapeDtypeStruct((2, GATE_HALF, D_LRU), jnp.bfloat16),
    ]
    scratch = [
        pltpu.VMEM((tile, D_IN), jnp.float32),
        pltpu.VMEM((N_LRU_SLABS, LRU_HALO + tile, LANES), jnp.float32),
        pltpu.VMEM((tile, D_LRU), jnp.float32),
        pltpu.VMEM((tile, 2 * D_LRU), jnp.float32),
        pltpu.VMEM((N_LRU_SLABS, n_seg * PROMPT_PITCH, LANES), jnp.float32),
        pltpu.VMEM((N_LRU_SLABS, n_seg * PROMPT_PITCH, LANES), jnp.float32),
        pltpu.VMEM((N_CONV_SLABS, CONV_HALO + tile, LANES), jnp.float32),
        pltpu.VMEM((tile, D_CONV), jnp.bfloat16),
        pltpu.VMEM((tile, D_MIX), jnp.bfloat16),
        pltpu.VMEM((N_LRU_SLABS, 2 * SUBLANES, LANES), jnp.float32),
        pltpu.VMEM((D_XATTN, N_XHEADS * N_MEM), jnp.bfloat16),
        pltpu.VMEM((D_XATTN, N_XHEADS * N_MEM), jnp.bfloat16),
        pltpu.VMEM((tile, N_XHEADS * N_MEM), jnp.float32),
        pltpu.VMEM((tile, N_XHEADS * N_MEM), jnp.bfloat16),
        pltpu.VMEM((D_MODEL, D_IN), jnp.bfloat16),
        pltpu.VMEM((D_MIX, D_MODEL), jnp.bfloat16),
        pltpu.VMEM((D_CONV, D_CONV), jnp.bfloat16),
        pltpu.VMEM((2, GATE_HALF, D_LRU), jnp.bfloat16),
        pltpu.SemaphoreType.DMA((4,)),
    ]
    return pl.pallas_call(
        _prompt_kernel,
        grid=(bsz, n_t),
        in_specs=in_specs,
        out_specs=out_specs,
        out_shape=out_shape,
        scratch_shapes=scratch,
        compiler_params=pltpu.CompilerParams(
            dimension_semantics=("arbitrary", "arbitrary"),
            vmem_limit_bytes=VMEM_LIMIT_BYTES),
        name="prompt_layer",
    )(x, mem, *weights)


def _sample_kernel(nb, steps,
                   x_ref, h0_ref, lb_ref, cb_ref, kt_hbm, vt_hbm,
                   win_bf, w_lc_ref, b_lc_ref, wg_bf,
                   b_ga_ref, b_gx_ref, lam_ref, w_dw_ref, b_dw_ref, ln_g_ref, ln_b_ref,
                   wpw_bf, b_pw_ref, wout_bf, g_final_ref,
                   y_ref, h_out_ref, lb_out_ref, cb_out_ref,
                   z_s, xc_s, xl_s, pre_s, a_s, u_s, uc_s, ln_s, mix_s, sc_s, p_s,
                   kt_ref, vt_ref, kv_sem):
    rows = nb * steps
    pitch = SAMPLE_PITCH
    cpitch = CONV_HALO + steps
    prow = nb * pitch

    group = pl.program_id(0)
    first = group * nb
    n_groups = kt_ref.shape[0] // nb

    def kv_copies(grp):
        rows_of = pl.ds(grp * nb, nb)
        return (pltpu.make_async_copy(kt_hbm.at[rows_of], kt_ref.at[rows_of], kv_sem.at[2 * grp]),
                pltpu.make_async_copy(vt_hbm.at[rows_of], vt_ref.at[rows_of], kv_sem.at[2 * grp + 1]))

    @pl.when(group == 0)
    def _():
        for grp in range(n_groups):
            for copy in kv_copies(grp):
                copy.start()
        xl_s[prow - SAMPLE_HALO:prow, :] = jnp.zeros((SAMPLE_HALO, D_LRU), jnp.float32)

    x = x_ref[...].reshape(rows, D_MODEL)
    z_s[...] = _dot(_bf16(_rms_scaled(x)), win_bf[...])

    for b in range(nb):
        for l in range(N_LRU_SLABS):
            lo = l * LANES
            zcol = OFF_LRU_IN + lo
            xc_s[l, b * pitch:b * pitch + SAMPLE_HALO, :] = jnp.zeros((SAMPLE_HALO, LANES), jnp.float32)
            for k in range(LRU_CONV - 1):
                r = b * pitch + SAMPLE_HALO - (LRU_CONV - 1) + k
                xc_s[l, r:r + 1, :] = lb_ref[k, b:b + 1, lo:lo + LANES]
                rn = (b + 1) * steps - (LRU_CONV - 1) + k
                lb_out_ref[k, b:b + 1, lo:lo + LANES] = z_s[rn:rn + 1, zcol:zcol + LANES]
            xc_s[l, b * pitch + SAMPLE_HALO:(b + 1) * pitch, :] = (
                z_s[b * steps:(b + 1) * steps, zcol:zcol + LANES])
    n_conv = prow - SAMPLE_HALO
    for l in range(N_LRU_SLABS):
        lo = l * LANES
        w_row = lambda k, lo=lo: 0.5 * w_lc_ref[k:k + 1, lo:lo + LANES]
        xl = (_conv_taps(xc_s, l, w_row, 0, n_conv, LRU_CONV, SAMPLE_HALO)
              + 0.5 * b_lc_ref[:, lo:lo + LANES])
        xl_s[0:n_conv, lo:lo + LANES] = xl
    _lru_gates(_bf16(xl_s[...]), wg_bf, pre_s)

    hc_ls = (0.5 * LRU_C) * _log_sigmoid(lam_ref[...])
    hb_ga = 0.5 * b_ga_ref[...]
    hb_gx = 0.5 * b_gx_ref[...]
    for l in range(N_LRU_SLABS):
        lo = l * LANES
        ca, cx = _pre_cols(l)
        a, u = _lru_coeffs(pre_s[:, ca:ca + LANES], pre_s[:, cx:cx + LANES], xl_s[:, lo:lo + LANES],
                           hb_ga[:, lo:lo + LANES], hb_gx[:, lo:lo + LANES],
                           hc_ls[:, lo:lo + LANES])
        a_s[l, :, :] = a
        u_s[l, :, :] = u

    n_grp = nb // SUBLANES

    def scan_body(j, carry):
        new = []
        for g in range(n_grp):
            for l in range(N_LRU_SLABS):
                base = g * SUBLANES * pitch
                a_j = a_s[l, pl.ds(base + j, SUBLANES, stride=pitch), :]
                u_j = u_s[l, pl.ds(base + j, SUBLANES, stride=pitch), :]
                h = a_j * carry[g * N_LRU_SLABS + l] + u_j
                u_s[l, pl.ds(base + j, SUBLANES, stride=pitch), :] = h
                new.append(h)
        return tuple(new)

    h0 = tuple(h0_ref[g * SUBLANES:(g + 1) * SUBLANES, l * LANES:(l + 1) * LANES]
               for g in range(n_grp) for l in range(N_LRU_SLABS))
    h_fin = _unrolled(steps, scan_body, h0)
    for g in range(n_grp):
        for l in range(N_LRU_SLABS):
            h_out_ref[g * SUBLANES:(g + 1) * SUBLANES, l * LANES:(l + 1) * LANES] = (
                h_fin[g * N_LRU_SLABS + l])

    for b in range(nb):
        for l in range(N_CONV_SLABS):
            lo = l * LANES
            uc = _gated_of_twice(
                z_s[b * steps:(b + 1) * steps, OFF_GLU_A + lo:OFF_GLU_A + lo + LANES],
                z_s[b * steps:(b + 1) * steps, OFF_GLU_B + lo:OFF_GLU_B + lo + LANES])
            uc_s[l, b * cpitch + CONV_HALO:(b + 1) * cpitch, :] = uc
            for k in range(CONV_W - 1):
                r = b * cpitch + CONV_HALO - (CONV_W - 1) + k
                uc_s[l, r:r + 1, :] = cb_ref[k, b:b + 1, lo:lo + LANES]
                rn = (b + 1) * cpitch - (CONV_W - 1) + k
                cb_out_ref[k, b:b + 1, lo:lo + LANES] = uc_s[l, rn:rn + 1, :]

    for b in range(nb):
        r0 = b * cpitch
        o0 = b * steps
        ys = []
        for l in range(N_CONV_SLABS):
            lo = l * LANES
            w_row = lambda k, lo=lo: w_dw_ref[k, :, lo:lo + LANES]
            ys.append(_conv_taps(uc_s, l, w_row, r0, steps, CONV_W, CONV_HALO)
                      + b_dw_ref[:, lo:lo + LANES])
        outs = _layernorm_silu_slabs(ys, ln_g_ref, ln_b_ref)
        for l in range(N_CONV_SLABS):
            lo = l * LANES
            ln_s[o0:o0 + steps, lo:lo + LANES] = outs[l]

    mix_s[:, D_LRU:D_LRU + D_CONV] = _times_silu_of_twice_bf16(
        _dot(ln_s[...], wpw_bf[...]) + b_pw_ref[...], z_s[:, OFF_CONV_GATE:OFF_CONV_GATE + D_CONV])

    masks = _lane_head_masks()
    scale = SCORE_SCALE
    qrows = N_XHEADS * steps
    for copy in kv_copies(group):
        copy.wait()
    for b in range(nb):
        for l in range(N_LRU_SLABS):
            lo = l * LANES
            gate = z_s[b * steps:(b + 1) * steps, OFF_LRU_GATE + lo:OFF_LRU_GATE + lo + LANES]
            mix_s[b * steps:(b + 1) * steps, lo:lo + LANES] = _times_silu_of_twice_bf16(
                u_s[l, b * pitch:b * pitch + steps, :], gate)
    for b in range(nb):
        q = z_s[b * steps:(b + 1) * steps, OFF_Q:OFF_Q + D_XATTN]
        qm = jnp.concatenate([_bf16(q * (m * scale)) for m in masks], axis=0)
        sc_s[b * qrows:(b + 1) * qrows, :] = _dot(qm, _bf16(kt_ref[first + b]))
    s = sc_s[...]
    e = jnp.exp2(s - jnp.max(s, axis=-1, keepdims=True))
    p_s[...] = _bf16(e * (1.0 / jnp.sum(e, axis=-1, keepdims=True)))
    for b in range(nb):
        o_all = _dot_nt(p_s[b * qrows:(b + 1) * qrows, :], _bf16(vt_ref[first + b]))
        o = o_all[0:steps] * masks[0]
        for h in range(1, N_XHEADS):
            o = o + o_all[h * steps:(h + 1) * steps] * masks[h]
        gate = z_s[b * steps:(b + 1) * steps, OFF_ATTN_GATE:OFF_ATTN_GATE + D_XATTN]
        mix_s[b * steps:(b + 1) * steps, D_LRU + D_CONV:D_MIX] = _times_silu_of_twice_bf16(o, gate)

    y = x + _dot(mix_s[...], wout_bf[...])
    y_ref[...] = _rmsnorm(y, g_final_ref[...]).reshape(nb, steps, D_MODEL)


def _sample(x, h0, lb, cb, kt, vt, w, win_bf, wout_bf, wpw_bf, wg_bf):
    bsz, steps, _ = x.shape
    nb = SAMPLE_GROUP
    assert bsz % nb == 0 and nb % SUBLANES == 0
    assert steps + SAMPLE_HALO == SAMPLE_PITCH and steps >= CONV_W - 1
    rows = nb * steps
    prow = nb * SAMPLE_PITCH
    weights = [win_bf, w["w_lc"], w["b_lc"], wg_bf, w["b_ga"], w["b_gx"],
               w["lam"], w["w_dw"], w["b_dw"], w["ln_g"], w["ln_b"], wpw_bf, w["b_pw"],
               wout_bf, w["g_final"]]
    weight_specs = [
        _whole((D_MODEL, D_IN)), _whole((None, LRU_CONV, D_LRU)),
        _whole((1, D_LRU)), _whole((2, GATE_HALF, D_LRU)), _whole((1, D_LRU)), _whole((1, D_LRU)),
        _whole((1, D_LRU)), _whole((CONV_W, 1, D_CONV)), _whole((1, D_CONV)), _whole((1, D_CONV)),
        _whole((1, D_CONV)), _whole((D_CONV, D_CONV)), _whole((1, D_CONV)),
        _whole((D_MIX, D_MODEL)), _whole((1, D_MODEL)),
    ]
    in_specs = [
        pl.BlockSpec((nb, steps, D_MODEL), lambda g: (g, 0, 0)),
        pl.BlockSpec((nb, D_LRU), lambda g: (g, 0)),
        pl.BlockSpec((LRU_CONV - 1, nb, D_LRU), lambda g: (0, g, 0)),
        pl.BlockSpec((CONV_W - 1, nb, D_CONV), lambda g: (0, g, 0)),
        pl.BlockSpec(memory_space=pl.ANY),
        pl.BlockSpec(memory_space=pl.ANY),
    ] + weight_specs
    out_specs = [
        pl.BlockSpec((nb, steps, D_MODEL), lambda g: (g, 0, 0)),
        pl.BlockSpec((nb, D_LRU), lambda g: (g, 0)),
        pl.BlockSpec((LRU_CONV - 1, nb, D_LRU), lambda g: (0, g, 0)),
        pl.BlockSpec((CONV_W - 1, nb, D_CONV), lambda g: (0, g, 0)),
    ]
    out_shape = [
        jax.ShapeDtypeStruct((bsz, steps, D_MODEL), jnp.float32),
        jax.ShapeDtypeStruct((bsz, D_LRU), jnp.float32),
        jax.ShapeDtypeStruct((LRU_CONV - 1, bsz, D_LRU), jnp.float32),
        jax.ShapeDtypeStruct((CONV_W - 1, bsz, D_CONV), jnp.float32),
    ]
    scratch = [
        pltpu.VMEM((rows, D_IN), jnp.float32),
        pltpu.VMEM((N_LRU_SLABS, prow, LANES), jnp.float32),
        pltpu.VMEM((prow, D_LRU), jnp.float32),
        pltpu.VMEM((prow, 2 * D_LRU), jnp.float32),
        pltpu.VMEM((N_LRU_SLABS, prow, LANES), jnp.float32),
        pltpu.VMEM((N_LRU_SLABS, prow, LANES), jnp.float32),
        pltpu.VMEM((N_CONV_SLABS, nb * (CONV_HALO + steps), LANES), jnp.float32),
        pltpu.VMEM((rows, D_CONV), jnp.bfloat16),
        pltpu.VMEM((rows, D_MIX), jnp.bfloat16),
        pltpu.VMEM((N_XHEADS * rows, N_MEM), jnp.float32),
        pltpu.VMEM((N_XHEADS * rows, N_MEM), jnp.bfloat16),
        pltpu.VMEM((bsz, D_XATTN, N_MEM), jnp.float32),
        pltpu.VMEM((bsz, D_XATTN, N_MEM), jnp.float32),
        pltpu.SemaphoreType.DMA((2 * (bsz // nb),)),
    ]
    return pl.pallas_call(
        functools.partial(_sample_kernel, nb, steps),
        grid=(bsz // nb,),
        in_specs=in_specs,
        out_specs=out_specs,
        out_shape=out_shape,
        scratch_shapes=scratch,
        compiler_params=pltpu.CompilerParams(
            dimension_semantics=("arbitrary",),
            vmem_limit_bytes=VMEM_LIMIT_BYTES),
        name="sample_layer",
    )(x, h0, lb, cb, kt, vt, *weights)


def _time_major(state):
    return jnp.transpose(state, (1, 0, 2))


def _memory_t(mem):
    bsz = mem.shape[0]
    return jnp.transpose(mem, (0, 2, 3, 1)).reshape(bsz, D_XATTN, N_MEM)


def _memory_from_t(mem_t):
    bsz = mem_t.shape[0]
    return jnp.transpose(mem_t.reshape(bsz, N_XHEADS, XHEAD_DIM, N_MEM), (0, 3, 1, 2))


def kernel(x_prompt, x_sample, state_lru_h, state_lru_conv, state_conv, cache_mem_k, cache_mem_v,
           mem_prompt, g_norm, w_in, w_lru_conv, b_lru_conv, w_gate_a, b_gate_a, w_gate_x, b_gate_x,
           lru_lambda, w_dw, b_dw, ln_g, ln_b, w_pw, b_pw, g_mem, w_mem_k, w_mem_v, w_out, g_final):
    depth = g_norm.shape[0]
    assert depth == 1
    w = {
        "g_norm": g_norm, "w_in": w_in, "w_lc": w_lru_conv, "b_lc": b_lru_conv,
        "w_ga": w_gate_a, "w_gx": w_gate_x, "b_ga": b_gate_a, "b_gx": b_gate_x, "lam": lru_lambda,
        "w_dw": jnp.transpose(w_dw, (1, 0, 2)), "b_dw": b_dw, "ln_g": ln_g, "ln_b": ln_b,
        "w_pw": w_pw, "b_pw": b_pw, "w_out": w_out, "g_final": g_final.reshape(1, D_MODEL),
        "g_mem": g_mem, "w_mem_k": w_mem_k, "w_mem_v": w_mem_v,
    }
    y_prompt, mkt, mvt, ph, plb, pcb, win_bf, wout_bf, wpw_bf, wg_bf = _prompt(x_prompt, mem_prompt, w)

    y_sample, sh, slb, scb = _sample(
        x_sample, state_lru_h[0], _time_major(state_lru_conv[0]), _time_major(state_conv[0]),
        _memory_t(cache_mem_k[0]), _memory_t(cache_mem_v[0]), w, win_bf, wout_bf, wpw_bf, wg_bf)

    return (y_prompt, y_sample,
            ph[None], _time_major(plb)[None], _time_major(pcb)[None],
            _memory_from_t(mkt)[None], _memory_from_t(mvt)[None],
            sh[None], _time_major(slb)[None], _time_major(scb)[None])
```
